```python
import jax, jax.numpy as jnp
from jax import lax
import numpy as np

D_MODEL = 1024
BATCH = 2
SEQ = 8192
DEPTH = 4

GRID_W = 64
CTX_LEN = 256
N_BRANCH = 3
BRANCH_W = 512
ML_HEADS = 8
ML_DH = 64
ML_W = ML_HEADS * ML_DH
ML_QK_W = 2 * ML_W
CONV_K = 5
MLA_HEADS = 8
Q_LORA = 384
KV_LORA = 256
NOPE_D = 64
ROPE_D = 32
QK_D = NOPE_D + ROPE_D
MLA_DV = 64
RET_HEADS = 8
RET_DK = 32
RET_DV = 64
RET_W = RET_HEADS * RET_DV
N_EXPERTS = 16
EXPERT_FF = 1024
CAP_FACTOR = 2
CHUNK = 128
QBLK = 128
ROPE_BASE = 10000.0
EPS = 1e-6
PROJ_WIDTHS = (ML_QK_W, ML_W, ML_W, 4 * ML_HEADS, Q_LORA, KV_LORA, ROPE_D,
               RET_HEADS * RET_DK, RET_HEADS * RET_DK, RET_W, RET_W, N_BRANCH * D_MODEL)
N_IN = sum(PROJ_WIDTHS)

kernel_name = 'hybrid_mlstm_mla_retention_ecmoe_dit'


def _rmsnorm(x, w):
    xf = x.astype(jnp.float32)
    out = xf * lax.rsqrt(jnp.mean(xf * xf, axis=-1, keepdims=True) + EPS)
    return (out * w.astype(jnp.float32)).astype(x.dtype)


def _head_layernorm(y, w):
    yf = y.astype(jnp.float32)
    mu = jnp.mean(yf, axis=-1, keepdims=True)
    var = jnp.mean(jnp.square(yf - mu), axis=-1, keepdims=True)
    out = (yf - mu) * lax.rsqrt(var + EPS)
    b, l = y.shape[:2]
    return (out.reshape(b, l, -1) * w.astype(jnp.float32)).astype(w.dtype)


def _ada(cond, w, b):
    m = jax.nn.silu(cond) @ w + b
    return jnp.split(m[..., None, :], 6, axis=-1)


def _split_proj(p):
    idx = np.cumsum(PROJ_WIDTHS)[:-1].tolist()
    return jnp.split(p, idx, axis=-1)


def _rope_tables(row, col, dim):
    half = dim // 4
    inv = ROPE_BASE ** (-jnp.arange(half, dtype=jnp.float32) / half)
    ar = row[:, None] * inv
    ac = col[:, None] * inv
    return (jnp.cos(ar), jnp.sin(ar), jnp.cos(ac), jnp.sin(ac))


def _rot(x, cos, sin):
    xf = x.astype(jnp.float32)
    x1, x2 = jnp.split(xf, 2, axis=-1)
    c = cos[None, :, None, :]
    s = sin[None, :, None, :]
    return jnp.concatenate([x1 * c - x2 * s, x1 * s + x2 * c], axis=-1).astype(x.dtype)


def _rope2d(x, tabs):
    cr, sr, cc, sc = tabs
    d2 = x.shape[-1] // 2
    return jnp.concatenate([_rot(x[..., :d2], cr, sr), _rot(x[..., d2:], cc, sc)], axis=-1)


def _dwconv(x, w, b):
    k = w.shape[0]
    out = lax.conv_general_dilated(x, w[:, None, :].astype(x.dtype), window_strides=(1,),
                                   padding=[(k // 2, k // 2)],
                                   dimension_numbers=('NWC', 'WIO', 'NWC'),
                                   feature_group_count=x.shape[-1])
    return out + b


def _to_chunks(t):
    b, h, l = t.shape[:3]
    t = t.reshape((b, h, l // CHUNK, CHUNK) + t.shape[3:])
    return jnp.moveaxis(t, 2, 0)


def _from_chunks(t):
    t = jnp.moveaxis(t, 0, 2)
    b, h, nc, ch = t.shape[:4]
    return t.reshape((b, h, nc * ch) + t.shape[4:])


def _mlstm_scan(args, state):
    q, k, v, ig, lf = args
    idx = jnp.arange(CHUNK)
    causal = idx[:, None] >= idx[None, :]

    def step(carry, inp):
        C, n, m = carry
        qc, kc, vc, igc, lfc = inp
        b = jnp.cumsum(lfc, axis=-1)
        a = b + m[..., None]
        dlog = jnp.where(causal, b[..., :, None] - b[..., None, :] + igc[..., None, :], -jnp.inf)
        mt = jnp.maximum(a, jnp.max(dlog, axis=-1))
        w = jnp.exp(dlog - mt[..., None])
        inter = jnp.exp(a - mt)
        s = jnp.einsum('bhtd,bhsd->bhts', qc, kc) * w
        num = jnp.einsum('bhts,bhsv->bhtv', s, vc) + inter[..., None] * jnp.einsum('bhvd,bhtd->bhtv', C, qc)
        den = jnp.sum(s, axis=-1) + inter * jnp.einsum('bhd,bhtd->bht', n, qc)
        h = num / jnp.maximum(jnp.abs(den), jnp.exp(-mt))[..., None]
        bl = b[..., -1]
        g = bl[..., None] - b + igc
        m_new = jnp.maximum(bl + m, jnp.max(g, axis=-1))
        dp = jnp.exp(bl + m - m_new)
        wg = jnp.exp(g - m_new[..., None])
        C = dp[..., None, None] * C + jnp.einsum('bhs,bhsv,bhsd->bhvd', wg, vc, kc)
        n = dp[..., None] * n + jnp.einsum('bhs,bhsd->bhd', wg, kc)
        return (C, n, m_new), h

    state, hs = lax.scan(step, state, tuple(_to_chunks(t) for t in (q, k, v, ig, lf)))
    return _from_chunks(hs), state


def _retention_scan(args, lg, R):
    q, k, v = args
    idx = jnp.arange(CHUNK, dtype=jnp.float32)
    diff = idx[:, None] - idx[None, :]
    dmat = jnp.exp(jnp.where(diff[None] >= 0, diff[None] * lg[:, None, None], -jnp.inf))
    xi = jnp.exp((idx + 1.0)[None] * lg[:, None])
    kd = jnp.exp((CHUNK - 1.0 - idx)[None] * lg[:, None])
    cd = jnp.exp(CHUNK * lg)

    def step(R, inp):
        qc, kc, vc = inp
        s = jnp.einsum('bhtd,bhsd->bhts', qc, kc) * dmat
        o = jnp.einsum('bhts,bhsv->bhtv', s, vc) + xi[..., None] * jnp.einsum('bhtd,bhdv->bhtv', qc, R)
        R = cd[:, None, None] * R + jnp.einsum('bhsd,hs,bhsv->bhdv', kc, kd, vc)
        return R, o

    R, os_ = lax.scan(step, R, tuple(_to_chunks(t) for t in (q, k, v)))
    return _from_chunks(os_), R


def _bidirectional(scan_f, scan_b, ctx_f, lat_f, ctx_b, lat_b, init):
    flip = lambda ts: tuple(jnp.flip(t, axis=2) for t in ts)
    hc_f, st_f = scan_f(ctx_f, init)
    hl_f, _ = scan_f(lat_f, st_f)
    hc_b, st_b = scan_b(flip(ctx_b), init)
    hl_b, _ = scan_b(flip(lat_b), st_b)
    return hl_f + jnp.flip(hl_b, axis=2), hc_f + jnp.flip(hc_b, axis=2)


def _mlstm_inputs(qk_pre, v_pre, gates, conv_w, conv_b):
    b, l, _ = qk_pre.shape
    qk = jax.nn.silu(_dwconv(qk_pre, conv_w, conv_b))
    q, k = jnp.split(qk, 2, axis=-1)
    heads = lambda t: t.reshape(b, l, ML_HEADS, ML_DH).transpose(0, 2, 1, 3).astype(jnp.float32)
    q, k, v = heads(q), heads(k) * (ML_DH ** -0.5), heads(v_pre)
    g = gates.astype(jnp.float32).reshape(b, l, 4, ML_HEADS).transpose(2, 0, 3, 1)
    fwd = (q, k, v, g[0], jax.nn.log_sigmoid(g[1]))
    bwd = (q, k, v, g[2], jax.nn.log_sigmoid(g[3]))
    return fwd, bwd


def _mla_q(q_lat, qa_norm, wq_b, qn_w, tabs):
    b, l, _ = q_lat.shape
    q = (_rmsnorm(q_lat, qa_norm) @ wq_b).reshape(b, l, MLA_HEADS, QK_D)
    q = _rmsnorm(q, qn_w)
    if tabs is not None:
        q = jnp.concatenate([q[..., :NOPE_D], _rope2d(q[..., NOPE_D:], tabs)], axis=-1)
    return q


def _mla_kv(kv_lat, k_rope, kva_norm, wkv_b, kn_w, tabs):
    b, l, _ = kv_lat.shape
    kv = (_rmsnorm(kv_lat, kva_norm) @ wkv_b).reshape(b, l, MLA_HEADS, NOPE_D + MLA_DV)
    k_nope, v = kv[..., :NOPE_D], kv[..., NOPE_D:]
    k_r = jnp.broadcast_to(k_rope[:, :, None, :], (b, l, MLA_HEADS, ROPE_D))
    k = _rmsnorm(jnp.concatenate([k_nope, k_r], axis=-1), kn_w)
    if tabs is not None:
        k = jnp.concatenate([k[..., :NOPE_D], _rope2d(k[..., NOPE_D:], tabs)], axis=-1)
    return k, v


def _attend(q, k, v):
    b, s, h, dq = q.shape
    nb = s // QBLK
    qb = jnp.moveaxis(q.reshape(b, nb, QBLK, h, dq), 1, 0)
    scale = dq ** -0.5

    def one(qi):
        sc = jnp.einsum('bqhd,bkhd->bhqk', qi, k).astype(jnp.float32) * scale
        p = jax.nn.softmax(sc, axis=-1)
        return jnp.einsum('bhqk,bkhd->bqhd', p.astype(v.dtype), v)

    o = lax.map(one, qb)
    return jnp.moveaxis(o, 0, 1).reshape(b, s, h * v.shape[-1])


def _ret_inputs(q, k, v, tabs):
    b, l, _ = q.shape
    q = q.reshape(b, l, RET_HEADS, RET_DK)
    k = k.reshape(b, l, RET_HEADS, RET_DK)
    if tabs is not None:
        q, k = _rope2d(q, tabs), _rope2d(k, tabs)
    t = lambda a: a.transpose(0, 2, 1, 3).astype(jnp.float32)
    return (t(q), t(k) * (RET_DK ** -0.5), t(v.reshape(b, l, RET_HEADS, RET_DV)))


def _ret_log_gammas():
    lg = jnp.log1p(-jnp.exp2(-5.0 - jnp.arange(RET_HEADS, dtype=jnp.float32)))
    return lg, lg[::-1]


def _merge(ys, logits, w_branch, w_out):
    b, l, _ = logits.shape
    y = jnp.stack(ys, axis=2)
    br = jnp.einsum('blnc,ncd->blnd', y, w_branch)
    gate = jax.nn.sigmoid(logits.reshape(b, l, N_BRANCH, D_MODEL))
    return jnp.sum(gate * br, axis=2) @ w_out


def _mixer(hl, hc, tabs_mla, tabs_ret, need_ctx, w_in, b_in, conv_w, conv_b, ml_norm_w,
           qa_norm, wq_b, kva_norm, wkv_b, qn_w, kn_w, ret_norm_w, w_branch, w_out):
    pl = _split_proj(hl @ w_in + b_in)
    pc = _split_proj(hc @ w_in + b_in)
    bsz = hl.shape[0]
    lat_f, lat_b = _mlstm_inputs(pl[0], pl[1], pl[3], conv_w, conv_b)
    ctx_f, ctx_b = _mlstm_inputs(pc[0], pc[1], pc[3], conv_w, conv_b)
    init_m = (jnp.zeros((bsz, ML_HEADS, ML_DH, ML_DH), jnp.float32),
              jnp.zeros((bsz, ML_HEADS, ML_DH), jnp.float32),
              jnp.zeros((bsz, ML_HEADS), jnp.float32))
    al, ac = _bidirectional(_mlstm_scan, _mlstm_scan, ctx_f, lat_f, ctx_b, lat_b, init_m)
    ya_l = jax.nn.sigmoid(pl[2]) * _head_layernorm(al.transpose(0, 2, 1, 3), ml_norm_w)
    kc, vc = _mla_kv(pc[5], pc[6], kva_norm, wkv_b, kn_w, None)
    kl, vl = _mla_kv(pl[5], pl[6], kva_norm, wkv_b, kn_w, tabs_mla)
    ql = _mla_q(pl[4], qa_norm, wq_b, qn_w, tabs_mla)
    yb_l = _attend(ql, jnp.concatenate([kc, kl], axis=1), jnp.concatenate([vc, vl], axis=1))
    lg_f, lg_b = _ret_log_gammas()
    scan_f = lambda args, st: _retention_scan(args, lg_f, st)
    scan_b = lambda args, st: _retention_scan(args, lg_b, st)
    r_lat = _ret_inputs(pl[7], pl[8], pl[9], tabs_ret)
    r_ctx = _ret_inputs(pc[7], pc[8], pc[9], None)
    init_r = jnp.zeros((bsz, RET_HEADS, RET_DK, RET_DV), jnp.float32)
    rl, rc = _bidirectional(scan_f, scan_b, r_ctx, r_lat, r_ctx, r_lat, init_r)
    yc_l = jax.nn.silu(pl[10]) * _head_layernorm(rl.transpose(0, 2, 1, 3), ret_norm_w)
    y_lat = _merge((ya_l, yb_l, yc_l), pl[11], w_branch, w_out)
    if not need_ctx:
        return y_lat, None
    ya_c = jax.nn.sigmoid(pc[2]) * _head_layernorm(ac.transpose(0, 2, 1, 3), ml_norm_w)
    qc = _mla_q(pc[4], qa_norm, wq_b, qn_w, None)
    yb_c = _attend(qc, kc, vc)
    yc_c = jax.nn.silu(pc[10]) * _head_layernorm(rc.transpose(0, 2, 1, 3), ret_norm_w)
    y_ctx = _merge((ya_c, yb_c, yc_c), pc[11], w_branch, w_out)
    return y_lat, y_ctx


def _ec_moe(h, router_w, w1, w3, w2):
    bsz, n, _ = h.shape
    cap = CAP_FACTOR * n // N_EXPERTS
    aff = jax.nn.softmax((h @ router_w).astype(jnp.float32), axis=-1)
    g, idx = lax.top_k(jnp.swapaxes(aff, 1, 2), cap)
    bidx = jnp.arange(bsz)[:, None, None]
    xs = h[bidx, idx]
    hid = jax.nn.silu(jnp.einsum('becd,edf->becf', xs, w1)) * jnp.einsum('becd,edf->becf', xs, w3)
    ys = jnp.einsum('becf,efd->becd', hid, w2) * g[..., None].astype(h.dtype)
    return jnp.zeros_like(h).at[bidx, idx].add(ys.astype(h.dtype))


def setup_inputs(seed: int = 0) -> dict:
    key = jax.random.key(seed)
    ks = jax.random.split(key, 32)
    f32 = jnp.float32
    nrm = lambda k, shape: jax.random.normal(k, shape, f32)
    w = lambda k, shape, fan_in, s=1.0: nrm(k, shape) * (s * fan_in ** -0.5)
    gain = lambda k, shape: 1.0 + 0.02 * nrm(k, shape)
    bias = lambda k, shape: 0.02 * nrm(k, shape)
    D = D_MODEL
    b_in = bias(ks[9], (DEPTH, N_IN))
    goff = ML_QK_W + 2 * ML_W
    fb = jnp.linspace(3.0, 6.0, ML_HEADS, dtype=f32) + bias(ks[10], (DEPTH, 2, ML_HEADS))
    b_in = b_in.at[:, goff + ML_HEADS:goff + 2 * ML_HEADS].set(fb[:, 0])
    b_in = b_in.at[:, goff + 3 * ML_HEADS:goff + 4 * ML_HEADS].set(fb[:, 1])
    return {
        'x': nrm(ks[0], (BATCH, SEQ, D)),
        'c': nrm(ks[1], (BATCH, D)),
        'ctx': nrm(ks[2], (BATCH, CTX_LEN, D)),
        'c_ctx': nrm(ks[3], (D,)),
        'ada_w': w(ks[4], (DEPTH, D, 6 * D), D, 0.5),
        'ada_b': bias(ks[5], (DEPTH, 6 * D)),
        'norm1_w': gain(ks[6], (DEPTH, D)),
        'norm2_w': gain(ks[7], (DEPTH, D)),
        'w_in': w(ks[8], (DEPTH, D, N_IN), D),
        'b_in': b_in,
        'conv_w': w(ks[11], (DEPTH, CONV_K, ML_QK_W), CONV_K),
        'conv_b': bias(ks[12], (DEPTH, ML_QK_W)),
        'ml_norm_w': gain(ks[13], (DEPTH, ML_W)),
        'mla_qa_norm': gain(ks[14], (DEPTH, Q_LORA)),
        'mla_wq_b': w(ks[15], (DEPTH, Q_LORA, MLA_HEADS * QK_D), Q_LORA),
        'mla_kva_norm': gain(ks[16], (DEPTH, KV_LORA)),
        'mla_wkv_b': w(ks[17], (DEPTH, KV_LORA, MLA_HEADS * (NOPE_D + MLA_DV)), KV_LORA),
        'q_norm_w': gain(ks[18], (DEPTH, QK_D)),
        'k_norm_w': gain(ks[19], (DEPTH, QK_D)),
        'ret_norm_w': gain(ks[20], (DEPTH, RET_W)),
        'w_branch': w(ks[21], (DEPTH, N_BRANCH, BRANCH_W, D), BRANCH_W),
        'w_out': w(ks[22], (DEPTH, D, D), D),
        'router_w': w(ks[23], (DEPTH, D, N_EXPERTS), D),
        'exp_w1': w(ks[24], (DEPTH, N_EXPERTS, D, EXPERT_FF), D),
        'exp_w3': w(ks[25], (DEPTH, N_EXPERTS, D, EXPERT_FF), D),
        'exp_w2': w(ks[26], (DEPTH, N_EXPERTS, EXPERT_FF, D), EXPERT_FF),
    }


def reference(x, c, ctx, c_ctx, ada_w, ada_b, norm1_w, norm2_w, w_in, b_in, conv_w, conv_b,
              ml_norm_w, mla_qa_norm, mla_wq_b, mla_kva_norm, mla_wkv_b, q_norm_w, k_norm_w,
              ret_norm_w, w_branch, w_out, router_w, exp_w1, exp_w3, exp_w2):
    n_lat = x.shape[1]
    rows = n_lat // GRID_W
    row = jnp.repeat(jnp.arange(rows), GRID_W).astype(jnp.float32)
    col = jnp.tile(jnp.arange(GRID_W), rows).astype(jnp.float32)
    tabs_mla = _rope_tables(row, col, ROPE_D)
    tabs_ret = _rope_tables(row, col, RET_DK)
    for l in range(DEPTH):
        last = l == DEPTH - 1
        sh1, sc1, g1, sh2, sc2, g2 = _ada(c, ada_w[l], ada_b[l])
        csh1, csc1, cg1, csh2, csc2, cg2 = _ada(c_ctx, ada_w[l], ada_b[l])
        hl = _rmsnorm(x, norm1_w[l]) * (1.0 + sc1) + sh1
        hc = _rmsnorm(ctx, norm1_w[l]) * (1.0 + csc1) + csh1
        y_lat, y_ctx = _mixer(hl, hc, tabs_mla, tabs_ret, not last, w_in[l], b_in[l], conv_w[l], conv_b[l],
                              ml_norm_w[l], mla_qa_norm[l], mla_wq_b[l], mla_kva_norm[l], mla_wkv_b[l],
                              q_norm_w[l], k_norm_w[l], ret_norm_w[l], w_branch[l], w_out[l])
        x = x + g1 * y_lat
        hl = _rmsnorm(x, norm2_w[l]) * (1.0 + sc2) + sh2
        x = x + g2 * _ec_moe(hl, router_w[l], exp_w1[l], exp_w3[l], exp_w2[l])
        if not last:
            ctx = ctx + cg1 * y_ctx
            hc = _rmsnorm(ctx, norm2_w[l]) * (1.0 + csc2) + csh2
            ctx = ctx + cg2 * _ec_moe(hc, router_w[l], exp_w1[l], exp_w3[l], exp_w2[l])
    return x
```

```python
import functools

import jax
import jax.numpy as jnp
import numpy as np
from jax import lax
from jax.experimental import pallas as pl
from jax.experimental.pallas import tpu as pltpu

F32 = jnp.float32
BF16 = jnp.bfloat16
HIGHEST = lax.Precision.HIGHEST

D_MODEL = 1024
GRID_W = 64
N_BRANCH = 3
BRANCH_W = 512
ML_HEADS = 8
ML_DH = 64
CONV_K = 5
MLA_HEADS = 8
Q_LORA = 384
KV_LORA = 256
NOPE_D = 64
ROPE_D = 32
QK_D = NOPE_D + ROPE_D
MLA_DV = 64
RET_HEADS = 8
RET_DK = 32
RET_DV = 64
N_EXPERTS = 16
EXPERT_FF = 1024
CAP_FACTOR = 2
CHUNK = 128
ROPE_BASE = 10000.0
EPS = 1e-6

LANES = 128
TM = 256
VMEM_LIMIT = 56 * 1024 * 1024

OFF_BL, OFF_QK, OFF_V, OFF_OG, OFF_RV, OFF_RG = 0, 3072, 4096, 4608, 5120, 5632
OFF_RQ, OFF_RK, OFF_KV, OFF_QL, OFF_MISC = 6144, 6400, 6656, 6912, 7296
NCOL = 7680
TN_IN = 2560


def _sigmoid(x):
    return 1.0 / (1.0 + jnp.exp(-x))


def _silu(x):
    return x * _sigmoid(x)


def _log_sigmoid(x):
    return jnp.minimum(x, 0.0) - jnp.log1p(jnp.exp(-jnp.abs(x)))


def _dot(a, b, **kw):
    return jnp.dot(a, b, preferred_element_type=F32, **kw)


def _dot_nt(a, b, **kw):
    return lax.dot_general(a, b, (((1,), (1,)), ((), ())), preferred_element_type=F32, **kw)


def _dot_tn(a, b, **kw):
    return lax.dot_general(a, b, (((0,), (0,)), ((), ())), preferred_element_type=F32, **kw)


def _params(sem):
    return pltpu.CompilerParams(dimension_semantics=sem, vmem_limit_bytes=VMEM_LIMIT)


def _ada_kernel(c_ref, w_ref, b_ref, o_ref):
    s = _silu(c_ref[...])
    o_ref[...] = _dot(s, w_ref[...], precision=HIGHEST) + b_ref[...]


def _ada(cond8, ada_w, ada_b):
    depth = ada_w.shape[0]
    tn = 1536
    return pl.pallas_call(
        _ada_kernel,
        grid=(depth, 6 * D_MODEL // tn),
        in_specs=[pl.BlockSpec((8, D_MODEL), lambda l, j: (0, 0)),
                  pl.BlockSpec((None, D_MODEL, tn), lambda l, j: (l, 0, j)),
                  pl.BlockSpec((None, 1, tn), lambda l, j: (l, 0, j))],
        out_specs=pl.BlockSpec((None, 8, tn), lambda l, j: (l, 0, j)),
        out_shape=jax.ShapeDtypeStruct((depth, 8, 6 * D_MODEL), F32),
        compiler_params=_params(("arbitrary", "arbitrary")),
        name="ada",
    )(cond8, ada_w, ada_b.reshape(depth, 1, 6 * D_MODEL))


def _inproj_kernel(x_ref, mod_ref, nw_ref, w_ref, b_ref, o_ref, wb_ref):
    @pl.when((pl.program_id(1) == 0) & (pl.program_id(2) == 0))
    def _():
        wb_ref[...] = w_ref[...].astype(BF16)

    x = x_ref[...]
    hn = x * lax.rsqrt(jnp.mean(x * x, axis=-1, keepdims=True) + EPS) * nw_ref[...]
    sh = mod_ref[:, 0:D_MODEL]
    sc = mod_ref[:, D_MODEL:2 * D_MODEL]
    h = hn * (1.0 + sc) + sh
    o_ref[...] = _dot(h.astype(BF16), wb_ref[...]) + b_ref[...]


def _inproj(x, mods, nw, w, b, nlt):
    bsz, ltot, _ = x.shape

    def mrow(j, bb, i):
        return (jnp.where(i < nlt, bb, bsz), 0, 0)

    return pl.pallas_call(
        _inproj_kernel,
        grid=(NCOL // TN_IN, bsz, ltot // TM),
        in_specs=[pl.BlockSpec((None, TM, D_MODEL), lambda j, bb, i: (bb, i, 0)),
                  pl.BlockSpec((None, 1, 6 * D_MODEL), mrow),
                  pl.BlockSpec((1, D_MODEL), lambda j, bb, i: (0, 0)),
                  pl.BlockSpec((D_MODEL, TN_IN), lambda j, bb, i: (0, j)),
                  pl.BlockSpec((1, TN_IN), lambda j, bb, i: (0, j))],
        out_specs=pl.BlockSpec((None, TM, TN_IN), lambda j, bb, i: (bb, i, j)),
        out_shape=jax.ShapeDtypeStruct((bsz, ltot, NCOL), F32),
        scratch_shapes=[pltpu.VMEM((D_MODEL, TN_IN), BF16)],
        compiler_params=_params(("arbitrary", "arbitrary", "arbitrary")),
        name="inproj",
    )(x, mods, nw, w, b)


def _chunk_of(d, s, nl, ncx):
    fwd = jnp.where(s < ncx, nl + s, s - ncx)
    bwd = jnp.where(s < ncx, nl + ncx - 1 - s, nl - 1 - (s - ncx))
    return jnp.where(d == 0, fwd, bwd)


def _lane_iota(shape):
    return lax.broadcasted_iota(jnp.int32, shape, len(shape) - 1)


def _mlstm_kernel(qk_ref, hp_ref, hn_ref, v_ref, misc_ref, cw_ref, cb_ref, o_ref,
                  xe_ref, ct_ref, n_ref, m_ref, *, nl, ncx):
    d = pl.program_id(1)
    s = pl.program_id(2)
    c = _chunk_of(d, s, nl, ncx)

    @pl.when(s == 0)
    def _():
        ct_ref[...] = jnp.zeros_like(ct_ref)
        n_ref[...] = jnp.zeros_like(n_ref)
        m_ref[...] = jnp.zeros_like(m_ref)

    first = (c == 0) | (c == nl)
    last = (c == nl - 1) | (c == nl + ncx - 1)
    xe_ref[0:8, :] = jnp.where(first, 0.0, hp_ref[...])
    xe_ref[8:8 + CHUNK, :] = qk_ref[...]
    xe_ref[8 + CHUNK:16 + CHUNK, :] = jnp.where(last, 0.0, hn_ref[...])
    acc = jnp.broadcast_to(cb_ref[...], (CHUNK, 2 * ML_HEADS * ML_DH))
    for j in range(CONV_K):
        acc = acc + xe_ref[8 - CONV_K // 2 + j:8 - CONV_K // 2 + j + CHUNK, :] * cw_ref[j:j + 1, :]
    qk = _silu(acc)
    hw = ML_HEADS * ML_DH
    q_all = qk[:, :hw]
    k_all = qk[:, hw:] * (ML_DH ** -0.5)
    v_all = v_ref[...]

    g_raw = misc_ref[...]
    g_dir = jnp.where(d == 0, g_raw, pltpu.roll(g_raw, LANES - 2 * ML_HEADS, axis=1))
    ls = _log_sigmoid(g_dir)
    ti = lax.broadcasted_iota(jnp.int32, (CHUNK, CHUNK), 0)
    si = lax.broadcasted_iota(jnp.int32, (CHUNK, CHUNK), 1)
    sign = 1 - 2 * d
    causal = (ti - si) * sign >= 0
    tri = jnp.where(causal, 1.0, 0.0)
    tri_t = jnp.where((si - ti) * sign >= 0, 1.0, 0.0)
    b_cols = _dot(tri, ls, precision=HIGHEST)
    b_rows = _dot(ls.T, tri_t, precision=HIGHEST)
    g_rows = g_dir.T
    bl_row = jnp.where(d == 0, b_cols[CHUNK - 1:CHUNK, :], b_cols[0:1, :])

    lane = _lane_iota((1, LANES))
    lo_half = lane < ML_DH
    outs = []
    for p in range(ML_HEADS // 2):
        qp = q_all[:, p * LANES:(p + 1) * LANES]
        kp = k_all[:, p * LANES:(p + 1) * LANES]
        vp = v_all[:, p * LANES:(p + 1) * LANES]
        kp_b = kp.astype(BF16)
        vp_b = vp.astype(BF16)
        ct = ct_ref[p]
        n_row = n_ref[p]
        qc = _dot(qp.astype(BF16), ct.astype(BF16))
        h_pair, wg_pair, dp_pair = [], [], []
        for j in range(2):
            h = 2 * p + j
            hmask = lo_half if j == 0 else jnp.logical_not(lo_half)
            m_h = m_ref[h:h + 1, 0:1]
            b_col = b_cols[:, ML_HEADS + h:ML_HEADS + h + 1]
            b_row = b_rows[ML_HEADS + h:ML_HEADS + h + 1, :]
            ig_row = g_rows[h:h + 1, :]
            ig_col = g_dir[:, h:h + 1]
            bl = bl_row[:, ML_HEADS + h:ML_HEADS + h + 1]
            a = b_col + m_h
            dlog = jnp.where(causal, b_col - b_row + ig_row, -jnp.inf)
            mt = jnp.maximum(a, jnp.max(dlog, axis=1, keepdims=True))
            w = jnp.exp(dlog - mt)
            inter = jnp.exp(a - mt)
            qm = jnp.where(hmask, qp, 0.0)
            sc = _dot_nt(qm.astype(BF16), kp_b) * w
            num = _dot(sc.astype(BF16), vp_b) + inter * qc
            den = jnp.sum(sc, axis=1, keepdims=True) + inter * jnp.sum(qm * n_row, axis=1, keepdims=True)
            h_pair.append(num / jnp.maximum(jnp.abs(den), jnp.exp(-mt)))
            g_col = bl - b_col + ig_col
            m_new = jnp.maximum(bl + m_h, jnp.max(g_col, axis=0, keepdims=True))
            dp_pair.append(jnp.exp(bl + m_h - m_new))
            wg_pair.append(jnp.exp(g_col - m_new))
            m_ref[h:h + 1, :] = jnp.broadcast_to(m_new, (1, LANES))
        outs.append(jnp.where(lo_half, h_pair[0], h_pair[1]))
        kw = kp * jnp.where(lo_half, wg_pair[0], wg_pair[1])
        dp_row = jnp.where(lo_half, dp_pair[0], dp_pair[1])
        ri = lax.broadcasted_iota(jnp.int32, (LANES, LANES), 0)
        ci = lax.broadcasted_iota(jnp.int32, (LANES, LANES), 1)
        same_head = (ri < ML_DH) == (ci < ML_DH)
        upd = _dot_tn(kw.astype(BF16), vp_b)
        ct_ref[p] = dp_row * ct + jnp.where(same_head, upd, 0.0)
        n_ref[p] = dp_row * n_row + jnp.sum(kw, axis=0, keepdims=True)
    o_ref[...] = jnp.concatenate(outs, axis=1)


def _mlstm(slab, conv_w, conv_b, nl, ncx):
    bsz, ltot, _ = slab.shape
    hw = ML_HEADS * ML_DH
    nrow8 = ltot // 8

    def ch(bb, d, s):
        return _chunk_of(d, s, nl, ncx)

    kern = functools.partial(_mlstm_kernel, nl=nl, ncx=ncx)
    return pl.pallas_call(
        kern,
        grid=(bsz, 2, nl + ncx),
        in_specs=[
            pl.BlockSpec((None, CHUNK, 2 * hw), lambda bb, d, s: (bb, ch(bb, d, s), OFF_QK // (2 * hw))),
            pl.BlockSpec((None, 8, 2 * hw),
                         lambda bb, d, s: (bb, jnp.maximum(ch(bb, d, s) * (CHUNK // 8) - 1, 0), OFF_QK // (2 * hw))),
            pl.BlockSpec((None, 8, 2 * hw),
                         lambda bb, d, s: (bb, jnp.minimum((ch(bb, d, s) + 1) * (CHUNK // 8), nrow8 - 1),
                                           OFF_QK // (2 * hw))),
            pl.BlockSpec((None, CHUNK, hw), lambda bb, d, s: (bb, ch(bb, d, s), OFF_V // hw)),
            pl.BlockSpec((None, CHUNK, LANES), lambda bb, d, s: (bb, ch(bb, d, s), OFF_MISC // LANES)),
            pl.BlockSpec((CONV_K, 2 * hw), lambda bb, d, s: (0, 0)),
            pl.BlockSpec((1, 2 * hw), lambda bb, d, s: (0, 0)),
        ],
        out_specs=pl.BlockSpec((None, None, CHUNK, hw), lambda bb, d, s: (d, bb, ch(bb, d, s), 0)),
        out_shape=jax.ShapeDtypeStruct((2, bsz, ltot, hw), F32),
        scratch_shapes=[pltpu.VMEM((CHUNK + 16, 2 * hw), F32),
                        pltpu.VMEM((ML_HEADS // 2, LANES, LANES), F32),
                        pltpu.VMEM((ML_HEADS // 2, 1, LANES), F32),
                        pltpu.VMEM((ML_HEADS, LANES), F32)],
        compiler_params=_params(("arbitrary", "arbitrary", "arbitrary")),
        name="mlstm",
    )(slab, slab, slab, slab, slab, conv_w, conv_b)


def _rope_lanes(x, tab_ref):
    return (x * tab_ref[0] + pltpu.roll(x, 8, axis=1) * tab_ref[1]
            + pltpu.roll(x, LANES - 8, axis=1) * tab_ref[2])


def _ret_kernel(q_ref, k_ref, v_ref, rope_ref, dmat_ref, xi_ref, kd_ref, cd_ref, o_ref, st_ref):
    s = pl.program_id(2)

    @pl.when(s == 0)
    def _():
        st_ref[...] = jnp.zeros_like(st_ref)

    v_all = v_ref[...]
    lane = _lane_iota((1, LANES))
    lo_half = lane < RET_DV
    ri = lax.broadcasted_iota(jnp.int32, (LANES, LANES), 0)
    ci = lax.broadcasted_iota(jnp.int32, (LANES, LANES), 1)
    outs = []
    for g in range(2):
        qg = _rope_lanes(q_ref[:, g * LANES:(g + 1) * LANES], rope_ref)
        kg = _rope_lanes(k_ref[:, g * LANES:(g + 1) * LANES], rope_ref) * (RET_DK ** -0.5)
        qg_b = qg.astype(BF16)
        kg_b = kg.astype(BF16)
        kw_b = (kg * kd_ref[:, g * LANES:(g + 1) * LANES]).astype(BF16)
        for a in range(2):
            p = 2 * g + a
            vp_b = v_all[:, p * LANES:(p + 1) * LANES].astype(BF16)
            st = st_ref[p]
            cross = _dot(qg_b, st.astype(BF16))
            sv = []
            for j in range(2):
                h = 2 * p + j
                q_lo = (2 * a + j) * RET_DK
                hm = (lane >= q_lo) & (lane < q_lo + RET_DK)
                sc = _dot_nt(jnp.where(hm, qg, 0.0).astype(BF16), kg_b) * dmat_ref[h]
                sv.append(_dot(sc.astype(BF16), vp_b))
            outs.append(jnp.where(lo_half, sv[0], sv[1]) + xi_ref[:, p * LANES:(p + 1) * LANES] * cross)
            r_lo = (ri >= 2 * a * RET_DK) & (ri < (2 * a + 1) * RET_DK)
            r_hi = (ri >= (2 * a + 1) * RET_DK) & (ri < (2 * a + 2) * RET_DK)
            valid = (r_lo & (ci < RET_DV)) | (r_hi & (ci >= RET_DV))
            upd = _dot_tn(kw_b, vp_b)
            st_ref[p] = cd_ref[:, p * LANES:(p + 1) * LANES] * st + jnp.where(valid, upd, 0.0)
    o_ref[...] = jnp.concatenate(outs, axis=1)


def _ret(slab, rope_r, dmat, xi, kd, cd, nl, ncx):
    bsz, ltot, _ = slab.shape
    qw = RET_HEADS * RET_DK
    vw = RET_HEADS * RET_DV

    def ch(bb, d, s):
        return _chunk_of(d, s, nl, ncx)

    return pl.pallas_call(
        _ret_kernel,
        grid=(bsz, 2, nl + ncx),
        in_specs=[
            pl.BlockSpec((None, CHUNK, qw), lambda bb, d, s: (bb, ch(bb, d, s), OFF_RQ // qw)),
            pl.BlockSpec((None, CHUNK, qw), lambda bb, d, s: (bb, ch(bb, d, s), OFF_RK // qw)),
            pl.BlockSpec((None, CHUNK, vw), lambda bb, d, s: (bb, ch(bb, d, s), OFF_RV // vw)),
            pl.BlockSpec((3, CHUNK, LANES), lambda bb, d, s: (0, ch(bb, d, s), 0)),
            pl.BlockSpec((None, RET_HEADS, CHUNK, CHUNK), lambda bb, d, s: (d, 0, 0, 0)),
            pl.BlockSpec((None, CHUNK, vw), lambda bb, d, s: (d, 0, 0)),
            pl.BlockSpec((None, CHUNK, qw), lambda bb, d, s: (d, 0, 0)),
            pl.BlockSpec((None, 1, vw), lambda bb, d, s: (d, 0, 0)),
        ],
        out_specs=pl.BlockSpec((None, None, CHUNK, vw), lambda bb, d, s: (d, bb, ch(bb, d, s), 0)),
        out_shape=jax.ShapeDtypeStruct((2, bsz, ltot, vw), F32),
        scratch_shapes=[pltpu.VMEM((RET_HEADS // 2, LANES, LANES), F32)],
        compiler_params=_params(("arbitrary", "arbitrary", "arbitrary")),
        name="ret",
    )(slab, slab, slab, rope_r, dmat, xi, kd, cd)


def _ret_tables():
    lg = jnp.log1p(-jnp.exp2(-5.0 - jnp.arange(RET_HEADS, dtype=F32)))
    idx = jnp.arange(CHUNK, dtype=F32)
    diff = idx[:, None] - idx[None, :]
    dm, xis, kds, cds = [], [], [], []
    for lgd, sign in ((lg, 1.0), (lg[::-1], -1.0)):
        dd = diff[None] * sign
        dm.append(jnp.exp(jnp.where(dd >= 0, dd * lgd[:, None, None], -jnp.inf)))
        order = idx if sign > 0 else (CHUNK - 1.0 - idx)
        xi = jnp.exp((order + 1.0)[None] * lgd[:, None])
        kdv = jnp.exp((CHUNK - 1.0 - order)[None] * lgd[:, None])
        cdv = jnp.exp(CHUNK * lgd)
        xis.append(jnp.repeat(xi.T, RET_DV, axis=1))
        kds.append(jnp.repeat(kdv.T, RET_DK, axis=1))
        cds.append(jnp.repeat(cdv, RET_DV)[None, :])
    return jnp.stack(dm), jnp.stack(xis), jnp.stack(kds), jnp.stack(cds)


def _rope_tables(n_lat, n_ctx):
    t = jnp.arange(n_lat)
    row = (t // GRID_W).astype(F32)
    col = (t % GRID_W).astype(F32)
    half = ROPE_D // 4
    inv = ROPE_BASE ** (-jnp.arange(half, dtype=F32) / half)
    ar, ac = row[:, None] * inv, col[:, None] * inv
    z = jnp.zeros_like(ar)
    c32 = jnp.concatenate([jnp.cos(ar), jnp.cos(ar), jnp.cos(ac), jnp.cos(ac)], axis=1)
    s1 = jnp.concatenate([z, jnp.sin(ar), z, jnp.sin(ac)], axis=1)
    s2 = jnp.concatenate([-jnp.sin(ar), z, -jnp.sin(ac), z], axis=1)
    tab = jnp.stack([c32, s1, s2])
    ident = jnp.stack([jnp.ones((n_ctx, ROPE_D), F32), jnp.zeros((n_ctx, ROPE_D), F32),
                       jnp.zeros((n_ctx, ROPE_D), F32)])
    return jnp.concatenate([tab, ident], axis=1)


def _mla_kernel(ql_ref, kv_ref, misc_ref, qan_ref, wqt_ref, kvn_ref, wk_ref, wvt_ref, qnw_ref, knw_ref,
                ropek_ref, ropeq_ref, qt_ref, k_ref, vt_ref):
    ql = ql_ref[...]
    qn = ql * lax.rsqrt(jnp.mean(ql * ql, axis=-1, keepdims=True) + EPS) * qan_ref[...]
    qt = _dot_nt(wqt_ref[...].astype(BF16), qn.astype(BF16))
    for h in range(MLA_HEADS):
        xh = qt[h * LANES:(h + 1) * LANES, :]
        ss = jnp.sum(xh * xh, axis=0, keepdims=True)
        xh = xh * lax.rsqrt(ss * (1.0 / QK_D) + EPS) * qnw_ref[...]
        xh = (xh * ropeq_ref[0] + pltpu.roll(xh, 8, axis=0) * ropeq_ref[1]
              + pltpu.roll(xh, LANES - 8, axis=0) * ropeq_ref[2])
        qt_ref[h * LANES:(h + 1) * LANES, :] = (xh * (QK_D ** -0.5)).astype(BF16)

    kv = kv_ref[...]
    kvn = (kv * lax.rsqrt(jnp.mean(kv * kv, axis=-1, keepdims=True) + EPS) * kvn_ref[...]).astype(BF16)
    kk = _dot(kvn, wk_ref[...].astype(BF16))
    lane = _lane_iota((1, LANES))
    rope_part = (lane >= NOPE_D) & (lane < QK_D)
    kr = jnp.where(rope_part, pltpu.roll(misc_ref[...], NOPE_D - 4 * ML_HEADS, axis=1), 0.0)
    for h in range(MLA_HEADS):
        xh = kk[:, h * LANES:(h + 1) * LANES] + kr
        ss = jnp.sum(xh * xh, axis=1, keepdims=True)
        xh = xh * lax.rsqrt(ss * (1.0 / QK_D) + EPS) * knw_ref[...]
        k_ref[:, h * LANES:(h + 1) * LANES] = _rope_lanes(xh, ropek_ref).astype(BF16)
    vt_ref[...] = _dot_nt(wvt_ref[...].astype(BF16), kvn).astype(BF16)


def _mla(slab, qan, wqt, kvn, wk, wvt, qnw, knw, rope_k, rope_q):
    bsz, ltot, _ = slab.shape
    hq = MLA_HEADS * LANES
    hv = MLA_HEADS * MLA_DV
    full = lambda shape: pl.BlockSpec(shape, lambda bb, i: (0,) * len(shape))
    return pl.pallas_call(
        _mla_kernel,
        grid=(bsz, ltot // TM),
        in_specs=[pl.BlockSpec((None, TM, Q_LORA), lambda bb, i: (bb, i, OFF_QL // Q_LORA)),
                  pl.BlockSpec((None, TM, KV_LORA), lambda bb, i: (bb, i, OFF_KV // KV_LORA)),
                  pl.BlockSpec((None, TM, LANES), lambda bb, i: (bb, i, OFF_MISC // LANES)),
                  full((1, Q_LORA)), full((hq, Q_LORA)), full((1, KV_LORA)), full((KV_LORA, hq)),
                  full((hv, KV_LORA)), full((LANES, 1)), full((1, LANES)),
                  pl.BlockSpec((3, TM, LANES), lambda bb, i: (0, i, 0)),
                  pl.BlockSpec((3, LANES, TM), lambda bb, i: (0, 0, i))],
        out_specs=[pl.BlockSpec((None, hq, TM), lambda bb, i: (bb, 0, i)),
                   pl.BlockSpec((None, TM, hq), lambda bb, i: (bb, i, 0)),
                   pl.BlockSpec((None, hv, TM), lambda bb, i: (bb, 0, i))],
        out_shape=[jax.ShapeDtypeStruct((bsz, hq, ltot), BF16),
                   jax.ShapeDtypeStruct((bsz, ltot, hq), BF16),
                   jax.ShapeDtypeStruct((bsz, hv, ltot), BF16)],
        compiler_params=_params(("arbitrary", "arbitrary")),
        name="mla",
    )(slab, slab, slab, qan, wqt, kvn, wk, wvt, qnw, knw, rope_k, rope_q)


ATT_SUB = 256


def _attn_kernel(qt_ref, k_ref, vt_ref, o_ref, acc_ref, m_ref, l_ref, *, tk):
    ki = pl.program_id(3)

    @pl.when(ki == 0)
    def _():
        acc_ref[...] = jnp.zeros_like(acc_ref)
        m_ref[...] = jnp.full_like(m_ref, -jnp.inf)
        l_ref[...] = jnp.zeros_like(l_ref)

    for j in range(2):
        q_t = qt_ref[j * LANES:(j + 1) * LANES, :]
        for u in range(tk // ATT_SUB):
            kb = k_ref[u * ATT_SUB:(u + 1) * ATT_SUB, j * LANES:(j + 1) * LANES]
            sc = _dot(kb, q_t)
            m_old = m_ref[j:j + 1, :]
            m_new = jnp.maximum(m_old, jnp.max(sc, axis=0, keepdims=True))
            p = jnp.exp(sc - m_new)
            alpha = jnp.exp(m_old - m_new)
            l_ref[j:j + 1, :] = alpha * l_ref[j:j + 1, :] + jnp.sum(p, axis=0, keepdims=True)
            vb = vt_ref[j * MLA_DV:(j + 1) * MLA_DV, u * ATT_SUB:(u + 1) * ATT_SUB]
            acc_ref[j * MLA_DV:(j + 1) * MLA_DV, :] = (
                alpha * acc_ref[j * MLA_DV:(j + 1) * MLA_DV, :] + _dot(vb, p.astype(BF16)))
            m_ref[j:j + 1, :] = m_new

    @pl.when(ki == pl.num_programs(3) - 1)
    def _():
        lane_rows = lax.broadcasted_iota(jnp.int32, acc_ref.shape, 0)
        inv = jnp.where(lane_rows < MLA_DV, 1.0 / l_ref[0:1, :], 1.0 / l_ref[1:2, :])
        o_ref[...] = (acc_ref[...] * inv).T


def _attn(qt, k, vt, tq, tk, q_off, nq, k_off, nk):
    bsz, hq, ltot = qt.shape
    hv = vt.shape[1]
    kern = functools.partial(_attn_kernel, tk=tk)
    return pl.pallas_call(
        kern,
        grid=(bsz, MLA_HEADS // 2, nq, nk),
        in_specs=[pl.BlockSpec((None, 2 * LANES, tq), lambda bb, hp, qi, ki: (bb, hp, qi + q_off)),
                  pl.BlockSpec((None, tk, 2 * LANES), lambda bb, hp, qi, ki: (bb, ki + k_off, hp)),
                  pl.BlockSpec((None, 2 * MLA_DV, tk), lambda bb, hp, qi, ki: (bb, hp, ki + k_off))],
        out_specs=pl.BlockSpec((None, tq, 2 * MLA_DV), lambda bb, hp, qi, ki: (bb, qi, hp)),
        out_shape=jax.ShapeDtypeStruct((bsz, nq * tq, hv), F32),
        scratch_shapes=[pltpu.VMEM((2 * MLA_DV, tq), F32), pltpu.VMEM((8, tq), F32), pltpu.VMEM((8, tq), F32)],
        compiler_params=_params(("arbitrary", "arbitrary", "arbitrary", "arbitrary")),
        name="attn",
    )(qt, k, vt)


def _split_dot(x, w_b):
    hi = x.astype(BF16)
    lo = (x - hi.astype(F32)).astype(BF16)
    return _dot(hi, w_b) + _dot(lo, w_b)


def _head_ln(x, g_ref, gt_ref, width):
    mu = _split_dot(_split_dot(x, g_ref[...]) * (1.0 / width), gt_ref[...])
    xc = x - mu
    var = _split_dot(_split_dot(xc * xc, g_ref[...]) * (1.0 / width), gt_ref[...])
    return xc * lax.rsqrt(var + EPS)


def _merge_kernel(hm_ref, og_ref, mlw_ref, yb_ref, hr_ref, rg_ref, rnw_ref, bl_ref, wbr_ref, wout_ref,
                  x_ref, mod_ref, n2w_ref, rw_ref, g_ref, gt_ref,
                  xo_ref, h2_ref, aff_ref, wbr_b, wout_b):
    @pl.when((pl.program_id(0) == 0) & (pl.program_id(1) == 0))
    def _():
        wbr_b[...] = wbr_ref[...].astype(BF16)
        wout_b[...] = wout_ref[...].astype(BF16)

    ya = _sigmoid(og_ref[...]) * (_head_ln(hm_ref[0] + hm_ref[1], g_ref, gt_ref, ML_DH) * mlw_ref[...])
    yc = _silu(rg_ref[...]) * (_head_ln(hr_ref[0] + hr_ref[1], g_ref, gt_ref, RET_DV) * rnw_ref[...])
    ys = (ya, yb_ref[...], yc)
    merged = None
    for n in range(N_BRANCH):
        br = _dot(ys[n].astype(BF16), wbr_b[n])
        term = _sigmoid(bl_ref[:, n * D_MODEL:(n + 1) * D_MODEL]) * br
        merged = term if merged is None else merged + term
    y = _dot(merged.astype(BF16), wout_b[...])
    g1 = mod_ref[:, 2 * D_MODEL:3 * D_MODEL]
    x = x_ref[...] + g1 * y
    xo_ref[...] = x
    hn = x * lax.rsqrt(jnp.mean(x * x, axis=-1, keepdims=True) + EPS) * n2w_ref[...]
    h2 = hn * (1.0 + mod_ref[:, 4 * D_MODEL:5 * D_MODEL]) + mod_ref[:, 3 * D_MODEL:4 * D_MODEL]
    h2_ref[...] = h2.astype(BF16)
    logits = _dot(h2, rw_ref[...], precision=HIGHEST)
    valid = _lane_iota((1, LANES)) < N_EXPERTS
    logits = jnp.where(valid, logits, -jnp.inf)
    e = jnp.exp(logits - jnp.max(logits, axis=-1, keepdims=True))
    aff_ref[...] = e / jnp.sum(e, axis=-1, keepdims=True)


def _merge(hm, slab, mlw, yb, hr, rnw, wbr, wout, x, mods, n2w, rw, g8, g8t, nlt):
    bsz, ltot, _ = x.shape
    hw = ML_HEADS * ML_DH

    def mrow(bb, i):
        return (jnp.where(i < nlt, bb, bsz), 0, 0)

    full = lambda shape: pl.BlockSpec(shape, lambda bb, i: (0,) * len(shape))
    row = lambda width, off: pl.BlockSpec((None, TM, width), lambda bb, i: (bb, i, off // width))
    return pl.pallas_call(
        _merge_kernel,
        grid=(bsz, ltot // TM),
        in_specs=[pl.BlockSpec((2, None, TM, hw), lambda bb, i: (0, bb, i, 0)),
                  row(hw, OFF_OG), full((1, hw)),
                  row(hw, 0),
                  pl.BlockSpec((2, None, TM, hw), lambda bb, i: (0, bb, i, 0)),
                  row(hw, OFF_RG), full((1, hw)),
                  row(N_BRANCH * D_MODEL, OFF_BL),
                  full((N_BRANCH, BRANCH_W, D_MODEL)), full((D_MODEL, D_MODEL)),
                  row(D_MODEL, 0),
                  pl.BlockSpec((None, 1, 6 * D_MODEL), mrow),
                  full((1, D_MODEL)), full((D_MODEL, LANES)), full((hw, LANES)), full((LANES, hw))],
        out_specs=[row(D_MODEL, 0), row(D_MODEL, 0), row(LANES, 0)],
        out_shape=[jax.ShapeDtypeStruct((bsz, ltot, D_MODEL), F32),
                   jax.ShapeDtypeStruct((bsz, ltot, D_MODEL), BF16),
                   jax.ShapeDtypeStruct((bsz, ltot, LANES), F32)],
        scratch_shapes=[pltpu.VMEM((N_BRANCH, BRANCH_W, D_MODEL), BF16), pltpu.VMEM((D_MODEL, D_MODEL), BF16)],
        compiler_params=_params(("arbitrary", "arbitrary")),
        name="merge",
    )(hm, slab, mlw, yb, hr, slab, rnw, slab, wbr, wout, x, mods, n2w, rw, g8, g8t)


def _select_kernel(aff_ref, gw_ref, pos_ref, post_ref, off_ref, *, n, cap, base0, base_step):
    b = pl.program_id(0)
    base = (base0 + b * base_step).astype(F32)
    bits = lax.bitcast_convert_type(aff_ref[...], jnp.int32)
    capf = jnp.float32(cap)

    def search(it, cur):
        cand = cur | jnp.left_shift(jnp.int32(1), 30 - it)
        cnt = jnp.sum(jnp.where(bits >= cand, 1.0, 0.0), axis=0, keepdims=True)
        return jnp.where(cnt >= capf, cand, cur)

    thr = lax.fori_loop(0, 31, search, jnp.zeros((1, LANES), jnp.int32))
    n_gt = jnp.sum(jnp.where(bits > thr, 1.0, 0.0), axis=0, keepdims=True)
    need = capf - n_gt

    ri = lax.broadcasted_iota(jnp.int32, (TM, TM), 0)
    ci = lax.broadcasted_iota(jnp.int32, (TM, TM), 1)
    strict = (ci < ri).astype(BF16)

    def tile(i, carry):
        c_eq, c_sel = carry
        r0 = pl.multiple_of(i * TM, TM)
        a = aff_ref[pl.ds(r0, TM), :]
        bt = lax.bitcast_convert_type(a, jnp.int32)
        eq = bt == thr
        rank = _dot(strict, eq.astype(BF16)) + c_eq
        sel = (bt > thr) | (eq & (rank < need))
        self_ = jnp.where(sel, 1.0, 0.0)
        pos = _dot(strict, self_.astype(BF16)) + c_sel
        gw_ref[pl.ds(r0, TM), :] = jnp.where(sel, a, 0.0)
        posv = jnp.where(sel, pos + base, -1.0)
        pos_ref[pl.ds(r0, TM), :] = posv
        post_ref[:, pl.ds(r0, TM)] = posv.T[0:N_EXPERTS, :]
        off_ref[pl.ds(i, 1), :] = c_sel
        return (c_eq + jnp.sum(jnp.where(eq, 1.0, 0.0), axis=0, keepdims=True),
                c_sel + jnp.sum(self_, axis=0, keepdims=True))

    zero = jnp.zeros((1, LANES), F32)
    lax.fori_loop(0, n // TM, tile, (zero, zero))


def _select(aff, n, cap, tile_off, base0, base_step):
    bsz = aff.shape[0]
    nt = n // TM
    kern = functools.partial(_select_kernel, n=n, cap=cap, base0=base0, base_step=base_step)
    return pl.pallas_call(
        kern,
        grid=(bsz,),
        in_specs=[pl.BlockSpec((None, n, LANES), lambda bb: (bb, tile_off, 0))],
        out_specs=[pl.BlockSpec((None, n, LANES), lambda bb: (bb, 0, 0)),
                   pl.BlockSpec((None, n, LANES), lambda bb: (bb, 0, 0)),
                   pl.BlockSpec((None, N_EXPERTS, n), lambda bb: (bb, 0, 0)),
                   pl.BlockSpec((None, nt, LANES), lambda bb: (bb, 0, 0))],
        out_shape=[jax.ShapeDtypeStruct((bsz, n, LANES), F32),
                   jax.ShapeDtypeStruct((bsz, n, LANES), F32),
                   jax.ShapeDtypeStruct((bsz, N_EXPERTS, n), F32),
                   jax.ShapeDtypeStruct((bsz, nt, LANES), F32)],
        compiler_params=_params(("arbitrary",)),
        name="select",
    )(aff)


WIN = TM + 16
MOE_TT = 768
FFN_ROW_CHUNKS = 4
FFN_TF = 256


def _moe_gather_kernel(st_ref, h_ref, pt_ref, xs_ref, *, nt256):
    e = pl.program_id(0)
    i = pl.program_id(1)

    @pl.when(i == 0)
    def _():
        xs_ref[...] = jnp.zeros_like(xs_ref)

    slot = lax.broadcasted_iota(jnp.int32, (WIN, TM), 0)
    for u in range(MOE_TT // TM):
        st = pl.multiple_of(st_ref[e * nt256 + i * (MOE_TT // TM) + u], 16)
        want = pt_ref[:, u * TM:(u + 1) * TM]
        onehot = jnp.where((slot + st).astype(F32) == want, 1.0, 0.0).astype(BF16)
        got = _dot(onehot, h_ref[u * TM:(u + 1) * TM, :])
        cur = xs_ref[pl.ds(st, WIN), :].astype(F32)
        xs_ref[pl.ds(st, WIN), :] = (cur + got).astype(BF16)


def _moe_gather(starts, h2, post, rows):
    ttot = h2.shape[0]
    kern = functools.partial(_moe_gather_kernel, nt256=ttot // TM)
    return pl.pallas_call(
        kern,
        grid_spec=pltpu.PrefetchScalarGridSpec(
            num_scalar_prefetch=1,
            grid=(N_EXPERTS, ttot // MOE_TT),
            in_specs=[pl.BlockSpec((MOE_TT, D_MODEL), lambda e, i, st: (i, 0)),
                      pl.BlockSpec((None, 1, MOE_TT), lambda e, i, st: (e, 0, i))],
            out_specs=pl.BlockSpec((None, rows, D_MODEL), lambda e, i, st: (e, 0, 0))),
        out_shape=jax.ShapeDtypeStruct((N_EXPERTS, rows, D_MODEL), BF16),
        compiler_params=_params(("arbitrary", "arbitrary")),
        name="moe_gather",
    )(starts, h2, post)


def _moe_ffn_kernel(xs_ref, w1_ref, w3_ref, w2_ref, ys_ref, acc_ref, *, ct, rows):
    f = pl.program_id(1)
    w1b = w1_ref[...].astype(BF16)
    w3b = w3_ref[...].astype(BF16)
    w2b = w2_ref[...].astype(BF16)
    rc = ct // FFN_ROW_CHUNKS
    for r in range(FFN_ROW_CHUNKS):
        xb = xs_ref[r * rc:(r + 1) * rc, :]
        hid = _silu(_dot(xb, w1b)) * _dot(xb, w3b)
        part = _dot(hid.astype(BF16), w2b)

        @pl.when(f == 0)
        def _():
            acc_ref[r * rc:(r + 1) * rc, :] = part

        @pl.when(f > 0)
        def _():
            acc_ref[r * rc:(r + 1) * rc, :] += part

    @pl.when(f == pl.num_programs(1) - 1)
    def _():
        ys_ref[0:ct, :] = acc_ref[...].astype(BF16)
        ys_ref[ct:rows, :] = jnp.zeros((rows - ct, D_MODEL), BF16)


def _moe_ffn(xs, w1, w3, w2, ct):
    rows = xs.shape[1]
    kern = functools.partial(_moe_ffn_kernel, ct=ct, rows=rows)
    return pl.pallas_call(
        kern,
        grid=(N_EXPERTS, EXPERT_FF // FFN_TF),
        in_specs=[pl.BlockSpec((None, ct, D_MODEL), lambda e, f: (e, 0, 0)),
                  pl.BlockSpec((None, D_MODEL, FFN_TF), lambda e, f: (e, 0, f)),
                  pl.BlockSpec((None, D_MODEL, FFN_TF), lambda e, f: (e, 0, f)),
                  pl.BlockSpec((None, FFN_TF, D_MODEL), lambda e, f: (e, f, 0))],
        out_specs=pl.BlockSpec((None, rows, D_MODEL), lambda e, f: (e, 0, 0)),
        out_shape=jax.ShapeDtypeStruct((N_EXPERTS, rows, D_MODEL), BF16),
        scratch_shapes=[pltpu.VMEM((ct, D_MODEL), F32)],
        compiler_params=_params(("arbitrary", "arbitrary")),
        name="moe_ffn",
    )(xs, w1, w3, w2)


def _combine_kernel(st_ref, x_ref, gw_ref, pos_ref, mod_ref, *rest):
    ys_refs = rest[:N_EXPERTS]
    o_ref = rest[N_EXPERTS]
    i = pl.program_id(0)
    nt = pl.num_programs(0)
    slot = lax.broadcasted_iota(jnp.int32, (TM, WIN), 1)
    pos = pos_ref[...]
    gw = gw_ref[...]
    acc = jnp.zeros((TM, D_MODEL), F32)
    for e in range(N_EXPERTS):
        st = st_ref[e * nt + i]
        onehot = jnp.where((slot + st).astype(F32) == pos[:, e:e + 1], 1.0, 0.0).astype(BF16)
        acc = acc + gw[:, e:e + 1] * _dot(onehot, ys_refs[e][0])
    g2 = mod_ref[:, 5 * D_MODEL:6 * D_MODEL]
    o_ref[...] = x_ref[...] + g2 * acc


def _combine(starts, x, gw, pos, mods_rows, ys, nlt, ltiles):
    ttot = x.shape[0]
    nt = ttot // TM
    bsz = ttot // (ltiles * TM)

    def mrow(i, st):
        return (jnp.where(i % ltiles < nlt, i // ltiles, bsz), 0, 0)

    def ys_spec(e):
        return pl.BlockSpec((pl.Element(1), pl.Element(WIN), pl.Element(D_MODEL)),
                            lambda i, st: (e, pl.multiple_of(st[e * nt + i], 16), 0))

    return pl.pallas_call(
        _combine_kernel,
        grid_spec=pltpu.PrefetchScalarGridSpec(
            num_scalar_prefetch=1,
            grid=(nt,),
            in_specs=[pl.BlockSpec((TM, D_MODEL), lambda i, st: (i, 0)),
                      pl.BlockSpec((TM, LANES), lambda i, st: (i, 0)),
                      pl.BlockSpec((TM, LANES), lambda i, st: (i, 0)),
                      pl.BlockSpec((None, 1, 6 * D_MODEL), mrow)]
                     + [ys_spec(e) for e in range(N_EXPERTS)],
            out_specs=pl.BlockSpec((TM, D_MODEL), lambda i, st: (i, 0))),
        out_shape=jax.ShapeDtypeStruct((ttot, D_MODEL), F32),
        compiler_params=_params(("arbitrary",)),
        name="combine",
    )(starts, x, gw, pos, mods_rows, *([ys] * N_EXPERTS))


def _permute_w_in(w, b):
    widths = (2 * ML_HEADS * ML_DH, ML_HEADS * ML_DH, ML_HEADS * ML_DH, 4 * ML_HEADS, Q_LORA, KV_LORA, ROPE_D,
              RET_HEADS * RET_DK, RET_HEADS * RET_DK, RET_HEADS * RET_DV, RET_HEADS * RET_DV, N_BRANCH * D_MODEL)
    offs = np.concatenate([[0], np.cumsum(widths)])
    seg = lambda a, k: a[..., offs[k]:offs[k + 1]]
    order = (11, 0, 1, 2, 9, 10, 7, 8, 5, 4, 3, 6)

    def build(a):
        parts = [seg(a, k) for k in order]
        used = sum(p.shape[-1] for p in parts)
        parts.append(jnp.zeros(a.shape[:-1] + (NCOL - used,), a.dtype))
        return jnp.concatenate(parts, axis=-1)

    return build(w), build(b)


def kernel(x, c, ctx, c_ctx, ada_w, ada_b, norm1_w, norm2_w, w_in, b_in, conv_w, conv_b, ml_norm_w, mla_qa_norm,
           mla_wq_b, mla_kva_norm, mla_wkv_b, q_norm_w, k_norm_w, ret_norm_w, w_branch, w_out, router_w,
           exp_w1, exp_w3, exp_w2):
    bsz, n_lat, _ = x.shape
    n_ctx = ctx.shape[1]
    depth = ada_w.shape[0]
    ltot = n_lat + n_ctx
    nlt = n_lat // TM
    ltiles = ltot // TM
    nl, ncx = n_lat // CHUNK, n_ctx // CHUNK
    assert n_lat % TM == 0 and n_ctx == TM and bsz + 1 <= 8 and (bsz * ltot) % MOE_TT == 0

    cond8 = jnp.zeros((8, D_MODEL), F32).at[:bsz].set(c).at[bsz].set(c_ctx)
    mods_all = _ada(cond8, ada_w, ada_b)

    rope32 = _rope_tables(n_lat, n_ctx)
    rope_r = jnp.tile(rope32, (1, 1, LANES // ROPE_D))
    ident = jnp.stack([jnp.ones((ltot, 1), F32), jnp.zeros((ltot, 1), F32), jnp.zeros((ltot, 1), F32)])
    rope_k = jnp.concatenate([jnp.broadcast_to(ident, (3, ltot, NOPE_D)), rope32,
                              jnp.broadcast_to(ident, (3, ltot, LANES - QK_D))], axis=2)
    rope_q = jnp.swapaxes(rope_k, 1, 2)
    dmat, xi, kd, cd = _ret_tables()

    gi = jnp.arange(ML_HEADS * ML_DH) // ML_DH
    g8 = (gi[:, None] == jnp.arange(LANES)[None, :]).astype(BF16)
    g8t = g8.T

    cap_l = CAP_FACTOR * n_lat // N_EXPERTS
    cap_c = CAP_FACTOR * n_ctx // N_EXPERTS
    ct = bsz * (cap_l + cap_c)
    assert ct % (16 * FFN_ROW_CHUNKS) == 0 and cap_l % 16 == 0 and cap_c % 16 == 0
    rows = ct + WIN

    xx = jnp.concatenate([x, ctx], axis=1)
    tk = 768 if ltot % 768 == 0 else TM
    tq = min(1024, n_lat)

    for l in range(depth):
        mods = mods_all[l].reshape(8, 1, 6 * D_MODEL)
        w_perm, b_perm = _permute_w_in(w_in[l], b_in[l][None, :])
        slab = _inproj(xx, mods, norm1_w[l][None, :], w_perm, b_perm, nlt)

        hm = _mlstm(slab, conv_w[l], conv_b[l][None, :], nl, ncx)
        hr = _ret(slab, rope_r, dmat, xi, kd, cd, nl, ncx)

        wq = mla_wq_b[l].reshape(Q_LORA, MLA_HEADS, QK_D)
        wqt = jnp.pad(wq, ((0, 0), (0, 0), (0, LANES - QK_D))).reshape(Q_LORA, MLA_HEADS * LANES).T
        wkv = mla_wkv_b[l].reshape(KV_LORA, MLA_HEADS, NOPE_D + MLA_DV)
        wk = jnp.pad(wkv[:, :, :NOPE_D], ((0, 0), (0, 0), (0, LANES - NOPE_D))).reshape(KV_LORA, MLA_HEADS * LANES)
        wvt = wkv[:, :, NOPE_D:].reshape(KV_LORA, MLA_HEADS * MLA_DV).T
        qnw = jnp.pad(q_norm_w[l], (0, LANES - QK_D))[:, None]
        knw = jnp.pad(k_norm_w[l], (0, LANES - QK_D))[None, :]
        qt, kk, vt = _mla(slab, mla_qa_norm[l][None, :], wqt, mla_kva_norm[l][None, :], wk, wvt, qnw, knw,
                          rope_k, rope_q)
        yb_l = _attn(qt, kk, vt, tq, tk, 0, n_lat // tq, 0, ltot // tk)
        yb_c = _attn(qt, kk, vt, TM, TM, nlt, 1, nlt, 1)
        yb = jnp.concatenate([yb_l, yb_c], axis=1)

        rw = jnp.pad(router_w[l], ((0, 0), (0, LANES - N_EXPERTS)))
        xm, h2, aff = _merge(hm, slab, ml_norm_w[l][None, :], yb, hr, ret_norm_w[l][None, :], w_branch[l],
                             w_out[l], xx, mods, norm2_w[l][None, :], rw, g8, g8t, nlt)

        gw_l, pos_l, post_l, off_l = _select(aff, n_lat, cap_l, 0, 0, cap_l)
        gw_c, pos_c, post_c, off_c = _select(aff, n_ctx, cap_c, nlt, bsz * cap_l, cap_c)
        gw = jnp.concatenate([gw_l, gw_c], axis=1).reshape(bsz * ltot, LANES)
        pos = jnp.concatenate([pos_l, pos_c], axis=1).reshape(bsz * ltot, LANES)
        post = jnp.concatenate([post_l, post_c], axis=2)
        post = jnp.swapaxes(post, 0, 1).reshape(N_EXPERTS, 1, bsz * ltot)
        base_l = (jnp.arange(bsz) * cap_l)[:, None, None]
        base_c = (bsz * cap_l + jnp.arange(bsz) * cap_c)[:, None, None]
        off = jnp.concatenate([off_l[:, :, :N_EXPERTS].astype(jnp.int32) // 16 * 16 + base_l,
                               off_c[:, :, :N_EXPERTS].astype(jnp.int32) // 16 * 16 + base_c], axis=1)
        starts = jnp.transpose(off, (2, 0, 1)).reshape(-1).astype(jnp.int32)

        xs = _moe_gather(starts, h2.reshape(bsz * ltot, D_MODEL), post, rows)
        ys = _moe_ffn(xs, exp_w1[l], exp_w3[l], exp_w2[l], ct)
        xx = _combine(starts, xm.reshape(bsz * ltot, D_MODEL), gw, pos, mods, ys, nlt, ltiles)
        xx = xx.reshape(bsz, ltot, D_MODEL)

    return xx[:, :n_lat]
```

```python
import functools

import jax
import jax.numpy as jnp
import numpy as np
from jax import lax
from jax.experimental import pallas as pl
from jax.experimental.pallas import tpu as pltpu

F32 = jnp.float32
BF16 = jnp.bfloat16
HIGHEST = lax.Precision.HIGHEST

D_MODEL = 1024
GRID_W = 64
N_BRANCH = 3
BRANCH_W = 512
ML_HEADS = 8
ML_DH = 64
CONV_K = 5
MLA_HEADS = 8
Q_LORA = 384
KV_LORA = 256
NOPE_D = 64
ROPE_D = 32
QK_D = NOPE_D + ROPE_D
MLA_DV = 64
RET_HEADS = 8
RET_DK = 32
RET_DV = 64
N_EXPERTS = 16
EXPERT_FF = 1024
CAP_FACTOR = 2
CHUNK = 128
ROPE_BASE = 10000.0
EPS = 1e-6
LOG2E = 1.4426950408889634

LANES = 128
TM = 256
VMEM_LIMIT = 56 * 1024 * 1024

OFF_BL, OFF_QK, OFF_V, OFF_OG, OFF_RV, OFF_RG = 0, 3072, 4096, 4608, 5120, 5632
OFF_RQ, OFF_RK, OFF_KV, OFF_QL, OFF_MISC = 6144, 6400, 6656, 6912, 7296
NCOL = 7680
TN_IN = 2560


def _sigmoid(x):
    return 1.0 / (1.0 + jnp.exp(-x))


def _silu(x):
    return x * _sigmoid(x)


def _log_sigmoid(x):
    return jnp.minimum(x, 0.0) - jnp.log1p(jnp.exp(-jnp.abs(x)))


def _dot(a, b, **kw):
    return jnp.dot(a, b, preferred_element_type=F32, **kw)


def _dot_nt(a, b, **kw):
    return lax.dot_general(a, b, (((1,), (1,)), ((), ())), preferred_element_type=F32, **kw)


def _dot_tn(a, b, **kw):
    return lax.dot_general(a, b, (((0,), (0,)), ((), ())), preferred_element_type=F32, **kw)


def _params(sem):
    return pltpu.CompilerParams(dimension_semantics=sem, vmem_limit_bytes=VMEM_LIMIT)


def _ada_kernel(c_ref, w_ref, b_ref, o_ref):
    s = _silu(c_ref[...])
    o_ref[...] = _dot(s, w_ref[...], precision=HIGHEST) + b_ref[...]


def _ada(cond8, ada_w, ada_b):
    depth = ada_w.shape[0]
    tn = 1536
    return pl.pallas_call(
        _ada_kernel,
        grid=(depth, 6 * D_MODEL // tn),
        in_specs=[pl.BlockSpec((8, D_MODEL), lambda l, j: (0, 0)),
                  pl.BlockSpec((None, D_MODEL, tn), lambda l, j: (l, 0, j)),
                  pl.BlockSpec((None, 1, tn), lambda l, j: (l, 0, j))],
        out_specs=pl.BlockSpec((None, 8, tn), lambda l, j: (l, 0, j)),
        out_shape=jax.ShapeDtypeStruct((depth, 8, 6 * D_MODEL), F32),
        compiler_params=_params(("arbitrary", "arbitrary")),
        name="ada",
    )(cond8, ada_w, ada_b.reshape(depth, 1, 6 * D_MODEL))


def _inproj_kernel(x_ref, mod_ref, nw_ref, w_ref, b_ref, o_ref, wb_ref):
    @pl.when((pl.program_id(1) == 0) & (pl.program_id(2) == 0))
    def _():
        wb_ref[...] = w_ref[...].astype(BF16)

    x = x_ref[...]
    hn = x * lax.rsqrt(jnp.mean(x * x, axis=-1, keepdims=True) + EPS) * nw_ref[...]
    sh = mod_ref[:, 0:D_MODEL]
    sc = mod_ref[:, D_MODEL:2 * D_MODEL]
    h = hn * (1.0 + sc) + sh
    o_ref[...] = _dot(h.astype(BF16), wb_ref[...]) + b_ref[...]


def _inproj(x, mods, nw, w, b, nlt):
    bsz, ltot, _ = x.shape

    def mrow(j, bb, i):
        return (jnp.where(i < nlt, bb, bsz), 0, 0)

    return pl.pallas_call(
        _inproj_kernel,
        grid=(NCOL // TN_IN, bsz, ltot // TM),
        in_specs=[pl.BlockSpec((None, TM, D_MODEL), lambda j, bb, i: (bb, i, 0)),
                  pl.BlockSpec((None, 1, 6 * D_MODEL), mrow),
                  pl.BlockSpec((1, D_MODEL), lambda j, bb, i: (0, 0)),
                  pl.BlockSpec((D_MODEL, TN_IN), lambda j, bb, i: (0, j)),
                  pl.BlockSpec((1, TN_IN), lambda j, bb, i: (0, j))],
        out_specs=pl.BlockSpec((None, TM, TN_IN), lambda j, bb, i: (bb, i, j)),
        out_shape=jax.ShapeDtypeStruct((bsz, ltot, NCOL), F32),
        scratch_shapes=[pltpu.VMEM((D_MODEL, TN_IN), BF16)],
        compiler_params=_params(("arbitrary", "arbitrary", "arbitrary")),
        name="inproj",
    )(x, mods, nw, w, b)


def _chunk_of(d, s, nl, ncx):
    fwd = jnp.where(s < ncx, nl + s, s - ncx)
    bwd = jnp.where(s < ncx, nl + ncx - 1 - s, nl - 1 - (s - ncx))
    return jnp.where(d == 0, fwd, bwd)


def _lane_iota(shape):
    return lax.broadcasted_iota(jnp.int32, shape, len(shape) - 1)


def _qkconv_kernel(qk_ref, hp_ref, hn_ref, cw_ref, cb_ref, q_ref, k_ref, xe_ref, *, nlt, ltiles):
    i = pl.program_id(1)
    first = (i == 0) | (i == nlt)
    last = (i == nlt - 1) | (i == ltiles - 1)
    xe_ref[0:8, :] = jnp.where(first, 0.0, hp_ref[...])
    xe_ref[8:8 + TM, :] = qk_ref[...]
    xe_ref[8 + TM:16 + TM, :] = jnp.where(last, 0.0, hn_ref[...])
    hw = ML_HEADS * ML_DH
    acc = jnp.broadcast_to(cb_ref[...], (TM, 2 * hw))
    for j in range(CONV_K):
        acc = acc + xe_ref[8 - CONV_K // 2 + j:8 - CONV_K // 2 + j + TM, :] * cw_ref[j:j + 1, :]
    qk = _silu(acc)
    q_ref[...] = qk[:, :hw].astype(BF16)
    k_ref[...] = (qk[:, hw:] * (ML_DH ** -0.5)).astype(BF16)


def _qkconv(slab, conv_w, conv_b, nlt):
    bsz, ltot, _ = slab.shape
    hw = ML_HEADS * ML_DH
    ltiles = ltot // TM
    nrow8 = ltot // 8
    kern = functools.partial(_qkconv_kernel, nlt=nlt, ltiles=ltiles)
    out = pl.BlockSpec((None, TM, hw), lambda bb, i: (bb, i, 0))
    return pl.pallas_call(
        kern,
        grid=(bsz, ltiles),
        in_specs=[pl.BlockSpec((None, TM, 2 * hw), lambda bb, i: (bb, i, OFF_QK // (2 * hw))),
                  pl.BlockSpec((None, 8, 2 * hw),
                               lambda bb, i: (bb, jnp.maximum(i * (TM // 8) - 1, 0), OFF_QK // (2 * hw))),
                  pl.BlockSpec((None, 8, 2 * hw),
                               lambda bb, i: (bb, jnp.minimum((i + 1) * (TM // 8), nrow8 - 1), OFF_QK // (2 * hw))),
                  pl.BlockSpec((CONV_K, 2 * hw), lambda bb, i: (0, 0)),
                  pl.BlockSpec((1, 2 * hw), lambda bb, i: (0, 0))],
        out_specs=[out, out],
        out_shape=[jax.ShapeDtypeStruct((bsz, ltot, hw), BF16)] * 2,
        scratch_shapes=[pltpu.VMEM((TM + 16, 2 * hw), F32)],
        compiler_params=_params(("arbitrary", "arbitrary")),
        name="qkconv",
    )(slab, slab, slab, conv_w, conv_b)


def _split2(x):
    hi = x.astype(BF16)
    return hi, (x - hi.astype(F32)).astype(BF16)


def _select_dot(x, sel_b):
    hi, lo = _split2(x)
    return _dot(hi, sel_b) + _dot(lo, sel_b)


GATE_LANE0 = ML_HEADS


def _mlstm_kernel(q_ref, k_ref, v_ref, misc_ref, o_ref, st_ref, m_ref):
    d = pl.program_id(1)
    s = pl.program_id(2)

    @pl.when(s == 0)
    def _():
        st_ref[...] = jnp.zeros_like(st_ref)
        m_ref[...] = jnp.zeros_like(m_ref)

    lane = _lane_iota((1, LANES))
    head_lane = (lane >= GATE_LANE0) & (lane < GATE_LANE0 + ML_HEADS)
    g_raw = misc_ref[...]
    g_fg = jnp.where(d == 0, g_raw, pltpu.roll(g_raw, LANES - 2 * ML_HEADS, axis=1))
    g_ig = jnp.where(head_lane, pltpu.roll(g_fg, ML_HEADS, axis=1), 0.0)
    ls = jnp.where(head_lane, _log_sigmoid(g_fg), 0.0)
    ti = lax.broadcasted_iota(jnp.int32, (CHUNK, CHUNK), 0)
    si = lax.broadcasted_iota(jnp.int32, (CHUNK, CHUNK), 1)
    sign = 1 - 2 * d
    causal = (ti - si) * sign >= 0
    tri = jnp.where(causal, 1.0, 0.0)
    tri_t = jnp.where((si - ti) * sign >= 0, 1.0, 0.0)
    b_cols = _dot(tri, ls, precision=HIGHEST)
    b_rows = _dot(ls.T, tri_t, precision=HIGHEST)
    r_rows = g_ig.T - b_rows
    bl_row = jnp.where(d == 0, b_cols[CHUNK - 1:CHUNK, :], b_cols[0:1, :])
    m_row = m_ref[...]

    s8 = _lane_iota((ML_HEADS, CHUNK))
    order = jnp.where(d == 0, s8, CHUNK - 1 - s8)
    cm8 = r_rows[GATE_LANE0:GATE_LANE0 + ML_HEADS, :]
    step = 1
    while step < CHUNK:
        shifted = jnp.where(d == 0, pltpu.roll(cm8, step, axis=1), pltpu.roll(cm8, CHUNK - step, axis=1))
        cm8 = jnp.where(order >= step, jnp.maximum(cm8, shifted), cm8)
        step *= 2
    cm = jnp.concatenate([jnp.zeros((GATE_LANE0, CHUNK), F32), cm8,
                          jnp.zeros((LANES - GATE_LANE0 - ML_HEADS, CHUNK), F32)], axis=0).T
    a_cols = b_cols + m_row
    mt_cols = jnp.maximum(a_cols, b_cols + cm)
    inter_cols = jnp.exp(a_cols - mt_cols)
    emt_cols = jnp.exp(-mt_cols)
    g_cols = bl_row - b_cols + g_ig
    m_new = jnp.maximum(bl_row + m_row, jnp.max(g_cols, axis=0, keepdims=True))
    dp_row = jnp.exp(bl_row + m_row - m_new)
    wg_cols = jnp.exp(g_cols - m_new)
    m_ref[...] = m_new

    lo_half = lane < ML_DH
    gi = lax.broadcasted_iota(jnp.int32, (LANES, 2 * LANES), 0)
    li = lax.broadcasted_iota(jnp.int32, (LANES, 2 * LANES), 1)
    li_in = jnp.where(li >= LANES, li - LANES, li)
    pairs = range(ML_HEADS // 2)
    heads = range(ML_HEADS)
    q_all = q_ref[...]
    k_all = k_ref[...]
    v_all = v_ref[...]
    ones_b = jnp.ones((CHUNK, LANES), BF16)
    qp = [q_all[:, p * LANES:(p + 1) * LANES] for p in pairs]
    kp = [k_all[:, p * LANES:(p + 1) * LANES] for p in pairs]
    va = [jnp.concatenate([v_all[:, p * LANES:(p + 1) * LANES].astype(BF16), ones_b], axis=1) for p in pairs]
    st = [st_ref[p] for p in pairs]
    hmask = [lo_half if h % 2 == 0 else jnp.logical_not(lo_half) for h in heads]

    qs = [_dot(qp[p], st[p].astype(BF16)) for p in pairs]
    qp_f = [x.astype(F32) for x in qp]
    qk = [_dot_nt(jnp.where(hmask[h], qp_f[h // 2], 0.0).astype(BF16), kp[h // 2]) for h in heads]

    c_full, halves = [], []
    dp_cols = jnp.where(ti >= 0, dp_row, 0.0)
    stacked = jnp.concatenate([inter_cols, emt_cols, wg_cols, dp_cols], axis=0)
    for p in pairs:
        sel_full = jnp.where(gi == GATE_LANE0 + 2 * p + jnp.where(li >= LANES, 1, 0), 1.0, 0.0).astype(BF16)
        sel_half = jnp.where(ti == GATE_LANE0 + 2 * p + jnp.where(si >= ML_DH, 1, 0), 1.0, 0.0).astype(BF16)
        c_full.append(_select_dot(b_cols - mt_cols, sel_full))
        halves.append(_select_dot(stacked, sel_half))

    sc, sv = [], []
    for h in heads:
        p, j = h // 2, h % 2
        dlog = c_full[p][:, j * LANES:(j + 1) * LANES] + r_rows[GATE_LANE0 + h:GATE_LANE0 + h + 1, :]
        w = jnp.exp(jnp.where(causal, dlog, -jnp.inf))
        sc.append((qk[h] * w).astype(BF16))
    sv = [_dot(sc[h], va[h // 2]) for h in heads]
    upd = []
    for p in pairs:
        wg_pair = halves[p][2 * CHUNK:3 * CHUNK]
        upd.append(_dot_tn((kp[p].astype(F32) * wg_pair).astype(BF16), va[p]))

    same_head = (gi < ML_DH) == (li_in < ML_DH)
    lane2 = _lane_iota((1, 2 * LANES))
    lo_half2 = jnp.where(lane2 >= LANES, lane2 - LANES, lane2) < ML_DH
    outs = []
    for p in pairs:
        inter_pair = halves[p][0:CHUNK]
        emt_pair = halves[p][CHUNK:2 * CHUNK]
        dp_pair = halves[p][3 * CHUNK:4 * CHUNK]
        sv_pair = jnp.where(lo_half2, sv[2 * p], sv[2 * p + 1])
        num = sv_pair[:, :LANES] + inter_pair * qs[p][:, :LANES]
        den = sv_pair[:, LANES:] + inter_pair * qs[p][:, LANES:]
        outs.append(num / jnp.maximum(jnp.abs(den), emt_pair))
        st_ref[p] = (jnp.concatenate([dp_pair, dp_pair], axis=1) * st[p]
                     + jnp.where(same_head, upd[p], 0.0))
    o_ref[...] = jnp.concatenate(outs, axis=1)


def _mlstm(q, k, slab, nl, ncx):
    bsz, ltot, _ = slab.shape
    hw = ML_HEADS * ML_DH

    def ch(bb, d, s):
        return _chunk_of(d, s, nl, ncx)

    return pl.pallas_call(
        _mlstm_kernel,
        grid=(bsz, 2, nl + ncx),
        in_specs=[
            pl.BlockSpec((None, CHUNK, hw), lambda bb, d, s: (bb, ch(bb, d, s), 0)),
            pl.BlockSpec((None, CHUNK, hw), lambda bb, d, s: (bb, ch(bb, d, s), 0)),
            pl.BlockSpec((None, CHUNK, hw), lambda bb, d, s: (bb, ch(bb, d, s), OFF_V // hw)),
            pl.BlockSpec((None, CHUNK, LANES), lambda bb, d, s: (bb, ch(bb, d, s), OFF_MISC // LANES)),
        ],
        out_specs=pl.BlockSpec((None, None, CHUNK, hw), lambda bb, d, s: (d, bb, ch(bb, d, s), 0)),
        out_shape=jax.ShapeDtypeStruct((2, bsz, ltot, hw), F32),
        scratch_shapes=[pltpu.VMEM((ML_HEADS // 2, LANES, 2 * LANES), F32),
                        pltpu.VMEM((1, LANES), F32)],
        compiler_params=_params(("arbitrary", "arbitrary", "arbitrary")),
        name="mlstm",
    )(q, k, slab, slab)


def _rope_lanes(x, tab_ref):
    return (x * tab_ref[0] + pltpu.roll(x, 8, axis=1) * tab_ref[1]
            + pltpu.roll(x, LANES - 8, axis=1) * tab_ref[2])


def _ret_kernel(q_ref, k_ref, v_ref, rope_ref, dmat_ref, xi_ref, kd_ref, cd_ref, o_ref, st_ref):
    s = pl.program_id(2)

    @pl.when(s == 0)
    def _():
        st_ref[...] = jnp.zeros_like(st_ref)

    v_all = v_ref[...]
    lane = _lane_iota((1, LANES))
    lo_half = lane < RET_DV
    ri = lax.broadcasted_iota(jnp.int32, (LANES, LANES), 0)
    ci = lax.broadcasted_iota(jnp.int32, (LANES, LANES), 1)
    heads = range(RET_HEADS)
    pairs = range(RET_HEADS // 2)
    qg = [_rope_lanes(q_ref[:, g * LANES:(g + 1) * LANES], rope_ref) for g in range(2)]
    kg = [_rope_lanes(k_ref[:, g * LANES:(g + 1) * LANES], rope_ref) * (RET_DK ** -0.5) for g in range(2)]
    qg_b = [x.astype(BF16) for x in qg]
    kg_b = [x.astype(BF16) for x in kg]
    kw_b = [(kg[g] * kd_ref[:, g * LANES:(g + 1) * LANES]).astype(BF16) for g in range(2)]
    vp_b = [v_all[:, p * LANES:(p + 1) * LANES].astype(BF16) for p in pairs]
    st = [st_ref[p] for p in pairs]

    cross = [_dot(qg_b[p // 2], st[p].astype(BF16)) for p in pairs]
    qk = []
    for h in heads:
        q_lo = (h % 4) * RET_DK
        hm = (lane >= q_lo) & (lane < q_lo + RET_DK)
        qk.append(_dot_nt(jnp.where(hm, qg[h // 4], 0.0).astype(BF16), kg_b[h // 4]))
    sv = [_dot((qk[h] * dmat_ref[h]).astype(BF16), vp_b[h // 2]) for h in heads]
    upd = [_dot_tn(kw_b[p // 2], vp_b[p]) for p in pairs]

    outs = []
    for p in pairs:
        a = p % 2
        outs.append(jnp.where(lo_half, sv[2 * p], sv[2 * p + 1]) + xi_ref[:, p * LANES:(p + 1) * LANES] * cross[p])
        r_lo = (ri >= 2 * a * RET_DK) & (ri < (2 * a + 1) * RET_DK)
        r_hi = (ri >= (2 * a + 1) * RET_DK) & (ri < (2 * a + 2) * RET_DK)
        valid = (r_lo & (ci < RET_DV)) | (r_hi & (ci >= RET_DV))
        st_ref[p] = cd_ref[:, p * LANES:(p + 1) * LANES] * st[p] + jnp.where(valid, upd[p], 0.0)
    o_ref[...] = jnp.concatenate(outs, axis=1)


def _ret(slab, rope_r, dmat, xi, kd, cd, nl, ncx):
    bsz, ltot, _ = slab.shape
    qw = RET_HEADS * RET_DK
    vw = RET_HEADS * RET_DV

    def ch(bb, d, s):
        return _chunk_of(d, s, nl, ncx)

    return pl.pallas_call(
        _ret_kernel,
        grid=(bsz, 2, nl + ncx),
        in_specs=[
            pl.BlockSpec((None, CHUNK, qw), lambda bb, d, s: (bb, ch(bb, d, s), OFF_RQ // qw)),
            pl.BlockSpec((None, CHUNK, qw), lambda bb, d, s: (bb, ch(bb, d, s), OFF_RK // qw)),
            pl.BlockSpec((None, CHUNK, vw), lambda bb, d, s: (bb, ch(bb, d, s), OFF_RV // vw)),
            pl.BlockSpec((3, CHUNK, LANES), lambda bb, d, s: (0, ch(bb, d, s), 0)),
            pl.BlockSpec((None, RET_HEADS, CHUNK, CHUNK), lambda bb, d, s: (d, 0, 0, 0)),
            pl.BlockSpec((None, CHUNK, vw), lambda bb, d, s: (d, 0, 0)),
            pl.BlockSpec((None, CHUNK, qw), lambda bb, d, s: (d, 0, 0)),
            pl.BlockSpec((None, 1, vw), lambda bb, d, s: (d, 0, 0)),
        ],
        out_specs=pl.BlockSpec((None, None, CHUNK, vw), lambda bb, d, s: (d, bb, ch(bb, d, s), 0)),
        out_shape=jax.ShapeDtypeStruct((2, bsz, ltot, vw), F32),
        scratch_shapes=[pltpu.VMEM((RET_HEADS // 2, LANES, LANES), F32)],
        compiler_params=_params(("arbitrary", "arbitrary", "arbitrary")),
        name="ret",
    )(slab, slab, slab, rope_r, dmat, xi, kd, cd)


def _ret_tables():
    lg = jnp.log1p(-jnp.exp2(-5.0 - jnp.arange(RET_HEADS, dtype=F32)))
    idx = jnp.arange(CHUNK, dtype=F32)
    diff = idx[:, None] - idx[None, :]
    dm, xis, kds, cds = [], [], [], []
    for lgd, sign in ((lg, 1.0), (lg[::-1], -1.0)):
        dd = diff[None] * sign
        dm.append(jnp.exp(jnp.where(dd >= 0, dd * lgd[:, None, None], -jnp.inf)))
        order = idx if sign > 0 else (CHUNK - 1.0 - idx)
        xi = jnp.exp((order + 1.0)[None] * lgd[:, None])
        kdv = jnp.exp((CHUNK - 1.0 - order)[None] * lgd[:, None])
        cdv = jnp.exp(CHUNK * lgd)
        xis.append(jnp.repeat(xi.T, RET_DV, axis=1))
        kds.append(jnp.repeat(kdv.T, RET_DK, axis=1))
        cds.append(jnp.repeat(cdv, RET_DV)[None, :])
    return jnp.stack(dm), jnp.stack(xis), jnp.stack(kds), jnp.stack(cds)


def _rope_tables(n_lat, n_ctx):
    t = jnp.arange(n_lat)
    row = (t // GRID_W).astype(F32)
    col = (t % GRID_W).astype(F32)
    half = ROPE_D // 4
    inv = ROPE_BASE ** (-jnp.arange(half, dtype=F32) / half)
    ar, ac = row[:, None] * inv, col[:, None] * inv
    z = jnp.zeros_like(ar)
    c32 = jnp.concatenate([jnp.cos(ar), jnp.cos(ar), jnp.cos(ac), jnp.cos(ac)], axis=1)
    s1 = jnp.concatenate([z, jnp.sin(ar), z, jnp.sin(ac)], axis=1)
    s2 = jnp.concatenate([-jnp.sin(ar), z, -jnp.sin(ac), z], axis=1)
    tab = jnp.stack([c32, s1, s2])
    ident = jnp.stack([jnp.ones((n_ctx, ROPE_D), F32), jnp.zeros((n_ctx, ROPE_D), F32),
                       jnp.zeros((n_ctx, ROPE_D), F32)])
    return jnp.concatenate([tab, ident], axis=1)


def _mla_kernel(ql_ref, kv_ref, misc_ref, qan_ref, wqt_ref, kvn_ref, wk_ref, wvt_ref, qnw_ref, knw_ref,
                ropek_ref, ropeq_ref, qt_ref, k_ref, vt_ref):
    ql = ql_ref[...]
    qn = ql * lax.rsqrt(jnp.mean(ql * ql, axis=-1, keepdims=True) + EPS) * qan_ref[...]
    qt = _dot_nt(wqt_ref[...].astype(BF16), qn.astype(BF16))
    for h in range(MLA_HEADS):
        xh = qt[h * LANES:(h + 1) * LANES, :]
        ss = jnp.sum(xh * xh, axis=0, keepdims=True)
        xh = xh * lax.rsqrt(ss * (1.0 / QK_D) + EPS) * qnw_ref[...]
        xh = (xh * ropeq_ref[0] + pltpu.roll(xh, 8, axis=0) * ropeq_ref[1]
              + pltpu.roll(xh, LANES - 8, axis=0) * ropeq_ref[2])
        qt_ref[h * LANES:(h + 1) * LANES, :] = (xh * (QK_D ** -0.5 * LOG2E)).astype(BF16)

    kv = kv_ref[...]
    kvn = (kv * lax.rsqrt(jnp.mean(kv * kv, axis=-1, keepdims=True) + EPS) * kvn_ref[...]).astype(BF16)
    kk = _dot(kvn, wk_ref[...].astype(BF16))
    lane = _lane_iota((1, LANES))
    rope_part = (lane >= NOPE_D) & (lane < QK_D)
    kr = jnp.where(rope_part, pltpu.roll(misc_ref[...], NOPE_D - 4 * ML_HEADS, axis=1), 0.0)
    for h in range(MLA_HEADS):
        xh = kk[:, h * LANES:(h + 1) * LANES] + kr
        ss = jnp.sum(xh * xh, axis=1, keepdims=True)
        xh = xh * lax.rsqrt(ss * (1.0 / QK_D) + EPS) * knw_ref[...]
        k_ref[:, h * LANES:(h + 1) * LANES] = _rope_lanes(xh, ropek_ref).astype(BF16)
    vt_ref[...] = _dot_nt(wvt_ref[...].astype(BF16), kvn).astype(BF16)


def _mla(slab, qan, wqt, kvn, wk, wvt, qnw, knw, rope_k, rope_q):
    bsz, ltot, _ = slab.shape
    hq = MLA_HEADS * LANES
    hv = MLA_HEADS * MLA_DV
    full = lambda shape: pl.BlockSpec(shape, lambda bb, i: (0,) * len(shape))
    return pl.pallas_call(
        _mla_kernel,
        grid=(bsz, ltot // TM),
        in_specs=[pl.BlockSpec((None, TM, Q_LORA), lambda bb, i: (bb, i, OFF_QL // Q_LORA)),
                  pl.BlockSpec((None, TM, KV_LORA), lambda bb, i: (bb, i, OFF_KV // KV_LORA)),
                  pl.BlockSpec((None, TM, LANES), lambda bb, i: (bb, i, OFF_MISC // LANES)),
                  full((1, Q_LORA)), full((hq, Q_LORA)), full((1, KV_LORA)), full((KV_LORA, hq)),
                  full((hv, KV_LORA)), full((LANES, 1)), full((1, LANES)),
                  pl.BlockSpec((3, TM, LANES), lambda bb, i: (0, i, 0)),
                  pl.BlockSpec((3, LANES, TM), lambda bb, i: (0, 0, i))],
        out_specs=[pl.BlockSpec((None, hq, TM), lambda bb, i: (bb, 0, i)),
                   pl.BlockSpec((None, TM, hq), lambda bb, i: (bb, i, 0)),
                   pl.BlockSpec((None, None, hv, TM), lambda bb, i: (bb, i, 0, 0))],
        out_shape=[jax.ShapeDtypeStruct((bsz, hq, ltot), BF16),
                   jax.ShapeDtypeStruct((bsz, ltot, hq), BF16),
                   jax.ShapeDtypeStruct((bsz, ltot // TM, hv, TM), BF16)],
        compiler_params=_params(("arbitrary", "arbitrary")),
        name="mla",
    )(slab, slab, slab, qan, wqt, kvn, wk, wvt, qnw, knw, rope_k, rope_q)


ATT_QC = 256
ATT_ONES = 16
ATT_AHEAD = 3


def _attn_kernel(qt_ref, k_ref, vt_ref, o_ref, acc_ref, m_ref, l_ref, *, r, nkb):
    tq = qt_ref.shape[1]
    tkb = r * TM
    acc_ref[...] = jnp.zeros_like(acc_ref)
    m_ref[...] = jnp.full_like(m_ref, -jnp.inf)
    l_ref[...] = jnp.zeros_like(l_ref)
    ones = jnp.ones((ATT_ONES, TM), BF16)

    units = [(j, c) for j in range(2) for c in range(tq // ATT_QC)]

    def key_block(kb, carry):
        r0 = pl.multiple_of(kb * tkb, tkb)

        def scores(u):
            j, c = units[u]
            return _dot(k_ref[pl.ds(r0, tkb), j * LANES:(j + 1) * LANES],
                        qt_ref[j * LANES:(j + 1) * LANES, c * ATT_QC:(c + 1) * ATT_QC])

        ahead = min(ATT_AHEAD, len(units))
        pending = [scores(u) for u in range(ahead)]
        for u, (j, c) in enumerate(units):
            sc = pending.pop(0)
            if u + ahead < len(units):
                pending.append(scores(u + ahead))
            cs = slice(c * ATT_QC, (c + 1) * ATT_QC)
            m_old = m_ref[j:j + 1, cs]
            m_new = jnp.maximum(m_old, jnp.max(sc, axis=0, keepdims=True))
            p = jnp.exp2(sc - m_new).astype(BF16)
            alpha = jnp.exp2(m_old - m_new)
            pv = None
            for t in range(r):
                vt = jnp.concatenate([vt_ref[kb * r + t, j * MLA_DV:(j + 1) * MLA_DV, :], ones], axis=0)
                part = _dot(vt, p[t * TM:(t + 1) * TM])
                pv = part if pv is None else pv + part
            acc_ref[j * MLA_DV:(j + 1) * MLA_DV, cs] = (
                alpha * acc_ref[j * MLA_DV:(j + 1) * MLA_DV, cs] + pv[0:MLA_DV])
            l_ref[j:j + 1, cs] = alpha * l_ref[j:j + 1, cs] + pv[MLA_DV:MLA_DV + 1]
            m_ref[j:j + 1, cs] = m_new
        return carry

    lax.fori_loop(0, nkb, key_block, 0)
    head_rows = lax.broadcasted_iota(jnp.int32, acc_ref.shape, 0)
    inv = jnp.where(head_rows < MLA_DV, 1.0 / l_ref[0:1, :], 1.0 / l_ref[1:2, :])
    o_ref[...] = (acc_ref[...] * inv).T


def _attn(qt, k, vt, tq, q_off, nq, key_tile0, r, nkb):
    bsz = qt.shape[0]
    hv = vt.shape[2]
    nkt = r * nkb
    assert key_tile0 % nkt == 0
    kern = functools.partial(_attn_kernel, r=r, nkb=nkb)
    return pl.pallas_call(
        kern,
        grid=(bsz, MLA_HEADS // 2, nq),
        in_specs=[pl.BlockSpec((None, 2 * LANES, tq), lambda bb, hp, qi: (bb, hp, qi + q_off)),
                  pl.BlockSpec((None, nkt * TM, 2 * LANES), lambda bb, hp, qi: (bb, key_tile0 // nkt, hp)),
                  pl.BlockSpec((None, nkt, 2 * MLA_DV, TM), lambda bb, hp, qi: (bb, key_tile0 // nkt, hp, 0))],
        out_specs=pl.BlockSpec((None, tq, 2 * MLA_DV), lambda bb, hp, qi: (bb, qi, hp)),
        out_shape=jax.ShapeDtypeStruct((bsz, nq * tq, hv), F32),
        scratch_shapes=[pltpu.VMEM((2 * MLA_DV, tq), F32), pltpu.VMEM((8, tq), F32), pltpu.VMEM((8, tq), F32)],
        compiler_params=_params(("arbitrary", "arbitrary", "arbitrary")),
        name="attn",
    )(qt, k, vt)


def _split_dot(x, w_b):
    hi = x.astype(BF16)
    lo = (x - hi.astype(F32)).astype(BF16)
    return _dot(hi, w_b) + _dot(lo, w_b)


def _head_ln(x, g_ref, gt_ref, width):
    mu = _split_dot(_split_dot(x, g_ref[...]) * (1.0 / width), gt_ref[...])
    xc = x - mu
    var = _split_dot(_split_dot(xc * xc, g_ref[...]) * (1.0 / width), gt_ref[...])
    return xc * lax.rsqrt(var + EPS)


def _merge_kernel(hm_ref, og_ref, mlw_ref, yb_ref, hr_ref, rg_ref, rnw_ref, bl_ref, wbr_ref, wout_ref,
                  x_ref, mod_ref, n2w_ref, rw_ref, g_ref, gt_ref,
                  xo_ref, h2_ref, aff_ref, wbr_b, wout_b):
    @pl.when((pl.program_id(0) == 0) & (pl.program_id(1) == 0))
    def _():
        wbr_b[...] = wbr_ref[...].astype(BF16)
        wout_b[...] = wout_ref[...].astype(BF16)

    ya = _sigmoid(og_ref[...]) * (_head_ln(hm_ref[0] + hm_ref[1], g_ref, gt_ref, ML_DH) * mlw_ref[...])
    yc = _silu(rg_ref[...]) * (_head_ln(hr_ref[0] + hr_ref[1], g_ref, gt_ref, RET_DV) * rnw_ref[...])
    ys = (ya, yb_ref[...], yc)
    merged = None
    for n in range(N_BRANCH):
        br = _dot(ys[n].astype(BF16), wbr_b[n])
        term = _sigmoid(bl_ref[:, n * D_MODEL:(n + 1) * D_MODEL]) * br
        merged = term if merged is None else merged + term
    y = _dot(merged.astype(BF16), wout_b[...])
    g1 = mod_ref[:, 2 * D_MODEL:3 * D_MODEL]
    x = x_ref[...] + g1 * y
    xo_ref[...] = x
    hn = x * lax.rsqrt(jnp.mean(x * x, axis=-1, keepdims=True) + EPS) * n2w_ref[...]
    h2 = hn * (1.0 + mod_ref[:, 4 * D_MODEL:5 * D_MODEL]) + mod_ref[:, 3 * D_MODEL:4 * D_MODEL]
    h2_ref[...] = h2.astype(BF16)
    logits = _dot(h2, rw_ref[...], precision=HIGHEST)
    valid = _lane_iota((1, LANES)) < N_EXPERTS
    logits = jnp.where(valid, logits, -jnp.inf)
    e = jnp.exp(logits - jnp.max(logits, axis=-1, keepdims=True))
    aff_ref[...] = e / jnp.sum(e, axis=-1, keepdims=True)


def _merge(hm, slab, mlw, yb, hr, rnw, wbr, wout, x, mods, n2w, rw, g8, g8t, nlt):
    bsz, ltot, _ = x.shape
    hw = ML_HEADS * ML_DH

    def mrow(bb, i):
        return (jnp.where(i < nlt, bb, bsz), 0, 0)

    full = lambda shape: pl.BlockSpec(shape, lambda bb, i: (0,) * len(shape))
    row = lambda width, off: pl.BlockSpec((None, TM, width), lambda bb, i: (bb, i, off // width))
    return pl.pallas_call(
        _merge_kernel,
        grid=(bsz, ltot // TM),
        in_specs=[pl.BlockSpec((2, None, TM, hw), lambda bb, i: (0, bb, i, 0)),
                  row(hw, OFF_OG), full((1, hw)),
                  row(hw, 0),
                  pl.BlockSpec((2, None, TM, hw), lambda bb, i: (0, bb, i, 0)),
                  row(hw, OFF_RG), full((1, hw)),
                  row(N_BRANCH * D_MODEL, OFF_BL),
                  full((N_BRANCH, BRANCH_W, D_MODEL)), full((D_MODEL, D_MODEL)),
                  row(D_MODEL, 0),
                  pl.BlockSpec((None, 1, 6 * D_MODEL), mrow),
                  full((1, D_MODEL)), full((D_MODEL, LANES)), full((hw, LANES)), full((LANES, hw))],
        out_specs=[row(D_MODEL, 0), row(D_MODEL, 0), row(LANES, 0)],
        out_shape=[jax.ShapeDtypeStruct((bsz, ltot, D_MODEL), F32),
                   jax.ShapeDtypeStruct((bsz, ltot, D_MODEL), BF16),
                   jax.ShapeDtypeStruct((bsz, ltot, LANES), F32)],
        scratch_shapes=[pltpu.VMEM((N_BRANCH, BRANCH_W, D_MODEL), BF16), pltpu.VMEM((D_MODEL, D_MODEL), BF16)],
        compiler_params=_params(("arbitrary", "arbitrary")),
        name="merge",
    )(hm, slab, mlw, yb, hr, slab, rnw, slab, wbr, wout, x, mods, n2w, rw, g8, g8t)


def _select_kernel(aff_ref, pos_ref, post_ref, gwt_ref, off_ref, *, n, cap, base0, base_step):
    b = pl.program_id(0)
    base = (base0 + b * base_step).astype(F32)
    bits = lax.bitcast_convert_type(aff_ref[...], jnp.int32)
    capf = jnp.float32(cap)

    def search(it, cur):
        cand = cur | jnp.left_shift(jnp.int32(1), 30 - it)
        cnt = jnp.sum(jnp.where(bits >= cand, 1.0, 0.0), axis=0, keepdims=True)
        return jnp.where(cnt >= capf, cand, cur)

    thr = lax.fori_loop(0, 31, search, jnp.zeros((1, LANES), jnp.int32))
    n_gt = jnp.sum(jnp.where(bits > thr, 1.0, 0.0), axis=0, keepdims=True)
    need = capf - n_gt

    ri = lax.broadcasted_iota(jnp.int32, (TM, TM), 0)
    ci = lax.broadcasted_iota(jnp.int32, (TM, TM), 1)
    strict = (ci < ri).astype(BF16)

    def tile(i, carry):
        c_eq, c_sel = carry
        r0 = pl.multiple_of(i * TM, TM)
        a = aff_ref[pl.ds(r0, TM), :]
        bt = lax.bitcast_convert_type(a, jnp.int32)
        eq = bt == thr
        rank = _dot(strict, eq.astype(BF16)) + c_eq
        sel = (bt > thr) | (eq & (rank < need))
        self_ = jnp.where(sel, 1.0, 0.0)
        pos = _dot(strict, self_.astype(BF16)) + c_sel
        posv = jnp.where(sel, pos + base, -1.0)
        pos_ref[pl.ds(r0, TM), :] = posv
        post_ref[:, pl.ds(r0, TM)] = posv.T[0:N_EXPERTS, :]
        gwt_ref[:, pl.ds(r0, TM)] = jnp.where(sel, a, 0.0).T[0:N_EXPERTS, :]
        off_ref[pl.ds(2 * i, 1), :] = c_sel
        off_ref[pl.ds(2 * i + 1, 1), :] = c_sel + jnp.sum(self_[0:TM // 2], axis=0, keepdims=True)
        return (c_eq + jnp.sum(jnp.where(eq, 1.0, 0.0), axis=0, keepdims=True),
                c_sel + jnp.sum(self_, axis=0, keepdims=True))

    zero = jnp.zeros((1, LANES), F32)
    lax.fori_loop(0, n // TM, tile, (zero, zero))


def _select(aff, n, cap, tile_off, base0, base_step):
    bsz = aff.shape[0]
    nt = n // TM
    kern = functools.partial(_select_kernel, n=n, cap=cap, base0=base0, base_step=base_step)
    return pl.pallas_call(
        kern,
        grid=(bsz,),
        in_specs=[pl.BlockSpec((None, n, LANES), lambda bb: (bb, tile_off, 0))],
        out_specs=[pl.BlockSpec((None, n, LANES), lambda bb: (bb, 0, 0)),
                   pl.BlockSpec((None, N_EXPERTS, n), lambda bb: (bb, 0, 0)),
                   pl.BlockSpec((None, N_EXPERTS, n), lambda bb: (bb, 0, 0)),
                   pl.BlockSpec((None, 2 * nt, LANES), lambda bb: (bb, 0, 0))],
        out_shape=[jax.ShapeDtypeStruct((bsz, n, LANES), F32),
                   jax.ShapeDtypeStruct((bsz, N_EXPERTS, n), F32),
                   jax.ShapeDtypeStruct((bsz, N_EXPERTS, n), F32),
                   jax.ShapeDtypeStruct((bsz, 2 * nt, LANES), F32)],
        compiler_params=_params(("arbitrary",)),
        name="select",
    )(aff)


SLOT_ALIGN = 16
WIN = TM + SLOT_ALIGN
TC = TM // 2
WINC = TC + SLOT_ALIGN
GATHER_MAX_TILES = 11
FFN_ROW_CHUNKS = 4
FFN_TF = 256


def _moe_gather_kernel(st_ref, h_ref, pt_ref, gt_ref, xs_ref, gs_ref, *, nt256, tiles):
    e = pl.program_id(0)
    i = pl.program_id(1)

    @pl.when(i == 0)
    def _():
        xs_ref[...] = jnp.zeros_like(xs_ref)
        gs_ref[...] = jnp.zeros_like(gs_ref)

    slot = lax.broadcasted_iota(jnp.int32, (WIN, TM), 0)
    for u in range(tiles):
        st = pl.multiple_of(st_ref[e * nt256 + i * tiles + u], SLOT_ALIGN)
        st2 = pl.multiple_of(st + SLOT_ALIGN, SLOT_ALIGN)
        match = (slot + st).astype(F32) == pt_ref[:, u * TM:(u + 1) * TM]
        onehot = jnp.where(match, 1.0, 0.0).astype(BF16)
        got = _dot(onehot, h_ref[u * TM:(u + 1) * TM, :])
        head = xs_ref[pl.ds(st, SLOT_ALIGN), :].astype(F32)
        xs_ref[pl.ds(st, SLOT_ALIGN), :] = (head + got[0:SLOT_ALIGN]).astype(BF16)
        xs_ref[pl.ds(st2, TM), :] = got[SLOT_ALIGN:WIN].astype(BF16)
        gate = jnp.sum(jnp.where(match, gt_ref[:, u * TM:(u + 1) * TM], 0.0), axis=1, keepdims=True)
        gate = jnp.broadcast_to(gate, (WIN, LANES))
        gs_ref[pl.ds(st, SLOT_ALIGN), :] = gs_ref[pl.ds(st, SLOT_ALIGN), :] + gate[0:SLOT_ALIGN]
        gs_ref[pl.ds(st2, TM), :] = gate[SLOT_ALIGN:WIN]


def _moe_gather(starts, h2, post, gwt, rows, tiles):
    ttot = h2.shape[0]
    tt = tiles * TM
    kern = functools.partial(_moe_gather_kernel, nt256=ttot // TM, tiles=tiles)
    return pl.pallas_call(
        kern,
        grid_spec=pltpu.PrefetchScalarGridSpec(
            num_scalar_prefetch=1,
            grid=(N_EXPERTS, ttot // tt),
            in_specs=[pl.BlockSpec((tt, D_MODEL), lambda e, i, st: (i, 0)),
                      pl.BlockSpec((None, 1, tt), lambda e, i, st: (e, 0, i)),
                      pl.BlockSpec((None, 1, tt), lambda e, i, st: (e, 0, i))],
            out_specs=[pl.BlockSpec((None, rows, D_MODEL), lambda e, i, st: (e, 0, 0)),
                       pl.BlockSpec((None, rows, LANES), lambda e, i, st: (e, 0, 0))]),
        out_shape=[jax.ShapeDtypeStruct((N_EXPERTS, rows, D_MODEL), BF16),
                   jax.ShapeDtypeStruct((N_EXPERTS, rows, LANES), F32)],
        compiler_params=_params(("arbitrary", "arbitrary")),
        name="moe_gather",
    )(starts, h2, post, gwt)


def _moe_ffn_kernel(xs_ref, gs_ref, w1_ref, w3_ref, w2_ref, ys_ref, acc_ref, *, ct, rows):
    f = pl.program_id(1)
    w1b = w1_ref[...].astype(BF16)
    w3b = w3_ref[...].astype(BF16)
    w2b = w2_ref[...].astype(BF16)
    rc = ct // FFN_ROW_CHUNKS
    for r in range(FFN_ROW_CHUNKS):
        xb = xs_ref[r * rc:(r + 1) * rc, :]
        hid = _silu(_dot(xb, w1b)) * _dot(xb, w3b)
        part = _dot(hid.astype(BF16), w2b)

        @pl.when(f == 0)
        def _():
            acc_ref[r * rc:(r + 1) * rc, :] = part

        @pl.when(f > 0)
        def _():
            acc_ref[r * rc:(r + 1) * rc, :] += part

    @pl.when(f == pl.num_programs(1) - 1)
    def _():
        gate = gs_ref[...]
        for c in range(D_MODEL // LANES):
            cs = slice(c * LANES, (c + 1) * LANES)
            ys_ref[0:ct, cs] = (acc_ref[:, cs] * gate).astype(BF16)
        ys_ref[ct:rows, :] = jnp.zeros((rows - ct, D_MODEL), BF16)


def _moe_ffn(xs, gs, w1, w3, w2, ct):
    rows = xs.shape[1]
    kern = functools.partial(_moe_ffn_kernel, ct=ct, rows=rows)
    return pl.pallas_call(
        kern,
        grid=(N_EXPERTS, EXPERT_FF // FFN_TF),
        in_specs=[pl.BlockSpec((None, ct, D_MODEL), lambda e, f: (e, 0, 0)),
                  pl.BlockSpec((None, ct, LANES), lambda e, f: (e, 0, 0)),
                  pl.BlockSpec((None, D_MODEL, FFN_TF), lambda e, f: (e, 0, f)),
                  pl.BlockSpec((None, D_MODEL, FFN_TF), lambda e, f: (e, 0, f)),
                  pl.BlockSpec((None, FFN_TF, D_MODEL), lambda e, f: (e, f, 0))],
        out_specs=pl.BlockSpec((None, rows, D_MODEL), lambda e, f: (e, 0, 0)),
        out_shape=jax.ShapeDtypeStruct((N_EXPERTS, rows, D_MODEL), BF16),
        scratch_shapes=[pltpu.VMEM((ct, D_MODEL), F32)],
        compiler_params=_params(("arbitrary", "arbitrary")),
        name="moe_ffn",
    )(xs, gs, w1, w3, w2)


def _combine_kernel(st_ref, x_ref, pos_ref, mod_ref, *rest):
    nsub = TM // TC
    ys_refs = rest[:N_EXPERTS * nsub]
    o_ref = rest[N_EXPERTS * nsub]
    i = pl.program_id(0)
    ntc = pl.num_programs(0) * nsub
    slot = lax.broadcasted_iota(jnp.int32, (TC, WINC), 1)
    g2 = mod_ref[:, 5 * D_MODEL:6 * D_MODEL]
    for u in range(nsub):
        rs = slice(u * TC, (u + 1) * TC)
        pos = pos_ref[rs, :]
        acc = None
        for e in range(N_EXPERTS):
            st = st_ref[e * ntc + i * nsub + u]
            onehot = jnp.where((slot + st).astype(F32) == pos[:, e:e + 1], 1.0, 0.0).astype(BF16)
            part = _dot(onehot, ys_refs[e * nsub + u][0])
            acc = part if acc is None else acc + part
        o_ref[rs, :] = x_ref[rs, :] + g2 * acc


def _combine(starts, x, pos, mods_rows, ys, nlt, ltiles):
    ttot = x.shape[0]
    nt = ttot // TM
    nsub = TM // TC
    bsz = ttot // (ltiles * TM)

    def mrow(i, st):
        return (jnp.where(i % ltiles < nlt, i // ltiles, bsz), 0, 0)

    def ys_spec(e, u):
        return pl.BlockSpec((pl.Element(1), pl.Element(WINC), pl.Element(D_MODEL)),
                            lambda i, st: (e, pl.multiple_of(st[e * nt * nsub + i * nsub + u], SLOT_ALIGN), 0))

    return pl.pallas_call(
        _combine_kernel,
        grid_spec=pltpu.PrefetchScalarGridSpec(
            num_scalar_prefetch=1,
            grid=(nt,),
            in_specs=[pl.BlockSpec((TM, D_MODEL), lambda i, st: (i, 0)),
                      pl.BlockSpec((TM, LANES), lambda i, st: (i, 0)),
                      pl.BlockSpec((None, 1, 6 * D_MODEL), mrow)]
                     + [ys_spec(e, u) for e in range(N_EXPERTS) for u in range(nsub)],
            out_specs=pl.BlockSpec((TM, D_MODEL), lambda i, st: (i, 0))),
        out_shape=jax.ShapeDtypeStruct((ttot, D_MODEL), F32),
        compiler_params=_params(("arbitrary",)),
        name="combine",
    )(starts, x, pos, mods_rows, *([ys] * (N_EXPERTS * nsub)))


def _permute_w_in(w, b):
    widths = (2 * ML_HEADS * ML_DH, ML_HEADS * ML_DH, ML_HEADS * ML_DH, 4 * ML_HEADS, Q_LORA, KV_LORA, ROPE_D,
              RET_HEADS * RET_DK, RET_HEADS * RET_DK, RET_HEADS * RET_DV, RET_HEADS * RET_DV, N_BRANCH * D_MODEL)
    offs = np.concatenate([[0], np.cumsum(widths)])
    seg = lambda a, k: a[..., offs[k]:offs[k + 1]]
    order = (11, 0, 1, 2, 9, 10, 7, 8, 5, 4, 3, 6)

    def build(a):
        parts = [seg(a, k) for k in order]
        used = sum(p.shape[-1] for p in parts)
        parts.append(jnp.zeros(a.shape[:-1] + (NCOL - used,), a.dtype))
        return jnp.concatenate(parts, axis=-1)

    return build(w), build(b)


def kernel(x, c, ctx, c_ctx, ada_w, ada_b, norm1_w, norm2_w, w_in, b_in, conv_w, conv_b, ml_norm_w, mla_qa_norm,
           mla_wq_b, mla_kva_norm, mla_wkv_b, q_norm_w, k_norm_w, ret_norm_w, w_branch, w_out, router_w,
           exp_w1, exp_w3, exp_w2):
    bsz, n_lat, _ = x.shape
    n_ctx = ctx.shape[1]
    depth = ada_w.shape[0]
    ltot = n_lat + n_ctx
    nlt = n_lat // TM
    ltiles = ltot // TM
    nl, ncx = n_lat // CHUNK, n_ctx // CHUNK
    assert n_lat % TM == 0 and n_ctx == TM and bsz + 1 <= 8
    gather_tiles = max(t for t in range(1, GATHER_MAX_TILES + 1) if (bsz * ltiles) % t == 0)

    cond8 = jnp.zeros((8, D_MODEL), F32).at[:bsz].set(c).at[bsz].set(c_ctx)
    mods_all = _ada(cond8, ada_w, ada_b)

    rope32 = _rope_tables(n_lat, n_ctx)
    rope_r = jnp.tile(rope32, (1, 1, LANES // ROPE_D))
    ident = jnp.stack([jnp.ones((ltot, 1), F32), jnp.zeros((ltot, 1), F32), jnp.zeros((ltot, 1), F32)])
    rope_k = jnp.concatenate([jnp.broadcast_to(ident, (3, ltot, NOPE_D)), rope32,
                              jnp.broadcast_to(ident, (3, ltot, LANES - QK_D))], axis=2)
    rope_q = jnp.swapaxes(rope_k, 1, 2)
    dmat, xi, kd, cd = _ret_tables()

    gi = jnp.arange(ML_HEADS * ML_DH) // ML_DH
    g8 = (gi[:, None] == jnp.arange(LANES)[None, :]).astype(BF16)
    g8t = g8.T

    cap_l = CAP_FACTOR * n_lat // N_EXPERTS
    cap_c = CAP_FACTOR * n_ctx // N_EXPERTS
    ct = bsz * (cap_l + cap_c)
    assert ct % (16 * FFN_ROW_CHUNKS) == 0 and cap_l % 16 == 0 and cap_c % 16 == 0
    rows = ct + WIN

    xx = jnp.concatenate([x, ctx], axis=1)
    att_r = 3 if ltiles % 3 == 0 else 1
    tq = min(1024, n_lat)

    for l in range(depth):
        mods = mods_all[l].reshape(8, 1, 6 * D_MODEL)
        w_perm, b_perm = _permute_w_in(w_in[l], b_in[l][None, :])
        slab = _inproj(xx, mods, norm1_w[l][None, :], w_perm, b_perm, nlt)

        mq, mk = _qkconv(slab, conv_w[l], conv_b[l][None, :], nlt)
        hm = _mlstm(mq, mk, slab, nl, ncx)
        hr = _ret(slab, rope_r, dmat, xi, kd, cd, nl, ncx)

        wq = mla_wq_b[l].reshape(Q_LORA, MLA_HEADS, QK_D)
        wqt = jnp.pad(wq, ((0, 0), (0, 0), (0, LANES - QK_D))).reshape(Q_LORA, MLA_HEADS * LANES).T
        wkv = mla_wkv_b[l].reshape(KV_LORA, MLA_HEADS, NOPE_D + MLA_DV)
        wk = jnp.pad(wkv[:, :, :NOPE_D], ((0, 0), (0, 0), (0, LANES - NOPE_D))).reshape(KV_LORA, MLA_HEADS * LANES)
        wvt = wkv[:, :, NOPE_D:].reshape(KV_LORA, MLA_HEADS * MLA_DV).T
        qnw = jnp.pad(q_norm_w[l], (0, LANES - QK_D))[:, None]
        knw = jnp.pad(k_norm_w[l], (0, LANES - QK_D))[None, :]
        qt, kk, vt = _mla(slab, mla_qa_norm[l][None, :], wqt, mla_kva_norm[l][None, :], wk, wvt, qnw, knw,
                          rope_k, rope_q)
        yb_l = _attn(qt, kk, vt, tq, 0, n_lat // tq, 0, att_r, ltiles // att_r)
        yb_c = _attn(qt, kk, vt, TM, nlt, 1, nlt, 1, 1)
        yb = jnp.concatenate([yb_l, yb_c], axis=1)

        rw = jnp.pad(router_w[l], ((0, 0), (0, LANES - N_EXPERTS)))
        xm, h2, aff = _merge(hm, slab, ml_norm_w[l][None, :], yb, hr, ret_norm_w[l][None, :], w_branch[l],
                             w_out[l], xx, mods, norm2_w[l][None, :], rw, g8, g8t, nlt)

        pos_l, post_l, gwt_l, off_l = _select(aff, n_lat, cap_l, 0, 0, cap_l + cap_c)
        pos_c, post_c, gwt_c, off_c = _select(aff, n_ctx, cap_c, nlt, cap_l, cap_l + cap_c)
        pos = jnp.concatenate([pos_l, pos_c], axis=1).reshape(bsz * ltot, LANES)
        expert_major = lambda a_l, a_c: jnp.swapaxes(jnp.concatenate([a_l, a_c], axis=2), 0, 1).reshape(
            N_EXPERTS, 1, bsz * ltot)
        post = expert_major(post_l, post_c)
        gwt = expert_major(gwt_l, gwt_c)
        base_l = (jnp.arange(bsz) * (cap_l + cap_c))[:, None, None]
        base_c = base_l + cap_l
        off = jnp.concatenate(
            [off_l[:, :, :N_EXPERTS].astype(jnp.int32) // SLOT_ALIGN * SLOT_ALIGN + base_l,
             off_c[:, :, :N_EXPERTS].astype(jnp.int32) // SLOT_ALIGN * SLOT_ALIGN + base_c], axis=1)
        starts_c = jnp.transpose(off, (2, 0, 1)).reshape(-1)
        starts_g = jnp.transpose(off[:, ::2], (2, 0, 1)).reshape(-1)

        xs, gs = _moe_gather(starts_g, h2.reshape(bsz * ltot, D_MODEL), post, gwt, rows, gather_tiles)
        ys = _moe_ffn(xs, gs, exp_w1[l], exp_w3[l], exp_w2[l], ct)
        xx = _combine(starts_c, xm.reshape(bsz * ltot, D_MODEL), pos, mods, ys, nlt, ltiles)
        xx = xx.reshape(bsz, ltot, D_MODEL)

    return xx[:, :n_lat]
```

```python
import functools

import jax
import jax.numpy as jnp
import numpy as np
from jax import lax
from jax.experimental import pallas as pl
from jax.experimental.pallas import tpu as pltpu

F32 = jnp.float32
BF16 = jnp.bfloat16
HIGHEST = lax.Precision.HIGHEST

D_MODEL = 1024
GRID_W = 64
N_BRANCH = 3
BRANCH_W = 512
ML_HEADS = 8
ML_DH = 64
CONV_K = 5
MLA_HEADS = 8
Q_LORA = 384
KV_LORA = 256
NOPE_D = 64
ROPE_D = 32
QK_D = NOPE_D + ROPE_D
MLA_DV = 64
RET_HEADS = 8
RET_DK = 32
RET_DV = 64
N_EXPERTS = 16
EXPERT_FF = 1024
CAP_FACTOR = 2
CHUNK = 128
ROPE_BASE = 10000.0
EPS = 1e-6
LOG2E = 1.4426950408889634

LANES = 128
TM = 256
ROW_SUB = 128
VMEM_LIMIT = 56 * 1024 * 1024

OFF_BL, OFF_QK, OFF_V, OFF_OG, OFF_RV, OFF_RG = 0, 3072, 4096, 4608, 5120, 5632
OFF_RQ, OFF_RK, OFF_KV, OFF_QL, OFF_MISC = 6144, 6400, 6656, 6912, 7296
NCOL = 7680
TN_IN = 2560


def _sigmoid(x):
    return 1.0 / (1.0 + jnp.exp(-x))


def _silu(x):
    return x * _sigmoid(x)


def _log_sigmoid(x):
    return jnp.minimum(x, 0.0) - jnp.log1p(jnp.exp(-jnp.abs(x)))


def _dot(a, b, **kw):
    return jnp.dot(a, b, preferred_element_type=F32, **kw)


def _dot_nt(a, b, **kw):
    return lax.dot_general(a, b, (((1,), (1,)), ((), ())), preferred_element_type=F32, **kw)


def _dot_tn(a, b, **kw):
    return lax.dot_general(a, b, (((0,), (0,)), ((), ())), preferred_element_type=F32, **kw)


def _params(sem):
    return pltpu.CompilerParams(dimension_semantics=sem, vmem_limit_bytes=VMEM_LIMIT)


def _ada_kernel(c_ref, w_ref, b_ref, o_ref):
    s = _silu(c_ref[...])
    o_ref[...] = _dot(s, w_ref[...], precision=HIGHEST) + b_ref[...]


def _ada(cond8, ada_w, ada_b):
    depth = ada_w.shape[0]
    tn = 1536
    return pl.pallas_call(
        _ada_kernel,
        grid=(depth, 6 * D_MODEL // tn),
        in_specs=[pl.BlockSpec((8, D_MODEL), lambda l, j: (0, 0)),
                  pl.BlockSpec((None, D_MODEL, tn), lambda l, j: (l, 0, j)),
                  pl.BlockSpec((None, 1, tn), lambda l, j: (l, 0, j))],
        out_specs=pl.BlockSpec((None, 8, tn), lambda l, j: (l, 0, j)),
        out_shape=jax.ShapeDtypeStruct((depth, 8, 6 * D_MODEL), F32),
        compiler_params=_params(("arbitrary", "arbitrary")),
        name="ada",
    )(cond8, ada_w, ada_b.reshape(depth, 1, 6 * D_MODEL))


def _inproj_kernel(x_ref, mod_ref, nw_ref, w_ref, b_ref, o_ref):
    sh = mod_ref[:, 0:D_MODEL]
    sc = mod_ref[:, D_MODEL:2 * D_MODEL]
    for r0 in range(0, TM, ROW_SUB):
        x = x_ref[r0:r0 + ROW_SUB, :]
        hn = x * lax.rsqrt(jnp.mean(x * x, axis=-1, keepdims=True) + EPS) * nw_ref[...]
        h = (hn * (1.0 + sc) + sh).astype(BF16)
        for c0 in range(0, NCOL, TN_IN):
            o_ref[r0:r0 + ROW_SUB, c0:c0 + TN_IN] = _dot(h, w_ref[:, c0:c0 + TN_IN]) + b_ref[:, c0:c0 + TN_IN]


def _inproj(x, mods, nw, w_all, b_all, layer, nlt):
    bsz, ltot, _ = x.shape

    def mrow(bb, i):
        return (jnp.where(i < nlt, bb, bsz), 0, 0)

    return pl.pallas_call(
        _inproj_kernel,
        grid=(bsz, ltot // TM),
        in_specs=[pl.BlockSpec((None, TM, D_MODEL), lambda bb, i: (bb, i, 0)),
                  pl.BlockSpec((None, 1, 6 * D_MODEL), mrow),
                  pl.BlockSpec((1, D_MODEL), lambda bb, i: (0, 0)),
                  pl.BlockSpec((None, D_MODEL, NCOL), lambda bb, i: (layer, 0, 0)),
                  pl.BlockSpec((None, 1, NCOL), lambda bb, i: (layer, 0, 0))],
        out_specs=pl.BlockSpec((None, TM, NCOL), lambda bb, i: (bb, i, 0)),
        out_shape=jax.ShapeDtypeStruct((bsz, ltot, NCOL), F32),
        compiler_params=_params(("arbitrary", "arbitrary")),
        name="inproj",
    )(x, mods, nw, w_all, b_all)


def _chunk_of(d, s, nl, ncx):
    fwd = jnp.where(s < ncx, nl + s, s - ncx)
    bwd = jnp.where(s < ncx, nl + ncx - 1 - s, nl - 1 - (s - ncx))
    return jnp.where(d == 0, fwd, bwd)


def _lane_iota(shape):
    return lax.broadcasted_iota(jnp.int32, shape, len(shape) - 1)


def _qkconv_kernel(qk_ref, hp_ref, hn_ref, cw_ref, cb_ref, q_ref, k_ref, xe_ref, *, nlt, ltiles):
    i = pl.program_id(1)
    first = (i == 0) | (i == nlt)
    last = (i == nlt - 1) | (i == ltiles - 1)
    xe_ref[0:8, :] = jnp.where(first, 0.0, hp_ref[...])
    xe_ref[8:8 + TM, :] = qk_ref[...]
    xe_ref[8 + TM:16 + TM, :] = jnp.where(last, 0.0, hn_ref[...])
    hw = ML_HEADS * ML_DH
    acc = jnp.broadcast_to(cb_ref[...], (TM, 2 * hw))
    for j in range(CONV_K):
        acc = acc + xe_ref[8 - CONV_K // 2 + j:8 - CONV_K // 2 + j + TM, :] * cw_ref[j:j + 1, :]
    qk = _silu(acc)
    q_ref[...] = qk[:, :hw].astype(BF16)
    k_ref[...] = (qk[:, hw:] * (ML_DH ** -0.5)).astype(BF16)


def _qkconv(slab, conv_w, conv_b, nlt):
    bsz, ltot, _ = slab.shape
    hw = ML_HEADS * ML_DH
    ltiles = ltot // TM
    nrow8 = ltot // 8
    kern = functools.partial(_qkconv_kernel, nlt=nlt, ltiles=ltiles)
    out = pl.BlockSpec((None, TM, hw), lambda bb, i: (bb, i, 0))
    return pl.pallas_call(
        kern,
        grid=(bsz, ltiles),
        in_specs=[pl.BlockSpec((None, TM, 2 * hw), lambda bb, i: (bb, i, OFF_QK // (2 * hw))),
                  pl.BlockSpec((None, 8, 2 * hw),
                               lambda bb, i: (bb, jnp.maximum(i * (TM // 8) - 1, 0), OFF_QK // (2 * hw))),
                  pl.BlockSpec((None, 8, 2 * hw),
                               lambda bb, i: (bb, jnp.minimum((i + 1) * (TM // 8), nrow8 - 1), OFF_QK // (2 * hw))),
                  pl.BlockSpec((CONV_K, 2 * hw), lambda bb, i: (0, 0)),
                  pl.BlockSpec((1, 2 * hw), lambda bb, i: (0, 0))],
        out_specs=[out, out],
        out_shape=[jax.ShapeDtypeStruct((bsz, ltot, hw), BF16)] * 2,
        scratch_shapes=[pltpu.VMEM((TM + 16, 2 * hw), F32)],
        compiler_params=_params(("arbitrary", "arbitrary")),
        name="qkconv",
    )(slab, slab, slab, conv_w, conv_b)


def _split2(x):
    hi = x.astype(BF16)
    return hi, (x - hi.astype(F32)).astype(BF16)


def _select_dot(x, sel_b):
    hi, lo = _split2(x)
    return _dot(hi, sel_b) + _dot(lo, sel_b)


GATE_LANE0 = ML_HEADS


def _mlstm2_kernel(qf_ref, kf_ref, vf_ref, gf_ref, qb_ref, kb_ref, vb_ref, gb_ref, of_ref, ob_ref, st_ref, m_ref):
    s = pl.program_id(1)

    @pl.when(s == 0)
    def _():
        st_ref[...] = jnp.zeros_like(st_ref)
        m_ref[...] = jnp.zeros_like(m_ref)

    dirs = (0, 1)
    pairs = range(ML_HEADS // 2)
    heads = range(ML_HEADS)
    q_refs, k_refs, v_refs, g_refs, o_refs = (qf_ref, qb_ref), (kf_ref, kb_ref), (vf_ref, vb_ref), (gf_ref, gb_ref), \
        (of_ref, ob_ref)
    lane = _lane_iota((1, LANES))
    head_lane = (lane >= GATE_LANE0) & (lane < GATE_LANE0 + ML_HEADS)
    lo_half = lane < ML_DH
    ti = lax.broadcasted_iota(jnp.int32, (CHUNK, CHUNK), 0)
    si = lax.broadcasted_iota(jnp.int32, (CHUNK, CHUNK), 1)
    causal = (ti >= si, ti <= si)
    tri = [jnp.where(causal[d], 1.0, 0.0) for d in dirs]
    tri_t = [jnp.where(causal[1 - d], 1.0, 0.0) for d in dirs]
    s8 = _lane_iota((ML_HEADS, CHUNK))
    gi = lax.broadcasted_iota(jnp.int32, (LANES, 2 * LANES), 0)
    li = lax.broadcasted_iota(jnp.int32, (LANES, 2 * LANES), 1)
    li_in = jnp.where(li >= LANES, li - LANES, li)
    same_head = (gi < ML_DH) == (li_in < ML_DH)
    lane2 = _lane_iota((1, 2 * LANES))
    lo_half2 = jnp.where(lane2 >= LANES, lane2 - LANES, lane2) < ML_DH
    hmask = [lo_half if h % 2 == 0 else jnp.logical_not(lo_half) for h in heads]
    ones_b = jnp.ones((CHUNK, LANES), BF16)
    sel_full = [jnp.where(gi == GATE_LANE0 + 2 * p + jnp.where(li >= LANES, 1, 0), 1.0, 0.0).astype(BF16)
                for p in pairs]
    sel_half = [jnp.where(ti == GATE_LANE0 + 2 * p + jnp.where(si >= ML_DH, 1, 0), 1.0, 0.0).astype(BF16)
                for p in pairs]

    g_ig, ls = [], []
    for d in dirs:
        g_raw = g_refs[d][...]
        g_fg = g_raw if d == 0 else pltpu.roll(g_raw, LANES - 2 * ML_HEADS, axis=1)
        g_ig.append(jnp.where(head_lane, pltpu.roll(g_fg, ML_HEADS, axis=1), 0.0))
        ls.append(jnp.where(head_lane, _log_sigmoid(g_fg), 0.0))
    b_cols = [_dot(tri[d], ls[d], precision=HIGHEST) for d in dirs]
    b_rows = [_dot(ls[d].T, tri_t[d], precision=HIGHEST) for d in dirs]

    qp = [[q_refs[d][:, p * LANES:(p + 1) * LANES] for p in pairs] for d in dirs]
    kp = [[k_refs[d][:, p * LANES:(p + 1) * LANES] for p in pairs] for d in dirs]
    va = [[jnp.concatenate([v_refs[d][:, p * LANES:(p + 1) * LANES].astype(BF16), ones_b], axis=1) for p in pairs]
          for d in dirs]
    st = [[st_ref[d, p] for p in pairs] for d in dirs]
    qs = [[_dot(qp[d][p], st[d][p].astype(BF16)) for p in pairs] for d in dirs]
    qk = [[_dot_nt(jnp.where(hmask[h], qp[d][h // 2].astype(F32), 0.0).astype(BF16), kp[d][h // 2])
           for h in heads] for d in dirs]

    r_rows, cdiff, stacked = [], [], []
    for d in dirs:
        r_rows.append(g_ig[d].T - b_rows[d])
        bl_row = b_cols[d][CHUNK - 1:CHUNK, :] if d == 0 else b_cols[d][0:1, :]
        m_row = m_ref[d]
        cm8 = r_rows[d][GATE_LANE0:GATE_LANE0 + ML_HEADS, :]
        order = s8 if d == 0 else CHUNK - 1 - s8
        step = 1
        while step < CHUNK:
            shifted = pltpu.roll(cm8, step if d == 0 else CHUNK - step, axis=1)
            cm8 = jnp.where(order >= step, jnp.maximum(cm8, shifted), cm8)
            step *= 2
        cm = jnp.concatenate([jnp.zeros((GATE_LANE0, CHUNK), F32), cm8,
                              jnp.zeros((LANES - GATE_LANE0 - ML_HEADS, CHUNK), F32)], axis=0).T
        a_cols = b_cols[d] + m_row
        mt_cols = jnp.maximum(a_cols, b_cols[d] + cm)
        g_cols = bl_row - b_cols[d] + g_ig[d]
        m_new = jnp.maximum(bl_row + m_row, jnp.max(g_cols, axis=0, keepdims=True))
        dp_cols = jnp.where(ti >= 0, jnp.exp(bl_row + m_row - m_new), 0.0)
        m_ref[d] = m_new
        cdiff.append(b_cols[d] - mt_cols)
        stacked.append(jnp.concatenate([jnp.exp(a_cols - mt_cols), jnp.exp(-mt_cols),
                                        jnp.exp(g_cols - m_new), dp_cols], axis=0))

    c_full = [[_select_dot(cdiff[d], sel_full[p]) for p in pairs] for d in dirs]
    halves = [[_select_dot(stacked[d], sel_half[p]) for p in pairs] for d in dirs]

    sc = [[None] * ML_HEADS for _ in dirs]
    for d in dirs:
        for h in heads:
            p, j = h // 2, h % 2
            dlog = c_full[d][p][:, j * LANES:(j + 1) * LANES] + r_rows[d][GATE_LANE0 + h:GATE_LANE0 + h + 1, :]
            sc[d][h] = (qk[d][h] * jnp.exp(jnp.where(causal[d], dlog, -jnp.inf))).astype(BF16)
    sv = [[_dot(sc[d][h], va[d][h // 2]) for h in heads] for d in dirs]
    upd = [[_dot_tn((kp[d][p].astype(F32) * halves[d][p][2 * CHUNK:3 * CHUNK]).astype(BF16), va[d][p])
            for p in pairs] for d in dirs]

    for d in dirs:
        outs = []
        for p in pairs:
            inter_pair = halves[d][p][0:CHUNK]
            emt_pair = halves[d][p][CHUNK:2 * CHUNK]
            dp_pair = halves[d][p][3 * CHUNK:4 * CHUNK]
            sv_pair = jnp.where(lo_half2, sv[d][2 * p], sv[d][2 * p + 1])
            num = sv_pair[:, :LANES] + inter_pair * qs[d][p][:, :LANES]
            den = sv_pair[:, LANES:] + inter_pair * qs[d][p][:, LANES:]
            outs.append(num / jnp.maximum(jnp.abs(den), emt_pair))
            st_ref[d, p] = (jnp.concatenate([dp_pair, dp_pair], axis=1) * st[d][p]
                            + jnp.where(same_head, upd[d][p], 0.0))
        o_refs[d][...] = jnp.concatenate(outs, axis=1)


def _mlstm2(q, k, slab, nl, ncx):
    bsz, ltot, _ = slab.shape
    hw = ML_HEADS * ML_DH

    def specs(d):
        ch = lambda bb, s: _chunk_of(d, s, nl, ncx)
        return [pl.BlockSpec((None, CHUNK, hw), lambda bb, s: (bb, ch(bb, s), 0)),
                pl.BlockSpec((None, CHUNK, hw), lambda bb, s: (bb, ch(bb, s), 0)),
                pl.BlockSpec((None, CHUNK, hw), lambda bb, s: (bb, ch(bb, s), OFF_V // hw)),
                pl.BlockSpec((None, CHUNK, LANES), lambda bb, s: (bb, ch(bb, s), OFF_MISC // LANES))]

    out = lambda d: pl.BlockSpec((None, CHUNK, hw), lambda bb, s: (bb, _chunk_of(d, s, nl, ncx), 0))
    return pl.pallas_call(
        _mlstm2_kernel,
        grid=(bsz, nl + ncx),
        in_specs=specs(0) + specs(1),
        out_specs=[out(0), out(1)],
        out_shape=[jax.ShapeDtypeStruct((bsz, ltot, hw), F32)] * 2,
        scratch_shapes=[pltpu.VMEM((2, ML_HEADS // 2, LANES, 2 * LANES), F32),
                        pltpu.VMEM((2, 1, LANES), F32)],
        compiler_params=_params(("arbitrary", "arbitrary")),
        name="mlstm",
    )(q, k, slab, slab, q, k, slab, slab)


def _rope_lanes(x, tab_ref):
    return (x * tab_ref[0] + pltpu.roll(x, 8, axis=1) * tab_ref[1]
            + pltpu.roll(x, LANES - 8, axis=1) * tab_ref[2])


def _ret_kernel(qf_ref, kf_ref, vf_ref, rf_ref, qb_ref, kb_ref, vb_ref, rb_ref, dmat_ref, xi_ref, kd_ref, cd_ref,
                of_ref, ob_ref, st_ref):
    s = pl.program_id(1)

    @pl.when(s == 0)
    def _():
        st_ref[...] = jnp.zeros_like(st_ref)

    dirs = (0, 1)
    q_refs, k_refs, v_refs, r_refs, o_refs = (qf_ref, qb_ref), (kf_ref, kb_ref), (vf_ref, vb_ref), (rf_ref, rb_ref), \
        (of_ref, ob_ref)
    lane = _lane_iota((1, LANES))
    lo_half = lane < RET_DV
    ri = lax.broadcasted_iota(jnp.int32, (LANES, LANES), 0)
    ci = lax.broadcasted_iota(jnp.int32, (LANES, LANES), 1)
    heads = range(RET_HEADS)
    pairs = range(RET_HEADS // 2)
    qg = [[_rope_lanes(q_refs[d][:, g * LANES:(g + 1) * LANES], r_refs[d]) for g in range(2)] for d in dirs]
    kg = [[_rope_lanes(k_refs[d][:, g * LANES:(g + 1) * LANES], r_refs[d]) * (RET_DK ** -0.5) for g in range(2)]
          for d in dirs]
    qg_b = [[x.astype(BF16) for x in qg[d]] for d in dirs]
    kg_b = [[x.astype(BF16) for x in kg[d]] for d in dirs]
    kw_b = [[(kg[d][g] * kd_ref[d, :, g * LANES:(g + 1) * LANES]).astype(BF16) for g in range(2)] for d in dirs]
    vp_b = [[v_refs[d][:, p * LANES:(p + 1) * LANES].astype(BF16) for p in pairs] for d in dirs]
    st = [[st_ref[d, p] for p in pairs] for d in dirs]

    cross = [[_dot(qg_b[d][p // 2], st[d][p].astype(BF16)) for p in pairs] for d in dirs]
    qk = [[None] * RET_HEADS for _ in dirs]
    for d in dirs:
        for h in heads:
            q_lo = (h % 4) * RET_DK
            hm = (lane >= q_lo) & (lane < q_lo + RET_DK)
            qk[d][h] = _dot_nt(jnp.where(hm, qg[d][h // 4], 0.0).astype(BF16), kg_b[d][h // 4])
    sv = [[_dot((qk[d][h] * dmat_ref[d, h]).astype(BF16), vp_b[d][h // 2]) for h in heads] for d in dirs]
    upd = [[_dot_tn(kw_b[d][p // 2], vp_b[d][p]) for p in pairs] for d in dirs]

    for d in dirs:
        outs = []
        for p in pairs:
            a = p % 2
            outs.append(jnp.where(lo_half, sv[d][2 * p], sv[d][2 * p + 1])
                        + xi_ref[d, :, p * LANES:(p + 1) * LANES] * cross[d][p])
            r_lo = (ri >= 2 * a * RET_DK) & (ri < (2 * a + 1) * RET_DK)
            r_hi = (ri >= (2 * a + 1) * RET_DK) & (ri < (2 * a + 2) * RET_DK)
            valid = (r_lo & (ci < RET_DV)) | (r_hi & (ci >= RET_DV))
            st_ref[d, p] = cd_ref[d, :, p * LANES:(p + 1) * LANES] * st[d][p] + jnp.where(valid, upd[d][p], 0.0)
        o_refs[d][...] = jnp.concatenate(outs, axis=1)


def _ret(slab, rope_r, dmat, xi, kd, cd, nl, ncx):
    bsz, ltot, _ = slab.shape
    qw = RET_HEADS * RET_DK
    vw = RET_HEADS * RET_DV

    def specs(d):
        ch = lambda bb, s: _chunk_of(d, s, nl, ncx)
        return [pl.BlockSpec((None, CHUNK, qw), lambda bb, s: (bb, ch(bb, s), OFF_RQ // qw)),
                pl.BlockSpec((None, CHUNK, qw), lambda bb, s: (bb, ch(bb, s), OFF_RK // qw)),
                pl.BlockSpec((None, CHUNK, vw), lambda bb, s: (bb, ch(bb, s), OFF_RV // vw)),
                pl.BlockSpec((3, CHUNK, LANES), lambda bb, s: (0, ch(bb, s), 0))]

    full = lambda a: pl.BlockSpec(a.shape, lambda bb, s: (0,) * a.ndim)
    out = lambda d: pl.BlockSpec((None, CHUNK, vw), lambda bb, s: (bb, _chunk_of(d, s, nl, ncx), 0))
    return pl.pallas_call(
        _ret_kernel,
        grid=(bsz, nl + ncx),
        in_specs=specs(0) + specs(1) + [full(dmat), full(xi), full(kd), full(cd)],
        out_specs=[out(0), out(1)],
        out_shape=[jax.ShapeDtypeStruct((bsz, ltot, vw), F32)] * 2,
        scratch_shapes=[pltpu.VMEM((2, RET_HEADS // 2, LANES, LANES), F32)],
        compiler_params=_params(("arbitrary", "arbitrary")),
        name="ret",
    )(slab, slab, slab, rope_r, slab, slab, slab, rope_r, dmat, xi, kd, cd)


def _ret_tables():
    lg = jnp.log1p(-jnp.exp2(-5.0 - jnp.arange(RET_HEADS, dtype=F32)))
    idx = jnp.arange(CHUNK, dtype=F32)
    diff = idx[:, None] - idx[None, :]
    dm, xis, kds, cds = [], [], [], []
    for lgd, sign in ((lg, 1.0), (lg[::-1], -1.0)):
        dd = diff[None] * sign
        dm.append(jnp.exp(jnp.where(dd >= 0, dd * lgd[:, None, None], -jnp.inf)))
        order = idx if sign > 0 else (CHUNK - 1.0 - idx)
        xi = jnp.exp((order + 1.0)[None] * lgd[:, None])
        kdv = jnp.exp((CHUNK - 1.0 - order)[None] * lgd[:, None])
        cdv = jnp.exp(CHUNK * lgd)
        xis.append(jnp.repeat(xi.T, RET_DV, axis=1))
        kds.append(jnp.repeat(kdv.T, RET_DK, axis=1))
        cds.append(jnp.repeat(cdv, RET_DV)[None, :])
    return jnp.stack(dm), jnp.stack(xis), jnp.stack(kds), jnp.stack(cds)


def _rope_tables(n_lat, n_ctx):
    t = jnp.arange(n_lat)
    row = (t // GRID_W).astype(F32)
    col = (t % GRID_W).astype(F32)
    half = ROPE_D // 4
    inv = ROPE_BASE ** (-jnp.arange(half, dtype=F32) / half)
    ar, ac = row[:, None] * inv, col[:, None] * inv
    z = jnp.zeros_like(ar)
    c32 = jnp.concatenate([jnp.cos(ar), jnp.cos(ar), jnp.cos(ac), jnp.cos(ac)], axis=1)
    s1 = jnp.concatenate([z, jnp.sin(ar), z, jnp.sin(ac)], axis=1)
    s2 = jnp.concatenate([-jnp.sin(ar), z, -jnp.sin(ac), z], axis=1)
    tab = jnp.stack([c32, s1, s2])
    ident = jnp.stack([jnp.ones((n_ctx, ROPE_D), F32), jnp.zeros((n_ctx, ROPE_D), F32),
                       jnp.zeros((n_ctx, ROPE_D), F32)])
    return jnp.concatenate([tab, ident], axis=1)


def _mla_kernel(ql_ref, kv_ref, misc_ref, qan_ref, wqt_ref, kvn_ref, wk_ref, wvt_ref, qnw_ref, knw_ref,
                ropek_ref, ropeq_ref, qt_ref, k_ref, vt_ref):
    ql = ql_ref[...]
    qn = ql * lax.rsqrt(jnp.mean(ql * ql, axis=-1, keepdims=True) + EPS) * qan_ref[...]
    qt = _dot_nt(wqt_ref[...].astype(BF16), qn.astype(BF16))
    for h in range(MLA_HEADS):
        xh = qt[h * LANES:(h + 1) * LANES, :]
        ss = jnp.sum(xh * xh, axis=0, keepdims=True)
        xh = xh * lax.rsqrt(ss * (1.0 / QK_D) + EPS) * qnw_ref[...]
        xh = (xh * ropeq_ref[0] + pltpu.roll(xh, 8, axis=0) * ropeq_ref[1]
              + pltpu.roll(xh, LANES - 8, axis=0) * ropeq_ref[2])
        qt_ref[h * LANES:(h + 1) * LANES, :] = (xh * (QK_D ** -0.5 * LOG2E)).astype(BF16)

    kv = kv_ref[...]
    kvn = (kv * lax.rsqrt(jnp.mean(kv * kv, axis=-1, keepdims=True) + EPS) * kvn_ref[...]).astype(BF16)
    kk = _dot(kvn, wk_ref[...].astype(BF16))
    lane = _lane_iota((1, LANES))
    rope_part = (lane >= NOPE_D) & (lane < QK_D)
    kr = jnp.where(rope_part, pltpu.roll(misc_ref[...], NOPE_D - 4 * ML_HEADS, axis=1), 0.0)
    for h in range(MLA_HEADS):
        xh = kk[:, h * LANES:(h + 1) * LANES] + kr
        ss = jnp.sum(xh * xh, axis=1, keepdims=True)
        xh = xh * lax.rsqrt(ss * (1.0 / QK_D) + EPS) * knw_ref[...]
        k_ref[:, h * LANES:(h + 1) * LANES] = _rope_lanes(xh, ropek_ref).astype(BF16)
    vt_ref[...] = _dot_nt(wvt_ref[...].astype(BF16), kvn).astype(BF16)


def _mla(slab, qan, wqt, kvn, wk, wvt, qnw, knw, rope_k, rope_q):
    bsz, ltot, _ = slab.shape
    hq = MLA_HEADS * LANES
    hv = MLA_HEADS * MLA_DV
    full = lambda shape: pl.BlockSpec(shape, lambda bb, i: (0,) * len(shape))
    return pl.pallas_call(
        _mla_kernel,
        grid=(bsz, ltot // TM),
        in_specs=[pl.BlockSpec((None, TM, Q_LORA), lambda bb, i: (bb, i, OFF_QL // Q_LORA)),
                  pl.BlockSpec((None, TM, KV_LORA), lambda bb, i: (bb, i, OFF_KV // KV_LORA)),
                  pl.BlockSpec((None, TM, LANES), lambda bb, i: (bb, i, OFF_MISC // LANES)),
                  full((1, Q_LORA)), full((hq, Q_LORA)), full((1, KV_LORA)), full((KV_LORA, hq)),
                  full((hv, KV_LORA)), full((LANES, 1)), full((1, LANES)),
                  pl.BlockSpec((3, TM, LANES), lambda bb, i: (0, i, 0)),
                  pl.BlockSpec((3, LANES, TM), lambda bb, i: (0, 0, i))],
        out_specs=[pl.BlockSpec((None, hq, TM), lambda bb, i: (bb, 0, i)),
                   pl.BlockSpec((None, TM, hq), lambda bb, i: (bb, i, 0)),
                   pl.BlockSpec((None, None, hv, TM), lambda bb, i: (bb, i, 0, 0))],
        out_shape=[jax.ShapeDtypeStruct((bsz, hq, ltot), BF16),
                   jax.ShapeDtypeStruct((bsz, ltot, hq), BF16),
                   jax.ShapeDtypeStruct((bsz, ltot // TM, hv, TM), BF16)],
        compiler_params=_params(("arbitrary", "arbitrary")),
        name="mla",
    )(slab, slab, slab, qan, wqt, kvn, wk, wvt, qnw, knw, rope_k, rope_q)


ATT_QC = 512
ATT_ONES = 16
ATT_AHEAD = 3


def _attn_kernel(qt_ref, k_ref, vt_ref, o_ref, acc_ref, m_ref, l_ref, *, r, nkb):
    tq = qt_ref.shape[1]
    tkb = r * TM
    acc_ref[...] = jnp.zeros_like(acc_ref)
    m_ref[...] = jnp.full_like(m_ref, -jnp.inf)
    l_ref[...] = jnp.zeros_like(l_ref)
    ones = jnp.ones((ATT_ONES, TM), BF16)

    qc = min(ATT_QC, tq)
    units = [(j, c) for j in range(2) for c in range(tq // qc)]

    def key_block(kb, carry):
        r0 = pl.multiple_of(kb * tkb, tkb)

        def scores(u):
            j, c = units[u]
            return _dot(k_ref[pl.ds(r0, tkb), j * LANES:(j + 1) * LANES],
                        qt_ref[j * LANES:(j + 1) * LANES, c * qc:(c + 1) * qc])

        ahead = min(ATT_AHEAD, len(units))
        pending = [scores(u) for u in range(ahead)]
        for u, (j, c) in enumerate(units):
            sc = pending.pop(0)
            if u + ahead < len(units):
                pending.append(scores(u + ahead))
            cs = slice(c * qc, (c + 1) * qc)
            m_old = m_ref[j:j + 1, cs]
            m_new = jnp.maximum(m_old, jnp.max(sc, axis=0, keepdims=True))
            p = jnp.exp2(sc - m_new).astype(BF16)
            alpha = jnp.exp2(m_old - m_new)
            pv = None
            for t in range(r):
                vt = jnp.concatenate([vt_ref[kb * r + t, j * MLA_DV:(j + 1) * MLA_DV, :], ones], axis=0)
                part = _dot(vt, p[t * TM:(t + 1) * TM])
                pv = part if pv is None else pv + part
            acc_ref[j * MLA_DV:(j + 1) * MLA_DV, cs] = (
                alpha * acc_ref[j * MLA_DV:(j + 1) * MLA_DV, cs] + pv[0:MLA_DV])
            l_ref[j:j + 1, cs] = alpha * l_ref[j:j + 1, cs] + pv[MLA_DV:MLA_DV + 1]
            m_ref[j:j + 1, cs] = m_new
        return carry

    lax.fori_loop(0, nkb, key_block, 0)
    head_rows = lax.broadcasted_iota(jnp.int32, acc_ref.shape, 0)
    inv = jnp.where(head_rows < MLA_DV, 1.0 / l_ref[0:1, :], 1.0 / l_ref[1:2, :])
    o_ref[...] = (acc_ref[...] * inv).T


def _attn(qt, k, vt, tq, q_off, nq, key_tile0, r, nkb):
    bsz = qt.shape[0]
    hv = vt.shape[2]
    nkt = r * nkb
    assert key_tile0 % nkt == 0
    kern = functools.partial(_attn_kernel, r=r, nkb=nkb)
    return pl.pallas_call(
        kern,
        grid=(bsz, MLA_HEADS // 2, nq),
        in_specs=[pl.BlockSpec((None, 2 * LANES, tq), lambda bb, hp, qi: (bb, hp, qi + q_off)),
                  pl.BlockSpec((None, nkt * TM, 2 * LANES), lambda bb, hp, qi: (bb, key_tile0 // nkt, hp)),
                  pl.BlockSpec((None, nkt, 2 * MLA_DV, TM), lambda bb, hp, qi: (bb, key_tile0 // nkt, hp, 0))],
        out_specs=pl.BlockSpec((None, tq, 2 * MLA_DV), lambda bb, hp, qi: (bb, qi, hp)),
        out_shape=jax.ShapeDtypeStruct((bsz, nq * tq, hv), F32),
        scratch_shapes=[pltpu.VMEM((2 * MLA_DV, tq), F32), pltpu.VMEM((8, tq), F32), pltpu.VMEM((8, tq), F32)],
        compiler_params=_params(("arbitrary", "arbitrary", "arbitrary")),
        name="attn",
    )(qt, k, vt)


def _split_dot(x, w_b):
    hi = x.astype(BF16)
    lo = (x - hi.astype(F32)).astype(BF16)
    return _dot(hi, w_b) + _dot(lo, w_b)


def _head_ln(x, g_ref, gt_ref, width):
    mu = _split_dot(_split_dot(x, g_ref[...]) * (1.0 / width), gt_ref[...])
    xc = x - mu
    var = _split_dot(_split_dot(xc * xc, g_ref[...]) * (1.0 / width), gt_ref[...])
    return xc * lax.rsqrt(var + EPS)


def _merge_kernel(hmf_ref, hmb_ref, og_ref, mlw_ref, yb_ref, hrf_ref, hrb_ref, rg_ref, rnw_ref, bl_ref, wbr_ref,
                  wout_ref, x_ref, mod_ref, n2w_ref, rw_ref, g_ref, gt_ref,
                  xo_ref, h2_ref, aff_ref, wbr_b, wout_b):
    @pl.when((pl.program_id(0) == 0) & (pl.program_id(1) == 0))
    def _():
        wbr_b[...] = wbr_ref[...].astype(BF16)
        wout_b[...] = wout_ref[...].astype(BF16)

    subs = [slice(r0, r0 + ROW_SUB) for r0 in range(0, TM, ROW_SUB)]
    ys = []
    for rs in subs:
        ya = _sigmoid(og_ref[rs, :]) * (_head_ln(hmf_ref[rs, :] + hmb_ref[rs, :], g_ref, gt_ref, ML_DH)
                                        * mlw_ref[...])
        yc = _silu(rg_ref[rs, :]) * (_head_ln(hrf_ref[rs, :] + hrb_ref[rs, :], g_ref, gt_ref, RET_DV)
                                     * rnw_ref[...])
        ys.append((ya.astype(BF16), yb_ref[rs, :].astype(BF16), yc.astype(BF16)))
    br = [[_dot(ys[i][n], wbr_b[n]) for n in range(N_BRANCH)] for i in range(len(subs))]
    merged = []
    for i, rs in enumerate(subs):
        m = None
        for n in range(N_BRANCH):
            term = _sigmoid(bl_ref[rs, n * D_MODEL:(n + 1) * D_MODEL]) * br[i][n]
            m = term if m is None else m + term
        merged.append(m.astype(BF16))
    y = [_dot(m, wout_b[...]) for m in merged]
    g1 = mod_ref[:, 2 * D_MODEL:3 * D_MODEL]
    h2s = []
    for i, rs in enumerate(subs):
        x = x_ref[rs, :] + g1 * y[i]
        xo_ref[rs, :] = x
        hn = x * lax.rsqrt(jnp.mean(x * x, axis=-1, keepdims=True) + EPS) * n2w_ref[...]
        h2 = hn * (1.0 + mod_ref[:, 4 * D_MODEL:5 * D_MODEL]) + mod_ref[:, 3 * D_MODEL:4 * D_MODEL]
        h2_ref[rs, :] = h2.astype(BF16)
        h2s.append(h2)
    logits = [_dot(h2, rw_ref[...], precision=HIGHEST) for h2 in h2s]
    valid = _lane_iota((1, LANES)) < N_EXPERTS
    for i, rs in enumerate(subs):
        lg = jnp.where(valid, logits[i], -jnp.inf)
        e = jnp.exp(lg - jnp.max(lg, axis=-1, keepdims=True))
        aff_ref[rs, :] = e / jnp.sum(e, axis=-1, keepdims=True)


def _merge(hmf, hmb, slab, mlw, yb, hrf, hrb, rnw, wbr, wout, x, mods, n2w, rw, g8, g8t, nlt):
    bsz, ltot, _ = x.shape
    hw = ML_HEADS * ML_DH

    def mrow(bb, i):
        return (jnp.where(i < nlt, bb, bsz), 0, 0)

    full = lambda shape: pl.BlockSpec(shape, lambda bb, i: (0,) * len(shape))
    row = lambda width, off: pl.BlockSpec((None, TM, width), lambda bb, i: (bb, i, off // width))
    return pl.pallas_call(
        _merge_kernel,
        grid=(bsz, ltot // TM),
        in_specs=[row(hw, 0), row(hw, 0),
                  row(hw, OFF_OG), full((1, hw)),
                  row(hw, 0),
                  row(hw, 0), row(hw, 0),
                  row(hw, OFF_RG), full((1, hw)),
                  row(N_BRANCH * D_MODEL, OFF_BL),
                  full((N_BRANCH, BRANCH_W, D_MODEL)), full((D_MODEL, D_MODEL)),
                  row(D_MODEL, 0),
                  pl.BlockSpec((None, 1, 6 * D_MODEL), mrow),
                  full((1, D_MODEL)), full((D_MODEL, LANES)), full((hw, LANES)), full((LANES, hw))],
        out_specs=[row(D_MODEL, 0), row(D_MODEL, 0), row(LANES, 0)],
        out_shape=[jax.ShapeDtypeStruct((bsz, ltot, D_MODEL), F32),
                   jax.ShapeDtypeStruct((bsz, ltot, D_MODEL), BF16),
                   jax.ShapeDtypeStruct((bsz, ltot, LANES), F32)],
        scratch_shapes=[pltpu.VMEM((N_BRANCH, BRANCH_W, D_MODEL), BF16), pltpu.VMEM((D_MODEL, D_MODEL), BF16)],
        compiler_params=_params(("arbitrary", "arbitrary")),
        name="merge",
    )(hmf, hmb, slab, mlw, yb, hrf, hrb, slab, rnw, slab, wbr, wout, x, mods, n2w, rw, g8, g8t)


def _select_kernel(aff_ref, pos_ref, post_ref, gwt_ref, off_ref, *, n, cap, base0, base_step):
    b = pl.program_id(0)
    base = (base0 + b * base_step).astype(F32)
    bits = lax.bitcast_convert_type(aff_ref[...], jnp.int32)
    capf = jnp.float32(cap)

    def search(it, cur):
        cand = cur | jnp.left_shift(jnp.int32(1), 30 - it)
        cnt = jnp.sum(jnp.where(bits >= cand, 1.0, 0.0), axis=0, keepdims=True)
        return jnp.where(cnt >= capf, cand, cur)

    thr = lax.fori_loop(0, 31, search, jnp.zeros((1, LANES), jnp.int32))
    n_gt = jnp.sum(jnp.where(bits > thr, 1.0, 0.0), axis=0, keepdims=True)
    need = capf - n_gt

    ri = lax.broadcasted_iota(jnp.int32, (TM, TM), 0)
    ci = lax.broadcasted_iota(jnp.int32, (TM, TM), 1)
    strict = (ci < ri).astype(BF16)

    def tile(i, carry):
        c_eq, c_sel = carry
        r0 = pl.multiple_of(i * TM, TM)
        a = aff_ref[pl.ds(r0, TM), :]
        bt = lax.bitcast_convert_type(a, jnp.int32)
        eq = bt == thr
        rank = _dot(strict, eq.astype(BF16)) + c_eq
        sel = (bt > thr) | (eq & (rank < need))
        self_ = jnp.where(sel, 1.0, 0.0)
        pos = _dot(strict, self_.astype(BF16)) + c_sel
        posv = jnp.where(sel, pos + base, -1.0)
        pos_ref[pl.ds(r0, TM), :] = posv
        post_ref[:, pl.ds(r0, TM)] = posv.T[0:N_EXPERTS, :]
        gwt_ref[:, pl.ds(r0, TM)] = jnp.where(sel, a, 0.0).T[0:N_EXPERTS, :]
        off_ref[pl.ds(2 * i, 1), :] = c_sel
        off_ref[pl.ds(2 * i + 1, 1), :] = c_sel + jnp.sum(self_[0:TM // 2], axis=0, keepdims=True)
        return (c_eq + jnp.sum(jnp.where(eq, 1.0, 0.0), axis=0, keepdims=True),
                c_sel + jnp.sum(self_, axis=0, keepdims=True))

    zero = jnp.zeros((1, LANES), F32)
    lax.fori_loop(0, n // TM, tile, (zero, zero))


def _select(aff, n, cap, tile_off, base0, base_step):
    bsz = aff.shape[0]
    nt = n // TM
    kern = functools.partial(_select_kernel, n=n, cap=cap, base0=base0, base_step=base_step)
    return pl.pallas_call(
        kern,
        grid=(bsz,),
        in_specs=[pl.BlockSpec((None, n, LANES), lambda bb: (bb, tile_off, 0))],
        out_specs=[pl.BlockSpec((None, n, LANES), lambda bb: (bb, 0, 0)),
                   pl.BlockSpec((None, N_EXPERTS, n), lambda bb: (bb, 0, 0)),
                   pl.BlockSpec((None, N_EXPERTS, n), lambda bb: (bb, 0, 0)),
                   pl.BlockSpec((None, 2 * nt, LANES), lambda bb: (bb, 0, 0))],
        out_shape=[jax.ShapeDtypeStruct((bsz, n, LANES), F32),
                   jax.ShapeDtypeStruct((bsz, N_EXPERTS, n), F32),
                   jax.ShapeDtypeStruct((bsz, N_EXPERTS, n), F32),
                   jax.ShapeDtypeStruct((bsz, 2 * nt, LANES), F32)],
        compiler_params=_params(("arbitrary",)),
        name="select",
    )(aff)


SLOT_ALIGN = 16
WIN = TM + SLOT_ALIGN
TC = TM // 2
WINC = TC + SLOT_ALIGN
GATHER_MAX_TILES = 11
FFN_ROW_CHUNKS = 4
FFN_TF = 256


def _moe_gather_kernel(st_ref, h_ref, pt_ref, gt_ref, xs_ref, gs_ref, *, nt256, tiles):
    e = pl.program_id(0)
    i = pl.program_id(1)

    @pl.when(i == 0)
    def _():
        xs_ref[...] = jnp.zeros_like(xs_ref)
        gs_ref[...] = jnp.zeros_like(gs_ref)

    slot = lax.broadcasted_iota(jnp.int32, (WIN, TM), 0)
    for u in range(tiles):
        st = pl.multiple_of(st_ref[e * nt256 + i * tiles + u], SLOT_ALIGN)
        st2 = pl.multiple_of(st + SLOT_ALIGN, SLOT_ALIGN)
        match = (slot + st).astype(F32) == pt_ref[:, u * TM:(u + 1) * TM]
        onehot = jnp.where(match, 1.0, 0.0).astype(BF16)
        got = _dot(onehot, h_ref[u * TM:(u + 1) * TM, :])
        head = xs_ref[pl.ds(st, SLOT_ALIGN), :].astype(F32)
        xs_ref[pl.ds(st, SLOT_ALIGN), :] = (head + got[0:SLOT_ALIGN]).astype(BF16)
        xs_ref[pl.ds(st2, TM), :] = got[SLOT_ALIGN:WIN].astype(BF16)
        gate = jnp.sum(jnp.where(match, gt_ref[:, u * TM:(u + 1) * TM], 0.0), axis=1, keepdims=True)
        gate = jnp.broadcast_to(gate, (WIN, LANES))
        gs_ref[pl.ds(st, SLOT_ALIGN), :] = gs_ref[pl.ds(st, SLOT_ALIGN), :] + gate[0:SLOT_ALIGN]
        gs_ref[pl.ds(st2, TM), :] = gate[SLOT_ALIGN:WIN]


def _moe_gather(starts, h2, post, gwt, rows, tiles):
    ttot = h2.shape[0]
    tt = tiles * TM
    kern = functools.partial(_moe_gather_kernel, nt256=ttot // TM, tiles=tiles)
    return pl.pallas_call(
        kern,
        grid_spec=pltpu.PrefetchScalarGridSpec(
            num_scalar_prefetch=1,
            grid=(N_EXPERTS, ttot // tt),
            in_specs=[pl.BlockSpec((tt, D_MODEL), lambda e, i, st: (i, 0)),
                      pl.BlockSpec((None, 1, tt), lambda e, i, st: (e, 0, i)),
                      pl.BlockSpec((None, 1, tt), lambda e, i, st: (e, 0, i))],
            out_specs=[pl.BlockSpec((None, rows, D_MODEL), lambda e, i, st: (e, 0, 0)),
                       pl.BlockSpec((None, rows, LANES), lambda e, i, st: (e, 0, 0))]),
        out_shape=[jax.ShapeDtypeStruct((N_EXPERTS, rows, D_MODEL), BF16),
                   jax.ShapeDtypeStruct((N_EXPERTS, rows, LANES), F32)],
        compiler_params=_params(("arbitrary", "arbitrary")),
        name="moe_gather",
    )(starts, h2, post, gwt)


def _moe_ffn_kernel(xs_ref, gs_ref, w1_ref, w3_ref, w2_ref, ys_ref, acc_ref, *, ct, rows):
    f = pl.program_id(1)

    @pl.when(f == 0)
    def _():
        acc_ref[...] = jnp.zeros_like(acc_ref)

    w1b = w1_ref[...].astype(BF16)
    w3b = w3_ref[...].astype(BF16)
    w2b = w2_ref[...].astype(BF16)
    rc = ct // FFN_ROW_CHUNKS
    chunks = [slice(r * rc, (r + 1) * rc) for r in range(FFN_ROW_CHUNKS)]
    up = [(_dot(xs_ref[chunks[0], :], w1b), _dot(xs_ref[chunks[0], :], w3b))]
    for r, rs in enumerate(chunks):
        a, b = up[r]
        if r + 1 < len(chunks):
            up.append((_dot(xs_ref[chunks[r + 1], :], w1b), _dot(xs_ref[chunks[r + 1], :], w3b)))
        acc_ref[rs, :] += _dot((_silu(a) * b).astype(BF16), w2b)

    @pl.when(f == pl.num_programs(1) - 1)
    def _():
        gate = gs_ref[...]
        for c in range(D_MODEL // LANES):
            cs = slice(c * LANES, (c + 1) * LANES)
            ys_ref[0:ct, cs] = (acc_ref[:, cs] * gate).astype(BF16)
        ys_ref[ct:rows, :] = jnp.zeros((rows - ct, D_MODEL), BF16)


def _moe_ffn(xs, gs, w1, w3, w2, layer, ct):
    rows = xs.shape[1]
    kern = functools.partial(_moe_ffn_kernel, ct=ct, rows=rows)
    return pl.pallas_call(
        kern,
        grid=(N_EXPERTS, EXPERT_FF // FFN_TF),
        in_specs=[pl.BlockSpec((None, ct, D_MODEL), lambda e, f: (e, 0, 0)),
                  pl.BlockSpec((None, ct, LANES), lambda e, f: (e, 0, 0)),
                  pl.BlockSpec((None, None, D_MODEL, FFN_TF), lambda e, f: (layer, e, 0, f)),
                  pl.BlockSpec((None, None, D_MODEL, FFN_TF), lambda e, f: (layer, e, 0, f)),
                  pl.BlockSpec((None, None, FFN_TF, D_MODEL), lambda e, f: (layer, e, f, 0))],
        out_specs=pl.BlockSpec((None, rows, D_MODEL), lambda e, f: (e, 0, 0)),
        out_shape=jax.ShapeDtypeStruct((N_EXPERTS, rows, D_MODEL), BF16),
        scratch_shapes=[pltpu.VMEM((ct, D_MODEL), F32)],
        compiler_params=_params(("arbitrary", "arbitrary")),
        name="moe_ffn",
    )(xs, gs, w1, w3, w2)


def _combine_kernel(st_ref, x_ref, pos_ref, mod_ref, *rest):
    nsub = TM // TC
    ys_refs = rest[:N_EXPERTS * nsub]
    o_ref = rest[N_EXPERTS * nsub]
    i = pl.program_id(0)
    ntc = pl.num_programs(0) * nsub
    slot = lax.broadcasted_iota(jnp.int32, (TC, WINC), 1)
    g2 = mod_ref[:, 5 * D_MODEL:6 * D_MODEL]
    for u in range(nsub):
        rs = slice(u * TC, (u + 1) * TC)
        pos = pos_ref[rs, :]
        acc = None
        for e in range(N_EXPERTS):
            st = st_ref[e * ntc + i * nsub + u]
            onehot = jnp.where((slot + st).astype(F32) == pos[:, e:e + 1], 1.0, 0.0).astype(BF16)
            part = _dot(onehot, ys_refs[e * nsub + u][0])
            acc = part if acc is None else acc + part
        o_ref[rs, :] = x_ref[rs, :] + g2 * acc


def _combine(starts, x, pos, mods_rows, ys, nlt, ltiles):
    ttot = x.shape[0]
    nt = ttot // TM
    nsub = TM // TC
    bsz = ttot // (ltiles * TM)

    def mrow(i, st):
        return (jnp.where(i % ltiles < nlt, i // ltiles, bsz), 0, 0)

    def ys_spec(e, u):
        return pl.BlockSpec((pl.Element(1), pl.Element(WINC), pl.Element(D_MODEL)),
                            lambda i, st: (e, pl.multiple_of(st[e * nt * nsub + i * nsub + u], SLOT_ALIGN), 0))

    return pl.pallas_call(
        _combine_kernel,
        grid_spec=pltpu.PrefetchScalarGridSpec(
            num_scalar_prefetch=1,
            grid=(nt,),
            in_specs=[pl.BlockSpec((TM, D_MODEL), lambda i, st: (i, 0)),
                      pl.BlockSpec((TM, LANES), lambda i, st: (i, 0)),
                      pl.BlockSpec((None, 1, 6 * D_MODEL), mrow)]
                     + [ys_spec(e, u) for e in range(N_EXPERTS) for u in range(nsub)],
            out_specs=pl.BlockSpec((TM, D_MODEL), lambda i, st: (i, 0))),
        out_shape=jax.ShapeDtypeStruct((ttot, D_MODEL), F32),
        compiler_params=_params(("arbitrary",)),
        name="combine",
    )(starts, x, pos, mods_rows, *([ys] * (N_EXPERTS * nsub)))


def _permute_w_in(w, b):
    widths = (2 * ML_HEADS * ML_DH, ML_HEADS * ML_DH, ML_HEADS * ML_DH, 4 * ML_HEADS, Q_LORA, KV_LORA, ROPE_D,
              RET_HEADS * RET_DK, RET_HEADS * RET_DK, RET_HEADS * RET_DV, RET_HEADS * RET_DV, N_BRANCH * D_MODEL)
    offs = np.concatenate([[0], np.cumsum(widths)])
    seg = lambda a, k: a[..., offs[k]:offs[k + 1]]
    order = (11, 0, 1, 2, 9, 10, 7, 8, 5, 4, 3, 6)

    def build(a):
        parts = [seg(a, k) for k in order]
        used = sum(p.shape[-1] for p in parts)
        parts.append(jnp.zeros(a.shape[:-1] + (NCOL - used,), a.dtype))
        return jnp.concatenate(parts, axis=-1)

    return build(w), build(b)


def kernel(x, c, ctx, c_ctx, ada_w, ada_b, norm1_w, norm2_w, w_in, b_in, conv_w, conv_b, ml_norm_w, mla_qa_norm,
           mla_wq_b, mla_kva_norm, mla_wkv_b, q_norm_w, k_norm_w, ret_norm_w, w_branch, w_out, router_w,
           exp_w1, exp_w3, exp_w2):
    bsz, n_lat, _ = x.shape
    n_ctx = ctx.shape[1]
    depth = ada_w.shape[0]
    ltot = n_lat + n_ctx
    nlt = n_lat // TM
    ltiles = ltot // TM
    nl, ncx = n_lat // CHUNK, n_ctx // CHUNK
    assert n_lat % TM == 0 and n_ctx == TM and bsz + 1 <= 8
    gather_tiles = max(t for t in range(1, GATHER_MAX_TILES + 1) if (bsz * ltiles) % t == 0)

    cond8 = jnp.zeros((8, D_MODEL), F32).at[:bsz].set(c).at[bsz].set(c_ctx)
    mods_all = _ada(cond8, ada_w, ada_b)

    rope32 = _rope_tables(n_lat, n_ctx)
    rope_r = jnp.tile(rope32, (1, 1, LANES // ROPE_D))
    ident = jnp.stack([jnp.ones((ltot, 1), F32), jnp.zeros((ltot, 1), F32), jnp.zeros((ltot, 1), F32)])
    rope_k = jnp.concatenate([jnp.broadcast_to(ident, (3, ltot, NOPE_D)), rope32,
                              jnp.broadcast_to(ident, (3, ltot, LANES - QK_D))], axis=2)
    rope_q = jnp.swapaxes(rope_k, 1, 2)
    dmat, xi, kd, cd = _ret_tables()

    gi = jnp.arange(ML_HEADS * ML_DH) // ML_DH
    g8 = (gi[:, None] == jnp.arange(LANES)[None, :]).astype(BF16)
    g8t = g8.T

    cap_l = CAP_FACTOR * n_lat // N_EXPERTS
    cap_c = CAP_FACTOR * n_ctx // N_EXPERTS
    ct = bsz * (cap_l + cap_c)
    assert ct % (16 * FFN_ROW_CHUNKS) == 0 and cap_l % 16 == 0 and cap_c % 16 == 0
    rows = ct + WIN

    w_perm, b_perm = _permute_w_in(w_in, b_in[:, None, :])
    w_perm = w_perm.astype(BF16)
    xx = jnp.concatenate([x, ctx], axis=1)
    att_r = 3 if ltiles % 3 == 0 else 1
    tq = min(2048, n_lat)

    for l in range(depth):
        mods = mods_all[l].reshape(8, 1, 6 * D_MODEL)
        slab = _inproj(xx, mods, norm1_w[l][None, :], w_perm, b_perm, l, nlt)

        mq, mk = _qkconv(slab, conv_w[l], conv_b[l][None, :], nlt)
        hmf, hmb = _mlstm2(mq, mk, slab, nl, ncx)
        hrf, hrb = _ret(slab, rope_r, dmat, xi, kd, cd, nl, ncx)

        wq = mla_wq_b[l].reshape(Q_LORA, MLA_HEADS, QK_D)
        wqt = jnp.pad(wq, ((0, 0), (0, 0), (0, LANES - QK_D))).reshape(Q_LORA, MLA_HEADS * LANES).T
        wkv = mla_wkv_b[l].reshape(KV_LORA, MLA_HEADS, NOPE_D + MLA_DV)
        wk = jnp.pad(wkv[:, :, :NOPE_D], ((0, 0), (0, 0), (0, LANES - NOPE_D))).reshape(KV_LORA, MLA_HEADS * LANES)
        wvt = wkv[:, :, NOPE_D:].reshape(KV_LORA, MLA_HEADS * MLA_DV).T
        qnw = jnp.pad(q_norm_w[l], (0, LANES - QK_D))[:, None]
        knw = jnp.pad(k_norm_w[l], (0, LANES - QK_D))[None, :]
        qt, kk, vt = _mla(slab, mla_qa_norm[l][None, :], wqt, mla_kva_norm[l][None, :], wk, wvt, qnw, knw,
                          rope_k, rope_q)
        yb_l = _attn(qt, kk, vt, tq, 0, n_lat // tq, 0, att_r, ltiles // att_r)
        yb_c = _attn(qt, kk, vt, TM, nlt, 1, nlt, 1, 1)
        yb = jnp.concatenate([yb_l, yb_c], axis=1)

        rw = jnp.pad(router_w[l], ((0, 0), (0, LANES - N_EXPERTS)))
        xm, h2, aff = _merge(hmf, hmb, slab, ml_norm_w[l][None, :], yb, hrf, hrb, ret_norm_w[l][None, :], w_branch[l],
                             w_out[l], xx, mods, norm2_w[l][None, :], rw, g8, g8t, nlt)

        pos_l, post_l, gwt_l, off_l = _select(aff, n_lat, cap_l, 0, 0, cap_l + cap_c)
        pos_c, post_c, gwt_c, off_c = _select(aff, n_ctx, cap_c, nlt, cap_l, cap_l + cap_c)
        pos = jnp.concatenate([pos_l, pos_c], axis=1).reshape(bsz * ltot, LANES)
        expert_major = lambda a_l, a_c: jnp.swapaxes(jnp.concatenate([a_l, a_c], axis=2), 0, 1).reshape(
            N_EXPERTS, 1, bsz * ltot)
        post = expert_major(post_l, post_c)
        gwt = expert_major(gwt_l, gwt_c)
        base_l = (jnp.arange(bsz) * (cap_l + cap_c))[:, None, None]
        base_c = base_l + cap_l
        off = jnp.concatenate(
            [off_l[:, :, :N_EXPERTS].astype(jnp.int32) // SLOT_ALIGN * SLOT_ALIGN + base_l,
             off_c[:, :, :N_EXPERTS].astype(jnp.int32) // SLOT_ALIGN * SLOT_ALIGN + base_c], axis=1)
        starts_c = jnp.transpose(off, (2, 0, 1)).reshape(-1)
        starts_g = jnp.transpose(off[:, ::2], (2, 0, 1)).reshape(-1)

        xs, gs = _moe_gather(starts_g, h2.reshape(bsz * ltot, D_MODEL), post, gwt, rows, gather_tiles)
        ys = _moe_ffn(xs, gs, exp_w1, exp_w3, exp_w2, l, ct)
        xx = _combine(starts_c, xm.reshape(bsz * ltot, D_MODEL), pos, mods, ys, nlt, ltiles)
        xx = xx.reshape(bsz, ltot, D_MODEL)

    return xx[:, :n_lat]
```

```python
import functools

import jax
import jax.numpy as jnp
import numpy as np
from jax import lax
from jax.experimental import pallas as pl
from jax.experimental.pallas import tpu as pltpu

F32 = jnp.float32
BF16 = jnp.bfloat16
HIGHEST = lax.Precision.HIGHEST

D_MODEL = 1024
GRID_W = 64
N_BRANCH = 3
BRANCH_W = 512
ML_HEADS = 8
ML_DH = 64
CONV_K = 5
MLA_HEADS = 8
Q_LORA = 384
KV_LORA = 256
NOPE_D = 64
ROPE_D = 32
QK_D = NOPE_D + ROPE_D
MLA_DV = 64
RET_HEADS = 8
RET_DK = 32
RET_DV = 64
N_EXPERTS = 16
EXPERT_FF = 1024
CAP_FACTOR = 2
CHUNK = 128
ROPE_BASE = 10000.0
EPS = 1e-6
LOG2E = 1.4426950408889634

LANES = 128
TM = 256
ROW_SUB = 128
VMEM_LIMIT = 56 * 1024 * 1024

OFF_BL, OFF_QK, OFF_V, OFF_OG, OFF_RV, OFF_RG = 0, 3072, 4096, 4608, 5120, 5632
OFF_RQ, OFF_RK, OFF_KV, OFF_QL, OFF_MISC = 6144, 6400, 6656, 6912, 7296
NCOL = 7680
TN_IN = 2560


def _sigmoid(x):
    return 1.0 / (1.0 + jnp.exp(-x))


def _silu(x):
    return x * _sigmoid(x)


def _log_sigmoid(x):
    return jnp.minimum(x, 0.0) - jnp.log1p(jnp.exp(-jnp.abs(x)))


def _dot(a, b, **kw):
    return jnp.dot(a, b, preferred_element_type=F32, **kw)


def _dot_nt(a, b, **kw):
    return lax.dot_general(a, b, (((1,), (1,)), ((), ())), preferred_element_type=F32, **kw)


def _dot_tn(a, b, **kw):
    return lax.dot_general(a, b, (((0,), (0,)), ((), ())), preferred_element_type=F32, **kw)


def _params(sem):
    return pltpu.CompilerParams(dimension_semantics=sem, vmem_limit_bytes=VMEM_LIMIT)


def _ada_kernel(c_ref, w_ref, b_ref, o_ref):
    s = _silu(c_ref[...])
    o_ref[...] = _dot(s, w_ref[...], precision=HIGHEST) + b_ref[...]


def _ada(cond8, ada_w, ada_b):
    depth = ada_w.shape[0]
    tn = 1536
    return pl.pallas_call(
        _ada_kernel,
        grid=(depth, 6 * D_MODEL // tn),
        in_specs=[pl.BlockSpec((8, D_MODEL), lambda l, j: (0, 0)),
                  pl.BlockSpec((None, D_MODEL, tn), lambda l, j: (l, 0, j)),
                  pl.BlockSpec((None, 1, tn), lambda l, j: (l, 0, j))],
        out_specs=pl.BlockSpec((None, 8, tn), lambda l, j: (l, 0, j)),
        out_shape=jax.ShapeDtypeStruct((depth, 8, 6 * D_MODEL), F32),
        compiler_params=_params(("arbitrary", "arbitrary")),
        name="ada",
    )(cond8, ada_w, ada_b.reshape(depth, 1, 6 * D_MODEL))


def _inproj_kernel(x_ref, mod_ref, nw_ref, w_ref, b_ref, o_ref):
    sh = mod_ref[:, 0:D_MODEL]
    sc = mod_ref[:, D_MODEL:2 * D_MODEL]
    for r0 in range(0, TM, ROW_SUB):
        x = x_ref[r0:r0 + ROW_SUB, :]
        hn = x * lax.rsqrt(jnp.mean(x * x, axis=-1, keepdims=True) + EPS) * nw_ref[...]
        h = (hn * (1.0 + sc) + sh).astype(BF16)
        for c0 in range(0, NCOL, TN_IN):
            o_ref[r0:r0 + ROW_SUB, c0:c0 + TN_IN] = _dot(h, w_ref[:, c0:c0 + TN_IN]) + b_ref[:, c0:c0 + TN_IN]


def _inproj(x, mods, nw, w_all, b_all, layer, nlt):
    bsz, ltot, _ = x.shape

    def mrow(bb, i):
        return (jnp.where(i < nlt, bb, bsz), 0, 0)

    return pl.pallas_call(
        _inproj_kernel,
        grid=(bsz, ltot // TM),
        in_specs=[pl.BlockSpec((None, TM, D_MODEL), lambda bb, i: (bb, i, 0)),
                  pl.BlockSpec((None, 1, 6 * D_MODEL), mrow),
                  pl.BlockSpec((1, D_MODEL), lambda bb, i: (0, 0)),
                  pl.BlockSpec((None, D_MODEL, NCOL), lambda bb, i: (layer, 0, 0)),
                  pl.BlockSpec((None, 1, NCOL), lambda bb, i: (layer, 0, 0))],
        out_specs=pl.BlockSpec((None, TM, NCOL), lambda bb, i: (bb, i, 0)),
        out_shape=jax.ShapeDtypeStruct((bsz, ltot, NCOL), F32),
        compiler_params=_params(("arbitrary", "arbitrary")),
        name="inproj",
    )(x, mods, nw, w_all, b_all)


def _chunk_of(d, s, nl, ncx):
    fwd = jnp.where(s < ncx, nl + s, s - ncx)
    bwd = jnp.where(s < ncx, nl + ncx - 1 - s, nl - 1 - (s - ncx))
    return jnp.where(d == 0, fwd, bwd)


def _lane_iota(shape):
    return lax.broadcasted_iota(jnp.int32, shape, len(shape) - 1)


def _qkconv_kernel(qk_ref, hp_ref, hn_ref, cw_ref, cb_ref, q_ref, k_ref, xe_ref, *, nlt, ltiles):
    i = pl.program_id(1)
    first = (i == 0) | (i == nlt)
    last = (i == nlt - 1) | (i == ltiles - 1)
    xe_ref[0:8, :] = jnp.where(first, 0.0, hp_ref[...])
    xe_ref[8:8 + TM, :] = qk_ref[...]
    xe_ref[8 + TM:16 + TM, :] = jnp.where(last, 0.0, hn_ref[...])
    hw = ML_HEADS * ML_DH
    acc = jnp.broadcast_to(cb_ref[...], (TM, 2 * hw))
    for j in range(CONV_K):
        acc = acc + xe_ref[8 - CONV_K // 2 + j:8 - CONV_K // 2 + j + TM, :] * cw_ref[j:j + 1, :]
    qk = _silu(acc)
    q_ref[...] = qk[:, :hw].astype(BF16)
    k_ref[...] = (qk[:, hw:] * (ML_DH ** -0.5)).astype(BF16)


def _qkconv(slab, conv_w, conv_b, nlt):
    bsz, ltot, _ = slab.shape
    hw = ML_HEADS * ML_DH
    ltiles = ltot // TM
    nrow8 = ltot // 8
    kern = functools.partial(_qkconv_kernel, nlt=nlt, ltiles=ltiles)
    out = pl.BlockSpec((None, TM, hw), lambda bb, i: (bb, i, 0))
    return pl.pallas_call(
        kern,
        grid=(bsz, ltiles),
        in_specs=[pl.BlockSpec((None, TM, 2 * hw), lambda bb, i: (bb, i, OFF_QK // (2 * hw))),
                  pl.BlockSpec((None, 8, 2 * hw),
                               lambda bb, i: (bb, jnp.maximum(i * (TM // 8) - 1, 0), OFF_QK // (2 * hw))),
                  pl.BlockSpec((None, 8, 2 * hw),
                               lambda bb, i: (bb, jnp.minimum((i + 1) * (TM // 8), nrow8 - 1), OFF_QK // (2 * hw))),
                  pl.BlockSpec((CONV_K, 2 * hw), lambda bb, i: (0, 0)),
                  pl.BlockSpec((1, 2 * hw), lambda bb, i: (0, 0))],
        out_specs=[out, out],
        out_shape=[jax.ShapeDtypeStruct((bsz, ltot, hw), BF16)] * 2,
        scratch_shapes=[pltpu.VMEM((TM + 16, 2 * hw), F32)],
        compiler_params=_params(("arbitrary", "arbitrary")),
        name="qkconv",
    )(slab, slab, slab, conv_w, conv_b)


def _split2(x):
    hi = x.astype(BF16)
    return hi, (x - hi.astype(F32)).astype(BF16)


def _split3(x):
    hi = x.astype(BF16)
    r1 = x - hi.astype(F32)
    mid = r1.astype(BF16)
    return hi, mid, (r1 - mid.astype(F32)).astype(BF16)


def _select_dot(x, sel_b):
    hi, lo = _split2(x)
    return _dot(hi, sel_b) + _dot(lo, sel_b)


GATE_LANE0 = ML_HEADS


def _mlstm2_kernel(qf_ref, kf_ref, vf_ref, gf_ref, qb_ref, kb_ref, vb_ref, gb_ref, of_ref, ob_ref, st_ref, m_ref):
    s = pl.program_id(1)

    @pl.when(s == 0)
    def _():
        st_ref[...] = jnp.zeros_like(st_ref)
        m_ref[...] = jnp.zeros_like(m_ref)

    dirs = (0, 1)
    pairs = range(ML_HEADS // 2)
    heads = range(ML_HEADS)
    q_refs, k_refs, v_refs, g_refs, o_refs = (qf_ref, qb_ref), (kf_ref, kb_ref), (vf_ref, vb_ref), (gf_ref, gb_ref), \
        (of_ref, ob_ref)
    lane = _lane_iota((1, LANES))
    head_lane = (lane >= GATE_LANE0) & (lane < GATE_LANE0 + ML_HEADS)
    lo_half = lane < ML_DH
    ti = lax.broadcasted_iota(jnp.int32, (CHUNK, CHUNK), 0)
    si = lax.broadcasted_iota(jnp.int32, (CHUNK, CHUNK), 1)
    causal = (ti >= si, ti <= si)
    tri = [jnp.where(causal[d], 1.0, 0.0).astype(BF16) for d in dirs]
    tri_t = [jnp.where(causal[1 - d], 1.0, 0.0).astype(BF16) for d in dirs]
    s8 = _lane_iota((ML_HEADS, CHUNK))
    gi = lax.broadcasted_iota(jnp.int32, (LANES, 2 * LANES), 0)
    li = lax.broadcasted_iota(jnp.int32, (LANES, 2 * LANES), 1)
    li_in = jnp.where(li >= LANES, li - LANES, li)
    same_head = (gi < ML_DH) == (li_in < ML_DH)
    lane2 = _lane_iota((1, 2 * LANES))
    lo_half2 = jnp.where(lane2 >= LANES, lane2 - LANES, lane2) < ML_DH
    hmask = [lo_half if h % 2 == 0 else jnp.logical_not(lo_half) for h in heads]
    ones_b = jnp.ones((CHUNK, LANES), BF16)
    sel_full = [jnp.where(gi == GATE_LANE0 + 2 * p + jnp.where(li >= LANES, 1, 0), 1.0, 0.0).astype(BF16)
                for p in pairs]
    quarter = jnp.where(li >= LANES, 2, 0) + jnp.where(li_in >= ML_DH, 1, 0)
    sel_half = [jnp.where(gi == GATE_LANE0 + 4 * q + quarter, 1.0, 0.0).astype(BF16)
                for q in range(ML_HEADS // 4)]
    t16 = lax.broadcasted_iota(jnp.int32, (16, LANES), 0)

    g_ig, ls = [], []
    for d in dirs:
        g_raw = g_refs[d][...]
        g_fg = g_raw if d == 0 else pltpu.roll(g_raw, LANES - 2 * ML_HEADS, axis=1)
        g_ig.append(jnp.where(head_lane, pltpu.roll(g_fg, ML_HEADS, axis=1), 0.0))
        ls.append(jnp.where(head_lane, _log_sigmoid(g_fg), 0.0))
    ls3 = [_split3(ls[d]) for d in dirs]
    lst3 = [_split3(ls[d].T) for d in dirs]
    b_cols = [sum(_dot(tri[d], t) for t in ls3[d]) for d in dirs]
    b_rows = [sum(_dot(t, tri_t[d]) for t in lst3[d]) for d in dirs]
    yield

    qp = [[q_refs[d][:, p * LANES:(p + 1) * LANES] for p in pairs] for d in dirs]
    kp = [[k_refs[d][:, p * LANES:(p + 1) * LANES] for p in pairs] for d in dirs]
    va = [[jnp.concatenate([v_refs[d][:, p * LANES:(p + 1) * LANES].astype(BF16), ones_b], axis=1) for p in pairs]
          for d in dirs]
    st = [[st_ref[d, p] for p in pairs] for d in dirs]
    qs = [[_dot(qp[d][p], st[d][p].astype(BF16)) for p in pairs] for d in dirs]
    qk = [[_dot_nt(jnp.where(hmask[h], qp[d][h // 2].astype(F32), 0.0).astype(BF16), kp[d][h // 2])
           for h in heads] for d in dirs]
    yield

    r_rows, cdiff, stacked = [], [], []
    for d in dirs:
        r_rows.append(g_ig[d].T - b_rows[d])
        bl_row = b_cols[d][CHUNK - 1:CHUNK, :] if d == 0 else b_cols[d][0:1, :]
        m_row = m_ref[d]
        cm8 = r_rows[d][GATE_LANE0:GATE_LANE0 + ML_HEADS, :]
        order = s8 if d == 0 else CHUNK - 1 - s8
        step = 1
        while step < CHUNK:
            shifted = pltpu.roll(cm8, step if d == 0 else CHUNK - step, axis=1)
            cm8 = jnp.where(order >= step, jnp.maximum(cm8, shifted), cm8)
            step *= 2
        cm = jnp.concatenate([jnp.zeros((GATE_LANE0, CHUNK), F32), cm8,
                              jnp.zeros((LANES - GATE_LANE0 - ML_HEADS, CHUNK), F32)], axis=0).T
        a_cols = b_cols[d] + m_row
        mt_cols = jnp.maximum(a_cols, b_cols[d] + cm)
        g_cols = bl_row - b_cols[d] + g_ig[d]
        m_new = jnp.maximum(bl_row + m_row, jnp.max(g_cols, axis=0, keepdims=True))
        dp_cols = jnp.where(t16 >= 0, jnp.exp(bl_row + m_row - m_new), 0.0)
        m_ref[d] = m_new
        cdiff.append(b_cols[d] - mt_cols)
        stacked.append(jnp.concatenate([jnp.exp(a_cols - mt_cols), jnp.exp(-mt_cols),
                                        jnp.exp(g_cols - m_new), dp_cols], axis=0))

    c_full = [[_select_dot(cdiff[d], sel_full[p]) for p in pairs] for d in dirs]
    halves2 = [[_select_dot(stacked[d], sel_half[q]) for q in range(ML_HEADS // 4)] for d in dirs]
    halves = [[halves2[d][p // 2][:, (p % 2) * LANES:(p % 2 + 1) * LANES] for p in pairs]
              for d in dirs]
    yield

    sc = [[None] * ML_HEADS for _ in dirs]
    for d in dirs:
        for h in heads:
            p, j = h // 2, h % 2
            dlog = c_full[d][p][:, j * LANES:(j + 1) * LANES] + r_rows[d][GATE_LANE0 + h:GATE_LANE0 + h + 1, :]
            sc[d][h] = (qk[d][h] * jnp.exp(jnp.where(causal[d], dlog, -jnp.inf))).astype(BF16)
    sv = [[_dot(sc[d][h], va[d][h // 2]) for h in heads] for d in dirs]
    upd = [[_dot_tn((kp[d][p].astype(F32) * halves[d][p][2 * CHUNK:3 * CHUNK]).astype(BF16), va[d][p])
            for p in pairs] for d in dirs]
    yield

    for d in dirs:
        outs = []
        for p in pairs:
            inter_pair = halves[d][p][0:CHUNK]
            emt_pair = halves[d][p][CHUNK:2 * CHUNK]
            dp_pair = halves[d][p][3 * CHUNK:3 * CHUNK + 1]
            sv_pair = jnp.where(lo_half2, sv[d][2 * p], sv[d][2 * p + 1])
            num = sv_pair[:, :LANES] + inter_pair * qs[d][p][:, :LANES]
            den = sv_pair[:, LANES:] + inter_pair * qs[d][p][:, LANES:]
            outs.append(num / jnp.maximum(jnp.abs(den), emt_pair))
            st_ref[d, p] = (jnp.concatenate([dp_pair, dp_pair], axis=1) * st[d][p]
                            + jnp.where(same_head, upd[d][p], 0.0))
        o_refs[d][...] = jnp.concatenate(outs, axis=1)


N_MLSTM_IN, N_RET_IN = 8, 12


def _scan_kernel(*refs):
    n_in = N_MLSTM_IN + N_RET_IN
    ml_in, rt_in = refs[:N_MLSTM_IN], refs[N_MLSTM_IN:n_in]
    mo_f, mo_b, ro_f, ro_b, ml_st, ml_m, rt_st = refs[n_in:]
    bodies = [_mlstm2_kernel(*ml_in, mo_f, mo_b, ml_st, ml_m), _ret_kernel(*rt_in, ro_f, ro_b, rt_st)]
    while bodies:
        for body in list(bodies):
            if next(body, "done") == "done":
                bodies.remove(body)


def _scans(q, k, slab, rope_r, dmat, xi, kd, cd, nl, ncx):
    bsz, ltot, _ = slab.shape
    hw = ML_HEADS * ML_DH
    qw = RET_HEADS * RET_DK
    vw = RET_HEADS * RET_DV

    def ml_specs(d):
        ch = lambda bb, s: _chunk_of(d, s, nl, ncx)
        return [pl.BlockSpec((None, CHUNK, hw), lambda bb, s: (bb, ch(bb, s), 0)),
                pl.BlockSpec((None, CHUNK, hw), lambda bb, s: (bb, ch(bb, s), 0)),
                pl.BlockSpec((None, CHUNK, hw), lambda bb, s: (bb, ch(bb, s), OFF_V // hw)),
                pl.BlockSpec((None, CHUNK, LANES), lambda bb, s: (bb, ch(bb, s), OFF_MISC // LANES))]

    def rt_specs(d):
        ch = lambda bb, s: _chunk_of(d, s, nl, ncx)
        return [pl.BlockSpec((None, CHUNK, qw), lambda bb, s: (bb, ch(bb, s), OFF_RQ // qw)),
                pl.BlockSpec((None, CHUNK, qw), lambda bb, s: (bb, ch(bb, s), OFF_RK // qw)),
                pl.BlockSpec((None, CHUNK, vw), lambda bb, s: (bb, ch(bb, s), OFF_RV // vw)),
                pl.BlockSpec((3, CHUNK, LANES), lambda bb, s: (0, ch(bb, s), 0))]

    full = lambda a: pl.BlockSpec(a.shape, lambda bb, s: (0,) * a.ndim)
    out = lambda d, w: pl.BlockSpec((None, CHUNK, w), lambda bb, s: (bb, _chunk_of(d, s, nl, ncx), 0))
    return pl.pallas_call(
        _scan_kernel,
        grid=(bsz, nl + ncx),
        in_specs=(ml_specs(0) + ml_specs(1) + rt_specs(0) + rt_specs(1)
                  + [full(dmat), full(xi), full(kd), full(cd)]),
        out_specs=[out(0, hw), out(1, hw), out(0, vw), out(1, vw)],
        out_shape=[jax.ShapeDtypeStruct((bsz, ltot, hw), F32)] * 2 + [jax.ShapeDtypeStruct((bsz, ltot, vw), F32)] * 2,
        scratch_shapes=[pltpu.VMEM((2, ML_HEADS // 2, LANES, 2 * LANES), F32),
                        pltpu.VMEM((2, 1, LANES), F32),
                        pltpu.VMEM((2, RET_HEADS // 2, LANES, LANES), F32)],
        compiler_params=_params(("arbitrary", "arbitrary")),
        name="scans",
    )(q, k, slab, slab, q, k, slab, slab, slab, slab, slab, rope_r, slab, slab, slab, rope_r, dmat, xi, kd, cd)


def _rope_lanes(x, tab_ref):
    return (x * tab_ref[0] + pltpu.roll(x, 8, axis=1) * tab_ref[1]
            + pltpu.roll(x, LANES - 8, axis=1) * tab_ref[2])


def _ret_kernel(qf_ref, kf_ref, vf_ref, rf_ref, qb_ref, kb_ref, vb_ref, rb_ref, dmat_ref, xi_ref, kd_ref, cd_ref,
                of_ref, ob_ref, st_ref):
    s = pl.program_id(1)

    @pl.when(s == 0)
    def _():
        st_ref[...] = jnp.zeros_like(st_ref)

    dirs = (0, 1)
    q_refs, k_refs, v_refs, r_refs, o_refs = (qf_ref, qb_ref), (kf_ref, kb_ref), (vf_ref, vb_ref), (rf_ref, rb_ref), \
        (of_ref, ob_ref)
    lane = _lane_iota((1, LANES))
    lo_half = lane < RET_DV
    ri = lax.broadcasted_iota(jnp.int32, (LANES, LANES), 0)
    ci = lax.broadcasted_iota(jnp.int32, (LANES, LANES), 1)
    heads = range(RET_HEADS)
    pairs = range(RET_HEADS // 2)
    qg = [[_rope_lanes(q_refs[d][:, g * LANES:(g + 1) * LANES], r_refs[d]) for g in range(2)] for d in dirs]
    kg = [[_rope_lanes(k_refs[d][:, g * LANES:(g + 1) * LANES], r_refs[d]) * (RET_DK ** -0.5) for g in range(2)]
          for d in dirs]
    qg_b = [[x.astype(BF16) for x in qg[d]] for d in dirs]
    kg_b = [[x.astype(BF16) for x in kg[d]] for d in dirs]
    kw_b = [[(kg[d][g] * kd_ref[d, :, g * LANES:(g + 1) * LANES]).astype(BF16) for g in range(2)] for d in dirs]
    vp_b = [[v_refs[d][:, p * LANES:(p + 1) * LANES].astype(BF16) for p in pairs] for d in dirs]
    st = [[st_ref[d, p] for p in pairs] for d in dirs]

    cross = [[_dot(qg_b[d][p // 2], st[d][p].astype(BF16)) for p in pairs] for d in dirs]
    qk = [[None] * RET_HEADS for _ in dirs]
    for d in dirs:
        for h in heads:
            q_lo = (h % 4) * RET_DK
            hm = (lane >= q_lo) & (lane < q_lo + RET_DK)
            qk[d][h] = _dot_nt(jnp.where(hm, qg[d][h // 4], 0.0).astype(BF16), kg_b[d][h // 4])
    yield
    sv = [[_dot((qk[d][h] * dmat_ref[d, h]).astype(BF16), vp_b[d][h // 2]) for h in heads] for d in dirs]
    upd = [[_dot_tn(kw_b[d][p // 2], vp_b[d][p]) for p in pairs] for d in dirs]
    yield

    for d in dirs:
        outs = []
        for p in pairs:
            a = p % 2
            outs.append(jnp.where(lo_half, sv[d][2 * p], sv[d][2 * p + 1])
                        + xi_ref[d, :, p * LANES:(p + 1) * LANES] * cross[d][p])
            r_lo = (ri >= 2 * a * RET_DK) & (ri < (2 * a + 1) * RET_DK)
            r_hi = (ri >= (2 * a + 1) * RET_DK) & (ri < (2 * a + 2) * RET_DK)
            valid = (r_lo & (ci < RET_DV)) | (r_hi & (ci >= RET_DV))
            st_ref[d, p] = cd_ref[d, :, p * LANES:(p + 1) * LANES] * st[d][p] + jnp.where(valid, upd[d][p], 0.0)
        o_refs[d][...] = jnp.concatenate(outs, axis=1)


def _ret_tables():
    lg = jnp.log1p(-jnp.exp2(-5.0 - jnp.arange(RET_HEADS, dtype=F32)))
    idx = jnp.arange(CHUNK, dtype=F32)
    diff = idx[:, None] - idx[None, :]
    dm, xis, kds, cds = [], [], [], []
    for lgd, sign in ((lg, 1.0), (lg[::-1], -1.0)):
        dd = diff[None] * sign
        dm.append(jnp.exp(jnp.where(dd >= 0, dd * lgd[:, None, None], -jnp.inf)))
        order = idx if sign > 0 else (CHUNK - 1.0 - idx)
        xi = jnp.exp((order + 1.0)[None] * lgd[:, None])
        kdv = jnp.exp((CHUNK - 1.0 - order)[None] * lgd[:, None])
        cdv = jnp.exp(CHUNK * lgd)
        xis.append(jnp.repeat(xi.T, RET_DV, axis=1))
        kds.append(jnp.repeat(kdv.T, RET_DK, axis=1))
        cds.append(jnp.repeat(cdv, RET_DV)[None, :])
    return jnp.stack(dm), jnp.stack(xis), jnp.stack(kds), jnp.stack(cds)


def _rope_tables(n_lat, n_ctx):
    t = jnp.arange(n_lat)
    row = (t // GRID_W).astype(F32)
    col = (t % GRID_W).astype(F32)
    half = ROPE_D // 4
    inv = ROPE_BASE ** (-jnp.arange(half, dtype=F32) / half)
    ar, ac = row[:, None] * inv, col[:, None] * inv
    z = jnp.zeros_like(ar)
    c32 = jnp.concatenate([jnp.cos(ar), jnp.cos(ar), jnp.cos(ac), jnp.cos(ac)], axis=1)
    s1 = jnp.concatenate([z, jnp.sin(ar), z, jnp.sin(ac)], axis=1)
    s2 = jnp.concatenate([-jnp.sin(ar), z, -jnp.sin(ac), z], axis=1)
    tab = jnp.stack([c32, s1, s2])
    ident = jnp.stack([jnp.ones((n_ctx, ROPE_D), F32), jnp.zeros((n_ctx, ROPE_D), F32),
                       jnp.zeros((n_ctx, ROPE_D), F32)])
    return jnp.concatenate([tab, ident], axis=1)


def _mla_kernel(ql_ref, kv_ref, misc_ref, qan_ref, wqt_ref, kvn_ref, wk_ref, wvt_ref, qnw_ref, knw_ref,
                ropek_ref, ropeq_ref, qt_ref, k_ref, vt_ref):
    ql = ql_ref[...]
    qn = ql * lax.rsqrt(jnp.mean(ql * ql, axis=-1, keepdims=True) + EPS) * qan_ref[...]
    qt = _dot_nt(wqt_ref[...].astype(BF16), qn.astype(BF16))
    for h in range(MLA_HEADS):
        xh = qt[h * LANES:(h + 1) * LANES, :]
        ss = jnp.sum(xh * xh, axis=0, keepdims=True)
        xh = xh * lax.rsqrt(ss * (1.0 / QK_D) + EPS) * qnw_ref[...]
        xh = (xh * ropeq_ref[0] + pltpu.roll(xh, 8, axis=0) * ropeq_ref[1]
              + pltpu.roll(xh, LANES - 8, axis=0) * ropeq_ref[2])
        qt_ref[h * LANES:(h + 1) * LANES, :] = (xh * (QK_D ** -0.5 * LOG2E)).astype(BF16)

    kv = kv_ref[...]
    kvn = (kv * lax.rsqrt(jnp.mean(kv * kv, axis=-1, keepdims=True) + EPS) * kvn_ref[...]).astype(BF16)
    kk = _dot(kvn, wk_ref[...].astype(BF16))
    lane = _lane_iota((1, LANES))
    rope_part = (lane >= NOPE_D) & (lane < QK_D)
    kr = jnp.where(rope_part, pltpu.roll(misc_ref[...], NOPE_D - 4 * ML_HEADS, axis=1), 0.0)
    for h in range(MLA_HEADS):
        xh = kk[:, h * LANES:(h + 1) * LANES] + kr
        ss = jnp.sum(xh * xh, axis=1, keepdims=True)
        xh = xh * lax.rsqrt(ss * (1.0 / QK_D) + EPS) * knw_ref[...]
        k_ref[:, h * LANES:(h + 1) * LANES] = _rope_lanes(xh, ropek_ref).astype(BF16)
    vt_ref[...] = _dot_nt(wvt_ref[...].astype(BF16), kvn).astype(BF16)


def _mla(slab, qan, wqt, kvn, wk, wvt, qnw, knw, rope_k, rope_q):
    bsz, ltot, _ = slab.shape
    hq = MLA_HEADS * LANES
    hv = MLA_HEADS * MLA_DV
    full = lambda shape: pl.BlockSpec(shape, lambda bb, i: (0,) * len(shape))
    return pl.pallas_call(
        _mla_kernel,
        grid=(bsz, ltot // TM),
        in_specs=[pl.BlockSpec((None, TM, Q_LORA), lambda bb, i: (bb, i, OFF_QL // Q_LORA)),
                  pl.BlockSpec((None, TM, KV_LORA), lambda bb, i: (bb, i, OFF_KV // KV_LORA)),
                  pl.BlockSpec((None, TM, LANES), lambda bb, i: (bb, i, OFF_MISC // LANES)),
                  full((1, Q_LORA)), full((hq, Q_LORA)), full((1, KV_LORA)), full((KV_LORA, hq)),
                  full((hv, KV_LORA)), full((LANES, 1)), full((1, LANES)),
                  pl.BlockSpec((3, TM, LANES), lambda bb, i: (0, i, 0)),
                  pl.BlockSpec((3, LANES, TM), lambda bb, i: (0, 0, i))],
        out_specs=[pl.BlockSpec((None, hq, TM), lambda bb, i: (bb, 0, i)),
                   pl.BlockSpec((None, TM, hq), lambda bb, i: (bb, i, 0)),
                   pl.BlockSpec((None, None, hv, TM), lambda bb, i: (bb, i, 0, 0))],
        out_shape=[jax.ShapeDtypeStruct((bsz, hq, ltot), BF16),
                   jax.ShapeDtypeStruct((bsz, ltot, hq), BF16),
                   jax.ShapeDtypeStruct((bsz, ltot // TM, hv, TM), BF16)],
        compiler_params=_params(("arbitrary", "arbitrary")),
        name="mla",
    )(slab, slab, slab, qan, wqt, kvn, wk, wvt, qnw, knw, rope_k, rope_q)


ATT_QC = 512
ATT_ONES = 16
ATT_AHEAD = 3


def _attn_kernel(qt_ref, k_ref, vt_ref, o_ref, acc_ref, m_ref, l_ref, *, r, nkb):
    tq = qt_ref.shape[1]
    tkb = r * TM
    acc_ref[...] = jnp.zeros_like(acc_ref)
    m_ref[...] = jnp.full_like(m_ref, -jnp.inf)
    l_ref[...] = jnp.zeros_like(l_ref)
    ones = jnp.ones((ATT_ONES, TM), BF16)

    qc = min(ATT_QC, tq)
    units = [(j, c) for j in range(2) for c in range(tq // qc)]

    def key_block(kb, carry):
        r0 = pl.multiple_of(kb * tkb, tkb)

        def scores(u):
            j, c = units[u]
            return _dot(k_ref[pl.ds(r0, tkb), j * LANES:(j + 1) * LANES],
                        qt_ref[j * LANES:(j + 1) * LANES, c * qc:(c + 1) * qc])

        ahead = min(ATT_AHEAD, len(units))
        pending = [scores(u) for u in range(ahead)]
        for u, (j, c) in enumerate(units):
            sc = pending.pop(0)
            if u + ahead < len(units):
                pending.append(scores(u + ahead))
            cs = slice(c * qc, (c + 1) * qc)
            m_old = m_ref[j:j + 1, cs]
            m_new = jnp.maximum(m_old, jnp.max(sc, axis=0, keepdims=True))
            p = jnp.exp2(sc - m_new).astype(BF16)
            alpha = jnp.exp2(m_old - m_new)
            pv = None
            for t in range(r):
                vt = jnp.concatenate([vt_ref[kb * r + t, j * MLA_DV:(j + 1) * MLA_DV, :], ones], axis=0)
                part = _dot(vt, p[t * TM:(t + 1) * TM])
                pv = part if pv is None else pv + part
            acc_ref[j * MLA_DV:(j + 1) * MLA_DV, cs] = (
                alpha * acc_ref[j * MLA_DV:(j + 1) * MLA_DV, cs] + pv[0:MLA_DV])
            l_ref[j:j + 1, cs] = alpha * l_ref[j:j + 1, cs] + pv[MLA_DV:MLA_DV + 1]
            m_ref[j:j + 1, cs] = m_new
        return carry

    lax.fori_loop(0, nkb, key_block, 0)
    head_rows = lax.broadcasted_iota(jnp.int32, acc_ref.shape, 0)
    inv = jnp.where(head_rows < MLA_DV, 1.0 / l_ref[0:1, :], 1.0 / l_ref[1:2, :])
    o_ref[...] = (acc_ref[...] * inv).T


def _attn(qt, k, vt, tq, q_off, nq, key_tile0, r, nkb):
    bsz = qt.shape[0]
    hv = vt.shape[2]
    nkt = r * nkb
    assert key_tile0 % nkt == 0
    kern = functools.partial(_attn_kernel, r=r, nkb=nkb)
    return pl.pallas_call(
        kern,
        grid=(bsz, MLA_HEADS // 2, nq),
        in_specs=[pl.BlockSpec((None, 2 * LANES, tq), lambda bb, hp, qi: (bb, hp, qi + q_off)),
                  pl.BlockSpec((None, nkt * TM, 2 * LANES), lambda bb, hp, qi: (bb, key_tile0 // nkt, hp)),
                  pl.BlockSpec((None, nkt, 2 * MLA_DV, TM), lambda bb, hp, qi: (bb, key_tile0 // nkt, hp, 0))],
        out_specs=pl.BlockSpec((None, tq, 2 * MLA_DV), lambda bb, hp, qi: (bb, qi, hp)),
        out_shape=jax.ShapeDtypeStruct((bsz, nq * tq, hv), F32),
        scratch_shapes=[pltpu.VMEM((2 * MLA_DV, tq), F32), pltpu.VMEM((8, tq), F32), pltpu.VMEM((8, tq), F32)],
        compiler_params=_params(("arbitrary", "arbitrary", "arbitrary")),
        name="attn",
    )(qt, k, vt)


def _split_dot(x, w_b):
    hi = x.astype(BF16)
    lo = (x - hi.astype(F32)).astype(BF16)
    return _dot(hi, w_b) + _dot(lo, w_b)


def _head_ln(x, g_ref, gt_ref, width):
    mu = _split_dot(_split_dot(x, g_ref[...]) * (1.0 / width), gt_ref[...])
    xc = x - mu
    var = _split_dot(_split_dot(xc * xc, g_ref[...]) * (1.0 / width), gt_ref[...])
    return xc * lax.rsqrt(var + EPS)


def _merge_kernel(hmf_ref, hmb_ref, og_ref, mlw_ref, yb_ref, hrf_ref, hrb_ref, rg_ref, rnw_ref, bl_ref, wbr_ref,
                  wout_ref, x_ref, mod_ref, n2w_ref, rw_ref, g_ref, gt_ref,
                  xo_ref, h2_ref, aff_ref, wbr_b, wout_b):
    @pl.when((pl.program_id(0) == 0) & (pl.program_id(1) == 0))
    def _():
        wbr_b[...] = wbr_ref[...].astype(BF16)
        wout_b[...] = wout_ref[...].astype(BF16)

    subs = [slice(r0, r0 + ROW_SUB) for r0 in range(0, TM, ROW_SUB)]
    ys = []
    for rs in subs:
        ya = _sigmoid(og_ref[rs, :]) * (_head_ln(hmf_ref[rs, :] + hmb_ref[rs, :], g_ref, gt_ref, ML_DH)
                                        * mlw_ref[...])
        yc = _silu(rg_ref[rs, :]) * (_head_ln(hrf_ref[rs, :] + hrb_ref[rs, :], g_ref, gt_ref, RET_DV)
                                     * rnw_ref[...])
        ys.append((ya.astype(BF16), yb_ref[rs, :].astype(BF16), yc.astype(BF16)))
    br = [[_dot(ys[i][n], wbr_b[n]) for n in range(N_BRANCH)] for i in range(len(subs))]
    merged = []
    for i, rs in enumerate(subs):
        m = None
        for n in range(N_BRANCH):
            term = _sigmoid(bl_ref[rs, n * D_MODEL:(n + 1) * D_MODEL]) * br[i][n]
            m = term if m is None else m + term
        merged.append(m.astype(BF16))
    y = [_dot(m, wout_b[...]) for m in merged]
    g1 = mod_ref[:, 2 * D_MODEL:3 * D_MODEL]
    h2s = []
    for i, rs in enumerate(subs):
        x = x_ref[rs, :] + g1 * y[i]
        xo_ref[rs, :] = x
        hn = x * lax.rsqrt(jnp.mean(x * x, axis=-1, keepdims=True) + EPS) * n2w_ref[...]
        h2 = hn * (1.0 + mod_ref[:, 4 * D_MODEL:5 * D_MODEL]) + mod_ref[:, 3 * D_MODEL:4 * D_MODEL]
        h2_ref[rs, :] = h2.astype(BF16)
        h2s.append(h2)
    logits = []
    for h2 in h2s:
        hi, lo = _split2(h2)
        logits.append(_dot(hi, rw_ref[0]) + _dot(hi, rw_ref[1]) + _dot(lo, rw_ref[0]))
    valid = _lane_iota((1, LANES)) < N_EXPERTS
    for i, rs in enumerate(subs):
        lg = jnp.where(valid, logits[i], -jnp.inf)
        e = jnp.exp(lg - jnp.max(lg, axis=-1, keepdims=True))
        aff_ref[:, rs] = (e / jnp.sum(e, axis=-1, keepdims=True)).T[0:N_EXPERTS, :]


def _merge(hmf, hmb, slab, mlw, yb, hrf, hrb, rnw, wbr, wout, x, mods, n2w, rw, g8, g8t, nlt):
    bsz, ltot, _ = x.shape
    hw = ML_HEADS * ML_DH

    def mrow(bb, i):
        return (jnp.where(i < nlt, bb, bsz), 0, 0)

    full = lambda shape: pl.BlockSpec(shape, lambda bb, i: (0,) * len(shape))
    row = lambda width, off: pl.BlockSpec((None, TM, width), lambda bb, i: (bb, i, off // width))
    return pl.pallas_call(
        _merge_kernel,
        grid=(bsz, ltot // TM),
        in_specs=[row(hw, 0), row(hw, 0),
                  row(hw, OFF_OG), full((1, hw)),
                  row(hw, 0),
                  row(hw, 0), row(hw, 0),
                  row(hw, OFF_RG), full((1, hw)),
                  row(N_BRANCH * D_MODEL, OFF_BL),
                  full((N_BRANCH, BRANCH_W, D_MODEL)), full((D_MODEL, D_MODEL)),
                  row(D_MODEL, 0),
                  pl.BlockSpec((None, 1, 6 * D_MODEL), mrow),
                  full((1, D_MODEL)), full((2, D_MODEL, LANES)), full((hw, LANES)), full((LANES, hw))],
        out_specs=[row(D_MODEL, 0), row(D_MODEL, 0), pl.BlockSpec((None, N_EXPERTS, TM), lambda bb, i: (bb, 0, i))],
        out_shape=[jax.ShapeDtypeStruct((bsz, ltot, D_MODEL), F32),
                   jax.ShapeDtypeStruct((bsz, ltot, D_MODEL), BF16),
                   jax.ShapeDtypeStruct((bsz, N_EXPERTS, ltot), F32)],
        scratch_shapes=[pltpu.VMEM((N_BRANCH, BRANCH_W, D_MODEL), BF16), pltpu.VMEM((D_MODEL, D_MODEL), BF16)],
        compiler_params=_params(("arbitrary", "arbitrary")),
        name="merge",
    )(hmf, hmb, slab, mlw, yb, hrf, hrb, slab, rnw, slab, wbr, wout, x, mods, n2w, rw, g8, g8t)


def _select_kernel(aff_ref, pos_ref, post_ref, gwt_ref, off_ref, *, n, cap, base0, base_step):
    b = pl.program_id(0)
    base = (base0 + b * base_step).astype(F32)
    bits = lax.bitcast_convert_type(aff_ref[...], jnp.int32)
    capf = jnp.float32(cap)

    def search(it, cur):
        cand = cur | jnp.left_shift(jnp.int32(1), 30 - it)
        cnt = jnp.sum(jnp.where(bits >= cand, 1.0, 0.0), axis=1, keepdims=True)
        return jnp.where(cnt >= capf, cand, cur)

    thr = lax.fori_loop(0, 31, search, jnp.zeros((N_EXPERTS, 1), jnp.int32))
    n_gt = jnp.sum(jnp.where(bits > thr, 1.0, 0.0), axis=1, keepdims=True)
    need = capf - n_gt

    ri = lax.broadcasted_iota(jnp.int32, (TM, TM), 0)
    ci = lax.broadcasted_iota(jnp.int32, (TM, TM), 1)
    before = (ri < ci).astype(BF16)
    half_lane = _lane_iota((N_EXPERTS, LANES))
    unused = jnp.full((LANES - N_EXPERTS, TM), -1.0, F32)

    def tile(i, carry):
        c_eq, c_sel, offs = carry
        r0 = pl.multiple_of(i * TM, TM)
        a = aff_ref[:, pl.ds(r0, TM)]
        bt = lax.bitcast_convert_type(a, jnp.int32)
        eq = bt == thr
        rank = _dot(jnp.where(eq, 1.0, 0.0).astype(BF16), before) + c_eq
        sel = (bt > thr) | (eq & (rank < need))
        self_ = jnp.where(sel, 1.0, 0.0)
        pos = _dot(self_.astype(BF16), before) + c_sel
        posv = jnp.where(sel, pos + base, -1.0)
        post_ref[:, pl.ds(r0, TM)] = posv
        gwt_ref[:, pl.ds(r0, TM)] = jnp.where(sel, a, 0.0)
        pos_ref[pl.ds(r0, TM), :] = jnp.concatenate([posv, unused], axis=0).T
        c_half = c_sel + jnp.sum(self_[:, 0:TM // 2], axis=1, keepdims=True)
        offs = jnp.where(half_lane == 2 * i, c_sel, offs)
        offs = jnp.where(half_lane == 2 * i + 1, c_half, offs)
        return (c_eq + jnp.sum(jnp.where(eq, 1.0, 0.0), axis=1, keepdims=True),
                c_sel + jnp.sum(self_, axis=1, keepdims=True), offs)

    zero = jnp.zeros((N_EXPERTS, 1), F32)
    _, _, offs = lax.fori_loop(0, n // TM, tile, (zero, zero, jnp.zeros((N_EXPERTS, LANES), F32)))
    off_ref[...] = offs


def _select(afft, n, cap, tile_off, base0, base_step):
    bsz = afft.shape[0]
    assert 2 * (n // TM) <= LANES
    kern = functools.partial(_select_kernel, n=n, cap=cap, base0=base0, base_step=base_step)
    return pl.pallas_call(
        kern,
        grid=(bsz,),
        in_specs=[pl.BlockSpec((None, N_EXPERTS, n), lambda bb: (bb, 0, tile_off))],
        out_specs=[pl.BlockSpec((None, n, LANES), lambda bb: (bb, 0, 0)),
                   pl.BlockSpec((None, N_EXPERTS, n), lambda bb: (bb, 0, 0)),
                   pl.BlockSpec((None, N_EXPERTS, n), lambda bb: (bb, 0, 0)),
                   pl.BlockSpec((None, N_EXPERTS, LANES), lambda bb: (bb, 0, 0))],
        out_shape=[jax.ShapeDtypeStruct((bsz, n, LANES), F32),
                   jax.ShapeDtypeStruct((bsz, N_EXPERTS, n), F32),
                   jax.ShapeDtypeStruct((bsz, N_EXPERTS, n), F32),
                   jax.ShapeDtypeStruct((bsz, N_EXPERTS, LANES), F32)],
        compiler_params=_params(("arbitrary",)),
        name="select",
    )(afft)


SLOT_ALIGN = 16
WIN = TM + SLOT_ALIGN
TC = TM // 2
WINC = TC + SLOT_ALIGN
GATHER_MAX_TILES = 11
FFN_ROW_CHUNKS = 4
FFN_TF = 256


def _moe_gather_kernel(st_ref, h_ref, pt_ref, gt_ref, xs_ref, gs_ref, *, nt256, tiles):
    e = pl.program_id(0)
    i = pl.program_id(1)

    @pl.when(i == 0)
    def _():
        xs_ref[...] = jnp.zeros_like(xs_ref)
        gs_ref[...] = jnp.zeros_like(gs_ref)

    slot = lax.broadcasted_iota(jnp.int32, (WIN, TM), 0)
    for u in range(tiles):
        st = pl.multiple_of(st_ref[e * nt256 + i * tiles + u], SLOT_ALIGN)
        st2 = pl.multiple_of(st + SLOT_ALIGN, SLOT_ALIGN)
        match = (slot + st).astype(F32) == pt_ref[:, u * TM:(u + 1) * TM]
        onehot = jnp.where(match, 1.0, 0.0).astype(BF16)
        got = _dot(onehot, h_ref[u * TM:(u + 1) * TM, :])
        head = xs_ref[pl.ds(st, SLOT_ALIGN), :].astype(F32)
        xs_ref[pl.ds(st, SLOT_ALIGN), :] = (head + got[0:SLOT_ALIGN]).astype(BF16)
        xs_ref[pl.ds(st2, TM), :] = got[SLOT_ALIGN:WIN].astype(BF16)
        gate = jnp.sum(jnp.where(match, gt_ref[:, u * TM:(u + 1) * TM], 0.0), axis=1, keepdims=True)
        gate = jnp.broadcast_to(gate, (WIN, LANES))
        gs_ref[pl.ds(st, SLOT_ALIGN), :] = gs_ref[pl.ds(st, SLOT_ALIGN), :] + gate[0:SLOT_ALIGN]
        gs_ref[pl.ds(st2, TM), :] = gate[SLOT_ALIGN:WIN]


def _moe_gather(starts, h2, post, gwt, rows, tiles):
    ttot = h2.shape[0]
    tt = tiles * TM
    kern = functools.partial(_moe_gather_kernel, nt256=ttot // TM, tiles=tiles)
    return pl.pallas_call(
        kern,
        grid_spec=pltpu.PrefetchScalarGridSpec(
            num_scalar_prefetch=1,
            grid=(N_EXPERTS, ttot // tt),
            in_specs=[pl.BlockSpec((tt, D_MODEL), lambda e, i, st: (i, 0)),
                      pl.BlockSpec((None, 1, tt), lambda e, i, st: (e, 0, i)),
                      pl.BlockSpec((None, 1, tt), lambda e, i, st: (e, 0, i))],
            out_specs=[pl.BlockSpec((None, rows, D_MODEL), lambda e, i, st: (e, 0, 0)),
                       pl.BlockSpec((None, rows, LANES), lambda e, i, st: (e, 0, 0))]),
        out_shape=[jax.ShapeDtypeStruct((N_EXPERTS, rows, D_MODEL), BF16),
                   jax.ShapeDtypeStruct((N_EXPERTS, rows, LANES), F32)],
        compiler_params=_params(("arbitrary", "arbitrary")),
        name="moe_gather",
    )(starts, h2, post, gwt)


def _moe_ffn_kernel(xs_ref, gs_ref, w1_ref, w3_ref, w2_ref, ys_ref, acc_ref, *, ct, rows):
    f = pl.program_id(1)

    @pl.when(f == 0)
    def _():
        acc_ref[...] = jnp.zeros_like(acc_ref)

    w1b = w1_ref[...].astype(BF16)
    w3b = w3_ref[...].astype(BF16)
    w2b = w2_ref[...].astype(BF16)
    rc = ct // FFN_ROW_CHUNKS
    chunks = [slice(r * rc, (r + 1) * rc) for r in range(FFN_ROW_CHUNKS)]
    up = [(_dot(xs_ref[chunks[0], :], w1b), _dot(xs_ref[chunks[0], :], w3b))]
    for r, rs in enumerate(chunks):
        a, b = up[r]
        if r + 1 < len(chunks):
            up.append((_dot(xs_ref[chunks[r + 1], :], w1b), _dot(xs_ref[chunks[r + 1], :], w3b)))
        acc_ref[rs, :] += _dot((_silu(a) * b).astype(BF16), w2b)

    @pl.when(f == pl.num_programs(1) - 1)
    def _():
        gate = gs_ref[...]
        for c in range(D_MODEL // LANES):
            cs = slice(c * LANES, (c + 1) * LANES)
            ys_ref[0:ct, cs] = (acc_ref[:, cs] * gate).astype(BF16)
        ys_ref[ct:rows, :] = jnp.zeros((rows - ct, D_MODEL), BF16)


def _moe_ffn(xs, gs, w1, w3, w2, layer, ct):
    rows = xs.shape[1]
    kern = functools.partial(_moe_ffn_kernel, ct=ct, rows=rows)
    return pl.pallas_call(
        kern,
        grid=(N_EXPERTS, EXPERT_FF // FFN_TF),
        in_specs=[pl.BlockSpec((None, ct, D_MODEL), lambda e, f: (e, 0, 0)),
                  pl.BlockSpec((None, ct, LANES), lambda e, f: (e, 0, 0)),
                  pl.BlockSpec((None, None, D_MODEL, FFN_TF), lambda e, f: (layer, e, 0, f)),
                  pl.BlockSpec((None, None, D_MODEL, FFN_TF), lambda e, f: (layer, e, 0, f)),
                  pl.BlockSpec((None, None, FFN_TF, D_MODEL), lambda e, f: (layer, e, f, 0))],
        out_specs=pl.BlockSpec((None, rows, D_MODEL), lambda e, f: (e, 0, 0)),
        out_shape=jax.ShapeDtypeStruct((N_EXPERTS, rows, D_MODEL), BF16),
        scratch_shapes=[pltpu.VMEM((ct, D_MODEL), F32)],
        compiler_params=_params(("arbitrary", "arbitrary")),
        name="moe_ffn",
    )(xs, gs, w1, w3, w2)


def _combine_kernel(st_ref, x_ref, pos_ref, mod_ref, *rest):
    nsub = TM // TC
    ys_refs = rest[:N_EXPERTS * nsub]
    o_ref = rest[N_EXPERTS * nsub]
    i = pl.program_id(0)
    ntc = pl.num_programs(0) * nsub
    slot = lax.broadcasted_iota(jnp.int32, (TC, WINC), 1)
    g2 = mod_ref[:, 5 * D_MODEL:6 * D_MODEL]
    for u in range(nsub):
        rs = slice(u * TC, (u + 1) * TC)
        pos = pos_ref[rs, :]
        acc = None
        for e in range(N_EXPERTS):
            st = st_ref[e * ntc + i * nsub + u]
            onehot = jnp.where((slot + st).astype(F32) == pos[:, e:e + 1], 1.0, 0.0).astype(BF16)
            part = _dot(onehot, ys_refs[e * nsub + u][0])
            acc = part if acc is None else acc + part
        o_ref[rs, :] = x_ref[rs, :] + g2 * acc


def _combine(starts, x, pos, mods_rows, ys, nlt, ltiles):
    ttot = x.shape[0]
    nt = ttot // TM
    nsub = TM // TC
    bsz = ttot // (ltiles * TM)

    def mrow(i, st):
        return (jnp.where(i % ltiles < nlt, i // ltiles, bsz), 0, 0)

    def ys_spec(e, u):
        return pl.BlockSpec((pl.Element(1), pl.Element(WINC), pl.Element(D_MODEL)),
                            lambda i, st: (e, pl.multiple_of(st[e * nt * nsub + i * nsub + u], SLOT_ALIGN), 0))

    return pl.pallas_call(
        _combine_kernel,
        grid_spec=pltpu.PrefetchScalarGridSpec(
            num_scalar_prefetch=1,
            grid=(nt,),
            in_specs=[pl.BlockSpec((TM, D_MODEL), lambda i, st: (i, 0)),
                      pl.BlockSpec((TM, LANES), lambda i, st: (i, 0)),
                      pl.BlockSpec((None, 1, 6 * D_MODEL), mrow)]
                     + [ys_spec(e, u) for e in range(N_EXPERTS) for u in range(nsub)],
            out_specs=pl.BlockSpec((TM, D_MODEL), lambda i, st: (i, 0))),
        out_shape=jax.ShapeDtypeStruct((ttot, D_MODEL), F32),
        compiler_params=_params(("arbitrary",)),
        name="combine",
    )(starts, x, pos, mods_rows, *([ys] * (N_EXPERTS * nsub)))


def _permute_w_in(w, b):
    widths = (2 * ML_HEADS * ML_DH, ML_HEADS * ML_DH, ML_HEADS * ML_DH, 4 * ML_HEADS, Q_LORA, KV_LORA, ROPE_D,
              RET_HEADS * RET_DK, RET_HEADS * RET_DK, RET_HEADS * RET_DV, RET_HEADS * RET_DV, N_BRANCH * D_MODEL)
    offs = np.concatenate([[0], np.cumsum(widths)])
    seg = lambda a, k: a[..., offs[k]:offs[k + 1]]
    order = (11, 0, 1, 2, 9, 10, 7, 8, 5, 4, 3, 6)

    def build(a):
        parts = [seg(a, k) for k in order]
        used = sum(p.shape[-1] for p in parts)
        parts.append(jnp.zeros(a.shape[:-1] + (NCOL - used,), a.dtype))
        return jnp.concatenate(parts, axis=-1)

    return build(w), build(b)


def kernel(x, c, ctx, c_ctx, ada_w, ada_b, norm1_w, norm2_w, w_in, b_in, conv_w, conv_b, ml_norm_w, mla_qa_norm,
           mla_wq_b, mla_kva_norm, mla_wkv_b, q_norm_w, k_norm_w, ret_norm_w, w_branch, w_out, router_w,
           exp_w1, exp_w3, exp_w2):
    bsz, n_lat, _ = x.shape
    n_ctx = ctx.shape[1]
    depth = ada_w.shape[0]
    ltot = n_lat + n_ctx
    nlt = n_lat // TM
    ltiles = ltot // TM
    nl, ncx = n_lat // CHUNK, n_ctx // CHUNK
    assert n_lat % TM == 0 and n_ctx == TM and bsz + 1 <= 8
    gather_tiles = max(t for t in range(1, GATHER_MAX_TILES + 1) if (bsz * ltiles) % t == 0)

    cond8 = jnp.zeros((8, D_MODEL), F32).at[:bsz].set(c).at[bsz].set(c_ctx)
    mods_all = _ada(cond8, ada_w, ada_b)

    rope32 = _rope_tables(n_lat, n_ctx)
    rope_r = jnp.tile(rope32, (1, 1, LANES // ROPE_D))
    ident = jnp.stack([jnp.ones((ltot, 1), F32), jnp.zeros((ltot, 1), F32), jnp.zeros((ltot, 1), F32)])
    rope_k = jnp.concatenate([jnp.broadcast_to(ident, (3, ltot, NOPE_D)), rope32,
                              jnp.broadcast_to(ident, (3, ltot, LANES - QK_D))], axis=2)
    rope_q = jnp.swapaxes(rope_k, 1, 2)
    dmat, xi, kd, cd = _ret_tables()

    gi = jnp.arange(ML_HEADS * ML_DH) // ML_DH
    g8 = (gi[:, None] == jnp.arange(LANES)[None, :]).astype(BF16)
    g8t = g8.T

    cap_l = CAP_FACTOR * n_lat // N_EXPERTS
    cap_c = CAP_FACTOR * n_ctx // N_EXPERTS
    ct = bsz * (cap_l + cap_c)
    assert ct % (16 * FFN_ROW_CHUNKS) == 0 and cap_l % 16 == 0 and cap_c % 16 == 0
    rows = ct + WIN

    w_perm, b_perm = _permute_w_in(w_in, b_in[:, None, :])
    w_perm = w_perm.astype(BF16)
    xx = jnp.concatenate([x, ctx], axis=1)
    att_r = next(r for r in (11, 3, 1) if ltiles % r == 0)
    tq = min(2048, n_lat)

    for l in range(depth):
        mods = mods_all[l].reshape(8, 1, 6 * D_MODEL)
        slab = _inproj(xx, mods, norm1_w[l][None, :], w_perm, b_perm, l, nlt)

        mq, mk = _qkconv(slab, conv_w[l], conv_b[l][None, :], nlt)
        hmf, hmb, hrf, hrb = _scans(mq, mk, slab, rope_r, dmat, xi, kd, cd, nl, ncx)

        wq = mla_wq_b[l].reshape(Q_LORA, MLA_HEADS, QK_D)
        wqt = jnp.pad(wq, ((0, 0), (0, 0), (0, LANES - QK_D))).reshape(Q_LORA, MLA_HEADS * LANES).T
        wkv = mla_wkv_b[l].reshape(KV_LORA, MLA_HEADS, NOPE_D + MLA_DV)
        wk = jnp.pad(wkv[:, :, :NOPE_D], ((0, 0), (0, 0), (0, LANES - NOPE_D))).reshape(KV_LORA, MLA_HEADS * LANES)
        wvt = wkv[:, :, NOPE_D:].reshape(KV_LORA, MLA_HEADS * MLA_DV).T
        qnw = jnp.pad(q_norm_w[l], (0, LANES - QK_D))[:, None]
        knw = jnp.pad(k_norm_w[l], (0, LANES - QK_D))[None, :]
        qt, kk, vt = _mla(slab, mla_qa_norm[l][None, :], wqt, mla_kva_norm[l][None, :], wk, wvt, qnw, knw,
                          rope_k, rope_q)
        yb_l = _attn(qt, kk, vt, tq, 0, n_lat // tq, 0, att_r, ltiles // att_r)
        yb_c = _attn(qt, kk, vt, TM, nlt, 1, nlt, 1, 1)
        yb = jnp.concatenate([yb_l, yb_c], axis=1)

        rw = jnp.stack(_split2(jnp.pad(router_w[l], ((0, 0), (0, LANES - N_EXPERTS)))))
        xm, h2, aff = _merge(hmf, hmb, slab, ml_norm_w[l][None, :], yb, hrf, hrb, ret_norm_w[l][None, :], w_branch[l],
                             w_out[l], xx, mods, norm2_w[l][None, :], rw, g8, g8t, nlt)

        pos_l, post_l, gwt_l, off_l = _select(aff, n_lat, cap_l, 0, 0, cap_l + cap_c)
        pos_c, post_c, gwt_c, off_c = _select(aff, n_ctx, cap_c, nlt, cap_l, cap_l + cap_c)
        pos = jnp.concatenate([pos_l, pos_c], axis=1).reshape(bsz * ltot, LANES)
        expert_major = lambda a_l, a_c: jnp.swapaxes(jnp.concatenate([a_l, a_c], axis=2), 0, 1).reshape(
            N_EXPERTS, 1, bsz * ltot)
        post = expert_major(post_l, post_c)
        gwt = expert_major(gwt_l, gwt_c)
        base_l = (jnp.arange(bsz) * (cap_l + cap_c))[:, None, None]
        base_c = base_l + cap_l
        off = jnp.concatenate(
            [off_l[:, :, :2 * nlt].astype(jnp.int32) // SLOT_ALIGN * SLOT_ALIGN + base_l,
             off_c[:, :, :2].astype(jnp.int32) // SLOT_ALIGN * SLOT_ALIGN + base_c], axis=2)
        starts_c = jnp.swapaxes(off, 0, 1).reshape(-1)
        starts_g = jnp.swapaxes(off[:, :, ::2], 0, 1).reshape(-1)

        xs, gs = _moe_gather(starts_g, h2.reshape(bsz * ltot, D_MODEL), post, gwt, rows, gather_tiles)
        ys = _moe_ffn(xs, gs, exp_w1, exp_w3, exp_w2, l, ct)
        xx = _combine(starts_c, xm.reshape(bsz * ltot, D_MODEL), pos, mods, ys, nlt, ltiles)
        xx = xx.reshape(bsz, ltot, D_MODEL)

    return xx[:, :n_lat]
```

```python
import functools

import jax
import jax.numpy as jnp
import numpy as np
from jax import lax
from jax.experimental import pallas as pl
from jax.experimental.pallas import tpu as pltpu

F32 = jnp.float32
BF16 = jnp.bfloat16
HIGHEST = lax.Precision.HIGHEST

D_MODEL = 1024
GRID_W = 64
N_BRANCH = 3
BRANCH_W = 512
ML_HEADS = 8
ML_DH = 64
CONV_K = 5
MLA_HEADS = 8
Q_LORA = 384
KV_LORA = 256
NOPE_D = 64
ROPE_D = 32
QK_D = NOPE_D + ROPE_D
MLA_DV = 64
RET_HEADS = 8
RET_DK = 32
RET_DV = 64
N_EXPERTS = 16
EXPERT_FF = 1024
CAP_FACTOR = 2
CHUNK = 128
ROPE_BASE = 10000.0
EPS = 1e-6
LOG2E = 1.4426950408889634

LANES = 128
TM = 256
ROW_SUB = 128
VMEM_LIMIT = 56 * 1024 * 1024

OFF_BL, OFF_V, OFF_OG, OFF_RV, OFF_RG = 0, 3072, 3584, 4096, 4608
OFF_RQ, OFF_RK, OFF_KV, OFF_QL, OFF_MISC = 5120, 5376, 5632, 6144, 6528
SLAB_W = 6656
NCOL = SLAB_W + 2 * ML_HEADS * ML_DH
TN_IN = SLAB_W // 2


def _sigmoid(x):
    return 1.0 / (1.0 + jnp.exp(-x))


def _silu(x):
    return x * _sigmoid(x)


def _log_sigmoid(x):
    return jnp.minimum(x, 0.0) - jnp.log1p(jnp.exp(-jnp.abs(x)))


def _dot(a, b, **kw):
    return jnp.dot(a, b, preferred_element_type=F32, **kw)


def _dot_nt(a, b, **kw):
    return lax.dot_general(a, b, (((1,), (1,)), ((), ())), preferred_element_type=F32, **kw)


def _dot_tn(a, b, **kw):
    return lax.dot_general(a, b, (((0,), (0,)), ((), ())), preferred_element_type=F32, **kw)


def _params(sem):
    return pltpu.CompilerParams(dimension_semantics=sem, vmem_limit_bytes=VMEM_LIMIT)


def _ada_kernel(c_ref, w_ref, b_ref, o_ref):
    s = _silu(c_ref[...])
    o_ref[...] = _dot(s, w_ref[...], precision=HIGHEST) + b_ref[...]


def _ada(cond8, ada_w, ada_b):
    depth = ada_w.shape[0]
    tn = 1536
    return pl.pallas_call(
        _ada_kernel,
        grid=(depth, 6 * D_MODEL // tn),
        in_specs=[pl.BlockSpec((8, D_MODEL), lambda l, j: (0, 0)),
                  pl.BlockSpec((None, D_MODEL, tn), lambda l, j: (l, 0, j)),
                  pl.BlockSpec((None, 1, tn), lambda l, j: (l, 0, j))],
        out_specs=pl.BlockSpec((None, 8, tn), lambda l, j: (l, 0, j)),
        out_shape=jax.ShapeDtypeStruct((depth, 8, 6 * D_MODEL), F32),
        compiler_params=_params(("arbitrary", "arbitrary")),
        name="ada",
    )(cond8, ada_w, ada_b.reshape(depth, 1, 6 * D_MODEL))


def _inproj_kernel(x_ref, xp_ref, xn_ref, mod_ref, nw_ref, w_ref, b_ref, cw_ref, cb_ref, o_ref, q_ref, k_ref, xe_ref,
                   *, nlt, ltiles):
    i = pl.program_id(1)
    first = (i == 0) | (i == nlt)
    last = (i == nlt - 1) | (i == ltiles - 1)
    sh = mod_ref[:, 0:D_MODEL]
    sc = mod_ref[:, D_MODEL:2 * D_MODEL]
    hw = ML_HEADS * ML_DH

    def hidden(x):
        hn = x * lax.rsqrt(jnp.mean(x * x, axis=-1, keepdims=True) + EPS) * nw_ref[...]
        return (hn * (1.0 + sc) + sh).astype(BF16)

    def qk_cols(h):
        return _dot(h, w_ref[:, SLAB_W:NCOL]) + b_ref[:, SLAB_W:NCOL]

    subs = list(range(0, TM, ROW_SUB))
    halo = qk_cols(hidden(jnp.concatenate([xp_ref[...], xn_ref[...]], axis=0)))
    xe_ref[0:8, :] = jnp.where(first, 0.0, halo[0:8])
    xe_ref[8 + TM:16 + TM, :] = jnp.where(last, 0.0, halo[8:16])
    hs = [hidden(x_ref[r0:r0 + ROW_SUB, :]) for r0 in subs]
    for r0, h in zip(subs, hs):
        xe_ref[8 + r0:8 + r0 + ROW_SUB, :] = qk_cols(h)
    for r0, h in zip(subs, hs):
        for c0 in range(0, SLAB_W, TN_IN):
            o_ref[r0:r0 + ROW_SUB, c0:c0 + TN_IN] = _dot(h, w_ref[:, c0:c0 + TN_IN]) + b_ref[:, c0:c0 + TN_IN]
    acc = jnp.broadcast_to(cb_ref[...], (TM, 2 * hw))
    for j in range(CONV_K):
        acc = acc + xe_ref[8 - CONV_K // 2 + j:8 - CONV_K // 2 + j + TM, :] * cw_ref[j:j + 1, :]
    qk = _silu(acc)
    q_ref[...] = qk[:, :hw].astype(BF16)
    k_ref[...] = (qk[:, hw:] * (ML_DH ** -0.5)).astype(BF16)


def _inproj(x, mods, nw, w_all, b_all, conv_w, conv_b, layer, nlt):
    bsz, ltot, _ = x.shape
    hw = ML_HEADS * ML_DH
    ltiles = ltot // TM
    nrow8 = ltot // 8

    def mrow(bb, i):
        return (jnp.where(i < nlt, bb, bsz), 0, 0)

    qk_out = pl.BlockSpec((None, TM, hw), lambda bb, i: (bb, i, 0))
    return pl.pallas_call(
        functools.partial(_inproj_kernel, nlt=nlt, ltiles=ltiles),
        grid=(bsz, ltiles),
        in_specs=[pl.BlockSpec((None, TM, D_MODEL), lambda bb, i: (bb, i, 0)),
                  pl.BlockSpec((None, 8, D_MODEL), lambda bb, i: (bb, jnp.maximum(i * (TM // 8) - 1, 0), 0)),
                  pl.BlockSpec((None, 8, D_MODEL), lambda bb, i: (bb, jnp.minimum((i + 1) * (TM // 8), nrow8 - 1), 0)),
                  pl.BlockSpec((None, 1, 6 * D_MODEL), mrow),
                  pl.BlockSpec((1, D_MODEL), lambda bb, i: (0, 0)),
                  pl.BlockSpec((None, D_MODEL, NCOL), lambda bb, i: (layer, 0, 0)),
                  pl.BlockSpec((None, 1, NCOL), lambda bb, i: (layer, 0, 0)),
                  pl.BlockSpec((CONV_K, 2 * hw), lambda bb, i: (0, 0)),
                  pl.BlockSpec((1, 2 * hw), lambda bb, i: (0, 0))],
        out_specs=[pl.BlockSpec((None, TM, SLAB_W), lambda bb, i: (bb, i, 0)), qk_out, qk_out],
        out_shape=[jax.ShapeDtypeStruct((bsz, ltot, SLAB_W), F32),
                   jax.ShapeDtypeStruct((bsz, ltot, hw), BF16), jax.ShapeDtypeStruct((bsz, ltot, hw), BF16)],
        scratch_shapes=[pltpu.VMEM((TM + 16, 2 * hw), F32)],
        compiler_params=_params(("arbitrary", "arbitrary")),
        name="inproj",
    )(x, x, x, mods, nw, w_all, b_all, conv_w, conv_b)


def _chunk_of(d, s, nl, ncx):
    fwd = jnp.where(s < ncx, nl + s, s - ncx)
    bwd = jnp.where(s < ncx, nl + ncx - 1 - s, nl - 1 - (s - ncx))
    return jnp.where(d == 0, fwd, bwd)


def _lane_iota(shape):
    return lax.broadcasted_iota(jnp.int32, shape, len(shape) - 1)


def _split2(x):
    hi = x.astype(BF16)
    return hi, (x - hi.astype(F32)).astype(BF16)


def _split3(x):
    hi = x.astype(BF16)
    r1 = x - hi.astype(F32)
    mid = r1.astype(BF16)
    return hi, mid, (r1 - mid.astype(F32)).astype(BF16)


def _select_dot(x, sel_b):
    hi, lo = _split2(x)
    return _dot(hi, sel_b) + _dot(lo, sel_b)


GATE_LANE0 = ML_HEADS


def _mlstm2_kernel(qf_ref, kf_ref, vf_ref, gf_ref, qb_ref, kb_ref, vb_ref, gb_ref, of_ref, ob_ref, st_ref, m_ref):
    s = pl.program_id(1)

    @pl.when(s == 0)
    def _():
        st_ref[...] = jnp.zeros_like(st_ref)
        m_ref[...] = jnp.zeros_like(m_ref)

    dirs = (0, 1)
    pairs = range(ML_HEADS // 2)
    heads = range(ML_HEADS)
    q_refs, k_refs, v_refs, g_refs, o_refs = (qf_ref, qb_ref), (kf_ref, kb_ref), (vf_ref, vb_ref), (gf_ref, gb_ref), \
        (of_ref, ob_ref)
    lane = _lane_iota((1, LANES))
    head_lane = (lane >= GATE_LANE0) & (lane < GATE_LANE0 + ML_HEADS)
    lo_half = lane < ML_DH
    ti = lax.broadcasted_iota(jnp.int32, (CHUNK, CHUNK), 0)
    si = lax.broadcasted_iota(jnp.int32, (CHUNK, CHUNK), 1)
    causal = (ti >= si, ti <= si)
    tri = [jnp.where(causal[d], 1.0, 0.0).astype(BF16) for d in dirs]
    tri_t = [jnp.where(causal[1 - d], 1.0, 0.0).astype(BF16) for d in dirs]
    s8 = _lane_iota((ML_HEADS, CHUNK))
    gi = lax.broadcasted_iota(jnp.int32, (LANES, 2 * LANES), 0)
    li = lax.broadcasted_iota(jnp.int32, (LANES, 2 * LANES), 1)
    li_in = jnp.where(li >= LANES, li - LANES, li)
    same_head = (gi < ML_DH) == (li_in < ML_DH)
    lane2 = _lane_iota((1, 2 * LANES))
    lo_half2 = jnp.where(lane2 >= LANES, lane2 - LANES, lane2) < ML_DH
    hmask = [lo_half if h % 2 == 0 else jnp.logical_not(lo_half) for h in heads]
    ones_b = jnp.ones((CHUNK, LANES), BF16)
    sel_full = [jnp.where(gi == GATE_LANE0 + 2 * p + jnp.where(li >= LANES, 1, 0), 1.0, 0.0).astype(BF16)
                for p in pairs]
    quarter = jnp.where(li >= LANES, 2, 0) + jnp.where(li_in >= ML_DH, 1, 0)
    sel_half = [jnp.where(gi == GATE_LANE0 + 4 * q + quarter, 1.0, 0.0).astype(BF16)
                for q in range(ML_HEADS // 4)]
    t16 = lax.broadcasted_iota(jnp.int32, (16, LANES), 0)

    g_ig, ls = [], []
    for d in dirs:
        g_raw = g_refs[d][...]
        g_fg = g_raw if d == 0 else pltpu.roll(g_raw, LANES - 2 * ML_HEADS, axis=1)
        g_ig.append(jnp.where(head_lane, pltpu.roll(g_fg, ML_HEADS, axis=1), 0.0))
        ls.append(jnp.where(head_lane, _log_sigmoid(g_fg), 0.0))
    ls3 = [_split3(ls[d]) for d in dirs]
    lst3 = [_split3(ls[d].T) for d in dirs]
    b_cols = [sum(_dot(tri[d], t) for t in ls3[d]) for d in dirs]
    b_rows = [sum(_dot(t, tri_t[d]) for t in lst3[d]) for d in dirs]
    yield

    qp = [[q_refs[d][:, p * LANES:(p + 1) * LANES] for p in pairs] for d in dirs]
    kp = [[k_refs[d][:, p * LANES:(p + 1) * LANES] for p in pairs] for d in dirs]
    va = [[jnp.concatenate([v_refs[d][:, p * LANES:(p + 1) * LANES].astype(BF16), ones_b], axis=1) for p in pairs]
          for d in dirs]
    st = [[st_ref[d, p] for p in pairs] for d in dirs]
    qs = [[_dot(qp[d][p], st[d][p].astype(BF16)) for p in pairs] for d in dirs]
    qk = [[_dot_nt(jnp.where(hmask[h], qp[d][h // 2].astype(F32), 0.0).astype(BF16), kp[d][h // 2])
           for h in heads] for d in dirs]
    yield

    r_rows, cdiff, stacked = [], [], []
    for d in dirs:
        r_rows.append(g_ig[d].T - b_rows[d])
        bl_row = b_cols[d][CHUNK - 1:CHUNK, :] if d == 0 else b_cols[d][0:1, :]
        m_row = m_ref[d]
        cm8 = r_rows[d][GATE_LANE0:GATE_LANE0 + ML_HEADS, :]
        order = s8 if d == 0 else CHUNK - 1 - s8
        step = 1
        while step < CHUNK:
            shifted = pltpu.roll(cm8, step if d == 0 else CHUNK - step, axis=1)
            cm8 = jnp.where(order >= step, jnp.maximum(cm8, shifted), cm8)
            step *= 2
        cm = jnp.concatenate([jnp.zeros((GATE_LANE0, CHUNK), F32), cm8,
                              jnp.zeros((LANES - GATE_LANE0 - ML_HEADS, CHUNK), F32)], axis=0).T
        a_cols = b_cols[d] + m_row
        mt_cols = jnp.maximum(a_cols, b_cols[d] + cm)
        g_cols = bl_row - b_cols[d] + g_ig[d]
        m_new = jnp.maximum(bl_row + m_row, jnp.max(g_cols, axis=0, keepdims=True))
        dp_cols = jnp.where(t16 >= 0, jnp.exp(bl_row + m_row - m_new), 0.0)
        m_ref[d] = m_new
        cdiff.append(b_cols[d] - mt_cols)
        stacked.append(jnp.concatenate([jnp.exp(a_cols - mt_cols), jnp.exp(-mt_cols),
                                        jnp.exp(g_cols - m_new), dp_cols], axis=0))

    c_full = [[_select_dot(cdiff[d], sel_full[p]) for p in pairs] for d in dirs]
    halves2 = [[_select_dot(stacked[d], sel_half[q]) for q in range(ML_HEADS // 4)] for d in dirs]
    halves = [[halves2[d][p // 2][:, (p % 2) * LANES:(p % 2 + 1) * LANES] for p in pairs]
              for d in dirs]
    yield

    sc = [[None] * ML_HEADS for _ in dirs]
    for d in dirs:
        for h in heads:
            p, j = h // 2, h % 2
            dlog = c_full[d][p][:, j * LANES:(j + 1) * LANES] + r_rows[d][GATE_LANE0 + h:GATE_LANE0 + h + 1, :]
            sc[d][h] = (qk[d][h] * jnp.exp(jnp.where(causal[d], dlog, -jnp.inf))).astype(BF16)
    sv = [[_dot(sc[d][h], va[d][h // 2]) for h in heads] for d in dirs]
    upd = [[_dot_tn((kp[d][p].astype(F32) * halves[d][p][2 * CHUNK:3 * CHUNK]).astype(BF16), va[d][p])
            for p in pairs] for d in dirs]
    yield

    for d in dirs:
        outs = []
        for p in pairs:
            inter_pair = halves[d][p][0:CHUNK]
            emt_pair = halves[d][p][CHUNK:2 * CHUNK]
            dp_pair = halves[d][p][3 * CHUNK:3 * CHUNK + 1]
            sv_pair = jnp.where(lo_half2, sv[d][2 * p], sv[d][2 * p + 1])
            num = sv_pair[:, :LANES] + inter_pair * qs[d][p][:, :LANES]
            den = sv_pair[:, LANES:] + inter_pair * qs[d][p][:, LANES:]
            outs.append(num / jnp.maximum(jnp.abs(den), emt_pair))
            st_ref[d, p] = (jnp.concatenate([dp_pair, dp_pair], axis=1) * st[d][p]
                            + jnp.where(same_head, upd[d][p], 0.0))
        o_refs[d][...] = jnp.concatenate(outs, axis=1)


N_MLSTM_IN, N_RET_IN = 8, 12


def _scan_kernel(*refs):
    n_in = N_MLSTM_IN + N_RET_IN
    ml_in, rt_in = refs[:N_MLSTM_IN], refs[N_MLSTM_IN:n_in]
    mo_f, mo_b, ro_f, ro_b, ml_st, ml_m, rt_st = refs[n_in:]
    bodies = [_mlstm2_kernel(*ml_in, mo_f, mo_b, ml_st, ml_m), _ret_kernel(*rt_in, ro_f, ro_b, rt_st)]
    while bodies:
        for body in list(bodies):
            if next(body, "done") == "done":
                bodies.remove(body)


def _scans(q, k, slab, rope_r, dmat, xi, kd, cd, nl, ncx):
    bsz, ltot, _ = slab.shape
    hw = ML_HEADS * ML_DH
    qw = RET_HEADS * RET_DK
    vw = RET_HEADS * RET_DV

    def ml_specs(d):
        ch = lambda bb, s: _chunk_of(d, s, nl, ncx)
        return [pl.BlockSpec((None, CHUNK, hw), lambda bb, s: (bb, ch(bb, s), 0)),
                pl.BlockSpec((None, CHUNK, hw), lambda bb, s: (bb, ch(bb, s), 0)),
                pl.BlockSpec((None, CHUNK, hw), lambda bb, s: (bb, ch(bb, s), OFF_V // hw)),
                pl.BlockSpec((None, CHUNK, LANES), lambda bb, s: (bb, ch(bb, s), OFF_MISC // LANES))]

    def rt_specs(d):
        ch = lambda bb, s: _chunk_of(d, s, nl, ncx)
        return [pl.BlockSpec((None, CHUNK, qw), lambda bb, s: (bb, ch(bb, s), OFF_RQ // qw)),
                pl.BlockSpec((None, CHUNK, qw), lambda bb, s: (bb, ch(bb, s), OFF_RK // qw)),
                pl.BlockSpec((None, CHUNK, vw), lambda bb, s: (bb, ch(bb, s), OFF_RV // vw)),
                pl.BlockSpec((3, CHUNK, LANES), lambda bb, s: (0, ch(bb, s), 0))]

    full = lambda a: pl.BlockSpec(a.shape, lambda bb, s: (0,) * a.ndim)
    out = lambda d, w: pl.BlockSpec((None, CHUNK, w), lambda bb, s: (bb, _chunk_of(d, s, nl, ncx), 0))
    return pl.pallas_call(
        _scan_kernel,
        grid=(bsz, nl + ncx),
        in_specs=(ml_specs(0) + ml_specs(1) + rt_specs(0) + rt_specs(1)
                  + [full(dmat), full(xi), full(kd), full(cd)]),
        out_specs=[out(0, hw), out(1, hw), out(0, vw), out(1, vw)],
        out_shape=[jax.ShapeDtypeStruct((bsz, ltot, hw), F32)] * 2 + [jax.ShapeDtypeStruct((bsz, ltot, vw), F32)] * 2,
        scratch_shapes=[pltpu.VMEM((2, ML_HEADS // 2, LANES, 2 * LANES), F32),
                        pltpu.VMEM((2, 1, LANES), F32),
                        pltpu.VMEM((2, RET_HEADS // 2, LANES, LANES), F32)],
        compiler_params=_params(("arbitrary", "arbitrary")),
        name="scans",
    )(q, k, slab, slab, q, k, slab, slab, slab, slab, slab, rope_r, slab, slab, slab, rope_r, dmat, xi, kd, cd)


def _rope_lanes(x, tab_ref):
    return (x * tab_ref[0] + pltpu.roll(x, 8, axis=1) * tab_ref[1]
            + pltpu.roll(x, LANES - 8, axis=1) * tab_ref[2])


def _ret_kernel(qf_ref, kf_ref, vf_ref, rf_ref, qb_ref, kb_ref, vb_ref, rb_ref, dmat_ref, xi_ref, kd_ref, cd_ref,
                of_ref, ob_ref, st_ref):
    s = pl.program_id(1)

    @pl.when(s == 0)
    def _():
        st_ref[...] = jnp.zeros_like(st_ref)

    dirs = (0, 1)
    q_refs, k_refs, v_refs, r_refs, o_refs = (qf_ref, qb_ref), (kf_ref, kb_ref), (vf_ref, vb_ref), (rf_ref, rb_ref), \
        (of_ref, ob_ref)
    lane = _lane_iota((1, LANES))
    lo_half = lane < RET_DV
    ri = lax.broadcasted_iota(jnp.int32, (LANES, LANES), 0)
    ci = lax.broadcasted_iota(jnp.int32, (LANES, LANES), 1)
    heads = range(RET_HEADS)
    pairs = range(RET_HEADS // 2)
    qg = [[_rope_lanes(q_refs[d][:, g * LANES:(g + 1) * LANES], r_refs[d]) for g in range(2)] for d in dirs]
    kg = [[_rope_lanes(k_refs[d][:, g * LANES:(g + 1) * LANES], r_refs[d]) * (RET_DK ** -0.5) for g in range(2)]
          for d in dirs]
    qg_b = [[x.astype(BF16) for x in qg[d]] for d in dirs]
    kg_b = [[x.astype(BF16) for x in kg[d]] for d in dirs]
    kw_b = [[(kg[d][g] * kd_ref[d, :, g * LANES:(g + 1) * LANES]).astype(BF16) for g in range(2)] for d in dirs]
    vp_b = [[v_refs[d][:, p * LANES:(p + 1) * LANES].astype(BF16) for p in pairs] for d in dirs]
    st = [[st_ref[d, p] for p in pairs] for d in dirs]

    cross = [[_dot(qg_b[d][p // 2], st[d][p].astype(BF16)) for p in pairs] for d in dirs]
    qk = [[None] * RET_HEADS for _ in dirs]
    for d in dirs:
        for h in heads:
            q_lo = (h % 4) * RET_DK
            hm = (lane >= q_lo) & (lane < q_lo + RET_DK)
            qk[d][h] = _dot_nt(jnp.where(hm, qg[d][h // 4], 0.0).astype(BF16), kg_b[d][h // 4])
    yield
    sv = [[_dot((qk[d][h] * dmat_ref[d, h]).astype(BF16), vp_b[d][h // 2]) for h in heads] for d in dirs]
    upd = [[_dot_tn(kw_b[d][p // 2], vp_b[d][p]) for p in pairs] for d in dirs]
    yield

    for d in dirs:
        outs = []
        for p in pairs:
            a = p % 2
            outs.append(jnp.where(lo_half, sv[d][2 * p], sv[d][2 * p + 1])
                        + xi_ref[d, :, p * LANES:(p + 1) * LANES] * cross[d][p])
            r_lo = (ri >= 2 * a * RET_DK) & (ri < (2 * a + 1) * RET_DK)
            r_hi = (ri >= (2 * a + 1) * RET_DK) & (ri < (2 * a + 2) * RET_DK)
            valid = (r_lo & (ci < RET_DV)) | (r_hi & (ci >= RET_DV))
            st_ref[d, p] = cd_ref[d, :, p * LANES:(p + 1) * LANES] * st[d][p] + jnp.where(valid, upd[d][p], 0.0)
        o_refs[d][...] = jnp.concatenate(outs, axis=1)


def _ret_tables():
    lg = jnp.log1p(-jnp.exp2(-5.0 - jnp.arange(RET_HEADS, dtype=F32)))
    idx = jnp.arange(CHUNK, dtype=F32)
    diff = idx[:, None] - idx[None, :]
    dm, xis, kds, cds = [], [], [], []
    for lgd, sign in ((lg, 1.0), (lg[::-1], -1.0)):
        dd = diff[None] * sign
        dm.append(jnp.exp(jnp.where(dd >= 0, dd * lgd[:, None, None], -jnp.inf)))
        order = idx if sign > 0 else (CHUNK - 1.0 - idx)
        xi = jnp.exp((order + 1.0)[None] * lgd[:, None])
        kdv = jnp.exp((CHUNK - 1.0 - order)[None] * lgd[:, None])
        cdv = jnp.exp(CHUNK * lgd)
        xis.append(jnp.repeat(xi.T, RET_DV, axis=1))
        kds.append(jnp.repeat(kdv.T, RET_DK, axis=1))
        cds.append(jnp.repeat(cdv, RET_DV)[None, :])
    return jnp.stack(dm), jnp.stack(xis), jnp.stack(kds), jnp.stack(cds)


def _rope_tables(n_lat, n_ctx):
    t = jnp.arange(n_lat)
    row = (t // GRID_W).astype(F32)
    col = (t % GRID_W).astype(F32)
    half = ROPE_D // 4
    inv = ROPE_BASE ** (-jnp.arange(half, dtype=F32) / half)
    ar, ac = row[:, None] * inv, col[:, None] * inv
    z = jnp.zeros_like(ar)
    c32 = jnp.concatenate([jnp.cos(ar), jnp.cos(ar), jnp.cos(ac), jnp.cos(ac)], axis=1)
    s1 = jnp.concatenate([z, jnp.sin(ar), z, jnp.sin(ac)], axis=1)
    s2 = jnp.concatenate([-jnp.sin(ar), z, -jnp.sin(ac), z], axis=1)
    tab = jnp.stack([c32, s1, s2])
    ident = jnp.stack([jnp.ones((n_ctx, ROPE_D), F32), jnp.zeros((n_ctx, ROPE_D), F32),
                       jnp.zeros((n_ctx, ROPE_D), F32)])
    return jnp.concatenate([tab, ident], axis=1)


def _mla_kernel(ql_ref, kv_ref, misc_ref, qan_ref, wqt_ref, kvn_ref, wk_ref, wvt_ref, qnw_ref, knw_ref,
                ropek_ref, ropeq_ref, qt_ref, k_ref, vt_ref):
    ql = ql_ref[...]
    qn = ql * lax.rsqrt(jnp.mean(ql * ql, axis=-1, keepdims=True) + EPS) * qan_ref[...]
    qt = _dot_nt(wqt_ref[...].astype(BF16), qn.astype(BF16))
    for h in range(MLA_HEADS):
        xh = qt[h * LANES:(h + 1) * LANES, :]
        ss = jnp.sum(xh * xh, axis=0, keepdims=True)
        xh = xh * lax.rsqrt(ss * (1.0 / QK_D) + EPS) * qnw_ref[...]
        xh = (xh * ropeq_ref[0] + pltpu.roll(xh, 8, axis=0) * ropeq_ref[1]
              + pltpu.roll(xh, LANES - 8, axis=0) * ropeq_ref[2])
        qt_ref[h * LANES:(h + 1) * LANES, :] = (xh * (QK_D ** -0.5 * LOG2E)).astype(BF16)

    kv = kv_ref[...]
    kvn = (kv * lax.rsqrt(jnp.mean(kv * kv, axis=-1, keepdims=True) + EPS) * kvn_ref[...]).astype(BF16)
    kk = _dot(kvn, wk_ref[...].astype(BF16))
    lane = _lane_iota((1, LANES))
    rope_part = (lane >= NOPE_D) & (lane < QK_D)
    kr = jnp.where(rope_part, pltpu.roll(misc_ref[...], NOPE_D - 4 * ML_HEADS, axis=1), 0.0)
    for h in range(MLA_HEADS):
        xh = kk[:, h * LANES:(h + 1) * LANES] + kr
        ss = jnp.sum(xh * xh, axis=1, keepdims=True)
        xh = xh * lax.rsqrt(ss * (1.0 / QK_D) + EPS) * knw_ref[...]
        k_ref[:, h * LANES:(h + 1) * LANES] = _rope_lanes(xh, ropek_ref).astype(BF16)
    vt_ref[...] = _dot_nt(wvt_ref[...].astype(BF16), kvn).astype(BF16)


def _mla(slab, qan, wqt, kvn, wk, wvt, qnw, knw, rope_k, rope_q):
    bsz, ltot, _ = slab.shape
    hq = MLA_HEADS * LANES
    hv = MLA_HEADS * MLA_DV
    full = lambda shape: pl.BlockSpec(shape, lambda bb, i: (0,) * len(shape))
    return pl.pallas_call(
        _mla_kernel,
        grid=(bsz, ltot // TM),
        in_specs=[pl.BlockSpec((None, TM, Q_LORA), lambda bb, i: (bb, i, OFF_QL // Q_LORA)),
                  pl.BlockSpec((None, TM, KV_LORA), lambda bb, i: (bb, i, OFF_KV // KV_LORA)),
                  pl.BlockSpec((None, TM, LANES), lambda bb, i: (bb, i, OFF_MISC // LANES)),
                  full((1, Q_LORA)), full((hq, Q_LORA)), full((1, KV_LORA)), full((KV_LORA, hq)),
                  full((hv, KV_LORA)), full((LANES, 1)), full((1, LANES)),
                  pl.BlockSpec((3, TM, LANES), lambda bb, i: (0, i, 0)),
                  pl.BlockSpec((3, LANES, TM), lambda bb, i: (0, 0, i))],
        out_specs=[pl.BlockSpec((None, hq, TM), lambda bb, i: (bb, 0, i)),
                   pl.BlockSpec((None, TM, hq), lambda bb, i: (bb, i, 0)),
                   pl.BlockSpec((None, None, hv, TM), lambda bb, i: (bb, i, 0, 0))],
        out_shape=[jax.ShapeDtypeStruct((bsz, hq, ltot), BF16),
                   jax.ShapeDtypeStruct((bsz, ltot, hq), BF16),
                   jax.ShapeDtypeStruct((bsz, ltot // TM, hv, TM), BF16)],
        compiler_params=_params(("arbitrary", "arbitrary")),
        name="mla",
    )(slab, slab, slab, qan, wqt, kvn, wk, wvt, qnw, knw, rope_k, rope_q)


ATT_QC = 512
ATT_ONES = 16
ATT_AHEAD = 3


def _attn_kernel(qt_ref, k_ref, vt_ref, o_ref, acc_ref, m_ref, l_ref, *, r, nkb):
    tq = qt_ref.shape[1]
    tkb = r * TM
    acc_ref[...] = jnp.zeros_like(acc_ref)
    m_ref[...] = jnp.full_like(m_ref, -jnp.inf)
    l_ref[...] = jnp.zeros_like(l_ref)
    ones = jnp.ones((ATT_ONES, TM), BF16)

    qc = min(ATT_QC, tq)
    units = [(j, c) for j in range(2) for c in range(tq // qc)]

    def key_block(kb, carry):
        r0 = pl.multiple_of(kb * tkb, tkb)

        def scores(u):
            j, c = units[u]
            return _dot(k_ref[pl.ds(r0, tkb), j * LANES:(j + 1) * LANES],
                        qt_ref[j * LANES:(j + 1) * LANES, c * qc:(c + 1) * qc])

        ahead = min(ATT_AHEAD, len(units))
        pending = [scores(u) for u in range(ahead)]
        for u, (j, c) in enumerate(units):
            sc = pending.pop(0)
            if u + ahead < len(units):
                pending.append(scores(u + ahead))
            cs = slice(c * qc, (c + 1) * qc)
            m_old = m_ref[j:j + 1, cs]
            m_new = jnp.maximum(m_old, jnp.max(sc, axis=0, keepdims=True))
            p = jnp.exp2(sc - m_new).astype(BF16)
            alpha = jnp.exp2(m_old - m_new)
            pv = None
            for t in range(r):
                vt = jnp.concatenate([vt_ref[kb * r + t, j * MLA_DV:(j + 1) * MLA_DV, :], ones], axis=0)
                part = _dot(vt, p[t * TM:(t + 1) * TM])
                pv = part if pv is None else pv + part
            acc_ref[j * MLA_DV:(j + 1) * MLA_DV, cs] = (
                alpha * acc_ref[j * MLA_DV:(j + 1) * MLA_DV, cs] + pv[0:MLA_DV])
            l_ref[j:j + 1, cs] = alpha * l_ref[j:j + 1, cs] + pv[MLA_DV:MLA_DV + 1]
            m_ref[j:j + 1, cs] = m_new
        return carry

    lax.fori_loop(0, nkb, key_block, 0)
    head_rows = lax.broadcasted_iota(jnp.int32, acc_ref.shape, 0)
    inv = jnp.where(head_rows < MLA_DV, 1.0 / l_ref[0:1, :], 1.0 / l_ref[1:2, :])
    o_ref[...] = (acc_ref[...] * inv).T


def _attn(qt, k, vt, tq, q_off, nq, key_tile0, r, nkb):
    bsz = qt.shape[0]
    hv = vt.shape[2]
    nkt = r * nkb
    assert key_tile0 % nkt == 0
    kern = functools.partial(_attn_kernel, r=r, nkb=nkb)
    return pl.pallas_call(
        kern,
        grid=(bsz, MLA_HEADS // 2, nq),
        in_specs=[pl.BlockSpec((None, 2 * LANES, tq), lambda bb, hp, qi: (bb, hp, qi + q_off)),
                  pl.BlockSpec((None, nkt * TM, 2 * LANES), lambda bb, hp, qi: (bb, key_tile0 // nkt, hp)),
                  pl.BlockSpec((None, nkt, 2 * MLA_DV, TM), lambda bb, hp, qi: (bb, key_tile0 // nkt, hp, 0))],
        out_specs=pl.BlockSpec((None, tq, 2 * MLA_DV), lambda bb, hp, qi: (bb, qi, hp)),
        out_shape=jax.ShapeDtypeStruct((bsz, nq * tq, hv), F32),
        scratch_shapes=[pltpu.VMEM((2 * MLA_DV, tq), F32), pltpu.VMEM((8, tq), F32), pltpu.VMEM((8, tq), F32)],
        compiler_params=_params(("arbitrary", "arbitrary", "arbitrary")),
        name="attn",
    )(qt, k, vt)


def _split_dot(x, w_b):
    hi = x.astype(BF16)
    lo = (x - hi.astype(F32)).astype(BF16)
    return _dot(hi, w_b) + _dot(lo, w_b)


def _head_ln(x, g_ref, gt_ref, width):
    mu = _split_dot(_split_dot(x, g_ref[...]) * (1.0 / width), gt_ref[...])
    xc = x - mu
    var = _split_dot(_split_dot(xc * xc, g_ref[...]) * (1.0 / width), gt_ref[...])
    return xc * lax.rsqrt(var + EPS)


def _merge_kernel(hmf_ref, hmb_ref, og_ref, mlw_ref, ybl_ref, ybc_ref, hrf_ref, hrb_ref, rg_ref, rnw_ref, bl_ref,
                  wbr_ref, wout_ref, x_ref, mod_ref, n2w_ref, rw_ref, g_ref, gt_ref,
                  xo_ref, h2_ref, aff_ref, wbr_b, wout_b, *, nlt):
    @pl.when((pl.program_id(0) == 0) & (pl.program_id(1) == 0))
    def _():
        wbr_b[...] = wbr_ref[...].astype(BF16)
        wout_b[...] = wout_ref[...].astype(BF16)

    is_ctx = pl.program_id(1) >= nlt

    subs = [slice(r0, r0 + ROW_SUB) for r0 in range(0, TM, ROW_SUB)]
    ys = []
    for rs in subs:
        ya = _sigmoid(og_ref[rs, :]) * (_head_ln(hmf_ref[rs, :] + hmb_ref[rs, :], g_ref, gt_ref, ML_DH)
                                        * mlw_ref[...])
        yc = _silu(rg_ref[rs, :]) * (_head_ln(hrf_ref[rs, :] + hrb_ref[rs, :], g_ref, gt_ref, RET_DV)
                                     * rnw_ref[...])
        yb = jnp.where(is_ctx, ybc_ref[rs, :], ybl_ref[rs, :])
        ys.append((ya.astype(BF16), yb.astype(BF16), yc.astype(BF16)))
    br = [[_dot(ys[i][n], wbr_b[n]) for n in range(N_BRANCH)] for i in range(len(subs))]
    merged = []
    for i, rs in enumerate(subs):
        m = None
        for n in range(N_BRANCH):
            term = _sigmoid(bl_ref[rs, n * D_MODEL:(n + 1) * D_MODEL]) * br[i][n]
            m = term if m is None else m + term
        merged.append(m.astype(BF16))
    y = [_dot(m, wout_b[...]) for m in merged]
    g1 = mod_ref[:, 2 * D_MODEL:3 * D_MODEL]
    h2s = []
    for i, rs in enumerate(subs):
        x = x_ref[rs, :] + g1 * y[i]
        xo_ref[rs, :] = x
        hn = x * lax.rsqrt(jnp.mean(x * x, axis=-1, keepdims=True) + EPS) * n2w_ref[...]
        h2 = hn * (1.0 + mod_ref[:, 4 * D_MODEL:5 * D_MODEL]) + mod_ref[:, 3 * D_MODEL:4 * D_MODEL]
        h2_ref[rs, :] = h2.astype(BF16)
        h2s.append(h2)
    logits = []
    for h2 in h2s:
        hi, lo = _split2(h2)
        logits.append(_dot(hi, rw_ref[0]) + _dot(hi, rw_ref[1]) + _dot(lo, rw_ref[0]))
    valid = _lane_iota((1, LANES)) < N_EXPERTS
    for i, rs in enumerate(subs):
        lg = jnp.where(valid, logits[i], -jnp.inf)
        e = jnp.exp(lg - jnp.max(lg, axis=-1, keepdims=True))
        aff_ref[:, rs] = (e / jnp.sum(e, axis=-1, keepdims=True)).T[0:N_EXPERTS, :]


def _merge(hmf, hmb, slab, mlw, yb_l, yb_c, hrf, hrb, rnw, wbr, wout, x, mods, n2w, rw, g8, g8t, nlt):
    bsz, ltot, _ = x.shape
    hw = ML_HEADS * ML_DH

    def mrow(bb, i):
        return (jnp.where(i < nlt, bb, bsz), 0, 0)

    full = lambda shape: pl.BlockSpec(shape, lambda bb, i: (0,) * len(shape))
    row = lambda width, off: pl.BlockSpec((None, TM, width), lambda bb, i: (bb, i, off // width))
    return pl.pallas_call(
        functools.partial(_merge_kernel, nlt=nlt),
        grid=(bsz, ltot // TM),
        in_specs=[row(hw, 0), row(hw, 0),
                  row(hw, OFF_OG), full((1, hw)),
                  pl.BlockSpec((None, TM, hw), lambda bb, i: (bb, jnp.minimum(i, nlt - 1), 0)),
                  pl.BlockSpec((None, TM, hw), lambda bb, i: (bb, 0, 0)),
                  row(hw, 0), row(hw, 0),
                  row(hw, OFF_RG), full((1, hw)),
                  row(N_BRANCH * D_MODEL, OFF_BL),
                  full((N_BRANCH, BRANCH_W, D_MODEL)), full((D_MODEL, D_MODEL)),
                  row(D_MODEL, 0),
                  pl.BlockSpec((None, 1, 6 * D_MODEL), mrow),
                  full((1, D_MODEL)), full((2, D_MODEL, LANES)), full((hw, LANES)), full((LANES, hw))],
        out_specs=[row(D_MODEL, 0), row(D_MODEL, 0), pl.BlockSpec((None, N_EXPERTS, TM), lambda bb, i: (bb, 0, i))],
        out_shape=[jax.ShapeDtypeStruct((bsz, ltot, D_MODEL), F32),
                   jax.ShapeDtypeStruct((bsz, ltot, D_MODEL), BF16),
                   jax.ShapeDtypeStruct((bsz, N_EXPERTS, ltot), F32)],
        scratch_shapes=[pltpu.VMEM((N_BRANCH, BRANCH_W, D_MODEL), BF16), pltpu.VMEM((D_MODEL, D_MODEL), BF16)],
        compiler_params=_params(("arbitrary", "arbitrary")),
        name="merge",
    )(hmf, hmb, slab, mlw, yb_l, yb_c, hrf, hrb, slab, rnw, slab, wbr, wout, x, mods, n2w, rw, g8, g8t)


def _select_kernel(aff_ref, pos_ref, post_ref, gwt_ref, off_ref, *, n, cap, base0, base_step):
    b = pl.program_id(0)
    base = (base0 + b * base_step).astype(F32)
    bits = lax.bitcast_convert_type(aff_ref[...], jnp.int32)
    capf = jnp.float32(cap)

    def search(it, cur):
        cand = cur | jnp.left_shift(jnp.int32(1), 30 - it)
        cnt = jnp.sum(jnp.where(bits >= cand, 1.0, 0.0), axis=1, keepdims=True)
        return jnp.where(cnt >= capf, cand, cur)

    thr = lax.fori_loop(0, 31, search, jnp.zeros((N_EXPERTS, 1), jnp.int32))
    n_gt = jnp.sum(jnp.where(bits > thr, 1.0, 0.0), axis=1, keepdims=True)
    need = capf - n_gt

    ri = lax.broadcasted_iota(jnp.int32, (TM, TM), 0)
    ci = lax.broadcasted_iota(jnp.int32, (TM, TM), 1)
    before = (ri < ci).astype(BF16)
    half_lane = _lane_iota((N_EXPERTS, LANES))
    unused = jnp.full((LANES - N_EXPERTS, TM), -1.0, F32)

    def tile(i, carry):
        c_eq, c_sel, offs = carry
        r0 = pl.multiple_of(i * TM, TM)
        a = aff_ref[:, pl.ds(r0, TM)]
        bt = lax.bitcast_convert_type(a, jnp.int32)
        eq = bt == thr
        rank = _dot(jnp.where(eq, 1.0, 0.0).astype(BF16), before) + c_eq
        sel = (bt > thr) | (eq & (rank < need))
        self_ = jnp.where(sel, 1.0, 0.0)
        pos = _dot(self_.astype(BF16), before) + c_sel
        posv = jnp.where(sel, pos + base, -1.0)
        post_ref[:, pl.ds(r0, TM)] = posv
        gwt_ref[:, pl.ds(r0, TM)] = jnp.where(sel, a, 0.0)
        pos_ref[pl.ds(r0, TM), :] = jnp.concatenate([posv, unused], axis=0).T
        c_half = c_sel + jnp.sum(self_[:, 0:TM // 2], axis=1, keepdims=True)
        offs = jnp.where(half_lane == 2 * i, c_sel, offs)
        offs = jnp.where(half_lane == 2 * i + 1, c_half, offs)
        return (c_eq + jnp.sum(jnp.where(eq, 1.0, 0.0), axis=1, keepdims=True),
                c_sel + jnp.sum(self_, axis=1, keepdims=True), offs)

    zero = jnp.zeros((N_EXPERTS, 1), F32)
    _, _, offs = lax.fori_loop(0, n // TM, tile, (zero, zero, jnp.zeros((N_EXPERTS, LANES), F32)))
    off_ref[...] = offs


def _select(afft, n, cap, tile_off, base0, base_step):
    bsz = afft.shape[0]
    assert 2 * (n // TM) <= LANES
    kern = functools.partial(_select_kernel, n=n, cap=cap, base0=base0, base_step=base_step)
    return pl.pallas_call(
        kern,
        grid=(bsz,),
        in_specs=[pl.BlockSpec((None, N_EXPERTS, n), lambda bb: (bb, 0, tile_off))],
        out_specs=[pl.BlockSpec((None, n, LANES), lambda bb: (bb, 0, 0)),
                   pl.BlockSpec((None, N_EXPERTS, n), lambda bb: (bb, 0, 0)),
                   pl.BlockSpec((None, N_EXPERTS, n), lambda bb: (bb, 0, 0)),
                   pl.BlockSpec((None, N_EXPERTS, LANES), lambda bb: (bb, 0, 0))],
        out_shape=[jax.ShapeDtypeStruct((bsz, n, LANES), F32),
                   jax.ShapeDtypeStruct((bsz, N_EXPERTS, n), F32),
                   jax.ShapeDtypeStruct((bsz, N_EXPERTS, n), F32),
                   jax.ShapeDtypeStruct((bsz, N_EXPERTS, LANES), F32)],
        compiler_params=_params(("arbitrary",)),
        name="select",
    )(afft)


SLOT_ALIGN = 16
WIN = TM + SLOT_ALIGN
TC = TM // 2
WINC = TC + SLOT_ALIGN
GATHER_MAX_TILES = 11
FFN_ROW_CHUNKS = 4
FFN_TF = 256


def _moe_gather_kernel(st_ref, h_ref, pt_ref, gt_ref, xs_ref, gs_ref, *, nt256, tiles):
    e = pl.program_id(0)
    i = pl.program_id(1)

    @pl.when(i == 0)
    def _():
        xs_ref[...] = jnp.zeros_like(xs_ref)
        gs_ref[...] = jnp.zeros_like(gs_ref)

    slot = lax.broadcasted_iota(jnp.int32, (WIN, TM), 0)
    for u in range(tiles):
        st = pl.multiple_of(st_ref[e * nt256 + i * tiles + u], SLOT_ALIGN)
        st2 = pl.multiple_of(st + SLOT_ALIGN, SLOT_ALIGN)
        match = (slot + st).astype(F32) == pt_ref[:, u * TM:(u + 1) * TM]
        onehot = jnp.where(match, 1.0, 0.0).astype(BF16)
        got = _dot(onehot, h_ref[u * TM:(u + 1) * TM, :])
        head = xs_ref[pl.ds(st, SLOT_ALIGN), :].astype(F32)
        xs_ref[pl.ds(st, SLOT_ALIGN), :] = (head + got[0:SLOT_ALIGN]).astype(BF16)
        xs_ref[pl.ds(st2, TM), :] = got[SLOT_ALIGN:WIN].astype(BF16)
        gate = jnp.sum(jnp.where(match, gt_ref[:, u * TM:(u + 1) * TM], 0.0), axis=1, keepdims=True)
        gate = jnp.broadcast_to(gate, (WIN, LANES))
        gs_ref[pl.ds(st, SLOT_ALIGN), :] = gs_ref[pl.ds(st, SLOT_ALIGN), :] + gate[0:SLOT_ALIGN]
        gs_ref[pl.ds(st2, TM), :] = gate[SLOT_ALIGN:WIN]


def _moe_gather(starts, h2, post, gwt, rows, tiles):
    ttot = h2.shape[0]
    tt = tiles * TM
    kern = functools.partial(_moe_gather_kernel, nt256=ttot // TM, tiles=tiles)
    return pl.pallas_call(
        kern,
        grid_spec=pltpu.PrefetchScalarGridSpec(
            num_scalar_prefetch=1,
            grid=(N_EXPERTS, ttot // tt),
            in_specs=[pl.BlockSpec((tt, D_MODEL), lambda e, i, st: (i, 0)),
                      pl.BlockSpec((None, 1, tt), lambda e, i, st: (e, 0, i)),
                      pl.BlockSpec((None, 1, tt), lambda e, i, st: (e, 0, i))],
            out_specs=[pl.BlockSpec((None, rows, D_MODEL), lambda e, i, st: (e, 0, 0)),
                       pl.BlockSpec((None, rows, LANES), lambda e, i, st: (e, 0, 0))]),
        out_shape=[jax.ShapeDtypeStruct((N_EXPERTS, rows, D_MODEL), BF16),
                   jax.ShapeDtypeStruct((N_EXPERTS, rows, LANES), F32)],
        compiler_params=_params(("arbitrary", "arbitrary")),
        name="moe_gather",
    )(starts, h2, post, gwt)


def _moe_ffn_kernel(xs_ref, gs_ref, w1_ref, w3_ref, w2_ref, ys_ref, acc_ref, *, ct, rows):
    f = pl.program_id(1)

    @pl.when(f == 0)
    def _():
        acc_ref[...] = jnp.zeros_like(acc_ref)

    w1b = w1_ref[...].astype(BF16)
    w3b = w3_ref[...].astype(BF16)
    w2b = w2_ref[...].astype(BF16)
    rc = ct // FFN_ROW_CHUNKS
    chunks = [slice(r * rc, (r + 1) * rc) for r in range(FFN_ROW_CHUNKS)]
    up = [(_dot(xs_ref[chunks[0], :], w1b), _dot(xs_ref[chunks[0], :], w3b))]
    for r, rs in enumerate(chunks):
        a, b = up[r]
        if r + 1 < len(chunks):
            up.append((_dot(xs_ref[chunks[r + 1], :], w1b), _dot(xs_ref[chunks[r + 1], :], w3b)))
        acc_ref[rs, :] += _dot((_silu(a) * b).astype(BF16), w2b)

    @pl.when(f == pl.num_programs(1) - 1)
    def _():
        gate = gs_ref[...]
        for c in range(D_MODEL // LANES):
            cs = slice(c * LANES, (c + 1) * LANES)
            ys_ref[0:ct, cs] = (acc_ref[:, cs] * gate).astype(BF16)
        ys_ref[ct:rows, :] = jnp.zeros((rows - ct, D_MODEL), BF16)


def _moe_ffn(xs, gs, w1, w3, w2, layer, ct):
    rows = xs.shape[1]
    kern = functools.partial(_moe_ffn_kernel, ct=ct, rows=rows)
    return pl.pallas_call(
        kern,
        grid=(N_EXPERTS, EXPERT_FF // FFN_TF),
        in_specs=[pl.BlockSpec((None, ct, D_MODEL), lambda e, f: (e, 0, 0)),
                  pl.BlockSpec((None, ct, LANES), lambda e, f: (e, 0, 0)),
                  pl.BlockSpec((None, None, D_MODEL, FFN_TF), lambda e, f: (layer, e, 0, f)),
                  pl.BlockSpec((None, None, D_MODEL, FFN_TF), lambda e, f: (layer, e, 0, f)),
                  pl.BlockSpec((None, None, FFN_TF, D_MODEL), lambda e, f: (layer, e, f, 0))],
        out_specs=pl.BlockSpec((None, rows, D_MODEL), lambda e, f: (e, 0, 0)),
        out_shape=jax.ShapeDtypeStruct((N_EXPERTS, rows, D_MODEL), BF16),
        scratch_shapes=[pltpu.VMEM((ct, D_MODEL), F32)],
        compiler_params=_params(("arbitrary", "arbitrary")),
        name="moe_ffn",
    )(xs, gs, w1, w3, w2)


def _combine_kernel(st_ref, x_ref, pos_ref, mod_ref, *rest, tile_of, ntc):
    nsub = TM // TC
    ys_refs = rest[:N_EXPERTS * nsub]
    o_ref = rest[N_EXPERTS * nsub]
    i = tile_of(pl.program_id(0))
    slot = lax.broadcasted_iota(jnp.int32, (TC, WINC), 1)
    g2 = mod_ref[:, 5 * D_MODEL:6 * D_MODEL]
    for u in range(nsub):
        rs = slice(u * TC, (u + 1) * TC)
        pos = pos_ref[rs, :]
        acc = None
        for e in range(N_EXPERTS):
            st = st_ref[e * ntc + i * nsub + u]
            onehot = jnp.where((slot + st).astype(F32) == pos[:, e:e + 1], 1.0, 0.0).astype(BF16)
            part = _dot(onehot, ys_refs[e * nsub + u][0])
            acc = part if acc is None else acc + part
        o_ref[rs, :] = x_ref[rs, :] + g2 * acc


def _combine(starts, x, pos, mods_rows, ys, nlt, ltiles, latent_only):
    ttot = x.shape[0]
    nt = ttot // TM
    nsub = TM // TC
    bsz = ttot // (ltiles * TM)
    steps = bsz * nlt if latent_only else nt
    tile_of = (lambda j: (j // nlt) * ltiles + j % nlt) if latent_only else (lambda j: j)

    def mrow(j, st):
        i = tile_of(j)
        return (jnp.where(i % ltiles < nlt, i // ltiles, bsz), 0, 0)

    def ys_spec(e, u):
        return pl.BlockSpec(
            (pl.Element(1), pl.Element(WINC), pl.Element(D_MODEL)),
            lambda j, st: (e, pl.multiple_of(st[e * nt * nsub + tile_of(j) * nsub + u], SLOT_ALIGN), 0))

    return pl.pallas_call(
        functools.partial(_combine_kernel, tile_of=tile_of, ntc=nt * nsub),
        grid_spec=pltpu.PrefetchScalarGridSpec(
            num_scalar_prefetch=1,
            grid=(steps,),
            in_specs=[pl.BlockSpec((TM, D_MODEL), lambda j, st: (tile_of(j), 0)),
                      pl.BlockSpec((TM, LANES), lambda j, st: (tile_of(j), 0)),
                      pl.BlockSpec((None, 1, 6 * D_MODEL), mrow)]
                     + [ys_spec(e, u) for e in range(N_EXPERTS) for u in range(nsub)],
            out_specs=pl.BlockSpec((TM, D_MODEL), lambda j, st: (j, 0))),
        out_shape=jax.ShapeDtypeStruct((steps * TM, D_MODEL), F32),
        compiler_params=_params(("arbitrary",)),
        name="combine",
    )(starts, x, pos, mods_rows, *([ys] * (N_EXPERTS * nsub)))


def _permute_w_in(w, b):
    widths = (2 * ML_HEADS * ML_DH, ML_HEADS * ML_DH, ML_HEADS * ML_DH, 4 * ML_HEADS, Q_LORA, KV_LORA, ROPE_D,
              RET_HEADS * RET_DK, RET_HEADS * RET_DK, RET_HEADS * RET_DV, RET_HEADS * RET_DV, N_BRANCH * D_MODEL)
    offs = np.concatenate([[0], np.cumsum(widths)])
    seg = lambda a, k: a[..., offs[k]:offs[k + 1]]
    zeros = lambda a, n: jnp.zeros(a.shape[:-1] + (n,), a.dtype)

    def build(a):
        parts = [seg(a, k) for k in (11, 1, 2, 9, 10, 7, 8, 5)]
        parts.append(zeros(a, OFF_QL - (OFF_KV + KV_LORA)))
        parts += [seg(a, 4), seg(a, 3), seg(a, 6)]
        parts.append(zeros(a, SLAB_W - (OFF_MISC + 4 * ML_HEADS + ROPE_D)))
        parts.append(seg(a, 0))
        return jnp.concatenate(parts, axis=-1)

    return build(w), build(b)


def kernel(x, c, ctx, c_ctx, ada_w, ada_b, norm1_w, norm2_w, w_in, b_in, conv_w, conv_b, ml_norm_w, mla_qa_norm,
           mla_wq_b, mla_kva_norm, mla_wkv_b, q_norm_w, k_norm_w, ret_norm_w, w_branch, w_out, router_w,
           exp_w1, exp_w3, exp_w2):
    bsz, n_lat, _ = x.shape
    n_ctx = ctx.shape[1]
    depth = ada_w.shape[0]
    ltot = n_lat + n_ctx
    nlt = n_lat // TM
    ltiles = ltot // TM
    nl, ncx = n_lat // CHUNK, n_ctx // CHUNK
    assert n_lat % TM == 0 and n_ctx == TM and bsz + 1 <= 8
    gather_tiles = max(t for t in range(1, GATHER_MAX_TILES + 1) if (bsz * ltiles) % t == 0)

    cond8 = jnp.zeros((8, D_MODEL), F32).at[:bsz].set(c).at[bsz].set(c_ctx)
    mods_all = _ada(cond8, ada_w, ada_b)

    rope32 = _rope_tables(n_lat, n_ctx)
    rope_r = jnp.tile(rope32, (1, 1, LANES // ROPE_D))
    ident = jnp.stack([jnp.ones((ltot, 1), F32), jnp.zeros((ltot, 1), F32), jnp.zeros((ltot, 1), F32)])
    rope_k = jnp.concatenate([jnp.broadcast_to(ident, (3, ltot, NOPE_D)), rope32,
                              jnp.broadcast_to(ident, (3, ltot, LANES - QK_D))], axis=2)
    rope_q = jnp.swapaxes(rope_k, 1, 2)
    dmat, xi, kd, cd = _ret_tables()

    gi = jnp.arange(ML_HEADS * ML_DH) // ML_DH
    g8 = (gi[:, None] == jnp.arange(LANES)[None, :]).astype(BF16)
    g8t = g8.T

    cap_l = CAP_FACTOR * n_lat // N_EXPERTS
    cap_c = CAP_FACTOR * n_ctx // N_EXPERTS
    ct = bsz * (cap_l + cap_c)
    assert ct % (16 * FFN_ROW_CHUNKS) == 0 and cap_l % 16 == 0 and cap_c % 16 == 0
    rows = ct + WIN

    w_perm, b_perm = _permute_w_in(w_in, b_in[:, None, :])
    w_perm = w_perm.astype(BF16)
    xx = jnp.concatenate([x, ctx], axis=1)
    att_r = next(r for r in (11, 3, 1) if ltiles % r == 0)
    tq = min(2048, n_lat)

    for l in range(depth):
        mods = mods_all[l].reshape(8, 1, 6 * D_MODEL)
        slab, mq, mk = _inproj(xx, mods, norm1_w[l][None, :], w_perm, b_perm, conv_w[l], conv_b[l][None, :], l, nlt)
        hmf, hmb, hrf, hrb = _scans(mq, mk, slab, rope_r, dmat, xi, kd, cd, nl, ncx)

        wq = mla_wq_b[l].reshape(Q_LORA, MLA_HEADS, QK_D)
        wqt = jnp.pad(wq, ((0, 0), (0, 0), (0, LANES - QK_D))).reshape(Q_LORA, MLA_HEADS * LANES).T
        wkv = mla_wkv_b[l].reshape(KV_LORA, MLA_HEADS, NOPE_D + MLA_DV)
        wk = jnp.pad(wkv[:, :, :NOPE_D], ((0, 0), (0, 0), (0, LANES - NOPE_D))).reshape(KV_LORA, MLA_HEADS * LANES)
        wvt = wkv[:, :, NOPE_D:].reshape(KV_LORA, MLA_HEADS * MLA_DV).T
        qnw = jnp.pad(q_norm_w[l], (0, LANES - QK_D))[:, None]
        knw = jnp.pad(k_norm_w[l], (0, LANES - QK_D))[None, :]
        qt, kk, vt = _mla(slab, mla_qa_norm[l][None, :], wqt, mla_kva_norm[l][None, :], wk, wvt, qnw, knw,
                          rope_k, rope_q)
        yb_l = _attn(qt, kk, vt, tq, 0, n_lat // tq, 0, att_r, ltiles // att_r)
        yb_c = _attn(qt, kk, vt, TM, nlt, 1, nlt, 1, 1)

        rw = jnp.stack(_split2(jnp.pad(router_w[l], ((0, 0), (0, LANES - N_EXPERTS)))))
        xm, h2, aff = _merge(hmf, hmb, slab, ml_norm_w[l][None, :], yb_l, yb_c, hrf, hrb, ret_norm_w[l][None, :],
                             w_branch[l],
                             w_out[l], xx, mods, norm2_w[l][None, :], rw, g8, g8t, nlt)

        pos_l, post_l, gwt_l, off_l = _select(aff, n_lat, cap_l, 0, 0, cap_l + cap_c)
        pos_c, post_c, gwt_c, off_c = _select(aff, n_ctx, cap_c, nlt, cap_l, cap_l + cap_c)
        pos = jnp.concatenate([pos_l, pos_c], axis=1).reshape(bsz * ltot, LANES)
        expert_major = lambda a_l, a_c: jnp.swapaxes(jnp.concatenate([a_l, a_c], axis=2), 0, 1).reshape(
            N_EXPERTS, 1, bsz * ltot)
        post = expert_major(post_l, post_c)
        gwt = expert_major(gwt_l, gwt_c)
        base_l = (jnp.arange(bsz) * (cap_l + cap_c))[:, None, None]
        base_c = base_l + cap_l
        off = jnp.concatenate(
            [off_l[:, :, :2 * nlt].astype(jnp.int32) // SLOT_ALIGN * SLOT_ALIGN + base_l,
             off_c[:, :, :2].astype(jnp.int32) // SLOT_ALIGN * SLOT_ALIGN + base_c], axis=2)
        starts_c = jnp.swapaxes(off, 0, 1).reshape(-1)
        starts_g = jnp.swapaxes(off[:, :, ::2], 0, 1).reshape(-1)

        xs, gs = _moe_gather(starts_g, h2.reshape(bsz * ltot, D_MODEL), post, gwt, rows, gather_tiles)
        ys = _moe_ffn(xs, gs, exp_w1, exp_w3, exp_w2, l, ct)
        last = l == depth - 1
        xx = _combine(starts_c, xm.reshape(bsz * ltot, D_MODEL), pos, mods, ys, nlt, ltiles, last)
        xx = xx.reshape(bsz, n_lat if last else ltot, D_MODEL)

    return xx
```

```python
import functools

import jax
import jax.numpy as jnp
import numpy as np
from jax import lax
from jax.experimental import pallas as pl
from jax.experimental.pallas import tpu as pltpu

F32 = jnp.float32
BF16 = jnp.bfloat16
HIGHEST = lax.Precision.HIGHEST

D_MODEL = 1024
GRID_W = 64
N_BRANCH = 3
BRANCH_W = 512
ML_HEADS = 8
ML_DH = 64
CONV_K = 5
MLA_HEADS = 8
Q_LORA = 384
KV_LORA = 256
NOPE_D = 64
ROPE_D = 32
QK_D = NOPE_D + ROPE_D
MLA_DV = 64
RET_HEADS = 8
RET_DK = 32
RET_DV = 64
N_EXPERTS = 16
EXPERT_FF = 1024
CAP_FACTOR = 2
CHUNK = 128
ROPE_BASE = 10000.0
EPS = 1e-6
LOG2E = 1.4426950408889634

LANES = 128
TM = 256
ROW_SUB = 128
VMEM_LIMIT = 56 * 1024 * 1024

OFF_BL, OFF_V, OFF_OG, OFF_RV, OFF_RG = 0, 3072, 3584, 4096, 4608
OFF_RQ, OFF_RK, OFF_KV, OFF_QL, OFF_MISC = 5120, 5376, 5632, 6144, 6528
SLAB_W = 6656
NCOL = SLAB_W + 2 * ML_HEADS * ML_DH
TN_IN = SLAB_W // 2


def _sigmoid(x):
    return 1.0 / (1.0 + jnp.exp(-x))


def _silu(x):
    return x * _sigmoid(x)


def _log_sigmoid(x):
    return jnp.minimum(x, 0.0) - jnp.log1p(jnp.exp(-jnp.abs(x)))


def _dot(a, b, **kw):
    return jnp.dot(a, b, preferred_element_type=F32, **kw)


def _dot_nt(a, b, **kw):
    return lax.dot_general(a, b, (((1,), (1,)), ((), ())), preferred_element_type=F32, **kw)


def _dot_tn(a, b, **kw):
    return lax.dot_general(a, b, (((0,), (0,)), ((), ())), preferred_element_type=F32, **kw)


def _params(sem):
    return pltpu.CompilerParams(dimension_semantics=sem, vmem_limit_bytes=VMEM_LIMIT)


def _ada_kernel(c_ref, w_ref, b_ref, o_ref):
    s = _silu(c_ref[...])
    o_ref[...] = _dot(s, w_ref[...], precision=HIGHEST) + b_ref[...]


def _ada(cond8, ada_w, ada_b):
    depth = ada_w.shape[0]
    tn = 1536
    return pl.pallas_call(
        _ada_kernel,
        grid=(depth, 6 * D_MODEL // tn),
        in_specs=[pl.BlockSpec((8, D_MODEL), lambda l, j: (0, 0)),
                  pl.BlockSpec((None, D_MODEL, tn), lambda l, j: (l, 0, j)),
                  pl.BlockSpec((None, 1, tn), lambda l, j: (l, 0, j))],
        out_specs=pl.BlockSpec((None, 8, tn), lambda l, j: (l, 0, j)),
        out_shape=jax.ShapeDtypeStruct((depth, 8, 6 * D_MODEL), F32),
        compiler_params=_params(("arbitrary", "arbitrary")),
        name="ada",
    )(cond8, ada_w, ada_b.reshape(depth, 1, 6 * D_MODEL))


def _inproj_kernel(x_ref, xp_ref, xn_ref, mod_ref, nw_ref, w_ref, b_ref, cw_ref, cb_ref, o_ref, q_ref, k_ref, xe_ref,
                   *, nlt, ltiles):
    i = pl.program_id(1)
    first = (i == 0) | (i == nlt)
    last = (i == nlt - 1) | (i == ltiles - 1)
    sh = mod_ref[:, 0:D_MODEL]
    sc = mod_ref[:, D_MODEL:2 * D_MODEL]
    hw = ML_HEADS * ML_DH

    def hidden(x):
        hn = x * lax.rsqrt(jnp.mean(x * x, axis=-1, keepdims=True) + EPS) * nw_ref[...]
        return (hn * (1.0 + sc) + sh).astype(BF16)

    def qk_cols(h):
        return _dot(h, w_ref[:, SLAB_W:NCOL]) + b_ref[:, SLAB_W:NCOL]

    subs = list(range(0, TM, ROW_SUB))
    halo = qk_cols(hidden(jnp.concatenate([xp_ref[...], xn_ref[...]], axis=0)))
    xe_ref[0:8, :] = jnp.where(first, 0.0, halo[0:8])
    xe_ref[8 + TM:16 + TM, :] = jnp.where(last, 0.0, halo[8:16])
    hs = [hidden(x_ref[r0:r0 + ROW_SUB, :]) for r0 in subs]
    for r0, h in zip(subs, hs):
        xe_ref[8 + r0:8 + r0 + ROW_SUB, :] = qk_cols(h)
    for r0, h in zip(subs, hs):
        for c0 in range(0, SLAB_W, TN_IN):
            o_ref[r0:r0 + ROW_SUB, c0:c0 + TN_IN] = _dot(h, w_ref[:, c0:c0 + TN_IN]) + b_ref[:, c0:c0 + TN_IN]
    acc = jnp.broadcast_to(cb_ref[...], (TM, 2 * hw))
    for j in range(CONV_K):
        acc = acc + xe_ref[8 - CONV_K // 2 + j:8 - CONV_K // 2 + j + TM, :] * cw_ref[j:j + 1, :]
    qk = _silu(acc)
    q_ref[...] = qk[:, :hw].astype(BF16)
    k_ref[...] = (qk[:, hw:] * (ML_DH ** -0.5)).astype(BF16)


def _inproj(x, mods, nw, w_all, b_all, conv_w, conv_b, layer, nlt):
    bsz, ltot, _ = x.shape
    hw = ML_HEADS * ML_DH
    ltiles = ltot // TM
    nrow8 = ltot // 8

    def mrow(bb, i):
        return (jnp.where(i < nlt, bb, bsz), 0, 0)

    qk_out = pl.BlockSpec((None, TM, hw), lambda bb, i: (bb, i, 0))
    return pl.pallas_call(
        functools.partial(_inproj_kernel, nlt=nlt, ltiles=ltiles),
        grid=(bsz, ltiles),
        in_specs=[pl.BlockSpec((None, TM, D_MODEL), lambda bb, i: (bb, i, 0)),
                  pl.BlockSpec((None, 8, D_MODEL), lambda bb, i: (bb, jnp.maximum(i * (TM // 8) - 1, 0), 0)),
                  pl.BlockSpec((None, 8, D_MODEL), lambda bb, i: (bb, jnp.minimum((i + 1) * (TM // 8), nrow8 - 1), 0)),
                  pl.BlockSpec((None, 1, 6 * D_MODEL), mrow),
                  pl.BlockSpec((1, D_MODEL), lambda bb, i: (0, 0)),
                  pl.BlockSpec((None, D_MODEL, NCOL), lambda bb, i: (layer, 0, 0)),
                  pl.BlockSpec((None, 1, NCOL), lambda bb, i: (layer, 0, 0)),
                  pl.BlockSpec((CONV_K, 2 * hw), lambda bb, i: (0, 0)),
                  pl.BlockSpec((1, 2 * hw), lambda bb, i: (0, 0))],
        out_specs=[pl.BlockSpec((None, TM, SLAB_W), lambda bb, i: (bb, i, 0)), qk_out, qk_out],
        out_shape=[jax.ShapeDtypeStruct((bsz, ltot, SLAB_W), F32),
                   jax.ShapeDtypeStruct((bsz, ltot, hw), BF16), jax.ShapeDtypeStruct((bsz, ltot, hw), BF16)],
        scratch_shapes=[pltpu.VMEM((TM + 16, 2 * hw), F32)],
        compiler_params=_params(("arbitrary", "arbitrary")),
        name="inproj",
    )(x, x, x, mods, nw, w_all, b_all, conv_w, conv_b)


def _chunk_of(d, s, nl, ncx):
    fwd = jnp.where(s < ncx, nl + s, s - ncx)
    bwd = jnp.where(s < ncx, nl + ncx - 1 - s, nl - 1 - (s - ncx))
    return jnp.where(d == 0, fwd, bwd)


def _lane_iota(shape):
    return lax.broadcasted_iota(jnp.int32, shape, len(shape) - 1)


def _split2(x):
    hi = x.astype(BF16)
    return hi, (x - hi.astype(F32)).astype(BF16)


def _split3(x):
    hi = x.astype(BF16)
    r1 = x - hi.astype(F32)
    mid = r1.astype(BF16)
    return hi, mid, (r1 - mid.astype(F32)).astype(BF16)


def _select_dot(x, sel_b):
    hi, lo = _split2(x)
    return _dot(hi, sel_b) + _dot(lo, sel_b)


GATE_LANE0 = ML_HEADS


def _mlstm2_kernel(qf_ref, kf_ref, vf_ref, gf_ref, qb_ref, kb_ref, vb_ref, gb_ref, of_ref, ob_ref, st_ref, m_ref):
    s = pl.program_id(1)

    @pl.when(s == 0)
    def _():
        st_ref[...] = jnp.zeros_like(st_ref)
        m_ref[...] = jnp.zeros_like(m_ref)

    dirs = (0, 1)
    pairs = range(ML_HEADS // 2)
    heads = range(ML_HEADS)
    q_refs, k_refs, v_refs, g_refs, o_refs = (qf_ref, qb_ref), (kf_ref, kb_ref), (vf_ref, vb_ref), (gf_ref, gb_ref), \
        (of_ref, ob_ref)
    lane = _lane_iota((1, LANES))
    head_lane = (lane >= GATE_LANE0) & (lane < GATE_LANE0 + ML_HEADS)
    lo_half = lane < ML_DH
    ti = lax.broadcasted_iota(jnp.int32, (CHUNK, CHUNK), 0)
    si = lax.broadcasted_iota(jnp.int32, (CHUNK, CHUNK), 1)
    causal = (ti >= si, ti <= si)
    tri = [jnp.where(causal[d], 1.0, 0.0).astype(BF16) for d in dirs]
    tri_t = [jnp.where(causal[1 - d], 1.0, 0.0).astype(BF16) for d in dirs]
    s8 = _lane_iota((ML_HEADS, CHUNK))
    gi = lax.broadcasted_iota(jnp.int32, (LANES, 2 * LANES), 0)
    li = lax.broadcasted_iota(jnp.int32, (LANES, 2 * LANES), 1)
    li_in = jnp.where(li >= LANES, li - LANES, li)
    same_head = (gi < ML_DH) == (li_in < ML_DH)
    lane2 = _lane_iota((1, 2 * LANES))
    lo_half2 = jnp.where(lane2 >= LANES, lane2 - LANES, lane2) < ML_DH
    hmask = [lo_half if h % 2 == 0 else jnp.logical_not(lo_half) for h in heads]
    ones_b = jnp.ones((CHUNK, LANES), BF16)
    sel_full = [jnp.where(gi == GATE_LANE0 + 2 * p + jnp.where(li >= LANES, 1, 0), 1.0, 0.0).astype(BF16)
                for p in pairs]
    quarter = jnp.where(li >= LANES, 2, 0) + jnp.where(li_in >= ML_DH, 1, 0)
    sel_half = [jnp.where(gi == GATE_LANE0 + 4 * q + quarter, 1.0, 0.0).astype(BF16)
                for q in range(ML_HEADS // 4)]
    t16 = lax.broadcasted_iota(jnp.int32, (16, LANES), 0)

    g_ig, ls = [], []
    for d in dirs:
        g_raw = g_refs[d][...]
        g_fg = g_raw if d == 0 else pltpu.roll(g_raw, LANES - 2 * ML_HEADS, axis=1)
        g_ig.append(jnp.where(head_lane, pltpu.roll(g_fg, ML_HEADS, axis=1), 0.0))
        ls.append(jnp.where(head_lane, _log_sigmoid(g_fg), 0.0))
    ls3 = [_split3(ls[d]) for d in dirs]
    lst3 = [_split3(ls[d].T) for d in dirs]
    b_cols = [sum(_dot(tri[d], t) for t in ls3[d]) for d in dirs]
    b_rows = [sum(_dot(t, tri_t[d]) for t in lst3[d]) for d in dirs]
    yield

    qp = [[q_refs[d][:, p * LANES:(p + 1) * LANES] for p in pairs] for d in dirs]
    kp = [[k_refs[d][:, p * LANES:(p + 1) * LANES] for p in pairs] for d in dirs]
    va = [[jnp.concatenate([v_refs[d][:, p * LANES:(p + 1) * LANES].astype(BF16), ones_b], axis=1) for p in pairs]
          for d in dirs]
    st = [[st_ref[d, p] for p in pairs] for d in dirs]
    qs = [[_dot(qp[d][p], st[d][p].astype(BF16)) for p in pairs] for d in dirs]
    qk = [[_dot_nt(jnp.where(hmask[h], qp[d][h // 2].astype(F32), 0.0).astype(BF16), kp[d][h // 2])
           for h in heads] for d in dirs]
    yield

    r_rows, cdiff, stacked = [], [], []
    for d in dirs:
        r_rows.append(g_ig[d].T - b_rows[d])
        bl_row = b_cols[d][CHUNK - 1:CHUNK, :] if d == 0 else b_cols[d][0:1, :]
        m_row = m_ref[d]
        cm8 = r_rows[d][GATE_LANE0:GATE_LANE0 + ML_HEADS, :]
        order = s8 if d == 0 else CHUNK - 1 - s8
        step = 1
        while step < CHUNK:
            shifted = pltpu.roll(cm8, step if d == 0 else CHUNK - step, axis=1)
            cm8 = jnp.where(order >= step, jnp.maximum(cm8, shifted), cm8)
            step *= 2
        cm = jnp.concatenate([jnp.zeros((GATE_LANE0, CHUNK), F32), cm8,
                              jnp.zeros((LANES - GATE_LANE0 - ML_HEADS, CHUNK), F32)], axis=0).T
        a_cols = b_cols[d] + m_row
        mt_cols = jnp.maximum(a_cols, b_cols[d] + cm)
        g_cols = bl_row - b_cols[d] + g_ig[d]
        m_new = jnp.maximum(bl_row + m_row, jnp.max(g_cols, axis=0, keepdims=True))
        dp_cols = jnp.where(t16 >= 0, jnp.exp(bl_row + m_row - m_new), 0.0)
        m_ref[d] = m_new
        cdiff.append(b_cols[d] - mt_cols)
        stacked.append(jnp.concatenate([jnp.exp(a_cols - mt_cols), jnp.exp(-mt_cols),
                                        jnp.exp(g_cols - m_new), dp_cols], axis=0))

    c_full = [[_select_dot(cdiff[d], sel_full[p]) for p in pairs] for d in dirs]
    halves2 = [[_select_dot(stacked[d], sel_half[q]) for q in range(ML_HEADS // 4)] for d in dirs]
    halves = [[halves2[d][p // 2][:, (p % 2) * LANES:(p % 2 + 1) * LANES] for p in pairs]
              for d in dirs]
    yield

    sc = [[None] * ML_HEADS for _ in dirs]
    for d in dirs:
        for h in heads:
            p, j = h // 2, h % 2
            dlog = c_full[d][p][:, j * LANES:(j + 1) * LANES] + r_rows[d][GATE_LANE0 + h:GATE_LANE0 + h + 1, :]
            sc[d][h] = (qk[d][h] * jnp.exp(jnp.where(causal[d], dlog, -jnp.inf))).astype(BF16)
    sv = [[_dot(sc[d][h], va[d][h // 2]) for h in heads] for d in dirs]
    upd = [[_dot_tn((kp[d][p].astype(F32) * halves[d][p][2 * CHUNK:3 * CHUNK]).astype(BF16), va[d][p])
            for p in pairs] for d in dirs]
    yield

    for d in dirs:
        outs = []
        for p in pairs:
            inter_pair = halves[d][p][0:CHUNK]
            emt_pair = halves[d][p][CHUNK:2 * CHUNK]
            dp_pair = halves[d][p][3 * CHUNK:3 * CHUNK + 1]
            sv_pair = jnp.where(lo_half2, sv[d][2 * p], sv[d][2 * p + 1])
            num = sv_pair[:, :LANES] + inter_pair * qs[d][p][:, :LANES]
            den = sv_pair[:, LANES:] + inter_pair * qs[d][p][:, LANES:]
            outs.append(num / jnp.maximum(jnp.abs(den), emt_pair))
            st_ref[d, p] = (jnp.concatenate([dp_pair, dp_pair], axis=1) * st[d][p]
                            + jnp.where(same_head, upd[d][p], 0.0))
        o_refs[d][...] = jnp.concatenate(outs, axis=1)


N_MLSTM_IN, N_RET_IN = 8, 12


def _scan_kernel(*refs):
    n_in = N_MLSTM_IN + N_RET_IN
    ml_in, rt_in = refs[:N_MLSTM_IN], refs[N_MLSTM_IN:n_in]
    mo_f, mo_b, ro_f, ro_b, ml_st, ml_m, rt_st = refs[n_in:]
    bodies = [_mlstm2_kernel(*ml_in, mo_f, mo_b, ml_st, ml_m), _ret_kernel(*rt_in, ro_f, ro_b, rt_st)]
    while bodies:
        for body in list(bodies):
            if next(body, "done") == "done":
                bodies.remove(body)


def _scans(q, k, slab, rope_r, dmat, xi, kd, cd, nl, ncx):
    bsz, ltot, _ = slab.shape
    hw = ML_HEADS * ML_DH
    qw = RET_HEADS * RET_DK
    vw = RET_HEADS * RET_DV

    def ml_specs(d):
        ch = lambda bb, s: _chunk_of(d, s, nl, ncx)
        return [pl.BlockSpec((None, CHUNK, hw), lambda bb, s: (bb, ch(bb, s), 0)),
                pl.BlockSpec((None, CHUNK, hw), lambda bb, s: (bb, ch(bb, s), 0)),
                pl.BlockSpec((None, CHUNK, hw), lambda bb, s: (bb, ch(bb, s), OFF_V // hw)),
                pl.BlockSpec((None, CHUNK, LANES), lambda bb, s: (bb, ch(bb, s), OFF_MISC // LANES))]

    def rt_specs(d):
        ch = lambda bb, s: _chunk_of(d, s, nl, ncx)
        return [pl.BlockSpec((None, CHUNK, qw), lambda bb, s: (bb, ch(bb, s), OFF_RQ // qw)),
                pl.BlockSpec((None, CHUNK, qw), lambda bb, s: (bb, ch(bb, s), OFF_RK // qw)),
                pl.BlockSpec((None, CHUNK, vw), lambda bb, s: (bb, ch(bb, s), OFF_RV // vw)),
                pl.BlockSpec((3, CHUNK, LANES), lambda bb, s: (0, ch(bb, s), 0))]

    full = lambda a: pl.BlockSpec(a.shape, lambda bb, s: (0,) * a.ndim)
    out = lambda d, w: pl.BlockSpec((None, CHUNK, w), lambda bb, s: (bb, _chunk_of(d, s, nl, ncx), 0))
    return pl.pallas_call(
        _scan_kernel,
        grid=(bsz, nl + ncx),
        in_specs=(ml_specs(0) + ml_specs(1) + rt_specs(0) + rt_specs(1)
                  + [full(dmat), full(xi), full(kd), full(cd)]),
        out_specs=[out(0, hw), out(1, hw), out(0, vw), out(1, vw)],
        out_shape=[jax.ShapeDtypeStruct((bsz, ltot, hw), F32)] * 2 + [jax.ShapeDtypeStruct((bsz, ltot, vw), F32)] * 2,
        scratch_shapes=[pltpu.VMEM((2, ML_HEADS // 2, LANES, 2 * LANES), F32),
                        pltpu.VMEM((2, 1, LANES), F32),
                        pltpu.VMEM((2, RET_HEADS // 2, LANES, LANES), F32)],
        compiler_params=_params(("arbitrary", "arbitrary")),
        name="scans",
    )(q, k, slab, slab, q, k, slab, slab, slab, slab, slab, rope_r, slab, slab, slab, rope_r, dmat, xi, kd, cd)


def _rope_lanes(x, tab_ref):
    return (x * tab_ref[0] + pltpu.roll(x, 8, axis=1) * tab_ref[1]
            + pltpu.roll(x, LANES - 8, axis=1) * tab_ref[2])


def _ret_kernel(qf_ref, kf_ref, vf_ref, rf_ref, qb_ref, kb_ref, vb_ref, rb_ref, dmat_ref, xi_ref, kd_ref, cd_ref,
                of_ref, ob_ref, st_ref):
    s = pl.program_id(1)

    @pl.when(s == 0)
    def _():
        st_ref[...] = jnp.zeros_like(st_ref)

    dirs = (0, 1)
    q_refs, k_refs, v_refs, r_refs, o_refs = (qf_ref, qb_ref), (kf_ref, kb_ref), (vf_ref, vb_ref), (rf_ref, rb_ref), \
        (of_ref, ob_ref)
    lane = _lane_iota((1, LANES))
    lo_half = lane < RET_DV
    ri = lax.broadcasted_iota(jnp.int32, (LANES, LANES), 0)
    ci = lax.broadcasted_iota(jnp.int32, (LANES, LANES), 1)
    heads = range(RET_HEADS)
    pairs = range(RET_HEADS // 2)
    qg = [[_rope_lanes(q_refs[d][:, g * LANES:(g + 1) * LANES], r_refs[d]) for g in range(2)] for d in dirs]
    kg = [[_rope_lanes(k_refs[d][:, g * LANES:(g + 1) * LANES], r_refs[d]) * (RET_DK ** -0.5) for g in range(2)]
          for d in dirs]
    qg_b = [[x.astype(BF16) for x in qg[d]] for d in dirs]
    kg_b = [[x.astype(BF16) for x in kg[d]] for d in dirs]
    kw_b = [[(kg[d][g] * kd_ref[d, :, g * LANES:(g + 1) * LANES]).astype(BF16) for g in range(2)] for d in dirs]
    vp_b = [[v_refs[d][:, p * LANES:(p + 1) * LANES].astype(BF16) for p in pairs] for d in dirs]
    st = [[st_ref[d, p] for p in pairs] for d in dirs]

    cross = [[_dot(qg_b[d][p // 2], st[d][p].astype(BF16)) for p in pairs] for d in dirs]
    qk = [[None] * RET_HEADS for _ in dirs]
    for d in dirs:
        for h in heads:
            q_lo = (h % 4) * RET_DK
            hm = (lane >= q_lo) & (lane < q_lo + RET_DK)
            qk[d][h] = _dot_nt(jnp.where(hm, qg[d][h // 4], 0.0).astype(BF16), kg_b[d][h // 4])
    yield
    sv = [[_dot((qk[d][h] * dmat_ref[d, h]).astype(BF16), vp_b[d][h // 2]) for h in heads] for d in dirs]
    upd = [[_dot_tn(kw_b[d][p // 2], vp_b[d][p]) for p in pairs] for d in dirs]
    yield

    for d in dirs:
        outs = []
        for p in pairs:
            a = p % 2
            outs.append(jnp.where(lo_half, sv[d][2 * p], sv[d][2 * p + 1])
                        + xi_ref[d, :, p * LANES:(p + 1) * LANES] * cross[d][p])
            r_lo = (ri >= 2 * a * RET_DK) & (ri < (2 * a + 1) * RET_DK)
            r_hi = (ri >= (2 * a + 1) * RET_DK) & (ri < (2 * a + 2) * RET_DK)
            valid = (r_lo & (ci < RET_DV)) | (r_hi & (ci >= RET_DV))
            st_ref[d, p] = cd_ref[d, :, p * LANES:(p + 1) * LANES] * st[d][p] + jnp.where(valid, upd[d][p], 0.0)
        o_refs[d][...] = jnp.concatenate(outs, axis=1)


def _ret_tables():
    lg = jnp.log1p(-jnp.exp2(-5.0 - jnp.arange(RET_HEADS, dtype=F32)))
    idx = jnp.arange(CHUNK, dtype=F32)
    diff = idx[:, None] - idx[None, :]
    dm, xis, kds, cds = [], [], [], []
    for lgd, sign in ((lg, 1.0), (lg[::-1], -1.0)):
        dd = diff[None] * sign
        dm.append(jnp.exp(jnp.where(dd >= 0, dd * lgd[:, None, None], -jnp.inf)))
        order = idx if sign > 0 else (CHUNK - 1.0 - idx)
        xi = jnp.exp((order + 1.0)[None] * lgd[:, None])
        kdv = jnp.exp((CHUNK - 1.0 - order)[None] * lgd[:, None])
        cdv = jnp.exp(CHUNK * lgd)
        xis.append(jnp.repeat(xi.T, RET_DV, axis=1))
        kds.append(jnp.repeat(kdv.T, RET_DK, axis=1))
        cds.append(jnp.repeat(cdv, RET_DV)[None, :])
    return jnp.stack(dm), jnp.stack(xis), jnp.stack(kds), jnp.stack(cds)


def _rope_tables(n_lat, n_ctx):
    t = jnp.arange(n_lat)
    row = (t // GRID_W).astype(F32)
    col = (t % GRID_W).astype(F32)
    half = ROPE_D // 4
    inv = ROPE_BASE ** (-jnp.arange(half, dtype=F32) / half)
    ar, ac = row[:, None] * inv, col[:, None] * inv
    z = jnp.zeros_like(ar)
    c32 = jnp.concatenate([jnp.cos(ar), jnp.cos(ar), jnp.cos(ac), jnp.cos(ac)], axis=1)
    s1 = jnp.concatenate([z, jnp.sin(ar), z, jnp.sin(ac)], axis=1)
    s2 = jnp.concatenate([-jnp.sin(ar), z, -jnp.sin(ac), z], axis=1)
    tab = jnp.stack([c32, s1, s2])
    ident = jnp.stack([jnp.ones((n_ctx, ROPE_D), F32), jnp.zeros((n_ctx, ROPE_D), F32),
                       jnp.zeros((n_ctx, ROPE_D), F32)])
    return jnp.concatenate([tab, ident], axis=1)


def _mla_kernel(ql_ref, kv_ref, misc_ref, qan_ref, wqt_ref, kvn_ref, wk_ref, wvt_ref, qnw_ref, knw_ref,
                ropek_ref, ropeq_ref, qt_ref, k_ref, vt_ref):
    ql = ql_ref[...]
    qn = ql * lax.rsqrt(jnp.mean(ql * ql, axis=-1, keepdims=True) + EPS) * qan_ref[...]
    qt = _dot_nt(wqt_ref[...].astype(BF16), qn.astype(BF16))
    for h in range(MLA_HEADS):
        xh = qt[h * LANES:(h + 1) * LANES, :]
        ss = jnp.sum(xh * xh, axis=0, keepdims=True)
        xh = xh * lax.rsqrt(ss * (1.0 / QK_D) + EPS) * qnw_ref[...]
        xh = (xh * ropeq_ref[0] + pltpu.roll(xh, 8, axis=0) * ropeq_ref[1]
              + pltpu.roll(xh, LANES - 8, axis=0) * ropeq_ref[2])
        qt_ref[h * LANES:(h + 1) * LANES, :] = (xh * (QK_D ** -0.5 * LOG2E)).astype(BF16)

    kv = kv_ref[...]
    kvn = (kv * lax.rsqrt(jnp.mean(kv * kv, axis=-1, keepdims=True) + EPS) * kvn_ref[...]).astype(BF16)
    kk = _dot(kvn, wk_ref[...].astype(BF16))
    lane = _lane_iota((1, LANES))
    rope_part = (lane >= NOPE_D) & (lane < QK_D)
    kr = jnp.where(rope_part, pltpu.roll(misc_ref[...], NOPE_D - 4 * ML_HEADS, axis=1), 0.0)
    for h in range(MLA_HEADS):
        xh = kk[:, h * LANES:(h + 1) * LANES] + kr
        ss = jnp.sum(xh * xh, axis=1, keepdims=True)
        xh = xh * lax.rsqrt(ss * (1.0 / QK_D) + EPS) * knw_ref[...]
        k_ref[:, h * LANES:(h + 1) * LANES] = _rope_lanes(xh, ropek_ref).astype(BF16)
    vt_ref[...] = _dot_nt(wvt_ref[...].astype(BF16), kvn).astype(BF16)


def _mla(slab, qan, wqt, kvn, wk, wvt, qnw, knw, rope_k, rope_q):
    bsz, ltot, _ = slab.shape
    hq = MLA_HEADS * LANES
    hv = MLA_HEADS * MLA_DV
    full = lambda shape: pl.BlockSpec(shape, lambda bb, i: (0,) * len(shape))
    return pl.pallas_call(
        _mla_kernel,
        grid=(bsz, ltot // TM),
        in_specs=[pl.BlockSpec((None, TM, Q_LORA), lambda bb, i: (bb, i, OFF_QL // Q_LORA)),
                  pl.BlockSpec((None, TM, KV_LORA), lambda bb, i: (bb, i, OFF_KV // KV_LORA)),
                  pl.BlockSpec((None, TM, LANES), lambda bb, i: (bb, i, OFF_MISC // LANES)),
                  full((1, Q_LORA)), full((hq, Q_LORA)), full((1, KV_LORA)), full((KV_LORA, hq)),
                  full((hv, KV_LORA)), full((LANES, 1)), full((1, LANES)),
                  pl.BlockSpec((3, TM, LANES), lambda bb, i: (0, i, 0)),
                  pl.BlockSpec((3, LANES, TM), lambda bb, i: (0, 0, i))],
        out_specs=[pl.BlockSpec((None, hq, TM), lambda bb, i: (bb, 0, i)),
                   pl.BlockSpec((None, TM, hq), lambda bb, i: (bb, i, 0)),
                   pl.BlockSpec((None, None, hv, TM), lambda bb, i: (bb, i, 0, 0))],
        out_shape=[jax.ShapeDtypeStruct((bsz, hq, ltot), BF16),
                   jax.ShapeDtypeStruct((bsz, ltot, hq), BF16),
                   jax.ShapeDtypeStruct((bsz, ltot // TM, hv, TM), BF16)],
        compiler_params=_params(("arbitrary", "arbitrary")),
        name="mla",
    )(slab, slab, slab, qan, wqt, kvn, wk, wvt, qnw, knw, rope_k, rope_q)


ATT_QC = 512
ATT_ONES = 16
ATT_AHEAD = 3


ATT_BOUND_MAX = 60.0


def _attn_kernel(flag_ref, qt_ref, k_ref, vt_ref, ub_ref, o_ref, acc_ref, m_ref, l_ref, *, r, nkb):
    tq = qt_ref.shape[1]
    tkb = r * TM
    ones = jnp.ones((ATT_ONES, TM), BF16)
    qc = min(ATT_QC, tq)
    units = [(j, c) for j in range(2) for c in range(tq // qc)]
    ahead = min(ATT_AHEAD, len(units))
    bounded = flag_ref[0] == 1
    ub = ub_ref[0:1, 0:1]

    def scores(r0, u):
        j, c = units[u]
        return _dot(k_ref[pl.ds(r0, tkb), j * LANES:(j + 1) * LANES],
                    qt_ref[j * LANES:(j + 1) * LANES, c * qc:(c + 1) * qc])

    def weighted_values(kb, j, p):
        pv = None
        for t in range(r):
            vt = jnp.concatenate([vt_ref[kb * r + t, j * MLA_DV:(j + 1) * MLA_DV, :], ones], axis=0)
            part = _dot(vt, p[t * TM:(t + 1) * TM])
            pv = part if pv is None else pv + part
        return pv

    def bounded_block(kb, carry):
        r0 = pl.multiple_of(kb * tkb, tkb)
        pending = [scores(r0, u) for u in range(ahead)]
        for u, (j, c) in enumerate(units):
            sc = pending.pop(0)
            if u + ahead < len(units):
                pending.append(scores(r0, u + ahead))
            cs = slice(c * qc, (c + 1) * qc)
            pv = weighted_values(kb, j, jnp.exp2(sc - ub).astype(BF16))
            acc_ref[j * MLA_DV:(j + 1) * MLA_DV, cs] += pv[0:MLA_DV]
            l_ref[j:j + 1, cs] += pv[MLA_DV:MLA_DV + 1]
        return carry

    def online_block(kb, carry):
        r0 = pl.multiple_of(kb * tkb, tkb)
        pending = [scores(r0, u) for u in range(ahead)]
        for u, (j, c) in enumerate(units):
            sc = pending.pop(0)
            if u + ahead < len(units):
                pending.append(scores(r0, u + ahead))
            cs = slice(c * qc, (c + 1) * qc)
            m_old = m_ref[j:j + 1, cs]
            m_new = jnp.maximum(m_old, jnp.max(sc, axis=0, keepdims=True))
            alpha = jnp.exp2(m_old - m_new)
            pv = weighted_values(kb, j, jnp.exp2(sc - m_new).astype(BF16))
            acc_ref[j * MLA_DV:(j + 1) * MLA_DV, cs] = (
                alpha * acc_ref[j * MLA_DV:(j + 1) * MLA_DV, cs] + pv[0:MLA_DV])
            l_ref[j:j + 1, cs] = alpha * l_ref[j:j + 1, cs] + pv[MLA_DV:MLA_DV + 1]
            m_ref[j:j + 1, cs] = m_new
        return carry

    acc_ref[...] = jnp.zeros_like(acc_ref)
    l_ref[...] = jnp.zeros_like(l_ref)

    @pl.when(bounded)
    def _():
        lax.fori_loop(0, nkb, bounded_block, 0)

    @pl.when(jnp.logical_not(bounded))
    def _():
        m_ref[...] = jnp.full_like(m_ref, -jnp.inf)
        lax.fori_loop(0, nkb, online_block, 0)

    head_rows = lax.broadcasted_iota(jnp.int32, acc_ref.shape, 0)
    inv = jnp.where(head_rows < MLA_DV, 1.0 / l_ref[0:1, :], 1.0 / l_ref[1:2, :])
    o_ref[...] = (acc_ref[...] * inv).T


def _attn(flags, qt, k, vt, ub, tq, q_off, nq, key_tile0, r, nkb):
    bsz = qt.shape[0]
    hv = vt.shape[2]
    nkt = r * nkb
    assert key_tile0 % nkt == 0
    kern = functools.partial(_attn_kernel, r=r, nkb=nkb)
    return pl.pallas_call(
        kern,
        grid_spec=pltpu.PrefetchScalarGridSpec(
            num_scalar_prefetch=1,
            grid=(bsz, MLA_HEADS // 2, nq),
            in_specs=[pl.BlockSpec((None, 2 * LANES, tq), lambda bb, hp, qi, fl: (bb, hp, qi + q_off)),
                      pl.BlockSpec((None, nkt * TM, 2 * LANES), lambda bb, hp, qi, fl: (bb, key_tile0 // nkt, hp)),
                      pl.BlockSpec((None, nkt, 2 * MLA_DV, TM),
                                   lambda bb, hp, qi, fl: (bb, key_tile0 // nkt, hp, 0)),
                      pl.BlockSpec((8, LANES), lambda bb, hp, qi, fl: (0, 0))],
            out_specs=pl.BlockSpec((None, tq, 2 * MLA_DV), lambda bb, hp, qi, fl: (bb, qi, hp)),
            scratch_shapes=[pltpu.VMEM((2 * MLA_DV, tq), F32), pltpu.VMEM((8, tq), F32),
                            pltpu.VMEM((8, tq), F32)]),
        out_shape=jax.ShapeDtypeStruct((bsz, nq * tq, hv), F32),
        compiler_params=_params(("arbitrary", "arbitrary", "arbitrary")),
        name="attn",
    )(flags, qt, k, vt, ub)


def _split_dot(x, w_b):
    hi = x.astype(BF16)
    lo = (x - hi.astype(F32)).astype(BF16)
    return _dot(hi, w_b) + _dot(lo, w_b)


def _head_ln(x, g_ref, gt_ref, width):
    mu = _split_dot(_split_dot(x, g_ref[...]) * (1.0 / width), gt_ref[...])
    xc = x - mu
    var = _split_dot(_split_dot(xc * xc, g_ref[...]) * (1.0 / width), gt_ref[...])
    return xc * lax.rsqrt(var + EPS)


def _merge_kernel(hmf_ref, hmb_ref, og_ref, mlw_ref, ybl_ref, ybc_ref, hrf_ref, hrb_ref, rg_ref, rnw_ref, bl_ref,
                  wbr_ref, wout_ref, x_ref, mod_ref, n2w_ref, rw_ref, g_ref, gt_ref,
                  xo_ref, h2_ref, aff_ref, wbr_b, wout_b, *, nlt):
    @pl.when((pl.program_id(0) == 0) & (pl.program_id(1) == 0))
    def _():
        wbr_b[...] = wbr_ref[...].astype(BF16)
        wout_b[...] = wout_ref[...].astype(BF16)

    is_ctx = pl.program_id(1) >= nlt

    subs = [slice(r0, r0 + ROW_SUB) for r0 in range(0, TM, ROW_SUB)]
    ys = []
    for rs in subs:
        ya = _sigmoid(og_ref[rs, :]) * (_head_ln(hmf_ref[rs, :] + hmb_ref[rs, :], g_ref, gt_ref, ML_DH)
                                        * mlw_ref[...])
        yc = _silu(rg_ref[rs, :]) * (_head_ln(hrf_ref[rs, :] + hrb_ref[rs, :], g_ref, gt_ref, RET_DV)
                                     * rnw_ref[...])
        yb = jnp.where(is_ctx, ybc_ref[rs, :], ybl_ref[rs, :])
        ys.append((ya.astype(BF16), yb.astype(BF16), yc.astype(BF16)))
    br = [[_dot(ys[i][n], wbr_b[n]) for n in range(N_BRANCH)] for i in range(len(subs))]
    merged = []
    for i, rs in enumerate(subs):
        m = None
        for n in range(N_BRANCH):
            term = _sigmoid(bl_ref[rs, n * D_MODEL:(n + 1) * D_MODEL]) * br[i][n]
            m = term if m is None else m + term
        merged.append(m.astype(BF16))
    y = [_dot(m, wout_b[...]) for m in merged]
    g1 = mod_ref[:, 2 * D_MODEL:3 * D_MODEL]
    h2s = []
    for i, rs in enumerate(subs):
        x = x_ref[rs, :] + g1 * y[i]
        xo_ref[rs, :] = x
        hn = x * lax.rsqrt(jnp.mean(x * x, axis=-1, keepdims=True) + EPS) * n2w_ref[...]
        h2 = hn * (1.0 + mod_ref[:, 4 * D_MODEL:5 * D_MODEL]) + mod_ref[:, 3 * D_MODEL:4 * D_MODEL]
        h2_ref[rs, :] = h2.astype(BF16)
        h2s.append(h2)
    logits = []
    for h2 in h2s:
        hi, lo = _split2(h2)
        logits.append(_dot(hi, rw_ref[0]) + _dot(hi, rw_ref[1]) + _dot(lo, rw_ref[0]))
    valid = _lane_iota((1, LANES)) < N_EXPERTS
    for i, rs in enumerate(subs):
        lg = jnp.where(valid, logits[i], -jnp.inf)
        e = jnp.exp(lg - jnp.max(lg, axis=-1, keepdims=True))
        aff_ref[:, rs] = (e / jnp.sum(e, axis=-1, keepdims=True)).T[0:N_EXPERTS, :]


def _merge(hmf, hmb, slab, mlw, yb_l, yb_c, hrf, hrb, rnw, wbr, wout, x, mods, n2w, rw, g8, g8t, nlt):
    bsz, ltot, _ = x.shape
    hw = ML_HEADS * ML_DH

    def mrow(bb, i):
        return (jnp.where(i < nlt, bb, bsz), 0, 0)

    full = lambda shape: pl.BlockSpec(shape, lambda bb, i: (0,) * len(shape))
    row = lambda width, off: pl.BlockSpec((None, TM, width), lambda bb, i: (bb, i, off // width))
    return pl.pallas_call(
        functools.partial(_merge_kernel, nlt=nlt),
        grid=(bsz, ltot // TM),
        in_specs=[row(hw, 0), row(hw, 0),
                  row(hw, OFF_OG), full((1, hw)),
                  pl.BlockSpec((None, TM, hw), lambda bb, i: (bb, jnp.minimum(i, nlt - 1), 0)),
                  pl.BlockSpec((None, TM, hw), lambda bb, i: (bb, 0, 0)),
                  row(hw, 0), row(hw, 0),
                  row(hw, OFF_RG), full((1, hw)),
                  row(N_BRANCH * D_MODEL, OFF_BL),
                  full((N_BRANCH, BRANCH_W, D_MODEL)), full((D_MODEL, D_MODEL)),
                  row(D_MODEL, 0),
                  pl.BlockSpec((None, 1, 6 * D_MODEL), mrow),
                  full((1, D_MODEL)), full((2, D_MODEL, LANES)), full((hw, LANES)), full((LANES, hw))],
        out_specs=[row(D_MODEL, 0), row(D_MODEL, 0), pl.BlockSpec((None, N_EXPERTS, TM), lambda bb, i: (bb, 0, i))],
        out_shape=[jax.ShapeDtypeStruct((bsz, ltot, D_MODEL), F32),
                   jax.ShapeDtypeStruct((bsz, ltot, D_MODEL), BF16),
                   jax.ShapeDtypeStruct((bsz, N_EXPERTS, ltot), F32)],
        scratch_shapes=[pltpu.VMEM((N_BRANCH, BRANCH_W, D_MODEL), BF16), pltpu.VMEM((D_MODEL, D_MODEL), BF16)],
        compiler_params=_params(("arbitrary", "arbitrary")),
        name="merge",
    )(hmf, hmb, slab, mlw, yb_l, yb_c, hrf, hrb, slab, rnw, slab, wbr, wout, x, mods, n2w, rw, g8, g8t)


def _select_kernel(aff_ref, pos_ref, post_ref, gwt_ref, off_ref, *, n, cap, base0, base_step):
    b = pl.program_id(0)
    base = (base0 + b * base_step).astype(F32)
    bits = lax.bitcast_convert_type(aff_ref[...], jnp.int32)
    capf = jnp.float32(cap)

    def search(it, cur):
        cand = cur | jnp.left_shift(jnp.int32(1), 30 - it)
        cnt = jnp.sum(jnp.where(bits >= cand, 1.0, 0.0), axis=1, keepdims=True)
        return jnp.where(cnt >= capf, cand, cur)

    thr = lax.fori_loop(0, 31, search, jnp.zeros((N_EXPERTS, 1), jnp.int32))
    n_gt = jnp.sum(jnp.where(bits > thr, 1.0, 0.0), axis=1, keepdims=True)
    need = capf - n_gt

    ri = lax.broadcasted_iota(jnp.int32, (TM, TM), 0)
    ci = lax.broadcasted_iota(jnp.int32, (TM, TM), 1)
    before = (ri < ci).astype(BF16)
    half_lane = _lane_iota((N_EXPERTS, LANES))
    unused = jnp.full((LANES - N_EXPERTS, TM), -1.0, F32)

    def tile(i, carry):
        c_eq, c_sel, offs = carry
        r0 = pl.multiple_of(i * TM, TM)
        a = aff_ref[:, pl.ds(r0, TM)]
        bt = lax.bitcast_convert_type(a, jnp.int32)
        eq = bt == thr
        rank = _dot(jnp.where(eq, 1.0, 0.0).astype(BF16), before) + c_eq
        sel = (bt > thr) | (eq & (rank < need))
        self_ = jnp.where(sel, 1.0, 0.0)
        pos = _dot(self_.astype(BF16), before) + c_sel
        posv = jnp.where(sel, pos + base, -1.0)
        post_ref[:, pl.ds(r0, TM)] = posv
        gwt_ref[:, pl.ds(r0, TM)] = jnp.where(sel, a, 0.0)
        pos_ref[pl.ds(r0, TM), :] = jnp.concatenate([posv, unused], axis=0).T
        c_half = c_sel + jnp.sum(self_[:, 0:TM // 2], axis=1, keepdims=True)
        offs = jnp.where(half_lane == 2 * i, c_sel, offs)
        offs = jnp.where(half_lane == 2 * i + 1, c_half, offs)
        return (c_eq + jnp.sum(jnp.where(eq, 1.0, 0.0), axis=1, keepdims=True),
                c_sel + jnp.sum(self_, axis=1, keepdims=True), offs)

    zero = jnp.zeros((N_EXPERTS, 1), F32)
    _, _, offs = lax.fori_loop(0, n // TM, tile, (zero, zero, jnp.zeros((N_EXPERTS, LANES), F32)))
    off_ref[...] = offs


def _select(afft, n, cap, tile_off, base0, base_step):
    bsz = afft.shape[0]
    assert 2 * (n // TM) <= LANES
    kern = functools.partial(_select_kernel, n=n, cap=cap, base0=base0, base_step=base_step)
    return pl.pallas_call(
        kern,
        grid=(bsz,),
        in_specs=[pl.BlockSpec((None, N_EXPERTS, n), lambda bb: (bb, 0, tile_off))],
        out_specs=[pl.BlockSpec((None, n, LANES), lambda bb: (bb, 0, 0)),
                   pl.BlockSpec((None, N_EXPERTS, n), lambda bb: (bb, 0, 0)),
                   pl.BlockSpec((None, N_EXPERTS, n), lambda bb: (bb, 0, 0)),
                   pl.BlockSpec((None, N_EXPERTS, LANES), lambda bb: (bb, 0, 0))],
        out_shape=[jax.ShapeDtypeStruct((bsz, n, LANES), F32),
                   jax.ShapeDtypeStruct((bsz, N_EXPERTS, n), F32),
                   jax.ShapeDtypeStruct((bsz, N_EXPERTS, n), F32),
                   jax.ShapeDtypeStruct((bsz, N_EXPERTS, LANES), F32)],
        compiler_params=_params(("arbitrary",)),
        name="select",
    )(afft)


SLOT_ALIGN = 16
WIN = TM + SLOT_ALIGN
TC = TM // 2
WINC = TC + SLOT_ALIGN
GATHER_MAX_TILES = 11
FFN_ROW_CHUNKS = 4
FFN_TF = 256


def _moe_gather_kernel(st_ref, h_ref, pt_ref, gt_ref, xs_ref, gs_ref, *, nt256, tiles):
    e = pl.program_id(0)
    i = pl.program_id(1)

    @pl.when(i == 0)
    def _():
        xs_ref[...] = jnp.zeros_like(xs_ref)
        gs_ref[...] = jnp.zeros_like(gs_ref)

    slot = lax.broadcasted_iota(jnp.int32, (WIN, TM), 0)
    for u in range(tiles):
        st = pl.multiple_of(st_ref[e * nt256 + i * tiles + u], SLOT_ALIGN)
        st2 = pl.multiple_of(st + SLOT_ALIGN, SLOT_ALIGN)
        match = (slot + st).astype(F32) == pt_ref[:, u * TM:(u + 1) * TM]
        onehot = jnp.where(match, 1.0, 0.0).astype(BF16)
        got = _dot(onehot, h_ref[u * TM:(u + 1) * TM, :])
        head = xs_ref[pl.ds(st, SLOT_ALIGN), :].astype(F32)
        xs_ref[pl.ds(st, SLOT_ALIGN), :] = (head + got[0:SLOT_ALIGN]).astype(BF16)
        xs_ref[pl.ds(st2, TM), :] = got[SLOT_ALIGN:WIN].astype(BF16)
        gate = jnp.sum(jnp.where(match, gt_ref[:, u * TM:(u + 1) * TM], 0.0), axis=1, keepdims=True)
        gate = jnp.broadcast_to(gate, (WIN, LANES))
        gs_ref[pl.ds(st, SLOT_ALIGN), :] = gs_ref[pl.ds(st, SLOT_ALIGN), :] + gate[0:SLOT_ALIGN]
        gs_ref[pl.ds(st2, TM), :] = gate[SLOT_ALIGN:WIN]


def _moe_gather(starts, h2, post, gwt, rows, tiles):
    ttot = h2.shape[0]
    tt = tiles * TM
    kern = functools.partial(_moe_gather_kernel, nt256=ttot // TM, tiles=tiles)
    return pl.pallas_call(
        kern,
        grid_spec=pltpu.PrefetchScalarGridSpec(
            num_scalar_prefetch=1,
            grid=(N_EXPERTS, ttot // tt),
            in_specs=[pl.BlockSpec((tt, D_MODEL), lambda e, i, st: (i, 0)),
                      pl.BlockSpec((None, 1, tt), lambda e, i, st: (e, 0, i)),
                      pl.BlockSpec((None, 1, tt), lambda e, i, st: (e, 0, i))],
            out_specs=[pl.BlockSpec((None, rows, D_MODEL), lambda e, i, st: (e, 0, 0)),
                       pl.BlockSpec((None, rows, LANES), lambda e, i, st: (e, 0, 0))]),
        out_shape=[jax.ShapeDtypeStruct((N_EXPERTS, rows, D_MODEL), BF16),
                   jax.ShapeDtypeStruct((N_EXPERTS, rows, LANES), F32)],
        compiler_params=_params(("arbitrary", "arbitrary")),
        name="moe_gather",
    )(starts, h2, post, gwt)


def _moe_ffn_kernel(xs_ref, gs_ref, w1_ref, w3_ref, w2_ref, ys_ref, acc_ref, *, ct, rows):
    f = pl.program_id(1)

    @pl.when(f == 0)
    def _():
        acc_ref[...] = jnp.zeros_like(acc_ref)

    w1b = w1_ref[...].astype(BF16)
    w3b = w3_ref[...].astype(BF16)
    w2b = w2_ref[...].astype(BF16)
    rc = ct // FFN_ROW_CHUNKS
    chunks = [slice(r * rc, (r + 1) * rc) for r in range(FFN_ROW_CHUNKS)]
    up = [(_dot(xs_ref[chunks[0], :], w1b), _dot(xs_ref[chunks[0], :], w3b))]
    for r, rs in enumerate(chunks):
        a, b = up[r]
        if r + 1 < len(chunks):
            up.append((_dot(xs_ref[chunks[r + 1], :], w1b), _dot(xs_ref[chunks[r + 1], :], w3b)))
        acc_ref[rs, :] += _dot((_silu(a) * b).astype(BF16), w2b)

    @pl.when(f == pl.num_programs(1) - 1)
    def _():
        gate = gs_ref[...]
        for c in range(D_MODEL // LANES):
            cs = slice(c * LANES, (c + 1) * LANES)
            ys_ref[0:ct, cs] = (acc_ref[:, cs] * gate).astype(BF16)
        ys_ref[ct:rows, :] = jnp.zeros((rows - ct, D_MODEL), BF16)


def _moe_ffn(xs, gs, w1, w3, w2, layer, ct):
    rows = xs.shape[1]
    kern = functools.partial(_moe_ffn_kernel, ct=ct, rows=rows)
    return pl.pallas_call(
        kern,
        grid=(N_EXPERTS, EXPERT_FF // FFN_TF),
        in_specs=[pl.BlockSpec((None, ct, D_MODEL), lambda e, f: (e, 0, 0)),
                  pl.BlockSpec((None, ct, LANES), lambda e, f: (e, 0, 0)),
                  pl.BlockSpec((None, None, D_MODEL, FFN_TF), lambda e, f: (layer, e, 0, f)),
                  pl.BlockSpec((None, None, D_MODEL, FFN_TF), lambda e, f: (layer, e, 0, f)),
                  pl.BlockSpec((None, None, FFN_TF, D_MODEL), lambda e, f: (layer, e, f, 0))],
        out_specs=pl.BlockSpec((None, rows, D_MODEL), lambda e, f: (e, 0, 0)),
        out_shape=jax.ShapeDtypeStruct((N_EXPERTS, rows, D_MODEL), BF16),
        scratch_shapes=[pltpu.VMEM((ct, D_MODEL), F32)],
        compiler_params=_params(("arbitrary", "arbitrary")),
        name="moe_ffn",
    )(xs, gs, w1, w3, w2)


def _combine_kernel(st_ref, x_ref, pos_ref, mod_ref, *rest, tile_of, ntc):
    nsub = TM // TC
    ys_refs = rest[:N_EXPERTS * nsub]
    o_ref = rest[N_EXPERTS * nsub]
    i = tile_of(pl.program_id(0))
    slot = lax.broadcasted_iota(jnp.int32, (TC, WINC), 1)
    g2 = mod_ref[:, 5 * D_MODEL:6 * D_MODEL]
    for u in range(nsub):
        rs = slice(u * TC, (u + 1) * TC)
        pos = pos_ref[rs, :]
        acc = None
        for e in range(N_EXPERTS):
            st = st_ref[e * ntc + i * nsub + u]
            onehot = jnp.where((slot + st).astype(F32) == pos[:, e:e + 1], 1.0, 0.0).astype(BF16)
            part = _dot(onehot, ys_refs[e * nsub + u][0])
            acc = part if acc is None else acc + part
        o_ref[rs, :] = x_ref[rs, :] + g2 * acc


def _combine(starts, x, pos, mods_rows, ys, nlt, ltiles, latent_only):
    ttot = x.shape[0]
    nt = ttot // TM
    nsub = TM // TC
    bsz = ttot // (ltiles * TM)
    steps = bsz * nlt if latent_only else nt
    tile_of = (lambda j: (j // nlt) * ltiles + j % nlt) if latent_only else (lambda j: j)

    def mrow(j, st):
        i = tile_of(j)
        return (jnp.where(i % ltiles < nlt, i // ltiles, bsz), 0, 0)

    def ys_spec(e, u):
        return pl.BlockSpec(
            (pl.Element(1), pl.Element(WINC), pl.Element(D_MODEL)),
            lambda j, st: (e, pl.multiple_of(st[e * nt * nsub + tile_of(j) * nsub + u], SLOT_ALIGN), 0))

    return pl.pallas_call(
        functools.partial(_combine_kernel, tile_of=tile_of, ntc=nt * nsub),
        grid_spec=pltpu.PrefetchScalarGridSpec(
            num_scalar_prefetch=1,
            grid=(steps,),
            in_specs=[pl.BlockSpec((TM, D_MODEL), lambda j, st: (tile_of(j), 0)),
                      pl.BlockSpec((TM, LANES), lambda j, st: (tile_of(j), 0)),
                      pl.BlockSpec((None, 1, 6 * D_MODEL), mrow)]
                     + [ys_spec(e, u) for e in range(N_EXPERTS) for u in range(nsub)],
            out_specs=pl.BlockSpec((TM, D_MODEL), lambda j, st: (j, 0))),
        out_shape=jax.ShapeDtypeStruct((steps * TM, D_MODEL), F32),
        compiler_params=_params(("arbitrary",)),
        name="combine",
    )(starts, x, pos, mods_rows, *([ys] * (N_EXPERTS * nsub)))


def _permute_w_in(w, b):
    widths = (2 * ML_HEADS * ML_DH, ML_HEADS * ML_DH, ML_HEADS * ML_DH, 4 * ML_HEADS, Q_LORA, KV_LORA, ROPE_D,
              RET_HEADS * RET_DK, RET_HEADS * RET_DK, RET_HEADS * RET_DV, RET_HEADS * RET_DV, N_BRANCH * D_MODEL)
    offs = np.concatenate([[0], np.cumsum(widths)])
    seg = lambda a, k: a[..., offs[k]:offs[k + 1]]
    zeros = lambda a, n: jnp.zeros(a.shape[:-1] + (n,), a.dtype)

    def build(a):
        parts = [seg(a, k) for k in (11, 1, 2, 9, 10, 7, 8, 5)]
        parts.append(zeros(a, OFF_QL - (OFF_KV + KV_LORA)))
        parts += [seg(a, 4), seg(a, 3), seg(a, 6)]
        parts.append(zeros(a, SLAB_W - (OFF_MISC + 4 * ML_HEADS + ROPE_D)))
        parts.append(seg(a, 0))
        return jnp.concatenate(parts, axis=-1)

    return build(w), build(b)


def kernel(x, c, ctx, c_ctx, ada_w, ada_b, norm1_w, norm2_w, w_in, b_in, conv_w, conv_b, ml_norm_w, mla_qa_norm,
           mla_wq_b, mla_kva_norm, mla_wkv_b, q_norm_w, k_norm_w, ret_norm_w, w_branch, w_out, router_w,
           exp_w1, exp_w3, exp_w2):
    bsz, n_lat, _ = x.shape
    n_ctx = ctx.shape[1]
    depth = ada_w.shape[0]
    ltot = n_lat + n_ctx
    nlt = n_lat // TM
    ltiles = ltot // TM
    nl, ncx = n_lat // CHUNK, n_ctx // CHUNK
    assert n_lat % TM == 0 and n_ctx == TM and bsz + 1 <= 8
    gather_tiles = max(t for t in range(1, GATHER_MAX_TILES + 1) if (bsz * ltiles) % t == 0)

    cond8 = jnp.zeros((8, D_MODEL), F32).at[:bsz].set(c).at[bsz].set(c_ctx)
    mods_all = _ada(cond8, ada_w, ada_b)

    rope32 = _rope_tables(n_lat, n_ctx)
    rope_r = jnp.tile(rope32, (1, 1, LANES // ROPE_D))
    ident = jnp.stack([jnp.ones((ltot, 1), F32), jnp.zeros((ltot, 1), F32), jnp.zeros((ltot, 1), F32)])
    rope_k = jnp.concatenate([jnp.broadcast_to(ident, (3, ltot, NOPE_D)), rope32,
                              jnp.broadcast_to(ident, (3, ltot, LANES - QK_D))], axis=2)
    rope_q = jnp.swapaxes(rope_k, 1, 2)
    dmat, xi, kd, cd = _ret_tables()

    gi = jnp.arange(ML_HEADS * ML_DH) // ML_DH
    g8 = (gi[:, None] == jnp.arange(LANES)[None, :]).astype(BF16)
    g8t = g8.T

    cap_l = CAP_FACTOR * n_lat // N_EXPERTS
    cap_c = CAP_FACTOR * n_ctx // N_EXPERTS
    ct = bsz * (cap_l + cap_c)
    assert ct % (16 * FFN_ROW_CHUNKS) == 0 and cap_l % 16 == 0 and cap_c % 16 == 0
    rows = ct + WIN

    w_perm, b_perm = _permute_w_in(w_in, b_in[:, None, :])
    w_perm = w_perm.astype(BF16)
    xx = jnp.concatenate([x, ctx], axis=1)
    att_r = next(r for r in (11, 3, 1) if ltiles % r == 0)
    tq = min(2048, n_lat)

    for l in range(depth):
        mods = mods_all[l].reshape(8, 1, 6 * D_MODEL)
        slab, mq, mk = _inproj(xx, mods, norm1_w[l][None, :], w_perm, b_perm, conv_w[l], conv_b[l][None, :], l, nlt)
        hmf, hmb, hrf, hrb = _scans(mq, mk, slab, rope_r, dmat, xi, kd, cd, nl, ncx)

        wq = mla_wq_b[l].reshape(Q_LORA, MLA_HEADS, QK_D)
        wqt = jnp.pad(wq, ((0, 0), (0, 0), (0, LANES - QK_D))).reshape(Q_LORA, MLA_HEADS * LANES).T
        wkv = mla_wkv_b[l].reshape(KV_LORA, MLA_HEADS, NOPE_D + MLA_DV)
        wk = jnp.pad(wkv[:, :, :NOPE_D], ((0, 0), (0, 0), (0, LANES - NOPE_D))).reshape(KV_LORA, MLA_HEADS * LANES)
        wvt = wkv[:, :, NOPE_D:].reshape(KV_LORA, MLA_HEADS * MLA_DV).T
        qnw = jnp.pad(q_norm_w[l], (0, LANES - QK_D))[:, None]
        knw = jnp.pad(k_norm_w[l], (0, LANES - QK_D))[None, :]
        qt, kk, vt = _mla(slab, mla_qa_norm[l][None, :], wqt, mla_kva_norm[l][None, :], wk, wvt, qnw, knw,
                          rope_k, rope_q)
        bound = (1.01 * QK_D ** 0.5 * LOG2E) * jnp.max(jnp.abs(q_norm_w[l])) * jnp.max(jnp.abs(k_norm_w[l]))
        flags = (bound <= ATT_BOUND_MAX).astype(jnp.int32).reshape(1)
        ub = jnp.full((8, LANES), bound, F32)
        yb_l = _attn(flags, qt, kk, vt, ub, tq, 0, n_lat // tq, 0, att_r, ltiles // att_r)
        yb_c = _attn(flags, qt, kk, vt, ub, TM, nlt, 1, nlt, 1, 1)

        rw = jnp.stack(_split2(jnp.pad(router_w[l], ((0, 0), (0, LANES - N_EXPERTS)))))
        xm, h2, aff = _merge(hmf, hmb, slab, ml_norm_w[l][None, :], yb_l, yb_c, hrf, hrb, ret_norm_w[l][None, :],
                             w_branch[l],
                             w_out[l], xx, mods, norm2_w[l][None, :], rw, g8, g8t, nlt)

        pos_l, post_l, gwt_l, off_l = _select(aff, n_lat, cap_l, 0, 0, cap_l + cap_c)
        pos_c, post_c, gwt_c, off_c = _select(aff, n_ctx, cap_c, nlt, cap_l, cap_l + cap_c)
        pos = jnp.concatenate([pos_l, pos_c], axis=1).reshape(bsz * ltot, LANES)
        expert_major = lambda a_l, a_c: jnp.swapaxes(jnp.concatenate([a_l, a_c], axis=2), 0, 1).reshape(
            N_EXPERTS, 1, bsz * ltot)
        post = expert_major(post_l, post_c)
        gwt = expert_major(gwt_l, gwt_c)
        base_l = (jnp.arange(bsz) * (cap_l + cap_c))[:, None, None]
        base_c = base_l + cap_l
        off = jnp.concatenate(
            [off_l[:, :, :2 * nlt].astype(jnp.int32) // SLOT_ALIGN * SLOT_ALIGN + base_l,
             off_c[:, :, :2].astype(jnp.int32) // SLOT_ALIGN * SLOT_ALIGN + base_c], axis=2)
        starts_c = jnp.swapaxes(off, 0, 1).reshape(-1)
        starts_g = jnp.swapaxes(off[:, :, ::2], 0, 1).reshape(-1)

        xs, gs = _moe_gather(starts_g, h2.reshape(bsz * ltot, D_MODEL), post, gwt, rows, gather_tiles)
        ys = _moe_ffn(xs, gs, exp_w1, exp_w3, exp_w2, l, ct)
        last = l == depth - 1
        xx = _combine(starts_c, xm.reshape(bsz * ltot, D_MODEL), pos, mods, ys, nlt, ltiles, last)
        xx = xx.reshape(bsz, n_lat if last else ltot, D_MODEL)

    return xx
```

```python
import functools

import jax
import jax.numpy as jnp
import numpy as np
from jax import lax
from jax.experimental import pallas as pl
from jax.experimental.pallas import tpu as pltpu

F32 = jnp.float32
BF16 = jnp.bfloat16
HIGHEST = lax.Precision.HIGHEST

D_MODEL = 1024
GRID_W = 64
N_BRANCH = 3
BRANCH_W = 512
ML_HEADS = 8
ML_DH = 64
CONV_K = 5
MLA_HEADS = 8
Q_LORA = 384
KV_LORA = 256
NOPE_D = 64
ROPE_D = 32
QK_D = NOPE_D + ROPE_D
MLA_DV = 64
RET_HEADS = 8
RET_DK = 32
RET_DV = 64
N_EXPERTS = 16
EXPERT_FF = 1024
CAP_FACTOR = 2
CHUNK = 128
ROPE_BASE = 10000.0
EPS = 1e-6
LOG2E = 1.4426950408889634

LANES = 128
TM = 256
ROW_SUB = 128
VMEM_LIMIT = 56 * 1024 * 1024

OFF_BL, OFF_V, OFF_OG, OFF_RV, OFF_RG = 0, 3072, 3584, 4096, 4608
OFF_RQ, OFF_RK, OFF_KV, OFF_QL, OFF_MISC = 5120, 5376, 5632, 6144, 6528
SLAB_W = 6656
NCOL = SLAB_W + 2 * ML_HEADS * ML_DH
TN_IN = SLAB_W // 2


def _sigmoid(x):
    return 1.0 / (1.0 + jnp.exp(-x))


def _silu(x):
    return x * _sigmoid(x)


def _log_sigmoid(x):
    return jnp.minimum(x, 0.0) - jnp.log1p(jnp.exp(-jnp.abs(x)))


def _dot(a, b, **kw):
    return jnp.dot(a, b, preferred_element_type=F32, **kw)


def _dot_nt(a, b, **kw):
    return lax.dot_general(a, b, (((1,), (1,)), ((), ())), preferred_element_type=F32, **kw)


def _dot_tn(a, b, **kw):
    return lax.dot_general(a, b, (((0,), (0,)), ((), ())), preferred_element_type=F32, **kw)


def _params(sem):
    return pltpu.CompilerParams(dimension_semantics=sem, vmem_limit_bytes=VMEM_LIMIT)


def _ada_kernel(c_ref, w_ref, b_ref, o_ref):
    s = _silu(c_ref[...])
    o_ref[...] = _dot(s, w_ref[...], precision=HIGHEST) + b_ref[...]


def _ada(cond8, ada_w, ada_b):
    depth = ada_w.shape[0]
    tn = 1536
    return pl.pallas_call(
        _ada_kernel,
        grid=(depth, 6 * D_MODEL // tn),
        in_specs=[pl.BlockSpec((8, D_MODEL), lambda l, j: (0, 0)),
                  pl.BlockSpec((None, D_MODEL, tn), lambda l, j: (l, 0, j)),
                  pl.BlockSpec((None, 1, tn), lambda l, j: (l, 0, j))],
        out_specs=pl.BlockSpec((None, 8, tn), lambda l, j: (l, 0, j)),
        out_shape=jax.ShapeDtypeStruct((depth, 8, 6 * D_MODEL), F32),
        compiler_params=_params(("arbitrary", "arbitrary")),
        name="ada",
    )(cond8, ada_w, ada_b.reshape(depth, 1, 6 * D_MODEL))


def _inproj_kernel(x_ref, xp_ref, xn_ref, mod_ref, nw_ref, w_ref, b_ref, cw_ref, cb_ref, o_ref, q_ref, k_ref, xe_ref,
                   *, nlt, ltiles):
    i = pl.program_id(1)
    first = (i == 0) | (i == nlt)
    last = (i == nlt - 1) | (i == ltiles - 1)
    sh = mod_ref[:, 0:D_MODEL]
    sc = mod_ref[:, D_MODEL:2 * D_MODEL]
    hw = ML_HEADS * ML_DH

    def hidden(x):
        hn = x * lax.rsqrt(jnp.mean(x * x, axis=-1, keepdims=True) + EPS) * nw_ref[...]
        return (hn * (1.0 + sc) + sh).astype(BF16)

    def qk_cols(h):
        return _dot(h, w_ref[:, SLAB_W:NCOL]) + b_ref[:, SLAB_W:NCOL]

    subs = list(range(0, TM, ROW_SUB))
    halo = qk_cols(hidden(jnp.concatenate([xp_ref[...], xn_ref[...]], axis=0)))
    xe_ref[0:8, :] = jnp.where(first, 0.0, halo[0:8])
    xe_ref[8 + TM:16 + TM, :] = jnp.where(last, 0.0, halo[8:16])
    hs = [hidden(x_ref[r0:r0 + ROW_SUB, :]) for r0 in subs]
    for r0, h in zip(subs, hs):
        xe_ref[8 + r0:8 + r0 + ROW_SUB, :] = qk_cols(h)
    for r0, h in zip(subs, hs):
        for c0 in range(0, SLAB_W, TN_IN):
            o_ref[r0:r0 + ROW_SUB, c0:c0 + TN_IN] = _dot(h, w_ref[:, c0:c0 + TN_IN]) + b_ref[:, c0:c0 + TN_IN]
    acc = jnp.broadcast_to(cb_ref[...], (TM, 2 * hw))
    for j in range(CONV_K):
        acc = acc + xe_ref[8 - CONV_K // 2 + j:8 - CONV_K // 2 + j + TM, :] * cw_ref[j:j + 1, :]
    qk = _silu(acc)
    q_ref[...] = qk[:, :hw].astype(BF16)
    k_ref[...] = (qk[:, hw:] * (ML_DH ** -0.5)).astype(BF16)


def _inproj(x, mods, nw, w_all, b_all, conv_w, conv_b, layer, nlt):
    bsz, ltot, _ = x.shape
    hw = ML_HEADS * ML_DH
    ltiles = ltot // TM
    nrow8 = ltot // 8

    def mrow(bb, i):
        return (jnp.where(i < nlt, bb, bsz), 0, 0)

    qk_out = pl.BlockSpec((None, TM, hw), lambda bb, i: (bb, i, 0))
    return pl.pallas_call(
        functools.partial(_inproj_kernel, nlt=nlt, ltiles=ltiles),
        grid=(bsz, ltiles),
        in_specs=[pl.BlockSpec((None, TM, D_MODEL), lambda bb, i: (bb, i, 0)),
                  pl.BlockSpec((None, 8, D_MODEL), lambda bb, i: (bb, jnp.maximum(i * (TM // 8) - 1, 0), 0)),
                  pl.BlockSpec((None, 8, D_MODEL), lambda bb, i: (bb, jnp.minimum((i + 1) * (TM // 8), nrow8 - 1), 0)),
                  pl.BlockSpec((None, 1, 6 * D_MODEL), mrow),
                  pl.BlockSpec((1, D_MODEL), lambda bb, i: (0, 0)),
                  pl.BlockSpec((None, D_MODEL, NCOL), lambda bb, i: (layer, 0, 0)),
                  pl.BlockSpec((None, 1, NCOL), lambda bb, i: (layer, 0, 0)),
                  pl.BlockSpec((CONV_K, 2 * hw), lambda bb, i: (0, 0)),
                  pl.BlockSpec((1, 2 * hw), lambda bb, i: (0, 0))],
        out_specs=[pl.BlockSpec((None, TM, SLAB_W), lambda bb, i: (bb, i, 0)), qk_out, qk_out],
        out_shape=[jax.ShapeDtypeStruct((bsz, ltot, SLAB_W), F32),
                   jax.ShapeDtypeStruct((bsz, ltot, hw), BF16), jax.ShapeDtypeStruct((bsz, ltot, hw), BF16)],
        scratch_shapes=[pltpu.VMEM((TM + 16, 2 * hw), F32)],
        compiler_params=_params(("arbitrary", "arbitrary")),
        name="inproj",
    )(x, x, x, mods, nw, w_all, b_all, conv_w, conv_b)


def _chunk_of(d, s, nl, ncx):
    fwd = jnp.where(s < ncx, nl + s, s - ncx)
    bwd = jnp.where(s < ncx, nl + ncx - 1 - s, nl - 1 - (s - ncx))
    return jnp.where(d == 0, fwd, bwd)


def _lane_iota(shape):
    return lax.broadcasted_iota(jnp.int32, shape, len(shape) - 1)


def _split2(x):
    hi = x.astype(BF16)
    return hi, (x - hi.astype(F32)).astype(BF16)


def _split3(x):
    hi = x.astype(BF16)
    r1 = x - hi.astype(F32)
    mid = r1.astype(BF16)
    return hi, mid, (r1 - mid.astype(F32)).astype(BF16)


def _select_dot(x, sel_b):
    hi, lo = _split2(x)
    return _dot(hi, sel_b) + _dot(lo, sel_b)


GATE_LANE0 = ML_HEADS


def _mlstm2_kernel(qf_ref, kf_ref, vf_ref, gf_ref, qb_ref, kb_ref, vb_ref, gb_ref, of_ref, ob_ref, st_ref, m_ref):
    s = pl.program_id(1)

    @pl.when(s == 0)
    def _():
        st_ref[...] = jnp.zeros_like(st_ref)
        m_ref[...] = jnp.zeros_like(m_ref)

    dirs = (0, 1)
    pairs = range(ML_HEADS // 2)
    heads = range(ML_HEADS)
    q_refs, k_refs, v_refs, g_refs, o_refs = (qf_ref, qb_ref), (kf_ref, kb_ref), (vf_ref, vb_ref), (gf_ref, gb_ref), \
        (of_ref, ob_ref)
    lane = _lane_iota((1, LANES))
    head_lane = (lane >= GATE_LANE0) & (lane < GATE_LANE0 + ML_HEADS)
    lo_half = lane < ML_DH
    ti = lax.broadcasted_iota(jnp.int32, (CHUNK, CHUNK), 0)
    si = lax.broadcasted_iota(jnp.int32, (CHUNK, CHUNK), 1)
    causal = (ti >= si, ti <= si)
    tri = [jnp.where(causal[d], 1.0, 0.0).astype(BF16) for d in dirs]
    tri_t = [jnp.where(causal[1 - d], 1.0, 0.0).astype(BF16) for d in dirs]
    s8 = _lane_iota((ML_HEADS, CHUNK))
    gi = lax.broadcasted_iota(jnp.int32, (LANES, 2 * LANES), 0)
    li = lax.broadcasted_iota(jnp.int32, (LANES, 2 * LANES), 1)
    li_in = jnp.where(li >= LANES, li - LANES, li)
    same_head = (gi < ML_DH) == (li_in < ML_DH)
    lane2 = _lane_iota((1, 2 * LANES))
    lo_half2 = jnp.where(lane2 >= LANES, lane2 - LANES, lane2) < ML_DH
    hmask = [lo_half if h % 2 == 0 else jnp.logical_not(lo_half) for h in heads]
    ones_b = jnp.ones((CHUNK, LANES), BF16)
    sel_full = [jnp.where(gi == GATE_LANE0 + 2 * p + jnp.where(li >= LANES, 1, 0), 1.0, 0.0).astype(BF16)
                for p in pairs]
    quarter = jnp.where(li >= LANES, 2, 0) + jnp.where(li_in >= ML_DH, 1, 0)
    sel_half = [jnp.where(gi == GATE_LANE0 + 4 * q + quarter, 1.0, 0.0).astype(BF16)
                for q in range(ML_HEADS // 4)]
    t16 = lax.broadcasted_iota(jnp.int32, (16, LANES), 0)

    g_ig, ls = [], []
    for d in dirs:
        g_raw = g_refs[d][...]
        g_fg = g_raw if d == 0 else pltpu.roll(g_raw, LANES - 2 * ML_HEADS, axis=1)
        g_ig.append(jnp.where(head_lane, pltpu.roll(g_fg, ML_HEADS, axis=1), 0.0))
        ls.append(jnp.where(head_lane, _log_sigmoid(g_fg), 0.0))
    ls3 = [_split3(ls[d]) for d in dirs]
    lst3 = [_split3(ls[d].T) for d in dirs]
    b_cols = [sum(_dot(tri[d], t) for t in ls3[d]) for d in dirs]
    b_rows = [sum(_dot(t, tri_t[d]) for t in lst3[d]) for d in dirs]
    yield

    qp = [[q_refs[d][:, p * LANES:(p + 1) * LANES] for p in pairs] for d in dirs]
    kp = [[k_refs[d][:, p * LANES:(p + 1) * LANES] for p in pairs] for d in dirs]
    va = [[jnp.concatenate([v_refs[d][:, p * LANES:(p + 1) * LANES].astype(BF16), ones_b], axis=1) for p in pairs]
          for d in dirs]
    st = [[st_ref[d, p] for p in pairs] for d in dirs]
    qs = [[_dot(qp[d][p], st[d][p].astype(BF16)) for p in pairs] for d in dirs]
    qk = [[_dot_nt(jnp.where(hmask[h], qp[d][h // 2].astype(F32), 0.0).astype(BF16), kp[d][h // 2])
           for h in heads] for d in dirs]
    yield

    r_rows, cdiff, stacked = [], [], []
    for d in dirs:
        r_rows.append(g_ig[d].T - b_rows[d])
        bl_row = b_cols[d][CHUNK - 1:CHUNK, :] if d == 0 else b_cols[d][0:1, :]
        m_row = m_ref[d]
        cm8 = r_rows[d][GATE_LANE0:GATE_LANE0 + ML_HEADS, :]
        order = s8 if d == 0 else CHUNK - 1 - s8
        step = 1
        while step < CHUNK:
            shifted = pltpu.roll(cm8, step if d == 0 else CHUNK - step, axis=1)
            cm8 = jnp.where(order >= step, jnp.maximum(cm8, shifted), cm8)
            step *= 2
        cm = jnp.concatenate([jnp.zeros((GATE_LANE0, CHUNK), F32), cm8,
                              jnp.zeros((LANES - GATE_LANE0 - ML_HEADS, CHUNK), F32)], axis=0).T
        a_cols = b_cols[d] + m_row
        mt_cols = jnp.maximum(a_cols, b_cols[d] + cm)
        g_cols = bl_row - b_cols[d] + g_ig[d]
        m_new = jnp.maximum(bl_row + m_row, jnp.max(g_cols, axis=0, keepdims=True))
        dp_cols = jnp.where(t16 >= 0, jnp.exp(bl_row + m_row - m_new), 0.0)
        m_ref[d] = m_new
        cdiff.append(b_cols[d] - mt_cols)
        stacked.append(jnp.concatenate([jnp.exp(a_cols - mt_cols), jnp.exp(-mt_cols),
                                        jnp.exp(g_cols - m_new), dp_cols], axis=0))

    c_full = [[_select_dot(cdiff[d], sel_full[p]) for p in pairs] for d in dirs]
    halves2 = [[_select_dot(stacked[d], sel_half[q]) for q in range(ML_HEADS // 4)] for d in dirs]
    halves = [[halves2[d][p // 2][:, (p % 2) * LANES:(p % 2 + 1) * LANES] for p in pairs]
              for d in dirs]
    yield

    sc = [[None] * ML_HEADS for _ in dirs]
    for d in dirs:
        for h in heads:
            p, j = h // 2, h % 2
            dlog = c_full[d][p][:, j * LANES:(j + 1) * LANES] + r_rows[d][GATE_LANE0 + h:GATE_LANE0 + h + 1, :]
            sc[d][h] = (qk[d][h] * jnp.exp(jnp.where(causal[d], dlog, -jnp.inf))).astype(BF16)
    sv = [[_dot(sc[d][h], va[d][h // 2]) for h in heads] for d in dirs]
    upd = [[_dot_tn((kp[d][p].astype(F32) * halves[d][p][2 * CHUNK:3 * CHUNK]).astype(BF16), va[d][p])
            for p in pairs] for d in dirs]
    yield

    for d in dirs:
        outs = []
        for p in pairs:
            inter_pair = halves[d][p][0:CHUNK]
            emt_pair = halves[d][p][CHUNK:2 * CHUNK]
            dp_pair = halves[d][p][3 * CHUNK:3 * CHUNK + 1]
            sv_pair = jnp.where(lo_half2, sv[d][2 * p], sv[d][2 * p + 1])
            num = sv_pair[:, :LANES] + inter_pair * qs[d][p][:, :LANES]
            den = sv_pair[:, LANES:] + inter_pair * qs[d][p][:, LANES:]
            outs.append(num / jnp.maximum(jnp.abs(den), emt_pair))
            st_ref[d, p] = (jnp.concatenate([dp_pair, dp_pair], axis=1) * st[d][p]
                            + jnp.where(same_head, upd[d][p], 0.0))
        o_refs[d][...] = jnp.concatenate(outs, axis=1)


N_MLSTM_IN, N_RET_IN = 8, 12


def _scan_kernel(*refs):
    n_in = N_MLSTM_IN + N_RET_IN
    ml_in, rt_in = refs[:N_MLSTM_IN], refs[N_MLSTM_IN:n_in]
    mo_f, mo_b, ro_f, ro_b, ml_st, ml_m, rt_st = refs[n_in:]
    bodies = [_mlstm2_kernel(*ml_in, mo_f, mo_b, ml_st, ml_m), _ret_kernel(*rt_in, ro_f, ro_b, rt_st)]
    while bodies:
        for body in list(bodies):
            if next(body, "done") == "done":
                bodies.remove(body)


def _scans(q, k, slab, rope_r, dmat, xi, kd, cd, nl, ncx):
    bsz, ltot, _ = slab.shape
    hw = ML_HEADS * ML_DH
    qw = RET_HEADS * RET_DK
    vw = RET_HEADS * RET_DV

    def ml_specs(d):
        ch = lambda bb, s: _chunk_of(d, s, nl, ncx)
        return [pl.BlockSpec((None, CHUNK, hw), lambda bb, s: (bb, ch(bb, s), 0)),
                pl.BlockSpec((None, CHUNK, hw), lambda bb, s: (bb, ch(bb, s), 0)),
                pl.BlockSpec((None, CHUNK, hw), lambda bb, s: (bb, ch(bb, s), OFF_V // hw)),
                pl.BlockSpec((None, CHUNK, LANES), lambda bb, s: (bb, ch(bb, s), OFF_MISC // LANES))]

    def rt_specs(d):
        ch = lambda bb, s: _chunk_of(d, s, nl, ncx)
        return [pl.BlockSpec((None, CHUNK, qw), lambda bb, s: (bb, ch(bb, s), OFF_RQ // qw)),
                pl.BlockSpec((None, CHUNK, qw), lambda bb, s: (bb, ch(bb, s), OFF_RK // qw)),
                pl.BlockSpec((None, CHUNK, vw), lambda bb, s: (bb, ch(bb, s), OFF_RV // vw)),
                pl.BlockSpec((3, CHUNK, LANES), lambda bb, s: (0, ch(bb, s), 0))]

    full = lambda a: pl.BlockSpec(a.shape, lambda bb, s: (0,) * a.ndim)
    out = lambda d, w: pl.BlockSpec((None, CHUNK, w), lambda bb, s: (bb, _chunk_of(d, s, nl, ncx), 0))
    return pl.pallas_call(
        _scan_kernel,
        grid=(bsz, nl + ncx),
        in_specs=(ml_specs(0) + ml_specs(1) + rt_specs(0) + rt_specs(1)
                  + [full(dmat), full(xi), full(kd), full(cd)]),
        out_specs=[out(0, hw), out(1, hw), out(0, vw), out(1, vw)],
        out_shape=[jax.ShapeDtypeStruct((bsz, ltot, hw), F32)] * 2 + [jax.ShapeDtypeStruct((bsz, ltot, vw), F32)] * 2,
        scratch_shapes=[pltpu.VMEM((2, ML_HEADS // 2, LANES, 2 * LANES), F32),
                        pltpu.VMEM((2, 1, LANES), F32),
                        pltpu.VMEM((2, RET_HEADS // 2, LANES, LANES), F32)],
        compiler_params=_params(("arbitrary", "arbitrary")),
        name="scans",
    )(q, k, slab, slab, q, k, slab, slab, slab, slab, slab, rope_r, slab, slab, slab, rope_r, dmat, xi, kd, cd)


def _rope_lanes(x, tab_ref):
    return (x * tab_ref[0] + pltpu.roll(x, 8, axis=1) * tab_ref[1]
            + pltpu.roll(x, LANES - 8, axis=1) * tab_ref[2])


def _ret_kernel(qf_ref, kf_ref, vf_ref, rf_ref, qb_ref, kb_ref, vb_ref, rb_ref, dmat_ref, xi_ref, kd_ref, cd_ref,
                of_ref, ob_ref, st_ref):
    s = pl.program_id(1)

    @pl.when(s == 0)
    def _():
        st_ref[...] = jnp.zeros_like(st_ref)

    dirs = (0, 1)
    q_refs, k_refs, v_refs, r_refs, o_refs = (qf_ref, qb_ref), (kf_ref, kb_ref), (vf_ref, vb_ref), (rf_ref, rb_ref), \
        (of_ref, ob_ref)
    lane = _lane_iota((1, LANES))
    lo_half = lane < RET_DV
    ri = lax.broadcasted_iota(jnp.int32, (LANES, LANES), 0)
    ci = lax.broadcasted_iota(jnp.int32, (LANES, LANES), 1)
    heads = range(RET_HEADS)
    pairs = range(RET_HEADS // 2)
    qg = [[_rope_lanes(q_refs[d][:, g * LANES:(g + 1) * LANES], r_refs[d]) for g in range(2)] for d in dirs]
    kg = [[_rope_lanes(k_refs[d][:, g * LANES:(g + 1) * LANES], r_refs[d]) * (RET_DK ** -0.5) for g in range(2)]
          for d in dirs]
    qg_b = [[x.astype(BF16) for x in qg[d]] for d in dirs]
    kg_b = [[x.astype(BF16) for x in kg[d]] for d in dirs]
    kw_b = [[(kg[d][g] * kd_ref[d, :, g * LANES:(g + 1) * LANES]).astype(BF16) for g in range(2)] for d in dirs]
    vp_b = [[v_refs[d][:, p * LANES:(p + 1) * LANES].astype(BF16) for p in pairs] for d in dirs]
    st = [[st_ref[d, p] for p in pairs] for d in dirs]

    cross = [[_dot(qg_b[d][p // 2], st[d][p].astype(BF16)) for p in pairs] for d in dirs]
    qk = [[None] * RET_HEADS for _ in dirs]
    for d in dirs:
        for h in heads:
            q_lo = (h % 4) * RET_DK
            hm = (lane >= q_lo) & (lane < q_lo + RET_DK)
            qk[d][h] = _dot_nt(jnp.where(hm, qg[d][h // 4], 0.0).astype(BF16), kg_b[d][h // 4])
    yield
    sv = [[_dot((qk[d][h] * dmat_ref[d, h]).astype(BF16), vp_b[d][h // 2]) for h in heads] for d in dirs]
    upd = [[_dot_tn(kw_b[d][p // 2], vp_b[d][p]) for p in pairs] for d in dirs]
    yield

    for d in dirs:
        outs = []
        for p in pairs:
            a = p % 2
            outs.append(jnp.where(lo_half, sv[d][2 * p], sv[d][2 * p + 1])
                        + xi_ref[d, :, p * LANES:(p + 1) * LANES] * cross[d][p])
            r_lo = (ri >= 2 * a * RET_DK) & (ri < (2 * a + 1) * RET_DK)
            r_hi = (ri >= (2 * a + 1) * RET_DK) & (ri < (2 * a + 2) * RET_DK)
            valid = (r_lo & (ci < RET_DV)) | (r_hi & (ci >= RET_DV))
            st_ref[d, p] = cd_ref[d, :, p * LANES:(p + 1) * LANES] * st[d][p] + jnp.where(valid, upd[d][p], 0.0)
        o_refs[d][...] = jnp.concatenate(outs, axis=1)


def _ret_tables():
    lg = jnp.log1p(-jnp.exp2(-5.0 - jnp.arange(RET_HEADS, dtype=F32)))
    idx = jnp.arange(CHUNK, dtype=F32)
    diff = idx[:, None] - idx[None, :]
    dm, xis, kds, cds = [], [], [], []
    for lgd, sign in ((lg, 1.0), (lg[::-1], -1.0)):
        dd = diff[None] * sign
        dm.append(jnp.exp(jnp.where(dd >= 0, dd * lgd[:, None, None], -jnp.inf)))
        order = idx if sign > 0 else (CHUNK - 1.0 - idx)
        xi = jnp.exp((order + 1.0)[None] * lgd[:, None])
        kdv = jnp.exp((CHUNK - 1.0 - order)[None] * lgd[:, None])
        cdv = jnp.exp(CHUNK * lgd)
        xis.append(jnp.repeat(xi.T, RET_DV, axis=1))
        kds.append(jnp.repeat(kdv.T, RET_DK, axis=1))
        cds.append(jnp.repeat(cdv, RET_DV)[None, :])
    return jnp.stack(dm), jnp.stack(xis), jnp.stack(kds), jnp.stack(cds)


def _rope_tables(n_lat, n_ctx):
    t = jnp.arange(n_lat)
    row = (t // GRID_W).astype(F32)
    col = (t % GRID_W).astype(F32)
    half = ROPE_D // 4
    inv = ROPE_BASE ** (-jnp.arange(half, dtype=F32) / half)
    ar, ac = row[:, None] * inv, col[:, None] * inv
    z = jnp.zeros_like(ar)
    c32 = jnp.concatenate([jnp.cos(ar), jnp.cos(ar), jnp.cos(ac), jnp.cos(ac)], axis=1)
    s1 = jnp.concatenate([z, jnp.sin(ar), z, jnp.sin(ac)], axis=1)
    s2 = jnp.concatenate([-jnp.sin(ar), z, -jnp.sin(ac), z], axis=1)
    tab = jnp.stack([c32, s1, s2])
    ident = jnp.stack([jnp.ones((n_ctx, ROPE_D), F32), jnp.zeros((n_ctx, ROPE_D), F32),
                       jnp.zeros((n_ctx, ROPE_D), F32)])
    return jnp.concatenate([tab, ident], axis=1)


def _mla_kernel(ql_ref, kv_ref, misc_ref, qan_ref, wqt_ref, kvn_ref, wk_ref, wvt_ref, qnw_ref, knw_ref,
                ropek_ref, ropeq_ref, qt_ref, k_ref, vt_ref):
    ql = ql_ref[...]
    qn = ql * lax.rsqrt(jnp.mean(ql * ql, axis=-1, keepdims=True) + EPS) * qan_ref[...]
    qt = _dot_nt(wqt_ref[...].astype(BF16), qn.astype(BF16))
    for h in range(MLA_HEADS):
        xh = qt[h * LANES:(h + 1) * LANES, :]
        ss = jnp.sum(xh * xh, axis=0, keepdims=True)
        xh = xh * lax.rsqrt(ss * (1.0 / QK_D) + EPS) * qnw_ref[...]
        xh = (xh * ropeq_ref[0] + pltpu.roll(xh, 8, axis=0) * ropeq_ref[1]
              + pltpu.roll(xh, LANES - 8, axis=0) * ropeq_ref[2])
        qt_ref[h * LANES:(h + 1) * LANES, :] = (xh * (QK_D ** -0.5 * LOG2E)).astype(BF16)

    kv = kv_ref[...]
    kvn = (kv * lax.rsqrt(jnp.mean(kv * kv, axis=-1, keepdims=True) + EPS) * kvn_ref[...]).astype(BF16)
    kk = _dot(kvn, wk_ref[...].astype(BF16))
    lane = _lane_iota((1, LANES))
    rope_part = (lane >= NOPE_D) & (lane < QK_D)
    kr = jnp.where(rope_part, pltpu.roll(misc_ref[...], NOPE_D - 4 * ML_HEADS, axis=1), 0.0)
    for h in range(MLA_HEADS):
        xh = kk[:, h * LANES:(h + 1) * LANES] + kr
        ss = jnp.sum(xh * xh, axis=1, keepdims=True)
        xh = xh * lax.rsqrt(ss * (1.0 / QK_D) + EPS) * knw_ref[...]
        k_ref[:, h * LANES:(h + 1) * LANES] = _rope_lanes(xh, ropek_ref).astype(BF16)
    vt_ref[...] = _dot_nt(wvt_ref[...].astype(BF16), kvn).astype(BF16)


def _mla(slab, qan, wqt, kvn, wk, wvt, qnw, knw, rope_k, rope_q):
    bsz, ltot, _ = slab.shape
    hq = MLA_HEADS * LANES
    hv = MLA_HEADS * MLA_DV
    full = lambda shape: pl.BlockSpec(shape, lambda bb, i: (0,) * len(shape))
    return pl.pallas_call(
        _mla_kernel,
        grid=(bsz, ltot // TM),
        in_specs=[pl.BlockSpec((None, TM, Q_LORA), lambda bb, i: (bb, i, OFF_QL // Q_LORA)),
                  pl.BlockSpec((None, TM, KV_LORA), lambda bb, i: (bb, i, OFF_KV // KV_LORA)),
                  pl.BlockSpec((None, TM, LANES), lambda bb, i: (bb, i, OFF_MISC // LANES)),
                  full((1, Q_LORA)), full((hq, Q_LORA)), full((1, KV_LORA)), full((KV_LORA, hq)),
                  full((hv, KV_LORA)), full((LANES, 1)), full((1, LANES)),
                  pl.BlockSpec((3, TM, LANES), lambda bb, i: (0, i, 0)),
                  pl.BlockSpec((3, LANES, TM), lambda bb, i: (0, 0, i))],
        out_specs=[pl.BlockSpec((None, hq, TM), lambda bb, i: (bb, 0, i)),
                   pl.BlockSpec((None, TM, hq), lambda bb, i: (bb, i, 0)),
                   pl.BlockSpec((None, None, hv, TM), lambda bb, i: (bb, i, 0, 0))],
        out_shape=[jax.ShapeDtypeStruct((bsz, hq, ltot), BF16),
                   jax.ShapeDtypeStruct((bsz, ltot, hq), BF16),
                   jax.ShapeDtypeStruct((bsz, ltot // TM, hv, TM), BF16)],
        compiler_params=_params(("arbitrary", "arbitrary")),
        name="mla",
    )(slab, slab, slab, qan, wqt, kvn, wk, wvt, qnw, knw, rope_k, rope_q)


ATT_QC = 512
ATT_ONES = 16
ATT_AHEAD = 3


ATT_BOUND_MAX = 60.0


def _attn_kernel(flag_ref, qt_ref, k_ref, vt_ref, ub_ref, o_ref, acc_ref, m_ref, l_ref, *, r, nkb):
    tq = qt_ref.shape[1]
    tkb = r * TM
    ones = jnp.ones((ATT_ONES, TM), BF16)
    qc = min(ATT_QC, tq)
    units = [(j, c) for j in range(2) for c in range(tq // qc)]
    ahead = min(ATT_AHEAD, len(units))
    bounded = flag_ref[0] == 1
    ub = ub_ref[0:1, 0:1]

    def scores(r0, u):
        j, c = units[u]
        return _dot(k_ref[pl.ds(r0, tkb), j * LANES:(j + 1) * LANES],
                    qt_ref[j * LANES:(j + 1) * LANES, c * qc:(c + 1) * qc])

    def weighted_values(kb, j, p):
        pb = p.astype(BF16)
        pv = None
        for t in range(r):
            vt = jnp.concatenate([vt_ref[kb * r + t, j * MLA_DV:(j + 1) * MLA_DV, :], ones], axis=0)
            part = _dot(vt, pb[t * TM:(t + 1) * TM])
            pv = part if pv is None else pv + part
        return pv[0:MLA_DV], pv[MLA_DV:MLA_DV + 1]

    def bounded_block(kb, carry):
        r0 = pl.multiple_of(kb * tkb, tkb)
        pending = [scores(r0, u) for u in range(ahead)]
        for u, (j, c) in enumerate(units):
            sc = pending.pop(0)
            if u + ahead < len(units):
                pending.append(scores(r0, u + ahead))
            cs = slice(c * qc, (c + 1) * qc)
            pv, ps = weighted_values(kb, j, jnp.exp2(sc - ub))
            acc_ref[j * MLA_DV:(j + 1) * MLA_DV, cs] += pv
            l_ref[j:j + 1, cs] += ps
        return carry

    def online_block(kb, carry):
        r0 = pl.multiple_of(kb * tkb, tkb)
        pending = [scores(r0, u) for u in range(ahead)]
        for u, (j, c) in enumerate(units):
            sc = pending.pop(0)
            if u + ahead < len(units):
                pending.append(scores(r0, u + ahead))
            cs = slice(c * qc, (c + 1) * qc)
            m_old = m_ref[j:j + 1, cs]
            m_new = jnp.maximum(m_old, jnp.max(sc, axis=0, keepdims=True))
            alpha = jnp.exp2(m_old - m_new)
            pv, ps = weighted_values(kb, j, jnp.exp2(sc - m_new))
            acc_ref[j * MLA_DV:(j + 1) * MLA_DV, cs] = alpha * acc_ref[j * MLA_DV:(j + 1) * MLA_DV, cs] + pv
            l_ref[j:j + 1, cs] = alpha * l_ref[j:j + 1, cs] + ps
            m_ref[j:j + 1, cs] = m_new
        return carry

    acc_ref[...] = jnp.zeros_like(acc_ref)
    l_ref[...] = jnp.zeros_like(l_ref)

    @pl.when(bounded)
    def _():
        lax.fori_loop(0, nkb, bounded_block, 0)

    @pl.when(jnp.logical_not(bounded))
    def _():
        m_ref[...] = jnp.full_like(m_ref, -jnp.inf)
        lax.fori_loop(0, nkb, online_block, 0)

    head_rows = lax.broadcasted_iota(jnp.int32, acc_ref.shape, 0)
    inv = jnp.where(head_rows < MLA_DV, 1.0 / l_ref[0:1, :], 1.0 / l_ref[1:2, :])
    o_ref[...] = (acc_ref[...] * inv).T


def _attn(flags, qt, k, vt, ub, tq, q_off, nq, key_tile0, r, nkb):
    bsz = qt.shape[0]
    hv = vt.shape[2]
    nkt = r * nkb
    assert key_tile0 % nkt == 0
    kern = functools.partial(_attn_kernel, r=r, nkb=nkb)
    return pl.pallas_call(
        kern,
        grid_spec=pltpu.PrefetchScalarGridSpec(
            num_scalar_prefetch=1,
            grid=(bsz, MLA_HEADS // 2, nq),
            in_specs=[pl.BlockSpec((None, 2 * LANES, tq), lambda bb, hp, qi, fl: (bb, hp, qi + q_off)),
                      pl.BlockSpec((None, nkt * TM, 2 * LANES), lambda bb, hp, qi, fl: (bb, key_tile0 // nkt, hp)),
                      pl.BlockSpec((None, nkt, 2 * MLA_DV, TM),
                                   lambda bb, hp, qi, fl: (bb, key_tile0 // nkt, hp, 0)),
                      pl.BlockSpec((8, LANES), lambda bb, hp, qi, fl: (0, 0))],
            out_specs=pl.BlockSpec((None, tq, 2 * MLA_DV), lambda bb, hp, qi, fl: (bb, qi, hp)),
            scratch_shapes=[pltpu.VMEM((2 * MLA_DV, tq), F32), pltpu.VMEM((8, tq), F32),
                            pltpu.VMEM((8, tq), F32)]),
        out_shape=jax.ShapeDtypeStruct((bsz, nq * tq, hv), F32),
        compiler_params=_params(("arbitrary", "arbitrary", "arbitrary")),
        name="attn",
    )(flags, qt, k, vt, ub)


def _split_dot(x, w_b):
    hi = x.astype(BF16)
    lo = (x - hi.astype(F32)).astype(BF16)
    return _dot(hi, w_b) + _dot(lo, w_b)


def _head_ln(x, g_ref, gt_ref, width):
    mu = _split_dot(_split_dot(x, g_ref[...]) * (1.0 / width), gt_ref[...])
    xc = x - mu
    var = _split_dot(_split_dot(xc * xc, g_ref[...]) * (1.0 / width), gt_ref[...])
    return xc * lax.rsqrt(var + EPS)


def _merge_kernel(hmf_ref, hmb_ref, og_ref, mlw_ref, ybl_ref, ybc_ref, hrf_ref, hrb_ref, rg_ref, rnw_ref, bl_ref,
                  wbr_ref, wout_ref, x_ref, mod_ref, n2w_ref, rw_ref, g_ref, gt_ref,
                  xo_ref, h2_ref, aff_ref, wbr_b, wout_b, *, nlt):
    @pl.when((pl.program_id(0) == 0) & (pl.program_id(1) == 0))
    def _():
        wbr_b[...] = wbr_ref[...].astype(BF16)
        wout_b[...] = wout_ref[...].astype(BF16)

    is_ctx = pl.program_id(1) >= nlt

    subs = [slice(r0, r0 + ROW_SUB) for r0 in range(0, TM, ROW_SUB)]
    ys = []
    for rs in subs:
        ya = _sigmoid(og_ref[rs, :]) * (_head_ln(hmf_ref[rs, :] + hmb_ref[rs, :], g_ref, gt_ref, ML_DH)
                                        * mlw_ref[...])
        yc = _silu(rg_ref[rs, :]) * (_head_ln(hrf_ref[rs, :] + hrb_ref[rs, :], g_ref, gt_ref, RET_DV)
                                     * rnw_ref[...])
        yb = jnp.where(is_ctx, ybc_ref[rs, :], ybl_ref[rs, :])
        ys.append((ya.astype(BF16), yb.astype(BF16), yc.astype(BF16)))
    br = [[_dot(ys[i][n], wbr_b[n]) for n in range(N_BRANCH)] for i in range(len(subs))]
    merged = []
    for i, rs in enumerate(subs):
        m = None
        for n in range(N_BRANCH):
            term = _sigmoid(bl_ref[rs, n * D_MODEL:(n + 1) * D_MODEL]) * br[i][n]
            m = term if m is None else m + term
        merged.append(m.astype(BF16))
    y = [_dot(m, wout_b[...]) for m in merged]
    g1 = mod_ref[:, 2 * D_MODEL:3 * D_MODEL]
    h2s = []
    for i, rs in enumerate(subs):
        x = x_ref[rs, :] + g1 * y[i]
        xo_ref[rs, :] = x
        hn = x * lax.rsqrt(jnp.mean(x * x, axis=-1, keepdims=True) + EPS) * n2w_ref[...]
        h2 = hn * (1.0 + mod_ref[:, 4 * D_MODEL:5 * D_MODEL]) + mod_ref[:, 3 * D_MODEL:4 * D_MODEL]
        h2_ref[rs, :] = h2.astype(BF16)
        h2s.append(h2)
    logits = []
    for h2 in h2s:
        hi, lo = _split2(h2)
        logits.append(_dot(hi, rw_ref[0]) + _dot(hi, rw_ref[1]) + _dot(lo, rw_ref[0]))
    valid = _lane_iota((1, LANES)) < N_EXPERTS
    for i, rs in enumerate(subs):
        lg = jnp.where(valid, logits[i], -jnp.inf)
        e = jnp.exp(lg - jnp.max(lg, axis=-1, keepdims=True))
        aff_ref[:, rs] = (e / jnp.sum(e, axis=-1, keepdims=True)).T[0:N_EXPERTS, :]


def _merge(hmf, hmb, slab, mlw, yb_l, yb_c, hrf, hrb, rnw, wbr, wout, x, mods, n2w, rw, g8, g8t, nlt):
    bsz, ltot, _ = x.shape
    hw = ML_HEADS * ML_DH

    def mrow(bb, i):
        return (jnp.where(i < nlt, bb, bsz), 0, 0)

    full = lambda shape: pl.BlockSpec(shape, lambda bb, i: (0,) * len(shape))
    row = lambda width, off: pl.BlockSpec((None, TM, width), lambda bb, i: (bb, i, off // width))
    return pl.pallas_call(
        functools.partial(_merge_kernel, nlt=nlt),
        grid=(bsz, ltot // TM),
        in_specs=[row(hw, 0), row(hw, 0),
                  row(hw, OFF_OG), full((1, hw)),
                  pl.BlockSpec((None, TM, hw), lambda bb, i: (bb, jnp.minimum(i, nlt - 1), 0)),
                  pl.BlockSpec((None, TM, hw), lambda bb, i: (bb, 0, 0)),
                  row(hw, 0), row(hw, 0),
                  row(hw, OFF_RG), full((1, hw)),
                  row(N_BRANCH * D_MODEL, OFF_BL),
                  full((N_BRANCH, BRANCH_W, D_MODEL)), full((D_MODEL, D_MODEL)),
                  row(D_MODEL, 0),
                  pl.BlockSpec((None, 1, 6 * D_MODEL), mrow),
                  full((1, D_MODEL)), full((2, D_MODEL, LANES)), full((hw, LANES)), full((LANES, hw))],
        out_specs=[row(D_MODEL, 0), row(D_MODEL, 0), pl.BlockSpec((None, N_EXPERTS, TM), lambda bb, i: (bb, 0, i))],
        out_shape=[jax.ShapeDtypeStruct((bsz, ltot, D_MODEL), F32),
                   jax.ShapeDtypeStruct((bsz, ltot, D_MODEL), BF16),
                   jax.ShapeDtypeStruct((bsz, N_EXPERTS, ltot), F32)],
        scratch_shapes=[pltpu.VMEM((N_BRANCH, BRANCH_W, D_MODEL), BF16), pltpu.VMEM((D_MODEL, D_MODEL), BF16)],
        compiler_params=_params(("arbitrary", "arbitrary")),
        name="merge",
    )(hmf, hmb, slab, mlw, yb_l, yb_c, hrf, hrb, slab, rnw, slab, wbr, wout, x, mods, n2w, rw, g8, g8t)


def _select_kernel(aff_ref, pos_ref, post_ref, gwt_ref, off_ref, *, n, cap, base0, base_step):
    b = pl.program_id(0)
    base = (base0 + b * base_step).astype(F32)
    bits = lax.bitcast_convert_type(aff_ref[...], jnp.int32)
    capf = jnp.float32(cap)

    def search(it, cur):
        cand = cur | jnp.left_shift(jnp.int32(1), 30 - it)
        cnt = jnp.sum(jnp.where(bits >= cand, 1.0, 0.0), axis=1, keepdims=True)
        return jnp.where(cnt >= capf, cand, cur)

    thr = lax.fori_loop(0, 31, search, jnp.zeros((N_EXPERTS, 1), jnp.int32))
    n_gt = jnp.sum(jnp.where(bits > thr, 1.0, 0.0), axis=1, keepdims=True)
    need = capf - n_gt

    ri = lax.broadcasted_iota(jnp.int32, (TM, TM), 0)
    ci = lax.broadcasted_iota(jnp.int32, (TM, TM), 1)
    before = (ri < ci).astype(BF16)
    half_lane = _lane_iota((N_EXPERTS, LANES))
    unused = jnp.full((LANES - N_EXPERTS, TM), -1.0, F32)

    def tile(i, carry):
        c_eq, c_sel, offs = carry
        r0 = pl.multiple_of(i * TM, TM)
        a = aff_ref[:, pl.ds(r0, TM)]
        bt = lax.bitcast_convert_type(a, jnp.int32)
        eq = bt == thr
        rank = _dot(jnp.where(eq, 1.0, 0.0).astype(BF16), before) + c_eq
        sel = (bt > thr) | (eq & (rank < need))
        self_ = jnp.where(sel, 1.0, 0.0)
        pos = _dot(self_.astype(BF16), before) + c_sel
        posv = jnp.where(sel, pos + base, -1.0)
        post_ref[:, pl.ds(r0, TM)] = posv
        gwt_ref[:, pl.ds(r0, TM)] = jnp.where(sel, a, 0.0)
        pos_ref[pl.ds(r0, TM), :] = jnp.concatenate([posv, unused], axis=0).T
        c_half = c_sel + jnp.sum(self_[:, 0:TM // 2], axis=1, keepdims=True)
        offs = jnp.where(half_lane == 2 * i, c_sel, offs)
        offs = jnp.where(half_lane == 2 * i + 1, c_half, offs)
        return (c_eq + jnp.sum(jnp.where(eq, 1.0, 0.0), axis=1, keepdims=True),
                c_sel + jnp.sum(self_, axis=1, keepdims=True), offs)

    zero = jnp.zeros((N_EXPERTS, 1), F32)
    _, _, offs = lax.fori_loop(0, n // TM, tile, (zero, zero, jnp.zeros((N_EXPERTS, LANES), F32)))
    off_ref[...] = offs


def _select(afft, n, cap, tile_off, base0, base_step):
    bsz = afft.shape[0]
    assert 2 * (n // TM) <= LANES
    kern = functools.partial(_select_kernel, n=n, cap=cap, base0=base0, base_step=base_step)
    return pl.pallas_call(
        kern,
        grid=(bsz,),
        in_specs=[pl.BlockSpec((None, N_EXPERTS, n), lambda bb: (bb, 0, tile_off))],
        out_specs=[pl.BlockSpec((None, n, LANES), lambda bb: (bb, 0, 0)),
                   pl.BlockSpec((None, N_EXPERTS, n), lambda bb: (bb, 0, 0)),
                   pl.BlockSpec((None, N_EXPERTS, n), lambda bb: (bb, 0, 0)),
                   pl.BlockSpec((None, N_EXPERTS, LANES), lambda bb: (bb, 0, 0))],
        out_shape=[jax.ShapeDtypeStruct((bsz, n, LANES), F32),
                   jax.ShapeDtypeStruct((bsz, N_EXPERTS, n), F32),
                   jax.ShapeDtypeStruct((bsz, N_EXPERTS, n), F32),
                   jax.ShapeDtypeStruct((bsz, N_EXPERTS, LANES), F32)],
        compiler_params=_params(("arbitrary",)),
        name="select",
    )(afft)


SLOT_ALIGN = 16
WIN = TM + SLOT_ALIGN
TC = TM // 2
WINC = TC + SLOT_ALIGN
GATHER_MAX_TILES = 11
FFN_ROW_CHUNKS = 4
FFN_TF = 512


GATHER_EXPERTS = 2


def _moe_gather_kernel(st_ref, h_ref, pt_ref, gt_ref, xs_ref, gs_ref, *, nt256, tiles):
    e0 = pl.program_id(0) * GATHER_EXPERTS
    i = pl.program_id(1)

    @pl.when(i == 0)
    def _():
        xs_ref[...] = jnp.zeros_like(xs_ref)
        gs_ref[...] = jnp.zeros_like(gs_ref)

    slot = lax.broadcasted_iota(jnp.int32, (WIN, TM), 0)
    for u in range(tiles):
        starts = [pl.multiple_of(st_ref[(e0 + x) * nt256 + i * tiles + u], SLOT_ALIGN) for x in range(GATHER_EXPERTS)]
        match = [(slot + starts[x]).astype(F32) == pt_ref[x, :, u * TM:(u + 1) * TM] for x in range(GATHER_EXPERTS)]
        onehot = jnp.concatenate([jnp.where(m, 1.0, 0.0).astype(BF16) for m in match], axis=0)
        got_all = _dot(onehot, h_ref[u * TM:(u + 1) * TM, :])
        for x in range(GATHER_EXPERTS):
            st = starts[x]
            st2 = pl.multiple_of(st + SLOT_ALIGN, SLOT_ALIGN)
            got = got_all[x * WIN:(x + 1) * WIN]
            head = xs_ref[x, pl.ds(st, SLOT_ALIGN), :].astype(F32)
            xs_ref[x, pl.ds(st, SLOT_ALIGN), :] = (head + got[0:SLOT_ALIGN]).astype(BF16)
            xs_ref[x, pl.ds(st2, TM), :] = got[SLOT_ALIGN:WIN].astype(BF16)
            gate = jnp.sum(jnp.where(match[x], gt_ref[x, :, u * TM:(u + 1) * TM], 0.0), axis=1, keepdims=True)
            gate = jnp.broadcast_to(gate, (WIN, LANES))
            gs_ref[x, pl.ds(st, SLOT_ALIGN), :] = gs_ref[x, pl.ds(st, SLOT_ALIGN), :] + gate[0:SLOT_ALIGN]
            gs_ref[x, pl.ds(st2, TM), :] = gate[SLOT_ALIGN:WIN]


def _moe_gather(starts, h2, post, gwt, rows, tiles):
    ttot = h2.shape[0]
    tt = tiles * TM
    kern = functools.partial(_moe_gather_kernel, nt256=ttot // TM, tiles=tiles)
    return pl.pallas_call(
        kern,
        grid_spec=pltpu.PrefetchScalarGridSpec(
            num_scalar_prefetch=1,
            grid=(N_EXPERTS // GATHER_EXPERTS, ttot // tt),
            in_specs=[pl.BlockSpec((tt, D_MODEL), lambda e, i, st: (i, 0)),
                      pl.BlockSpec((GATHER_EXPERTS, 1, tt), lambda e, i, st: (e, 0, i)),
                      pl.BlockSpec((GATHER_EXPERTS, 1, tt), lambda e, i, st: (e, 0, i))],
            out_specs=[pl.BlockSpec((GATHER_EXPERTS, rows, D_MODEL), lambda e, i, st: (e, 0, 0)),
                       pl.BlockSpec((GATHER_EXPERTS, rows, LANES), lambda e, i, st: (e, 0, 0))]),
        out_shape=[jax.ShapeDtypeStruct((N_EXPERTS, rows, D_MODEL), BF16),
                   jax.ShapeDtypeStruct((N_EXPERTS, rows, LANES), F32)],
        compiler_params=_params(("arbitrary", "arbitrary")),
        name="moe_gather",
    )(starts, h2, post, gwt)


def _moe_ffn_kernel(xs_ref, gs_ref, w1_ref, w3_ref, w2_ref, ys_ref, acc_ref, *, ct, rows):
    f = pl.program_id(1)

    @pl.when(f == 0)
    def _():
        acc_ref[...] = jnp.zeros_like(acc_ref)

    w1b = w1_ref[...].astype(BF16)
    w3b = w3_ref[...].astype(BF16)
    w2b = w2_ref[...].astype(BF16)
    rc = ct // FFN_ROW_CHUNKS
    chunks = [slice(r * rc, (r + 1) * rc) for r in range(FFN_ROW_CHUNKS)]
    up = [(_dot(xs_ref[chunks[0], :], w1b), _dot(xs_ref[chunks[0], :], w3b))]
    for r, rs in enumerate(chunks):
        a, b = up[r]
        if r + 1 < len(chunks):
            up.append((_dot(xs_ref[chunks[r + 1], :], w1b), _dot(xs_ref[chunks[r + 1], :], w3b)))
        acc_ref[rs, :] += _dot((_silu(a) * b).astype(BF16), w2b)

    @pl.when(f == pl.num_programs(1) - 1)
    def _():
        gate = gs_ref[...]
        for c in range(D_MODEL // LANES):
            cs = slice(c * LANES, (c + 1) * LANES)
            ys_ref[0:ct, cs] = (acc_ref[:, cs] * gate).astype(BF16)
        ys_ref[ct:rows, :] = jnp.zeros((rows - ct, D_MODEL), BF16)


def _moe_ffn(xs, gs, w1, w3, w2, layer, ct):
    rows = xs.shape[1]
    kern = functools.partial(_moe_ffn_kernel, ct=ct, rows=rows)
    return pl.pallas_call(
        kern,
        grid=(N_EXPERTS, EXPERT_FF // FFN_TF),
        in_specs=[pl.BlockSpec((None, ct, D_MODEL), lambda e, f: (e, 0, 0)),
                  pl.BlockSpec((None, ct, LANES), lambda e, f: (e, 0, 0)),
                  pl.BlockSpec((None, None, D_MODEL, FFN_TF), lambda e, f: (layer, e, 0, f)),
                  pl.BlockSpec((None, None, D_MODEL, FFN_TF), lambda e, f: (layer, e, 0, f)),
                  pl.BlockSpec((None, None, FFN_TF, D_MODEL), lambda e, f: (layer, e, f, 0))],
        out_specs=pl.BlockSpec((None, rows, D_MODEL), lambda e, f: (e, 0, 0)),
        out_shape=jax.ShapeDtypeStruct((N_EXPERTS, rows, D_MODEL), BF16),
        scratch_shapes=[pltpu.VMEM((ct, D_MODEL), F32)],
        compiler_params=_params(("arbitrary", "arbitrary")),
        name="moe_ffn",
    )(xs, gs, w1, w3, w2)


def _combine_kernel(st_ref, x_ref, pos_ref, mod_ref, *rest, tile_of, ntc):
    nsub = TM // TC
    ys_refs = rest[:N_EXPERTS]
    o_ref = rest[N_EXPERTS]
    i = tile_of(pl.program_id(0))
    slot = lax.broadcasted_iota(jnp.int32, (TC, WINC), 1)
    g2 = mod_ref[:, 5 * D_MODEL:6 * D_MODEL]
    for u in range(nsub):
        rs = slice(u * TC, (u + 1) * TC)
        pos = pos_ref[rs, :]
        acc = None
        for e in range(N_EXPERTS):
            st = st_ref[e * ntc + i * nsub + u]
            delta = pl.multiple_of(st - st_ref[e * ntc + i * nsub], SLOT_ALIGN)
            onehot = jnp.where((slot + st).astype(F32) == pos[:, e:e + 1], 1.0, 0.0).astype(BF16)
            part = _dot(onehot, ys_refs[e][0, pl.ds(delta, WINC), :])
            acc = part if acc is None else acc + part
        o_ref[rs, :] = x_ref[rs, :] + g2 * acc


def _combine(starts, x, pos, mods_rows, ys, nlt, ltiles, latent_only):
    ttot = x.shape[0]
    nt = ttot // TM
    nsub = TM // TC
    bsz = ttot // (ltiles * TM)
    steps = bsz * nlt if latent_only else nt
    tile_of = (lambda j: (j // nlt) * ltiles + j % nlt) if latent_only else (lambda j: j)

    def mrow(j, st):
        i = tile_of(j)
        return (jnp.where(i % ltiles < nlt, i // ltiles, bsz), 0, 0)

    def ys_spec(e):
        return pl.BlockSpec(
            (pl.Element(1), pl.Element(WIN), pl.Element(D_MODEL)),
            lambda j, st: (e, pl.multiple_of(st[e * nt * nsub + tile_of(j) * nsub], SLOT_ALIGN), 0))

    return pl.pallas_call(
        functools.partial(_combine_kernel, tile_of=tile_of, ntc=nt * nsub),
        grid_spec=pltpu.PrefetchScalarGridSpec(
            num_scalar_prefetch=1,
            grid=(steps,),
            in_specs=[pl.BlockSpec((TM, D_MODEL), lambda j, st: (tile_of(j), 0)),
                      pl.BlockSpec((TM, LANES), lambda j, st: (tile_of(j), 0)),
                      pl.BlockSpec((None, 1, 6 * D_MODEL), mrow)]
                     + [ys_spec(e) for e in range(N_EXPERTS)],
            out_specs=pl.BlockSpec((TM, D_MODEL), lambda j, st: (j, 0))),
        out_shape=jax.ShapeDtypeStruct((steps * TM, D_MODEL), F32),
        compiler_params=_params(("arbitrary",)),
        name="combine",
    )(starts, x, pos, mods_rows, *([ys] * N_EXPERTS))


def _permute_w_in(w, b):
    widths = (2 * ML_HEADS * ML_DH, ML_HEADS * ML_DH, ML_HEADS * ML_DH, 4 * ML_HEADS, Q_LORA, KV_LORA, ROPE_D,
              RET_HEADS * RET_DK, RET_HEADS * RET_DK, RET_HEADS * RET_DV, RET_HEADS * RET_DV, N_BRANCH * D_MODEL)
    offs = np.concatenate([[0], np.cumsum(widths)])
    seg = lambda a, k: a[..., offs[k]:offs[k + 1]]
    zeros = lambda a, n: jnp.zeros(a.shape[:-1] + (n,), a.dtype)

    def build(a):
        parts = [seg(a, k) for k in (11, 1, 2, 9, 10, 7, 8, 5)]
        parts.append(zeros(a, OFF_QL - (OFF_KV + KV_LORA)))
        parts += [seg(a, 4), seg(a, 3), seg(a, 6)]
        parts.append(zeros(a, SLAB_W - (OFF_MISC + 4 * ML_HEADS + ROPE_D)))
        parts.append(seg(a, 0))
        return jnp.concatenate(parts, axis=-1)

    return build(w), build(b)


def kernel(x, c, ctx, c_ctx, ada_w, ada_b, norm1_w, norm2_w, w_in, b_in, conv_w, conv_b, ml_norm_w, mla_qa_norm,
           mla_wq_b, mla_kva_norm, mla_wkv_b, q_norm_w, k_norm_w, ret_norm_w, w_branch, w_out, router_w,
           exp_w1, exp_w3, exp_w2):
    bsz, n_lat, _ = x.shape
    n_ctx = ctx.shape[1]
    depth = ada_w.shape[0]
    ltot = n_lat + n_ctx
    nlt = n_lat // TM
    ltiles = ltot // TM
    nl, ncx = n_lat // CHUNK, n_ctx // CHUNK
    assert n_lat % TM == 0 and n_ctx == TM and bsz + 1 <= 8
    gather_tiles = max(t for t in range(1, GATHER_MAX_TILES + 1) if (bsz * ltiles) % t == 0)

    cond8 = jnp.zeros((8, D_MODEL), F32).at[:bsz].set(c).at[bsz].set(c_ctx)
    mods_all = _ada(cond8, ada_w, ada_b)

    rope32 = _rope_tables(n_lat, n_ctx)
    rope_r = jnp.tile(rope32, (1, 1, LANES // ROPE_D))
    ident = jnp.stack([jnp.ones((ltot, 1), F32), jnp.zeros((ltot, 1), F32), jnp.zeros((ltot, 1), F32)])
    rope_k = jnp.concatenate([jnp.broadcast_to(ident, (3, ltot, NOPE_D)), rope32,
                              jnp.broadcast_to(ident, (3, ltot, LANES - QK_D))], axis=2)
    rope_q = jnp.swapaxes(rope_k, 1, 2)
    dmat, xi, kd, cd = _ret_tables()

    gi = jnp.arange(ML_HEADS * ML_DH) // ML_DH
    g8 = (gi[:, None] == jnp.arange(LANES)[None, :]).astype(BF16)
    g8t = g8.T

    cap_l = CAP_FACTOR * n_lat // N_EXPERTS
    cap_c = CAP_FACTOR * n_ctx // N_EXPERTS
    ct = bsz * (cap_l + cap_c)
    assert ct % (16 * FFN_ROW_CHUNKS) == 0 and cap_l % 16 == 0 and cap_c % 16 == 0
    rows = ct + WIN

    w_perm, b_perm = _permute_w_in(w_in, b_in[:, None, :])
    w_perm = w_perm.astype(BF16)
    xx = jnp.concatenate([x, ctx], axis=1)
    att_r = next(r for r in (11, 3, 1) if ltiles % r == 0)
    tq = min(2048, n_lat)

    for l in range(depth):
        mods = mods_all[l].reshape(8, 1, 6 * D_MODEL)
        slab, mq, mk = _inproj(xx, mods, norm1_w[l][None, :], w_perm, b_perm, conv_w[l], conv_b[l][None, :], l, nlt)
        hmf, hmb, hrf, hrb = _scans(mq, mk, slab, rope_r, dmat, xi, kd, cd, nl, ncx)

        wq = mla_wq_b[l].reshape(Q_LORA, MLA_HEADS, QK_D)
        wqt = jnp.pad(wq, ((0, 0), (0, 0), (0, LANES - QK_D))).reshape(Q_LORA, MLA_HEADS * LANES).T
        wkv = mla_wkv_b[l].reshape(KV_LORA, MLA_HEADS, NOPE_D + MLA_DV)
        wk = jnp.pad(wkv[:, :, :NOPE_D], ((0, 0), (0, 0), (0, LANES - NOPE_D))).reshape(KV_LORA, MLA_HEADS * LANES)
        wvt = wkv[:, :, NOPE_D:].reshape(KV_LORA, MLA_HEADS * MLA_DV).T
        qnw = jnp.pad(q_norm_w[l], (0, LANES - QK_D))[:, None]
        knw = jnp.pad(k_norm_w[l], (0, LANES - QK_D))[None, :]
        qt, kk, vt = _mla(slab, mla_qa_norm[l][None, :], wqt, mla_kva_norm[l][None, :], wk, wvt, qnw, knw,
                          rope_k, rope_q)
        bound = (1.01 * QK_D ** 0.5 * LOG2E) * jnp.max(jnp.abs(q_norm_w[l])) * jnp.max(jnp.abs(k_norm_w[l]))
        flags = (bound <= ATT_BOUND_MAX).astype(jnp.int32).reshape(1)
        ub = jnp.full((8, LANES), bound, F32)
        yb_l = _attn(flags, qt, kk, vt, ub, tq, 0, n_lat // tq, 0, att_r, ltiles // att_r)
        yb_c = _attn(flags, qt, kk, vt, ub, TM, nlt, 1, nlt, 1, 1)

        rw = jnp.stack(_split2(jnp.pad(router_w[l], ((0, 0), (0, LANES - N_EXPERTS)))))
        xm, h2, aff = _merge(hmf, hmb, slab, ml_norm_w[l][None, :], yb_l, yb_c, hrf, hrb, ret_norm_w[l][None, :],
                             w_branch[l],
                             w_out[l], xx, mods, norm2_w[l][None, :], rw, g8, g8t, nlt)

        pos_l, post_l, gwt_l, off_l = _select(aff, n_lat, cap_l, 0, 0, cap_l + cap_c)
        pos_c, post_c, gwt_c, off_c = _select(aff, n_ctx, cap_c, nlt, cap_l, cap_l + cap_c)
        pos = jnp.concatenate([pos_l, pos_c], axis=1).reshape(bsz * ltot, LANES)
        expert_major = lambda a_l, a_c: jnp.swapaxes(jnp.concatenate([a_l, a_c], axis=2), 0, 1).reshape(
            N_EXPERTS, 1, bsz * ltot)
        post = expert_major(post_l, post_c)
        gwt = expert_major(gwt_l, gwt_c)
        base_l = (jnp.arange(bsz) * (cap_l + cap_c))[:, None, None]
        base_c = base_l + cap_l
        off = jnp.concatenate(
            [off_l[:, :, :2 * nlt].astype(jnp.int32) // SLOT_ALIGN * SLOT_ALIGN + base_l,
             off_c[:, :, :2].astype(jnp.int32) // SLOT_ALIGN * SLOT_ALIGN + base_c], axis=2)
        starts_c = jnp.swapaxes(off, 0, 1).reshape(-1)
        starts_g = jnp.swapaxes(off[:, :, ::2], 0, 1).reshape(-1)

        xs, gs = _moe_gather(starts_g, h2.reshape(bsz * ltot, D_MODEL), post, gwt, rows, gather_tiles)
        ys = _moe_ffn(xs, gs, exp_w1, exp_w3, exp_w2, l, ct)
        last = l == depth - 1
        xx = _combine(starts_c, xm.reshape(bsz * ltot, D_MODEL), pos, mods, ys, nlt, ltiles, last)
        xx = xx.reshape(bsz, n_lat if last else ltot, D_MODEL)

    return xx
```

```python
import functools

import jax
import jax.numpy as jnp
import numpy as np
from jax import lax
from jax.experimental import pallas as pl
from jax.experimental.pallas import tpu as pltpu

F32 = jnp.float32
BF16 = jnp.bfloat16
HIGHEST = lax.Precision.HIGHEST

D_MODEL = 1024
GRID_W = 64
N_BRANCH = 3
BRANCH_W = 512
ML_HEADS = 8
ML_DH = 64
CONV_K = 5
MLA_HEADS = 8
Q_LORA = 384
KV_LORA = 256
NOPE_D = 64
ROPE_D = 32
QK_D = NOPE_D + ROPE_D
MLA_DV = 64
RET_HEADS = 8
RET_DK = 32
RET_DV = 64
N_EXPERTS = 16
EXPERT_FF = 1024
CAP_FACTOR = 2
CHUNK = 128
ROPE_BASE = 10000.0
EPS = 1e-6
LOG2E = 1.4426950408889634

LANES = 128
TM = 256
ROW_SUB = 128
VMEM_LIMIT = 56 * 1024 * 1024

OFF_BL, OFF_V, OFF_OG, OFF_RV, OFF_RG = 0, 3072, 3584, 4096, 4608
OFF_RQ, OFF_RK, OFF_KV, OFF_QL, OFF_MISC = 5120, 5376, 5632, 6144, 6528
SLAB_W = 6656
NCOL = SLAB_W + 2 * ML_HEADS * ML_DH
TN_IN = SLAB_W // 2


def _sigmoid(x):
    return 1.0 / (1.0 + jnp.exp(-x))


def _silu(x):
    return x * _sigmoid(x)


def _log_sigmoid(x):
    return jnp.minimum(x, 0.0) - jnp.log1p(jnp.exp(-jnp.abs(x)))


def _dot(a, b, **kw):
    return jnp.dot(a, b, preferred_element_type=F32, **kw)


def _dot_nt(a, b, **kw):
    return lax.dot_general(a, b, (((1,), (1,)), ((), ())), preferred_element_type=F32, **kw)


def _dot_tn(a, b, **kw):
    return lax.dot_general(a, b, (((0,), (0,)), ((), ())), preferred_element_type=F32, **kw)


def _params(sem):
    return pltpu.CompilerParams(dimension_semantics=sem, vmem_limit_bytes=VMEM_LIMIT)


def _ada_kernel(c_ref, w_ref, b_ref, o_ref):
    s = _silu(c_ref[...])
    o_ref[...] = _dot(s, w_ref[...], precision=HIGHEST) + b_ref[...]


def _ada(cond8, ada_w, ada_b):
    depth = ada_w.shape[0]
    tn = 1536
    return pl.pallas_call(
        _ada_kernel,
        grid=(depth, 6 * D_MODEL // tn),
        in_specs=[pl.BlockSpec((8, D_MODEL), lambda l, j: (0, 0)),
                  pl.BlockSpec((None, D_MODEL, tn), lambda l, j: (l, 0, j)),
                  pl.BlockSpec((None, 1, tn), lambda l, j: (l, 0, j))],
        out_specs=pl.BlockSpec((None, 8, tn), lambda l, j: (l, 0, j)),
        out_shape=jax.ShapeDtypeStruct((depth, 8, 6 * D_MODEL), F32),
        compiler_params=_params(("arbitrary", "arbitrary")),
        name="ada",
    )(cond8, ada_w, ada_b.reshape(depth, 1, 6 * D_MODEL))


def _inproj_kernel(x_ref, xp_ref, xn_ref, mod_ref, nw_ref, w_ref, b_ref, cw_ref, cb_ref, o_ref, q_ref, k_ref, xe_ref,
                   *, nlt, ltiles):
    i = pl.program_id(1)
    first = (i == 0) | (i == nlt)
    last = (i == nlt - 1) | (i == ltiles - 1)
    sh = mod_ref[:, 0:D_MODEL]
    sc = mod_ref[:, D_MODEL:2 * D_MODEL]
    hw = ML_HEADS * ML_DH

    def hidden(x):
        hn = x * lax.rsqrt(jnp.mean(x * x, axis=-1, keepdims=True) + EPS) * nw_ref[...]
        return (hn * (1.0 + sc) + sh).astype(BF16)

    def qk_cols(h):
        return _dot(h, w_ref[:, SLAB_W:NCOL]) + b_ref[:, SLAB_W:NCOL]

    subs = list(range(0, TM, ROW_SUB))
    halo = qk_cols(hidden(jnp.concatenate([xp_ref[...], xn_ref[...]], axis=0)))
    xe_ref[0:8, :] = jnp.where(first, 0.0, halo[0:8])
    xe_ref[8 + TM:16 + TM, :] = jnp.where(last, 0.0, halo[8:16])
    hs = [hidden(x_ref[r0:r0 + ROW_SUB, :]) for r0 in subs]
    for r0, h in zip(subs, hs):
        xe_ref[8 + r0:8 + r0 + ROW_SUB, :] = qk_cols(h)
    for r0, h in zip(subs, hs):
        for c0 in range(0, SLAB_W, TN_IN):
            o_ref[r0:r0 + ROW_SUB, c0:c0 + TN_IN] = _dot(h, w_ref[:, c0:c0 + TN_IN]) + b_ref[:, c0:c0 + TN_IN]
    acc = jnp.broadcast_to(cb_ref[...], (TM, 2 * hw))
    for j in range(CONV_K):
        acc = acc + xe_ref[8 - CONV_K // 2 + j:8 - CONV_K // 2 + j + TM, :] * cw_ref[j:j + 1, :]
    qk = _silu(acc)
    q_ref[...] = qk[:, :hw].astype(BF16)
    k_ref[...] = (qk[:, hw:] * (ML_DH ** -0.5)).astype(BF16)


def _inproj(x, mods, nw, w_all, b_all, conv_w, conv_b, layer, nlt):
    bsz, ltot, _ = x.shape
    hw = ML_HEADS * ML_DH
    ltiles = ltot // TM
    nrow8 = ltot // 8

    def mrow(bb, i):
        return (jnp.where(i < nlt, bb, bsz), 0, 0)

    qk_out = pl.BlockSpec((None, TM, hw), lambda bb, i: (bb, i, 0))
    return pl.pallas_call(
        functools.partial(_inproj_kernel, nlt=nlt, ltiles=ltiles),
        grid=(bsz, ltiles),
        in_specs=[pl.BlockSpec((None, TM, D_MODEL), lambda bb, i: (bb, i, 0)),
                  pl.BlockSpec((None, 8, D_MODEL), lambda bb, i: (bb, jnp.maximum(i * (TM // 8) - 1, 0), 0)),
                  pl.BlockSpec((None, 8, D_MODEL), lambda bb, i: (bb, jnp.minimum((i + 1) * (TM // 8), nrow8 - 1), 0)),
                  pl.BlockSpec((None, 1, 6 * D_MODEL), mrow),
                  pl.BlockSpec((1, D_MODEL), lambda bb, i: (0, 0)),
                  pl.BlockSpec((None, D_MODEL, NCOL), lambda bb, i: (layer, 0, 0)),
                  pl.BlockSpec((None, 1, NCOL), lambda bb, i: (layer, 0, 0)),
                  pl.BlockSpec((CONV_K, 2 * hw), lambda bb, i: (0, 0)),
                  pl.BlockSpec((1, 2 * hw), lambda bb, i: (0, 0))],
        out_specs=[pl.BlockSpec((None, TM, SLAB_W), lambda bb, i: (bb, i, 0)), qk_out, qk_out],
        out_shape=[jax.ShapeDtypeStruct((bsz, ltot, SLAB_W), F32),
                   jax.ShapeDtypeStruct((bsz, ltot, hw), BF16), jax.ShapeDtypeStruct((bsz, ltot, hw), BF16)],
        scratch_shapes=[pltpu.VMEM((TM + 16, 2 * hw), F32)],
        compiler_params=_params(("arbitrary", "arbitrary")),
        name="inproj",
    )(x, x, x, mods, nw, w_all, b_all, conv_w, conv_b)


def _chunk_of(d, s, nl, ncx):
    fwd = jnp.where(s < ncx, nl + s, s - ncx)
    bwd = jnp.where(s < ncx, nl + ncx - 1 - s, nl - 1 - (s - ncx))
    return jnp.where(d == 0, fwd, bwd)


def _lane_iota(shape):
    return lax.broadcasted_iota(jnp.int32, shape, len(shape) - 1)


def _split2(x):
    hi = x.astype(BF16)
    return hi, (x - hi.astype(F32)).astype(BF16)


def _split3(x):
    hi = x.astype(BF16)
    r1 = x - hi.astype(F32)
    mid = r1.astype(BF16)
    return hi, mid, (r1 - mid.astype(F32)).astype(BF16)


def _select_dot(x, sel_b):
    hi, lo = _split2(x)
    return _dot(hi, sel_b) + _dot(lo, sel_b)


GATE_LANE0 = ML_HEADS


def _mlstm2_kernel(qf_ref, kf_ref, vf_ref, gf_ref, qb_ref, kb_ref, vb_ref, gb_ref, of_ref, ob_ref, st_ref, m_ref):
    s = pl.program_id(1)

    @pl.when(s == 0)
    def _():
        st_ref[...] = jnp.zeros_like(st_ref)
        m_ref[...] = jnp.zeros_like(m_ref)

    dirs = (0, 1)
    pairs = range(ML_HEADS // 2)
    heads = range(ML_HEADS)
    q_refs, k_refs, v_refs, g_refs, o_refs = (qf_ref, qb_ref), (kf_ref, kb_ref), (vf_ref, vb_ref), (gf_ref, gb_ref), \
        (of_ref, ob_ref)
    lane = _lane_iota((1, LANES))
    head_lane = (lane >= GATE_LANE0) & (lane < GATE_LANE0 + ML_HEADS)
    lo_half = lane < ML_DH
    ti = lax.broadcasted_iota(jnp.int32, (CHUNK, CHUNK), 0)
    si = lax.broadcasted_iota(jnp.int32, (CHUNK, CHUNK), 1)
    causal = (ti >= si, ti <= si)
    tri = [jnp.where(causal[d], 1.0, 0.0).astype(BF16) for d in dirs]
    tri_t = [jnp.where(causal[1 - d], 1.0, 0.0).astype(BF16) for d in dirs]
    s8 = _lane_iota((ML_HEADS, CHUNK))
    gi = lax.broadcasted_iota(jnp.int32, (LANES, 2 * LANES), 0)
    li = lax.broadcasted_iota(jnp.int32, (LANES, 2 * LANES), 1)
    li_in = jnp.where(li >= LANES, li - LANES, li)
    same_head = (gi < ML_DH) == (li_in < ML_DH)
    lane2 = _lane_iota((1, 2 * LANES))
    lo_half2 = jnp.where(lane2 >= LANES, lane2 - LANES, lane2) < ML_DH
    hmask = [lo_half if h % 2 == 0 else jnp.logical_not(lo_half) for h in heads]
    ones_b = jnp.ones((CHUNK, LANES), BF16)
    sel_full = [jnp.where(gi == GATE_LANE0 + 2 * p + jnp.where(li >= LANES, 1, 0), 1.0, 0.0).astype(BF16)
                for p in pairs]
    quarter = jnp.where(li >= LANES, 2, 0) + jnp.where(li_in >= ML_DH, 1, 0)
    sel_half = [jnp.where(gi == GATE_LANE0 + 4 * q + quarter, 1.0, 0.0).astype(BF16)
                for q in range(ML_HEADS // 4)]
    t16 = lax.broadcasted_iota(jnp.int32, (16, LANES), 0)

    g_ig, ls = [], []
    for d in dirs:
        g_raw = g_refs[d][...]
        g_fg = g_raw if d == 0 else pltpu.roll(g_raw, LANES - 2 * ML_HEADS, axis=1)
        g_ig.append(jnp.where(head_lane, pltpu.roll(g_fg, ML_HEADS, axis=1), 0.0))
        ls.append(jnp.where(head_lane, _log_sigmoid(g_fg), 0.0))
    ls3 = [_split3(ls[d]) for d in dirs]
    lst3 = [_split3(ls[d].T) for d in dirs]
    b_cols = [sum(_dot(tri[d], t) for t in ls3[d]) for d in dirs]
    b_rows = [sum(_dot(t, tri_t[d]) for t in lst3[d]) for d in dirs]
    yield

    qp = [[q_refs[d][:, p * LANES:(p + 1) * LANES] for p in pairs] for d in dirs]
    kp = [[k_refs[d][:, p * LANES:(p + 1) * LANES] for p in pairs] for d in dirs]
    va = [[jnp.concatenate([v_refs[d][:, p * LANES:(p + 1) * LANES].astype(BF16), ones_b], axis=1) for p in pairs]
          for d in dirs]
    st = [[st_ref[d, p] for p in pairs] for d in dirs]
    qs = [[_dot(qp[d][p], st[d][p].astype(BF16)) for p in pairs] for d in dirs]
    qk = [[_dot_nt(jnp.where(hmask[h], qp[d][h // 2].astype(F32), 0.0).astype(BF16), kp[d][h // 2])
           for h in heads] for d in dirs]
    yield

    r_rows, cdiff, stacked = [], [], []
    for d in dirs:
        r_rows.append(g_ig[d].T - b_rows[d])
        bl_row = b_cols[d][CHUNK - 1:CHUNK, :] if d == 0 else b_cols[d][0:1, :]
        m_row = m_ref[d]
        cm8 = r_rows[d][GATE_LANE0:GATE_LANE0 + ML_HEADS, :]
        order = s8 if d == 0 else CHUNK - 1 - s8
        step = 1
        while step < CHUNK:
            shifted = pltpu.roll(cm8, step if d == 0 else CHUNK - step, axis=1)
            cm8 = jnp.where(order >= step, jnp.maximum(cm8, shifted), cm8)
            step *= 2
        cm = jnp.concatenate([jnp.zeros((GATE_LANE0, CHUNK), F32), cm8,
                              jnp.zeros((LANES - GATE_LANE0 - ML_HEADS, CHUNK), F32)], axis=0).T
        a_cols = b_cols[d] + m_row
        mt_cols = jnp.maximum(a_cols, b_cols[d] + cm)
        g_cols = bl_row - b_cols[d] + g_ig[d]
        m_new = jnp.maximum(bl_row + m_row, jnp.max(g_cols, axis=0, keepdims=True))
        dp_cols = jnp.where(t16 >= 0, jnp.exp(bl_row + m_row - m_new), 0.0)
        m_ref[d] = m_new
        cdiff.append(b_cols[d] - mt_cols)
        stacked.append(jnp.concatenate([jnp.exp(a_cols - mt_cols), jnp.exp(-mt_cols),
                                        jnp.exp(g_cols - m_new), dp_cols], axis=0))

    c_full = [[_select_dot(cdiff[d], sel_full[p]) for p in pairs] for d in dirs]
    halves2 = [[_select_dot(stacked[d], sel_half[q]) for q in range(ML_HEADS // 4)] for d in dirs]
    halves = [[halves2[d][p // 2][:, (p % 2) * LANES:(p % 2 + 1) * LANES] for p in pairs]
              for d in dirs]
    yield

    sc = [[None] * ML_HEADS for _ in dirs]
    for d in dirs:
        for h in heads:
            p, j = h // 2, h % 2
            dlog = c_full[d][p][:, j * LANES:(j + 1) * LANES] + r_rows[d][GATE_LANE0 + h:GATE_LANE0 + h + 1, :]
            sc[d][h] = (qk[d][h] * jnp.exp(jnp.where(causal[d], dlog, -jnp.inf))).astype(BF16)
    sv = [[_dot(sc[d][h], va[d][h // 2]) for h in heads] for d in dirs]
    upd = [[_dot_tn((kp[d][p].astype(F32) * halves[d][p][2 * CHUNK:3 * CHUNK]).astype(BF16), va[d][p])
            for p in pairs] for d in dirs]
    yield

    for d in dirs:
        outs = []
        for p in pairs:
            inter_pair = halves[d][p][0:CHUNK]
            emt_pair = halves[d][p][CHUNK:2 * CHUNK]
            dp_pair = halves[d][p][3 * CHUNK:3 * CHUNK + 1]
            sv_pair = jnp.where(lo_half2, sv[d][2 * p], sv[d][2 * p + 1])
            num = sv_pair[:, :LANES] + inter_pair * qs[d][p][:, :LANES]
            den = sv_pair[:, LANES:] + inter_pair * qs[d][p][:, LANES:]
            outs.append(num / jnp.maximum(jnp.abs(den), emt_pair))
            st_ref[d, p] = (jnp.concatenate([dp_pair, dp_pair], axis=1) * st[d][p]
                            + jnp.where(same_head, upd[d][p], 0.0))
        o_refs[d][...] = jnp.concatenate(outs, axis=1)


N_MLSTM_IN, N_RET_IN = 8, 12


def _scan_kernel(*refs):
    n_in = N_MLSTM_IN + N_RET_IN
    ml_in, rt_in = refs[:N_MLSTM_IN], refs[N_MLSTM_IN:n_in]
    mo_f, mo_b, ro_f, ro_b, ml_st, ml_m, rt_st = refs[n_in:]
    bodies = [_mlstm2_kernel(*ml_in, mo_f, mo_b, ml_st, ml_m), _ret_kernel(*rt_in, ro_f, ro_b, rt_st)]
    while bodies:
        for body in list(bodies):
            if next(body, "done") == "done":
                bodies.remove(body)


def _scans(q, k, slab, rope_r, dmat, xi, kd, cd, nl, ncx):
    bsz, ltot, _ = slab.shape
    hw = ML_HEADS * ML_DH
    qw = RET_HEADS * RET_DK
    vw = RET_HEADS * RET_DV

    def ml_specs(d):
        ch = lambda bb, s: _chunk_of(d, s, nl, ncx)
        return [pl.BlockSpec((None, CHUNK, hw), lambda bb, s: (bb, ch(bb, s), 0)),
                pl.BlockSpec((None, CHUNK, hw), lambda bb, s: (bb, ch(bb, s), 0)),
                pl.BlockSpec((None, CHUNK, hw), lambda bb, s: (bb, ch(bb, s), OFF_V // hw)),
                pl.BlockSpec((None, CHUNK, LANES), lambda bb, s: (bb, ch(bb, s), OFF_MISC // LANES))]

    def rt_specs(d):
        ch = lambda bb, s: _chunk_of(d, s, nl, ncx)
        return [pl.BlockSpec((None, CHUNK, qw), lambda bb, s: (bb, ch(bb, s), OFF_RQ // qw)),
                pl.BlockSpec((None, CHUNK, qw), lambda bb, s: (bb, ch(bb, s), OFF_RK // qw)),
                pl.BlockSpec((None, CHUNK, vw), lambda bb, s: (bb, ch(bb, s), OFF_RV // vw)),
                pl.BlockSpec((3, CHUNK, LANES), lambda bb, s: (0, ch(bb, s), 0))]

    full = lambda a: pl.BlockSpec(a.shape, lambda bb, s: (0,) * a.ndim)
    out = lambda d, w: pl.BlockSpec((None, CHUNK, w), lambda bb, s: (bb, _chunk_of(d, s, nl, ncx), 0))
    return pl.pallas_call(
        _scan_kernel,
        grid=(bsz, nl + ncx),
        in_specs=(ml_specs(0) + ml_specs(1) + rt_specs(0) + rt_specs(1)
                  + [full(dmat), full(xi), full(kd), full(cd)]),
        out_specs=[out(0, hw), out(1, hw), out(0, vw), out(1, vw)],
        out_shape=[jax.ShapeDtypeStruct((bsz, ltot, hw), F32)] * 2 + [jax.ShapeDtypeStruct((bsz, ltot, vw), F32)] * 2,
        scratch_shapes=[pltpu.VMEM((2, ML_HEADS // 2, LANES, 2 * LANES), F32),
                        pltpu.VMEM((2, 1, LANES), F32),
                        pltpu.VMEM((2, RET_HEADS // 2, LANES, LANES), F32)],
        compiler_params=_params(("arbitrary", "arbitrary")),
        name="scans",
    )(q, k, slab, slab, q, k, slab, slab, slab, slab, slab, rope_r, slab, slab, slab, rope_r, dmat, xi, kd, cd)


def _rope_lanes(x, tab_ref):
    return (x * tab_ref[0] + pltpu.roll(x, 8, axis=1) * tab_ref[1]
            + pltpu.roll(x, LANES - 8, axis=1) * tab_ref[2])


def _ret_kernel(qf_ref, kf_ref, vf_ref, rf_ref, qb_ref, kb_ref, vb_ref, rb_ref, dmat_ref, xi_ref, kd_ref, cd_ref,
                of_ref, ob_ref, st_ref):
    s = pl.program_id(1)

    @pl.when(s == 0)
    def _():
        st_ref[...] = jnp.zeros_like(st_ref)

    dirs = (0, 1)
    q_refs, k_refs, v_refs, r_refs, o_refs = (qf_ref, qb_ref), (kf_ref, kb_ref), (vf_ref, vb_ref), (rf_ref, rb_ref), \
        (of_ref, ob_ref)
    lane = _lane_iota((1, LANES))
    lo_half = lane < RET_DV
    ri = lax.broadcasted_iota(jnp.int32, (LANES, LANES), 0)
    ci = lax.broadcasted_iota(jnp.int32, (LANES, LANES), 1)
    heads = range(RET_HEADS)
    pairs = range(RET_HEADS // 2)
    qg = [[_rope_lanes(q_refs[d][:, g * LANES:(g + 1) * LANES], r_refs[d]) for g in range(2)] for d in dirs]
    kg = [[_rope_lanes(k_refs[d][:, g * LANES:(g + 1) * LANES], r_refs[d]) * (RET_DK ** -0.5) for g in range(2)]
          for d in dirs]
    qg_b = [[x.astype(BF16) for x in qg[d]] for d in dirs]
    kg_b = [[x.astype(BF16) for x in kg[d]] for d in dirs]
    kw_b = [[(kg[d][g] * kd_ref[d, :, g * LANES:(g + 1) * LANES]).astype(BF16) for g in range(2)] for d in dirs]
    vp_b = [[v_refs[d][:, p * LANES:(p + 1) * LANES].astype(BF16) for p in pairs] for d in dirs]
    st = [[st_ref[d, p] for p in pairs] for d in dirs]

    cross = [[_dot(qg_b[d][p // 2], st[d][p].astype(BF16)) for p in pairs] for d in dirs]
    qk = [[None] * RET_HEADS for _ in dirs]
    for d in dirs:
        for h in heads:
            q_lo = (h % 4) * RET_DK
            hm = (lane >= q_lo) & (lane < q_lo + RET_DK)
            qk[d][h] = _dot_nt(jnp.where(hm, qg[d][h // 4], 0.0).astype(BF16), kg_b[d][h // 4])
    yield
    sv = [[_dot((qk[d][h] * dmat_ref[d, h]).astype(BF16), vp_b[d][h // 2]) for h in heads] for d in dirs]
    upd = [[_dot_tn(kw_b[d][p // 2], vp_b[d][p]) for p in pairs] for d in dirs]
    yield

    for d in dirs:
        outs = []
        for p in pairs:
            a = p % 2
            outs.append(jnp.where(lo_half, sv[d][2 * p], sv[d][2 * p + 1])
                        + xi_ref[d, :, p * LANES:(p + 1) * LANES] * cross[d][p])
            r_lo = (ri >= 2 * a * RET_DK) & (ri < (2 * a + 1) * RET_DK)
            r_hi = (ri >= (2 * a + 1) * RET_DK) & (ri < (2 * a + 2) * RET_DK)
            valid = (r_lo & (ci < RET_DV)) | (r_hi & (ci >= RET_DV))
            st_ref[d, p] = cd_ref[d, :, p * LANES:(p + 1) * LANES] * st[d][p] + jnp.where(valid, upd[d][p], 0.0)
        o_refs[d][...] = jnp.concatenate(outs, axis=1)


def _ret_tables():
    lg = jnp.log1p(-jnp.exp2(-5.0 - jnp.arange(RET_HEADS, dtype=F32)))
    idx = jnp.arange(CHUNK, dtype=F32)
    diff = idx[:, None] - idx[None, :]
    dm, xis, kds, cds = [], [], [], []
    for lgd, sign in ((lg, 1.0), (lg[::-1], -1.0)):
        dd = diff[None] * sign
        dm.append(jnp.exp(jnp.where(dd >= 0, dd * lgd[:, None, None], -jnp.inf)))
        order = idx if sign > 0 else (CHUNK - 1.0 - idx)
        xi = jnp.exp((order + 1.0)[None] * lgd[:, None])
        kdv = jnp.exp((CHUNK - 1.0 - order)[None] * lgd[:, None])
        cdv = jnp.exp(CHUNK * lgd)
        xis.append(jnp.repeat(xi.T, RET_DV, axis=1))
        kds.append(jnp.repeat(kdv.T, RET_DK, axis=1))
        cds.append(jnp.repeat(cdv, RET_DV)[None, :])
    return jnp.stack(dm), jnp.stack(xis), jnp.stack(kds), jnp.stack(cds)


def _rope_tables(n_lat, n_ctx):
    t = jnp.arange(n_lat)
    row = (t // GRID_W).astype(F32)
    col = (t % GRID_W).astype(F32)
    half = ROPE_D // 4
    inv = ROPE_BASE ** (-jnp.arange(half, dtype=F32) / half)
    ar, ac = row[:, None] * inv, col[:, None] * inv
    z = jnp.zeros_like(ar)
    c32 = jnp.concatenate([jnp.cos(ar), jnp.cos(ar), jnp.cos(ac), jnp.cos(ac)], axis=1)
    s1 = jnp.concatenate([z, jnp.sin(ar), z, jnp.sin(ac)], axis=1)
    s2 = jnp.concatenate([-jnp.sin(ar), z, -jnp.sin(ac), z], axis=1)
    tab = jnp.stack([c32, s1, s2])
    ident = jnp.stack([jnp.ones((n_ctx, ROPE_D), F32), jnp.zeros((n_ctx, ROPE_D), F32),
                       jnp.zeros((n_ctx, ROPE_D), F32)])
    return jnp.concatenate([tab, ident], axis=1)


def _mla_kernel(ql_ref, kv_ref, misc_ref, qan_ref, wqt_ref, kvn_ref, wkt_ref, wvt_ref, qnw_ref, knw_ref,
                rope_ref, qt_ref, k_ref, vt_ref):
    def norm_rope(xh, gain_ref):
        ss = jnp.sum(xh * xh, axis=0, keepdims=True)
        xh = xh * lax.rsqrt(ss * (1.0 / QK_D) + EPS) * gain_ref[...]
        return (xh * rope_ref[0] + pltpu.roll(xh, 8, axis=0) * rope_ref[1]
                + pltpu.roll(xh, LANES - 8, axis=0) * rope_ref[2])

    ql = ql_ref[...]
    qn = ql * lax.rsqrt(jnp.mean(ql * ql, axis=-1, keepdims=True) + EPS) * qan_ref[...]
    qt = _dot_nt(wqt_ref[...].astype(BF16), qn.astype(BF16))
    for h in range(MLA_HEADS):
        xh = norm_rope(qt[h * LANES:(h + 1) * LANES, :], qnw_ref)
        qt_ref[h * LANES:(h + 1) * LANES, :] = (xh * (QK_D ** -0.5 * LOG2E)).astype(BF16)

    kv = kv_ref[...]
    kvn = (kv * lax.rsqrt(jnp.mean(kv * kv, axis=-1, keepdims=True) + EPS) * kvn_ref[...]).astype(BF16)
    kt = _dot_nt(wkt_ref[...].astype(BF16), kvn)
    tm = kt.shape[1]
    shared = misc_ref[...].T[4 * ML_HEADS:4 * ML_HEADS + ROPE_D, :]
    kr = jnp.concatenate([jnp.zeros((NOPE_D, tm), F32), shared, jnp.zeros((LANES - QK_D, tm), F32)], axis=0)
    for h in range(MLA_HEADS):
        xh = norm_rope(kt[h * LANES:(h + 1) * LANES, :] + kr, knw_ref)
        k_ref[:, h * LANES:(h + 1) * LANES] = xh.T.astype(BF16)
    vt_ref[...] = _dot_nt(wvt_ref[...].astype(BF16), kvn).astype(BF16)


def _mla(slab, qan, wqt, kvn, wkt, wvt, qnw, knw, rope_q):
    bsz, ltot, _ = slab.shape
    hq = MLA_HEADS * LANES
    hv = MLA_HEADS * MLA_DV
    full = lambda shape: pl.BlockSpec(shape, lambda bb, i: (0,) * len(shape))
    return pl.pallas_call(
        _mla_kernel,
        grid=(bsz, ltot // TM),
        in_specs=[pl.BlockSpec((None, TM, Q_LORA), lambda bb, i: (bb, i, OFF_QL // Q_LORA)),
                  pl.BlockSpec((None, TM, KV_LORA), lambda bb, i: (bb, i, OFF_KV // KV_LORA)),
                  pl.BlockSpec((None, TM, LANES), lambda bb, i: (bb, i, OFF_MISC // LANES)),
                  full((1, Q_LORA)), full((hq, Q_LORA)), full((1, KV_LORA)), full((hq, KV_LORA)),
                  full((hv, KV_LORA)), full((LANES, 1)), full((LANES, 1)),
                  pl.BlockSpec((3, LANES, TM), lambda bb, i: (0, 0, i))],
        out_specs=[pl.BlockSpec((None, hq, TM), lambda bb, i: (bb, 0, i)),
                   pl.BlockSpec((None, TM, hq), lambda bb, i: (bb, i, 0)),
                   pl.BlockSpec((None, None, hv, TM), lambda bb, i: (bb, i, 0, 0))],
        out_shape=[jax.ShapeDtypeStruct((bsz, hq, ltot), BF16),
                   jax.ShapeDtypeStruct((bsz, ltot, hq), BF16),
                   jax.ShapeDtypeStruct((bsz, ltot // TM, hv, TM), BF16)],
        compiler_params=_params(("arbitrary", "arbitrary")),
        name="mla",
    )(slab, slab, slab, qan, wqt, kvn, wkt, wvt, qnw, knw, rope_q)


ATT_QC = 512
ATT_ONES = 16
ATT_AHEAD = 3


ATT_BOUND_MAX = 60.0


def _attn_kernel(flag_ref, qt_ref, k_ref, vt_ref, ub_ref, o_ref, acc_ref, m_ref, l_ref, *, r, nkb):
    tq = qt_ref.shape[1]
    tkb = r * TM
    ones = jnp.ones((ATT_ONES, TM), BF16)
    qc = min(ATT_QC, tq)
    units = [(j, c) for j in range(2) for c in range(tq // qc)]
    ahead = min(ATT_AHEAD, len(units))
    bounded = flag_ref[0] == 1
    ub = ub_ref[0:1, 0:1]

    def scores(r0, u):
        j, c = units[u]
        return _dot(k_ref[pl.ds(r0, tkb), j * LANES:(j + 1) * LANES],
                    qt_ref[j * LANES:(j + 1) * LANES, c * qc:(c + 1) * qc])

    def weighted_values(kb, j, p):
        pb = p.astype(BF16)
        pv = None
        for t in range(r):
            vt = jnp.concatenate([vt_ref[kb * r + t, j * MLA_DV:(j + 1) * MLA_DV, :], ones], axis=0)
            part = _dot(vt, pb[t * TM:(t + 1) * TM])
            pv = part if pv is None else pv + part
        return pv[0:MLA_DV], pv[MLA_DV:MLA_DV + 1]

    def bounded_block(kb, carry):
        r0 = pl.multiple_of(kb * tkb, tkb)
        pending = [scores(r0, u) for u in range(ahead)]
        for u, (j, c) in enumerate(units):
            sc = pending.pop(0)
            if u + ahead < len(units):
                pending.append(scores(r0, u + ahead))
            cs = slice(c * qc, (c + 1) * qc)
            pv, ps = weighted_values(kb, j, jnp.exp2(sc - ub))
            acc_ref[j * MLA_DV:(j + 1) * MLA_DV, cs] += pv
            l_ref[j:j + 1, cs] += ps
        return carry

    def online_block(kb, carry):
        r0 = pl.multiple_of(kb * tkb, tkb)
        pending = [scores(r0, u) for u in range(ahead)]
        for u, (j, c) in enumerate(units):
            sc = pending.pop(0)
            if u + ahead < len(units):
                pending.append(scores(r0, u + ahead))
            cs = slice(c * qc, (c + 1) * qc)
            m_old = m_ref[j:j + 1, cs]
            m_new = jnp.maximum(m_old, jnp.max(sc, axis=0, keepdims=True))
            alpha = jnp.exp2(m_old - m_new)
            pv, ps = weighted_values(kb, j, jnp.exp2(sc - m_new))
            acc_ref[j * MLA_DV:(j + 1) * MLA_DV, cs] = alpha * acc_ref[j * MLA_DV:(j + 1) * MLA_DV, cs] + pv
            l_ref[j:j + 1, cs] = alpha * l_ref[j:j + 1, cs] + ps
            m_ref[j:j + 1, cs] = m_new
        return carry

    acc_ref[...] = jnp.zeros_like(acc_ref)
    l_ref[...] = jnp.zeros_like(l_ref)

    @pl.when(bounded)
    def _():
        lax.fori_loop(0, nkb, bounded_block, 0)

    @pl.when(jnp.logical_not(bounded))
    def _():
        m_ref[...] = jnp.full_like(m_ref, -jnp.inf)
        lax.fori_loop(0, nkb, online_block, 0)

    head_rows = lax.broadcasted_iota(jnp.int32, acc_ref.shape, 0)
    inv = jnp.where(head_rows < MLA_DV, 1.0 / l_ref[0:1, :], 1.0 / l_ref[1:2, :])
    o_ref[...] = (acc_ref[...] * inv).T


def _attn(flags, qt, k, vt, ub, tq, q_off, nq, key_tile0, r, nkb):
    bsz = qt.shape[0]
    hv = vt.shape[2]
    nkt = r * nkb
    assert key_tile0 % nkt == 0
    kern = functools.partial(_attn_kernel, r=r, nkb=nkb)
    return pl.pallas_call(
        kern,
        grid_spec=pltpu.PrefetchScalarGridSpec(
            num_scalar_prefetch=1,
            grid=(bsz, MLA_HEADS // 2, nq),
            in_specs=[pl.BlockSpec((None, 2 * LANES, tq), lambda bb, hp, qi, fl: (bb, hp, qi + q_off)),
                      pl.BlockSpec((None, nkt * TM, 2 * LANES), lambda bb, hp, qi, fl: (bb, key_tile0 // nkt, hp)),
                      pl.BlockSpec((None, nkt, 2 * MLA_DV, TM),
                                   lambda bb, hp, qi, fl: (bb, key_tile0 // nkt, hp, 0)),
                      pl.BlockSpec((8, LANES), lambda bb, hp, qi, fl: (0, 0))],
            out_specs=pl.BlockSpec((None, tq, 2 * MLA_DV), lambda bb, hp, qi, fl: (bb, qi, hp)),
            scratch_shapes=[pltpu.VMEM((2 * MLA_DV, tq), F32), pltpu.VMEM((8, tq), F32),
                            pltpu.VMEM((8, tq), F32)]),
        out_shape=jax.ShapeDtypeStruct((bsz, nq * tq, hv), F32),
        compiler_params=_params(("arbitrary", "arbitrary", "arbitrary")),
        name="attn",
    )(flags, qt, k, vt, ub)


def _split_dot(x, w_b):
    hi = x.astype(BF16)
    lo = (x - hi.astype(F32)).astype(BF16)
    return _dot(hi, w_b) + _dot(lo, w_b)


def _head_ln(x, g_ref, gt_ref, width):
    mu = _split_dot(_split_dot(x, g_ref[...]) * (1.0 / width), gt_ref[...])
    xc = x - mu
    var = _split_dot(_split_dot(xc * xc, g_ref[...]) * (1.0 / width), gt_ref[...])
    return xc * lax.rsqrt(var + EPS)


def _merge_kernel(hmf_ref, hmb_ref, og_ref, mlw_ref, ybl_ref, ybc_ref, hrf_ref, hrb_ref, rg_ref, rnw_ref, bl_ref,
                  wbr_ref, wout_ref, x_ref, mod_ref, n2w_ref, rw_ref, g_ref, gt_ref,
                  xo_ref, h2_ref, aff_ref, wbr_b, wout_b, *, nlt):
    @pl.when((pl.program_id(0) == 0) & (pl.program_id(1) == 0))
    def _():
        wbr_b[...] = wbr_ref[...].astype(BF16)
        wout_b[...] = wout_ref[...].astype(BF16)

    is_ctx = pl.program_id(1) >= nlt

    subs = [slice(r0, r0 + ROW_SUB) for r0 in range(0, TM, ROW_SUB)]
    ys = []
    for rs in subs:
        ya = _sigmoid(og_ref[rs, :]) * (_head_ln(hmf_ref[rs, :] + hmb_ref[rs, :], g_ref, gt_ref, ML_DH)
                                        * mlw_ref[...])
        yc = _silu(rg_ref[rs, :]) * (_head_ln(hrf_ref[rs, :] + hrb_ref[rs, :], g_ref, gt_ref, RET_DV)
                                     * rnw_ref[...])
        yb = jnp.where(is_ctx, ybc_ref[rs, :], ybl_ref[rs, :])
        ys.append((ya.astype(BF16), yb.astype(BF16), yc.astype(BF16)))
    br = [[_dot(ys[i][n], wbr_b[n]) for n in range(N_BRANCH)] for i in range(len(subs))]
    merged = []
    for i, rs in enumerate(subs):
        m = None
        for n in range(N_BRANCH):
            term = _sigmoid(bl_ref[rs, n * D_MODEL:(n + 1) * D_MODEL]) * br[i][n]
            m = term if m is None else m + term
        merged.append(m.astype(BF16))
    y = [_dot(m, wout_b[...]) for m in merged]
    g1 = mod_ref[:, 2 * D_MODEL:3 * D_MODEL]
    h2s = []
    for i, rs in enumerate(subs):
        x = x_ref[rs, :] + g1 * y[i]
        xo_ref[rs, :] = x
        hn = x * lax.rsqrt(jnp.mean(x * x, axis=-1, keepdims=True) + EPS) * n2w_ref[...]
        h2 = hn * (1.0 + mod_ref[:, 4 * D_MODEL:5 * D_MODEL]) + mod_ref[:, 3 * D_MODEL:4 * D_MODEL]
        h2_ref[rs, :] = h2.astype(BF16)
        h2s.append(h2)
    logits = []
    for h2 in h2s:
        hi, lo = _split2(h2)
        logits.append(_dot(hi, rw_ref[0]) + _dot(hi, rw_ref[1]) + _dot(lo, rw_ref[0]))
    valid = _lane_iota((1, LANES)) < N_EXPERTS
    for i, rs in enumerate(subs):
        lg = jnp.where(valid, logits[i], -jnp.inf)
        e = jnp.exp(lg - jnp.max(lg, axis=-1, keepdims=True))
        aff_ref[:, rs] = (e / jnp.sum(e, axis=-1, keepdims=True)).T[0:N_EXPERTS, :]


def _merge(hmf, hmb, slab, mlw, yb_l, yb_c, hrf, hrb, rnw, wbr, wout, x, mods, n2w, rw, g8, g8t, nlt):
    bsz, ltot, _ = x.shape
    hw = ML_HEADS * ML_DH

    def mrow(bb, i):
        return (jnp.where(i < nlt, bb, bsz), 0, 0)

    full = lambda shape: pl.BlockSpec(shape, lambda bb, i: (0,) * len(shape))
    row = lambda width, off: pl.BlockSpec((None, TM, width), lambda bb, i: (bb, i, off // width))
    return pl.pallas_call(
        functools.partial(_merge_kernel, nlt=nlt),
        grid=(bsz, ltot // TM),
        in_specs=[row(hw, 0), row(hw, 0),
                  row(hw, OFF_OG), full((1, hw)),
                  pl.BlockSpec((None, TM, hw), lambda bb, i: (bb, jnp.minimum(i, nlt - 1), 0)),
                  pl.BlockSpec((None, TM, hw), lambda bb, i: (bb, 0, 0)),
                  row(hw, 0), row(hw, 0),
                  row(hw, OFF_RG), full((1, hw)),
                  row(N_BRANCH * D_MODEL, OFF_BL),
                  full((N_BRANCH, BRANCH_W, D_MODEL)), full((D_MODEL, D_MODEL)),
                  row(D_MODEL, 0),
                  pl.BlockSpec((None, 1, 6 * D_MODEL), mrow),
                  full((1, D_MODEL)), full((2, D_MODEL, LANES)), full((hw, LANES)), full((LANES, hw))],
        out_specs=[row(D_MODEL, 0), row(D_MODEL, 0), pl.BlockSpec((None, N_EXPERTS, TM), lambda bb, i: (bb, 0, i))],
        out_shape=[jax.ShapeDtypeStruct((bsz, ltot, D_MODEL), F32),
                   jax.ShapeDtypeStruct((bsz, ltot, D_MODEL), BF16),
                   jax.ShapeDtypeStruct((bsz, N_EXPERTS, ltot), F32)],
        scratch_shapes=[pltpu.VMEM((N_BRANCH, BRANCH_W, D_MODEL), BF16), pltpu.VMEM((D_MODEL, D_MODEL), BF16)],
        compiler_params=_params(("arbitrary", "arbitrary")),
        name="merge",
    )(hmf, hmb, slab, mlw, yb_l, yb_c, hrf, hrb, slab, rnw, slab, wbr, wout, x, mods, n2w, rw, g8, g8t)


def _select_kernel(aff_ref, pos_ref, post_ref, gwt_ref, off_ref, *, n, cap, base0, base_step):
    b = pl.program_id(0)
    base = (base0 + b * base_step).astype(F32)
    bits = lax.bitcast_convert_type(aff_ref[...], jnp.int32)
    capf = jnp.float32(cap)

    def search(it, cur):
        cand = cur | jnp.left_shift(jnp.int32(1), 30 - it)
        cnt = jnp.sum(jnp.where(bits >= cand, 1.0, 0.0), axis=1, keepdims=True)
        return jnp.where(cnt >= capf, cand, cur)

    thr = lax.fori_loop(0, 31, search, jnp.zeros((N_EXPERTS, 1), jnp.int32))
    n_gt = jnp.sum(jnp.where(bits > thr, 1.0, 0.0), axis=1, keepdims=True)
    need = capf - n_gt

    ri = lax.broadcasted_iota(jnp.int32, (TM, TM), 0)
    ci = lax.broadcasted_iota(jnp.int32, (TM, TM), 1)
    before = (ri < ci).astype(BF16)
    half_lane = _lane_iota((N_EXPERTS, LANES))
    unused = jnp.full((LANES - N_EXPERTS, TM), -1.0, F32)

    def tile(i, carry):
        c_eq, c_sel, offs = carry
        r0 = pl.multiple_of(i * TM, TM)
        a = aff_ref[:, pl.ds(r0, TM)]
        bt = lax.bitcast_convert_type(a, jnp.int32)
        eq = bt == thr
        rank = _dot(jnp.where(eq, 1.0, 0.0).astype(BF16), before) + c_eq
        sel = (bt > thr) | (eq & (rank < need))
        self_ = jnp.where(sel, 1.0, 0.0)
        pos = _dot(self_.astype(BF16), before) + c_sel
        posv = jnp.where(sel, pos + base, -1.0)
        post_ref[:, pl.ds(r0, TM)] = posv
        gwt_ref[:, pl.ds(r0, TM)] = jnp.where(sel, a, 0.0)
        pos_ref[pl.ds(r0, TM), :] = jnp.concatenate([posv, unused], axis=0).T
        c_half = c_sel + jnp.sum(self_[:, 0:TM // 2], axis=1, keepdims=True)
        offs = jnp.where(half_lane == 2 * i, c_sel, offs)
        offs = jnp.where(half_lane == 2 * i + 1, c_half, offs)
        return (c_eq + jnp.sum(jnp.where(eq, 1.0, 0.0), axis=1, keepdims=True),
                c_sel + jnp.sum(self_, axis=1, keepdims=True), offs)

    zero = jnp.zeros((N_EXPERTS, 1), F32)
    _, _, offs = lax.fori_loop(0, n // TM, tile, (zero, zero, jnp.zeros((N_EXPERTS, LANES), F32)))
    off_ref[...] = offs


def _select(afft, n, cap, tile_off, base0, base_step):
    bsz = afft.shape[0]
    assert 2 * (n // TM) <= LANES
    kern = functools.partial(_select_kernel, n=n, cap=cap, base0=base0, base_step=base_step)
    return pl.pallas_call(
        kern,
        grid=(bsz,),
        in_specs=[pl.BlockSpec((None, N_EXPERTS, n), lambda bb: (bb, 0, tile_off))],
        out_specs=[pl.BlockSpec((None, n, LANES), lambda bb: (bb, 0, 0)),
                   pl.BlockSpec((None, N_EXPERTS, n), lambda bb: (bb, 0, 0)),
                   pl.BlockSpec((None, N_EXPERTS, n), lambda bb: (bb, 0, 0)),
                   pl.BlockSpec((None, N_EXPERTS, LANES), lambda bb: (bb, 0, 0))],
        out_shape=[jax.ShapeDtypeStruct((bsz, n, LANES), F32),
                   jax.ShapeDtypeStruct((bsz, N_EXPERTS, n), F32),
                   jax.ShapeDtypeStruct((bsz, N_EXPERTS, n), F32),
                   jax.ShapeDtypeStruct((bsz, N_EXPERTS, LANES), F32)],
        compiler_params=_params(("arbitrary",)),
        name="select",
    )(afft)


SLOT_ALIGN = 16
WIN = TM + SLOT_ALIGN
TC = TM // 2
WINC = TC + SLOT_ALIGN
GATHER_MAX_TILES = 11
FFN_ROW_CHUNKS = 4
FFN_TF = 512


GATHER_EXPERTS = 2


def _moe_gather_kernel(st_ref, h_ref, pt_ref, gt_ref, xs_ref, gs_ref, *, nt256, tiles):
    e0 = pl.program_id(0) * GATHER_EXPERTS
    i = pl.program_id(1)

    @pl.when(i == 0)
    def _():
        xs_ref[...] = jnp.zeros_like(xs_ref)
        gs_ref[...] = jnp.zeros_like(gs_ref)

    slot = lax.broadcasted_iota(jnp.int32, (WIN, TM), 0)
    for u in range(tiles):
        starts = [pl.multiple_of(st_ref[(e0 + x) * nt256 + i * tiles + u], SLOT_ALIGN) for x in range(GATHER_EXPERTS)]
        match = [(slot + starts[x]).astype(F32) == pt_ref[x, :, u * TM:(u + 1) * TM] for x in range(GATHER_EXPERTS)]
        onehot = jnp.concatenate([jnp.where(m, 1.0, 0.0).astype(BF16) for m in match], axis=0)
        got_all = _dot(onehot, h_ref[u * TM:(u + 1) * TM, :])
        for x in range(GATHER_EXPERTS):
            st = starts[x]
            st2 = pl.multiple_of(st + SLOT_ALIGN, SLOT_ALIGN)
            got = got_all[x * WIN:(x + 1) * WIN]
            head = xs_ref[x, pl.ds(st, SLOT_ALIGN), :].astype(F32)
            xs_ref[x, pl.ds(st, SLOT_ALIGN), :] = (head + got[0:SLOT_ALIGN]).astype(BF16)
            xs_ref[x, pl.ds(st2, TM), :] = got[SLOT_ALIGN:WIN].astype(BF16)
            gate = jnp.sum(jnp.where(match[x], gt_ref[x, :, u * TM:(u + 1) * TM], 0.0), axis=1, keepdims=True)
            gate = jnp.broadcast_to(gate, (WIN, LANES))
            gs_ref[x, pl.ds(st, SLOT_ALIGN), :] = gs_ref[x, pl.ds(st, SLOT_ALIGN), :] + gate[0:SLOT_ALIGN]
            gs_ref[x, pl.ds(st2, TM), :] = gate[SLOT_ALIGN:WIN]


def _moe_gather(starts, h2, post, gwt, rows, tiles):
    ttot = h2.shape[0]
    tt = tiles * TM
    kern = functools.partial(_moe_gather_kernel, nt256=ttot // TM, tiles=tiles)
    return pl.pallas_call(
        kern,
        grid_spec=pltpu.PrefetchScalarGridSpec(
            num_scalar_prefetch=1,
            grid=(N_EXPERTS // GATHER_EXPERTS, ttot // tt),
            in_specs=[pl.BlockSpec((tt, D_MODEL), lambda e, i, st: (i, 0)),
                      pl.BlockSpec((GATHER_EXPERTS, 1, tt), lambda e, i, st: (e, 0, i)),
                      pl.BlockSpec((GATHER_EXPERTS, 1, tt), lambda e, i, st: (e, 0, i))],
            out_specs=[pl.BlockSpec((GATHER_EXPERTS, rows, D_MODEL), lambda e, i, st: (e, 0, 0)),
                       pl.BlockSpec((GATHER_EXPERTS, rows, LANES), lambda e, i, st: (e, 0, 0))]),
        out_shape=[jax.ShapeDtypeStruct((N_EXPERTS, rows, D_MODEL), BF16),
                   jax.ShapeDtypeStruct((N_EXPERTS, rows, LANES), F32)],
        compiler_params=_params(("arbitrary", "arbitrary")),
        name="moe_gather",
    )(starts, h2, post, gwt)


def _moe_ffn_kernel(xs_ref, gs_ref, w1_ref, w3_ref, w2_ref, ys_ref, acc_ref, *, ct, rows):
    f = pl.program_id(1)

    @pl.when(f == 0)
    def _():
        acc_ref[...] = jnp.zeros_like(acc_ref)

    w1b = w1_ref[...].astype(BF16)
    w3b = w3_ref[...].astype(BF16)
    w2b = w2_ref[...].astype(BF16)
    rc = ct // FFN_ROW_CHUNKS
    chunks = [slice(r * rc, (r + 1) * rc) for r in range(FFN_ROW_CHUNKS)]
    up = [(_dot(xs_ref[chunks[0], :], w1b), _dot(xs_ref[chunks[0], :], w3b))]
    for r, rs in enumerate(chunks):
        a, b = up[r]
        if r + 1 < len(chunks):
            up.append((_dot(xs_ref[chunks[r + 1], :], w1b), _dot(xs_ref[chunks[r + 1], :], w3b)))
        acc_ref[rs, :] += _dot((_silu(a) * b).astype(BF16), w2b)

    @pl.when(f == pl.num_programs(1) - 1)
    def _():
        gate = gs_ref[...]
        for c in range(D_MODEL // LANES):
            cs = slice(c * LANES, (c + 1) * LANES)
            ys_ref[0:ct, cs] = (acc_ref[:, cs] * gate).astype(BF16)
        ys_ref[ct:rows, :] = jnp.zeros((rows - ct, D_MODEL), BF16)


def _moe_ffn(xs, gs, w1, w3, w2, layer, ct):
    rows = xs.shape[1]
    kern = functools.partial(_moe_ffn_kernel, ct=ct, rows=rows)
    return pl.pallas_call(
        kern,
        grid=(N_EXPERTS, EXPERT_FF // FFN_TF),
        in_specs=[pl.BlockSpec((None, ct, D_MODEL), lambda e, f: (e, 0, 0)),
                  pl.BlockSpec((None, ct, LANES), lambda e, f: (e, 0, 0)),
                  pl.BlockSpec((None, None, D_MODEL, FFN_TF), lambda e, f: (layer, e, 0, f)),
                  pl.BlockSpec((None, None, D_MODEL, FFN_TF), lambda e, f: (layer, e, 0, f)),
                  pl.BlockSpec((None, None, FFN_TF, D_MODEL), lambda e, f: (layer, e, f, 0))],
        out_specs=pl.BlockSpec((None, rows, D_MODEL), lambda e, f: (e, 0, 0)),
        out_shape=jax.ShapeDtypeStruct((N_EXPERTS, rows, D_MODEL), BF16),
        scratch_shapes=[pltpu.VMEM((ct, D_MODEL), F32)],
        compiler_params=_params(("arbitrary", "arbitrary")),
        name="moe_ffn",
    )(xs, gs, w1, w3, w2)


def _combine_kernel(st_ref, x_ref, pos_ref, mod_ref, *rest, tile_of, ntc):
    nsub = TM // TC
    ys_refs = rest[:N_EXPERTS]
    o_ref = rest[N_EXPERTS]
    i = tile_of(pl.program_id(0))
    slot = lax.broadcasted_iota(jnp.int32, (TC, WINC), 1)
    g2 = mod_ref[:, 5 * D_MODEL:6 * D_MODEL]
    for u in range(nsub):
        rs = slice(u * TC, (u + 1) * TC)
        pos = pos_ref[rs, :]
        acc = None
        for e in range(N_EXPERTS):
            st = st_ref[e * ntc + i * nsub + u]
            delta = pl.multiple_of(st - st_ref[e * ntc + i * nsub], SLOT_ALIGN)
            onehot = jnp.where((slot + st).astype(F32) == pos[:, e:e + 1], 1.0, 0.0).astype(BF16)
            part = _dot(onehot, ys_refs[e][0, pl.ds(delta, WINC), :])
            acc = part if acc is None else acc + part
        o_ref[rs, :] = x_ref[rs, :] + g2 * acc


def _combine(starts, x, pos, mods_rows, ys, nlt, ltiles, latent_only):
    ttot = x.shape[0]
    nt = ttot // TM
    nsub = TM // TC
    bsz = ttot // (ltiles * TM)
    steps = bsz * nlt if latent_only else nt
    tile_of = (lambda j: (j // nlt) * ltiles + j % nlt) if latent_only else (lambda j: j)

    def mrow(j, st):
        i = tile_of(j)
        return (jnp.where(i % ltiles < nlt, i // ltiles, bsz), 0, 0)

    def ys_spec(e):
        return pl.BlockSpec(
            (pl.Element(1), pl.Element(WIN), pl.Element(D_MODEL)),
            lambda j, st: (e, pl.multiple_of(st[e * nt * nsub + tile_of(j) * nsub], SLOT_ALIGN), 0))

    return pl.pallas_call(
        functools.partial(_combine_kernel, tile_of=tile_of, ntc=nt * nsub),
        grid_spec=pltpu.PrefetchScalarGridSpec(
            num_scalar_prefetch=1,
            grid=(steps,),
            in_specs=[pl.BlockSpec((TM, D_MODEL), lambda j, st: (tile_of(j), 0)),
                      pl.BlockSpec((TM, LANES), lambda j, st: (tile_of(j), 0)),
                      pl.BlockSpec((None, 1, 6 * D_MODEL), mrow)]
                     + [ys_spec(e) for e in range(N_EXPERTS)],
            out_specs=pl.BlockSpec((TM, D_MODEL), lambda j, st: (j, 0))),
        out_shape=jax.ShapeDtypeStruct((steps * TM, D_MODEL), F32),
        compiler_params=_params(("arbitrary",)),
        name="combine",
    )(starts, x, pos, mods_rows, *([ys] * N_EXPERTS))


def _permute_w_in(w, b):
    widths = (2 * ML_HEADS * ML_DH, ML_HEADS * ML_DH, ML_HEADS * ML_DH, 4 * ML_HEADS, Q_LORA, KV_LORA, ROPE_D,
              RET_HEADS * RET_DK, RET_HEADS * RET_DK, RET_HEADS * RET_DV, RET_HEADS * RET_DV, N_BRANCH * D_MODEL)
    offs = np.concatenate([[0], np.cumsum(widths)])
    seg = lambda a, k: a[..., offs[k]:offs[k + 1]]
    zeros = lambda a, n: jnp.zeros(a.shape[:-1] + (n,), a.dtype)

    def build(a):
        parts = [seg(a, k) for k in (11, 1, 2, 9, 10, 7, 8, 5)]
        parts.append(zeros(a, OFF_QL - (OFF_KV + KV_LORA)))
        parts += [seg(a, 4), seg(a, 3), seg(a, 6)]
        parts.append(zeros(a, SLAB_W - (OFF_MISC + 4 * ML_HEADS + ROPE_D)))
        parts.append(seg(a, 0))
        return jnp.concatenate(parts, axis=-1)

    return build(w), build(b)


def kernel(x, c, ctx, c_ctx, ada_w, ada_b, norm1_w, norm2_w, w_in, b_in, conv_w, conv_b, ml_norm_w, mla_qa_norm,
           mla_wq_b, mla_kva_norm, mla_wkv_b, q_norm_w, k_norm_w, ret_norm_w, w_branch, w_out, router_w,
           exp_w1, exp_w3, exp_w2):
    bsz, n_lat, _ = x.shape
    n_ctx = ctx.shape[1]
    depth = ada_w.shape[0]
    ltot = n_lat + n_ctx
    nlt = n_lat // TM
    ltiles = ltot // TM
    nl, ncx = n_lat // CHUNK, n_ctx // CHUNK
    assert n_lat % TM == 0 and n_ctx == TM and bsz + 1 <= 8
    gather_tiles = max(t for t in range(1, GATHER_MAX_TILES + 1) if (bsz * ltiles) % t == 0)

    cond8 = jnp.zeros((8, D_MODEL), F32).at[:bsz].set(c).at[bsz].set(c_ctx)
    mods_all = _ada(cond8, ada_w, ada_b)

    rope32 = _rope_tables(n_lat, n_ctx)
    rope_r = jnp.tile(rope32, (1, 1, LANES // ROPE_D))
    ident = jnp.stack([jnp.ones((ltot, 1), F32), jnp.zeros((ltot, 1), F32), jnp.zeros((ltot, 1), F32)])
    rope_q = jnp.swapaxes(jnp.concatenate([jnp.broadcast_to(ident, (3, ltot, NOPE_D)), rope32,
                                           jnp.broadcast_to(ident, (3, ltot, LANES - QK_D))], axis=2), 1, 2)
    dmat, xi, kd, cd = _ret_tables()

    gi = jnp.arange(ML_HEADS * ML_DH) // ML_DH
    g8 = (gi[:, None] == jnp.arange(LANES)[None, :]).astype(BF16)
    g8t = g8.T

    cap_l = CAP_FACTOR * n_lat // N_EXPERTS
    cap_c = CAP_FACTOR * n_ctx // N_EXPERTS
    ct = bsz * (cap_l + cap_c)
    assert ct % (16 * FFN_ROW_CHUNKS) == 0 and cap_l % 16 == 0 and cap_c % 16 == 0
    rows = ct + WIN

    w_perm, b_perm = _permute_w_in(w_in, b_in[:, None, :])
    w_perm = w_perm.astype(BF16)
    xx = jnp.concatenate([x, ctx], axis=1)
    att_r = next(r for r in (11, 3, 1) if ltiles % r == 0)
    tq = min(2048, n_lat)

    for l in range(depth):
        mods = mods_all[l].reshape(8, 1, 6 * D_MODEL)
        slab, mq, mk = _inproj(xx, mods, norm1_w[l][None, :], w_perm, b_perm, conv_w[l], conv_b[l][None, :], l, nlt)
        hmf, hmb, hrf, hrb = _scans(mq, mk, slab, rope_r, dmat, xi, kd, cd, nl, ncx)

        wq = mla_wq_b[l].reshape(Q_LORA, MLA_HEADS, QK_D)
        wqt = jnp.pad(wq, ((0, 0), (0, 0), (0, LANES - QK_D))).reshape(Q_LORA, MLA_HEADS * LANES).T
        wkv = mla_wkv_b[l].reshape(KV_LORA, MLA_HEADS, NOPE_D + MLA_DV)
        wkt = jnp.pad(wkv[:, :, :NOPE_D], ((0, 0), (0, 0), (0, LANES - NOPE_D))).reshape(KV_LORA, MLA_HEADS * LANES).T
        wvt = wkv[:, :, NOPE_D:].reshape(KV_LORA, MLA_HEADS * MLA_DV).T
        qnw = jnp.pad(q_norm_w[l], (0, LANES - QK_D))[:, None]
        knw = jnp.pad(k_norm_w[l], (0, LANES - QK_D))[:, None]
        qt, kk, vt = _mla(slab, mla_qa_norm[l][None, :], wqt, mla_kva_norm[l][None, :], wkt, wvt, qnw, knw, rope_q)
        bound = (1.01 * QK_D ** 0.5 * LOG2E) * jnp.max(jnp.abs(q_norm_w[l])) * jnp.max(jnp.abs(k_norm_w[l]))
        flags = (bound <= ATT_BOUND_MAX).astype(jnp.int32).reshape(1)
        ub = jnp.full((8, LANES), bound, F32)
        yb_l = _attn(flags, qt, kk, vt, ub, tq, 0, n_lat // tq, 0, att_r, ltiles // att_r)
        yb_c = _attn(flags, qt, kk, vt, ub, TM, nlt, 1, nlt, 1, 1)

        rw = jnp.stack(_split2(jnp.pad(router_w[l], ((0, 0), (0, LANES - N_EXPERTS)))))
        xm, h2, aff = _merge(hmf, hmb, slab, ml_norm_w[l][None, :], yb_l, yb_c, hrf, hrb, ret_norm_w[l][None, :],
                             w_branch[l],
                             w_out[l], xx, mods, norm2_w[l][None, :], rw, g8, g8t, nlt)

        pos_l, post_l, gwt_l, off_l = _select(aff, n_lat, cap_l, 0, 0, cap_l + cap_c)
        pos_c, post_c, gwt_c, off_c = _select(aff, n_ctx, cap_c, nlt, cap_l, cap_l + cap_c)
        pos = jnp.concatenate([pos_l, pos_c], axis=1).reshape(bsz * ltot, LANES)
        expert_major = lambda a_l, a_c: jnp.swapaxes(jnp.concatenate([a_l, a_c], axis=2), 0, 1).reshape(
            N_EXPERTS, 1, bsz * ltot)
        post = expert_major(post_l, post_c)
        gwt = expert_major(gwt_l, gwt_c)
        base_l = (jnp.arange(bsz) * (cap_l + cap_c))[:, None, None]
        base_c = base_l + cap_l
        off = jnp.concatenate(
            [off_l[:, :, :2 * nlt].astype(jnp.int32) // SLOT_ALIGN * SLOT_ALIGN + base_l,
             off_c[:, :, :2].astype(jnp.int32) // SLOT_ALIGN * SLOT_ALIGN + base_c], axis=2)
        starts_c = jnp.swapaxes(off, 0, 1).reshape(-1)
        starts_g = jnp.swapaxes(off[:, :, ::2], 0, 1).reshape(-1)

        xs, gs = _moe_gather(starts_g, h2.reshape(bsz * ltot, D_MODEL), post, gwt, rows, gather_tiles)
        ys = _moe_ffn(xs, gs, exp_w1, exp_w3, exp_w2, l, ct)
        last = l == depth - 1
        xx = _combine(starts_c, xm.reshape(bsz * ltot, D_MODEL), pos, mods, ys, nlt, ltiles, last)
        xx = xx.reshape(bsz, n_lat if last else ltot, D_MODEL)

    return xx
```

```python
import functools

import jax
import jax.numpy as jnp
import numpy as np
from jax import lax
from jax.experimental import pallas as pl
from jax.experimental.pallas import tpu as pltpu

F32 = jnp.float32
BF16 = jnp.bfloat16
HIGHEST = lax.Precision.HIGHEST

D_MODEL = 1024
GRID_W = 64
N_BRANCH = 3
BRANCH_W = 512
ML_HEADS = 8
ML_DH = 64
CONV_K = 5
MLA_HEADS = 8
Q_LORA = 384
KV_LORA = 256
NOPE_D = 64
ROPE_D = 32
QK_D = NOPE_D + ROPE_D
MLA_DV = 64
RET_HEADS = 8
RET_DK = 32
RET_DV = 64
N_EXPERTS = 16
EXPERT_FF = 1024
CAP_FACTOR = 2
CHUNK = 128
ROPE_BASE = 10000.0
EPS = 1e-6
LOG2E = 1.4426950408889634

LANES = 128
TM = 256
ROW_SUB = 128
VMEM_LIMIT = 56 * 1024 * 1024

OFF_BL, OFF_V, OFF_OG, OFF_RV, OFF_RG = 0, 3072, 3584, 4096, 4608
OFF_RQ, OFF_RK, OFF_KV, OFF_QL, OFF_MISC = 5120, 5376, 5632, 6144, 6528
SLAB_W = 6656
NCOL = SLAB_W + 2 * ML_HEADS * ML_DH
TN_IN = SLAB_W // 2


def _sigmoid(x):
    return 1.0 / (1.0 + jnp.exp(-x))


def _silu(x):
    return x * _sigmoid(x)


def _log_sigmoid(x):
    return jnp.minimum(x, 0.0) - jnp.log1p(jnp.exp(-jnp.abs(x)))


def _dot(a, b, **kw):
    return jnp.dot(a, b, preferred_element_type=F32, **kw)


def _dot_nt(a, b, **kw):
    return lax.dot_general(a, b, (((1,), (1,)), ((), ())), preferred_element_type=F32, **kw)


def _dot_tn(a, b, **kw):
    return lax.dot_general(a, b, (((0,), (0,)), ((), ())), preferred_element_type=F32, **kw)


def _params(sem):
    return pltpu.CompilerParams(dimension_semantics=sem, vmem_limit_bytes=VMEM_LIMIT)


def _ada_kernel(c_ref, w_ref, b_ref, o_ref):
    s = _silu(c_ref[...])
    o_ref[...] = _dot(s, w_ref[...], precision=HIGHEST) + b_ref[...]


def _ada(cond8, ada_w, ada_b):
    depth = ada_w.shape[0]
    tn = 1536
    return pl.pallas_call(
        _ada_kernel,
        grid=(depth, 6 * D_MODEL // tn),
        in_specs=[pl.BlockSpec((8, D_MODEL), lambda l, j: (0, 0)),
                  pl.BlockSpec((None, D_MODEL, tn), lambda l, j: (l, 0, j)),
                  pl.BlockSpec((None, 1, tn), lambda l, j: (l, 0, j))],
        out_specs=pl.BlockSpec((None, 8, tn), lambda l, j: (l, 0, j)),
        out_shape=jax.ShapeDtypeStruct((depth, 8, 6 * D_MODEL), F32),
        compiler_params=_params(("arbitrary", "arbitrary")),
        name="ada",
    )(cond8, ada_w, ada_b.reshape(depth, 1, 6 * D_MODEL))


def _inproj_kernel(x_ref, xp_ref, xn_ref, mod_ref, nw_ref, w_ref, b_ref, cw_ref, cb_ref, o_ref, q_ref, k_ref, xe_ref,
                   *, nlt, ltiles):
    i = pl.program_id(1)
    first = (i == 0) | (i == nlt)
    last = (i == nlt - 1) | (i == ltiles - 1)
    sh = mod_ref[:, 0:D_MODEL]
    sc = mod_ref[:, D_MODEL:2 * D_MODEL]
    hw = ML_HEADS * ML_DH

    def hidden(x):
        hn = x * lax.rsqrt(jnp.mean(x * x, axis=-1, keepdims=True) + EPS) * nw_ref[...]
        return (hn * (1.0 + sc) + sh).astype(BF16)

    def qk_cols(h):
        return _dot(h, w_ref[:, SLAB_W:NCOL]) + b_ref[:, SLAB_W:NCOL]

    subs = list(range(0, TM, ROW_SUB))
    halo = qk_cols(hidden(jnp.concatenate([xp_ref[...], xn_ref[...]], axis=0)))
    xe_ref[0:8, :] = jnp.where(first, 0.0, halo[0:8])
    xe_ref[8 + TM:16 + TM, :] = jnp.where(last, 0.0, halo[8:16])
    hs = [hidden(x_ref[r0:r0 + ROW_SUB, :]) for r0 in subs]
    for r0, h in zip(subs, hs):
        xe_ref[8 + r0:8 + r0 + ROW_SUB, :] = qk_cols(h)
    for r0, h in zip(subs, hs):
        for c0 in range(0, SLAB_W, TN_IN):
            o_ref[r0:r0 + ROW_SUB, c0:c0 + TN_IN] = _dot(h, w_ref[:, c0:c0 + TN_IN]) + b_ref[:, c0:c0 + TN_IN]
    for c0 in range(0, 2 * hw, LANES):
        cs = slice(c0, c0 + LANES)
        acc = jnp.broadcast_to(cb_ref[:, cs], (TM, LANES))
        for j in range(CONV_K):
            acc = acc + xe_ref[8 - CONV_K // 2 + j:8 - CONV_K // 2 + j + TM, cs] * cw_ref[j:j + 1, cs]
        qk = _silu(acc)
        if c0 < hw:
            q_ref[:, cs] = qk.astype(BF16)
        else:
            k_ref[:, c0 - hw:c0 - hw + LANES] = (qk * (ML_DH ** -0.5)).astype(BF16)


def _inproj(x, mods, nw, w_all, b_all, conv_w, conv_b, layer, nlt):
    bsz, ltot, _ = x.shape
    hw = ML_HEADS * ML_DH
    ltiles = ltot // TM
    nrow8 = ltot // 8

    def mrow(bb, i):
        return (jnp.where(i < nlt, bb, bsz), 0, 0)

    qk_out = pl.BlockSpec((None, TM, hw), lambda bb, i: (bb, i, 0))
    return pl.pallas_call(
        functools.partial(_inproj_kernel, nlt=nlt, ltiles=ltiles),
        grid=(bsz, ltiles),
        in_specs=[pl.BlockSpec((None, TM, D_MODEL), lambda bb, i: (bb, i, 0)),
                  pl.BlockSpec((None, 8, D_MODEL), lambda bb, i: (bb, jnp.maximum(i * (TM // 8) - 1, 0), 0)),
                  pl.BlockSpec((None, 8, D_MODEL), lambda bb, i: (bb, jnp.minimum((i + 1) * (TM // 8), nrow8 - 1), 0)),
                  pl.BlockSpec((None, 1, 6 * D_MODEL), mrow),
                  pl.BlockSpec((1, D_MODEL), lambda bb, i: (0, 0)),
                  pl.BlockSpec((None, D_MODEL, NCOL), lambda bb, i: (layer, 0, 0)),
                  pl.BlockSpec((None, 1, NCOL), lambda bb, i: (layer, 0, 0)),
                  pl.BlockSpec((CONV_K, 2 * hw), lambda bb, i: (0, 0)),
                  pl.BlockSpec((1, 2 * hw), lambda bb, i: (0, 0))],
        out_specs=[pl.BlockSpec((None, TM, SLAB_W), lambda bb, i: (bb, i, 0)), qk_out, qk_out],
        out_shape=[jax.ShapeDtypeStruct((bsz, ltot, SLAB_W), F32),
                   jax.ShapeDtypeStruct((bsz, ltot, hw), BF16), jax.ShapeDtypeStruct((bsz, ltot, hw), BF16)],
        scratch_shapes=[pltpu.VMEM((TM + 16, 2 * hw), F32)],
        compiler_params=_params(("arbitrary", "arbitrary")),
        name="inproj",
    )(x, x, x, mods, nw, w_all, b_all, conv_w, conv_b)


def _chunk_of(d, s, nl, ncx):
    fwd = jnp.where(s < ncx, nl + s, s - ncx)
    bwd = jnp.where(s < ncx, nl + ncx - 1 - s, nl - 1 - (s - ncx))
    return jnp.where(d == 0, fwd, bwd)


def _lane_iota(shape):
    return lax.broadcasted_iota(jnp.int32, shape, len(shape) - 1)


def _split2(x):
    hi = x.astype(BF16)
    return hi, (x - hi.astype(F32)).astype(BF16)


def _split3(x):
    hi = x.astype(BF16)
    r1 = x - hi.astype(F32)
    mid = r1.astype(BF16)
    return hi, mid, (r1 - mid.astype(F32)).astype(BF16)


def _select_dot(x, sel_b):
    hi, lo = _split2(x)
    return _dot(hi, sel_b) + _dot(lo, sel_b)


GATE_LANE0 = ML_HEADS


def _mlstm2_kernel(qf_ref, kf_ref, vf_ref, gf_ref, qb_ref, kb_ref, vb_ref, gb_ref, of_ref, ob_ref, st_ref, m_ref):
    s = pl.program_id(1)

    @pl.when(s == 0)
    def _():
        st_ref[...] = jnp.zeros_like(st_ref)
        m_ref[...] = jnp.zeros_like(m_ref)

    dirs = (0, 1)
    pairs = range(ML_HEADS // 2)
    heads = range(ML_HEADS)
    q_refs, k_refs, v_refs, g_refs, o_refs = (qf_ref, qb_ref), (kf_ref, kb_ref), (vf_ref, vb_ref), (gf_ref, gb_ref), \
        (of_ref, ob_ref)
    lane = _lane_iota((1, LANES))
    head_lane = (lane >= GATE_LANE0) & (lane < GATE_LANE0 + ML_HEADS)
    lo_half = lane < ML_DH
    ti = lax.broadcasted_iota(jnp.int32, (CHUNK, CHUNK), 0)
    si = lax.broadcasted_iota(jnp.int32, (CHUNK, CHUNK), 1)
    causal = (ti >= si, ti <= si)
    tri = [jnp.where(causal[d], 1.0, 0.0).astype(BF16) for d in dirs]
    tri_t = [jnp.where(causal[1 - d], 1.0, 0.0).astype(BF16) for d in dirs]
    s8 = _lane_iota((ML_HEADS, CHUNK))
    gi = lax.broadcasted_iota(jnp.int32, (LANES, 2 * LANES), 0)
    li = lax.broadcasted_iota(jnp.int32, (LANES, 2 * LANES), 1)
    li_in = jnp.where(li >= LANES, li - LANES, li)
    same_head = (gi < ML_DH) == (li_in < ML_DH)
    lane2 = _lane_iota((1, 2 * LANES))
    lo_half2 = jnp.where(lane2 >= LANES, lane2 - LANES, lane2) < ML_DH
    hmask = [lo_half if h % 2 == 0 else jnp.logical_not(lo_half) for h in heads]
    ones_b = jnp.ones((CHUNK, LANES), BF16)
    sel_full = [jnp.where(gi == GATE_LANE0 + 2 * p + jnp.where(li >= LANES, 1, 0), 1.0, 0.0).astype(BF16)
                for p in pairs]
    quarter = jnp.where(li >= LANES, 2, 0) + jnp.where(li_in >= ML_DH, 1, 0)
    sel_half = [jnp.where(gi == GATE_LANE0 + 4 * q + quarter, 1.0, 0.0).astype(BF16)
                for q in range(ML_HEADS // 4)]
    t16 = lax.broadcasted_iota(jnp.int32, (16, LANES), 0)

    g_ig, ls = [], []
    for d in dirs:
        g_raw = g_refs[d][...]
        g_fg = g_raw if d == 0 else pltpu.roll(g_raw, LANES - 2 * ML_HEADS, axis=1)
        g_ig.append(jnp.where(head_lane, pltpu.roll(g_fg, ML_HEADS, axis=1), 0.0))
        ls.append(jnp.where(head_lane, _log_sigmoid(g_fg), 0.0))
    ls3 = [_split3(ls[d]) for d in dirs]
    lst3 = [_split3(ls[d].T) for d in dirs]
    b_cols = [sum(_dot(tri[d], t) for t in ls3[d]) for d in dirs]
    b_rows = [sum(_dot(t, tri_t[d]) for t in lst3[d]) for d in dirs]
    yield

    qp = [[q_refs[d][:, p * LANES:(p + 1) * LANES] for p in pairs] for d in dirs]
    kp = [[k_refs[d][:, p * LANES:(p + 1) * LANES] for p in pairs] for d in dirs]
    va = [[jnp.concatenate([v_refs[d][:, p * LANES:(p + 1) * LANES].astype(BF16), ones_b], axis=1) for p in pairs]
          for d in dirs]
    st = [[st_ref[d, p] for p in pairs] for d in dirs]
    qs = [[_dot(qp[d][p], st[d][p].astype(BF16)) for p in pairs] for d in dirs]
    qk = [[_dot_nt(jnp.where(hmask[h], qp[d][h // 2].astype(F32), 0.0).astype(BF16), kp[d][h // 2])
           for h in heads] for d in dirs]
    yield

    r_rows, cdiff, stacked = [], [], []
    for d in dirs:
        r_rows.append(g_ig[d].T - b_rows[d])
        bl_row = b_cols[d][CHUNK - 1:CHUNK, :] if d == 0 else b_cols[d][0:1, :]
        m_row = m_ref[d]
        cm8 = r_rows[d][GATE_LANE0:GATE_LANE0 + ML_HEADS, :]
        order = s8 if d == 0 else CHUNK - 1 - s8
        step = 1
        while step < CHUNK:
            shifted = pltpu.roll(cm8, step if d == 0 else CHUNK - step, axis=1)
            cm8 = jnp.where(order >= step, jnp.maximum(cm8, shifted), cm8)
            step *= 2
        cm = jnp.concatenate([jnp.zeros((GATE_LANE0, CHUNK), F32), cm8,
                              jnp.zeros((LANES - GATE_LANE0 - ML_HEADS, CHUNK), F32)], axis=0).T
        a_cols = b_cols[d] + m_row
        mt_cols = jnp.maximum(a_cols, b_cols[d] + cm)
        g_cols = bl_row - b_cols[d] + g_ig[d]
        m_new = jnp.maximum(bl_row + m_row, jnp.max(g_cols, axis=0, keepdims=True))
        dp_cols = jnp.where(t16 >= 0, jnp.exp(bl_row + m_row - m_new), 0.0)
        m_ref[d] = m_new
        cdiff.append(b_cols[d] - mt_cols)
        stacked.append(jnp.concatenate([jnp.exp(a_cols - mt_cols), jnp.exp(-mt_cols),
                                        jnp.exp(g_cols - m_new), dp_cols], axis=0))

    c_full = [[_select_dot(cdiff[d], sel_full[p]) for p in pairs] for d in dirs]
    halves2 = [[_select_dot(stacked[d], sel_half[q]) for q in range(ML_HEADS // 4)] for d in dirs]
    halves = [[halves2[d][p // 2][:, (p % 2) * LANES:(p % 2 + 1) * LANES] for p in pairs]
              for d in dirs]
    yield

    sc = [[None] * ML_HEADS for _ in dirs]
    for d in dirs:
        for h in heads:
            p, j = h // 2, h % 2
            dlog = c_full[d][p][:, j * LANES:(j + 1) * LANES] + r_rows[d][GATE_LANE0 + h:GATE_LANE0 + h + 1, :]
            sc[d][h] = (qk[d][h] * jnp.exp(jnp.where(causal[d], dlog, -jnp.inf))).astype(BF16)
    sv = [[_dot(sc[d][h], va[d][h // 2]) for h in heads] for d in dirs]
    upd = [[_dot_tn((kp[d][p].astype(F32) * halves[d][p][2 * CHUNK:3 * CHUNK]).astype(BF16), va[d][p])
            for p in pairs] for d in dirs]
    yield

    for d in dirs:
        outs = []
        for p in pairs:
            inter_pair = halves[d][p][0:CHUNK]
            emt_pair = halves[d][p][CHUNK:2 * CHUNK]
            dp_pair = halves[d][p][3 * CHUNK:3 * CHUNK + 1]
            sv_pair = jnp.where(lo_half2, sv[d][2 * p], sv[d][2 * p + 1])
            num = sv_pair[:, :LANES] + inter_pair * qs[d][p][:, :LANES]
            den = sv_pair[:, LANES:] + inter_pair * qs[d][p][:, LANES:]
            outs.append(num / jnp.maximum(jnp.abs(den), emt_pair))
            st_ref[d, p] = (jnp.concatenate([dp_pair, dp_pair], axis=1) * st[d][p]
                            + jnp.where(same_head, upd[d][p], 0.0))
        o_refs[d][...] = jnp.concatenate(outs, axis=1)


N_MLSTM_IN, N_RET_IN = 8, 12


def _scan_kernel(*refs):
    n_in = N_MLSTM_IN + N_RET_IN
    ml_in, rt_in = refs[:N_MLSTM_IN], refs[N_MLSTM_IN:n_in]
    mo_f, mo_b, ro_f, ro_b, ml_st, ml_m, rt_st = refs[n_in:]
    bodies = [_mlstm2_kernel(*ml_in, mo_f, mo_b, ml_st, ml_m), _ret_kernel(*rt_in, ro_f, ro_b, rt_st)]
    while bodies:
        for body in list(bodies):
            if next(body, "done") == "done":
                bodies.remove(body)


def _scans(q, k, slab, rope_r, dmat, xi, kd, cd, nl, ncx):
    bsz, ltot, _ = slab.shape
    hw = ML_HEADS * ML_DH
    qw = RET_HEADS * RET_DK
    vw = RET_HEADS * RET_DV

    def ml_specs(d):
        ch = lambda bb, s: _chunk_of(d, s, nl, ncx)
        return [pl.BlockSpec((None, CHUNK, hw), lambda bb, s: (bb, ch(bb, s), 0)),
                pl.BlockSpec((None, CHUNK, hw), lambda bb, s: (bb, ch(bb, s), 0)),
                pl.BlockSpec((None, CHUNK, hw), lambda bb, s: (bb, ch(bb, s), OFF_V // hw)),
                pl.BlockSpec((None, CHUNK, LANES), lambda bb, s: (bb, ch(bb, s), OFF_MISC // LANES))]

    def rt_specs(d):
        ch = lambda bb, s: _chunk_of(d, s, nl, ncx)
        return [pl.BlockSpec((None, CHUNK, qw), lambda bb, s: (bb, ch(bb, s), OFF_RQ // qw)),
                pl.BlockSpec((None, CHUNK, qw), lambda bb, s: (bb, ch(bb, s), OFF_RK // qw)),
                pl.BlockSpec((None, CHUNK, vw), lambda bb, s: (bb, ch(bb, s), OFF_RV // vw)),
                pl.BlockSpec((3, CHUNK, LANES), lambda bb, s: (0, ch(bb, s), 0))]

    full = lambda a: pl.BlockSpec(a.shape, lambda bb, s: (0,) * a.ndim)
    out = lambda d, w: pl.BlockSpec((None, CHUNK, w), lambda bb, s: (bb, _chunk_of(d, s, nl, ncx), 0))
    return pl.pallas_call(
        _scan_kernel,
        grid=(bsz, nl + ncx),
        in_specs=(ml_specs(0) + ml_specs(1) + rt_specs(0) + rt_specs(1)
                  + [full(dmat), full(xi), full(kd), full(cd)]),
        out_specs=[out(0, hw), out(1, hw), out(0, vw), out(1, vw)],
        out_shape=[jax.ShapeDtypeStruct((bsz, ltot, hw), F32)] * 2 + [jax.ShapeDtypeStruct((bsz, ltot, vw), F32)] * 2,
        scratch_shapes=[pltpu.VMEM((2, ML_HEADS // 2, LANES, 2 * LANES), F32),
                        pltpu.VMEM((2, 1, LANES), F32),
                        pltpu.VMEM((2, RET_HEADS // 2, LANES, LANES), F32)],
        compiler_params=_params(("arbitrary", "arbitrary")),
        name="scans",
    )(q, k, slab, slab, q, k, slab, slab, slab, slab, slab, rope_r, slab, slab, slab, rope_r, dmat, xi, kd, cd)


def _rope_lanes(x, tab_ref):
    return (x * tab_ref[0] + pltpu.roll(x, 8, axis=1) * tab_ref[1]
            + pltpu.roll(x, LANES - 8, axis=1) * tab_ref[2])


def _ret_kernel(qf_ref, kf_ref, vf_ref, rf_ref, qb_ref, kb_ref, vb_ref, rb_ref, dmat_ref, xi_ref, kd_ref, cd_ref,
                of_ref, ob_ref, st_ref):
    s = pl.program_id(1)

    @pl.when(s == 0)
    def _():
        st_ref[...] = jnp.zeros_like(st_ref)

    dirs = (0, 1)
    q_refs, k_refs, v_refs, r_refs, o_refs = (qf_ref, qb_ref), (kf_ref, kb_ref), (vf_ref, vb_ref), (rf_ref, rb_ref), \
        (of_ref, ob_ref)
    lane = _lane_iota((1, LANES))
    lo_half = lane < RET_DV
    ri = lax.broadcasted_iota(jnp.int32, (LANES, LANES), 0)
    ci = lax.broadcasted_iota(jnp.int32, (LANES, LANES), 1)
    heads = range(RET_HEADS)
    pairs = range(RET_HEADS // 2)
    qg = [[_rope_lanes(q_refs[d][:, g * LANES:(g + 1) * LANES], r_refs[d]) for g in range(2)] for d in dirs]
    kg = [[_rope_lanes(k_refs[d][:, g * LANES:(g + 1) * LANES], r_refs[d]) * (RET_DK ** -0.5) for g in range(2)]
          for d in dirs]
    qg_b = [[x.astype(BF16) for x in qg[d]] for d in dirs]
    kg_b = [[x.astype(BF16) for x in kg[d]] for d in dirs]
    kw_b = [[(kg[d][g] * kd_ref[d, :, g * LANES:(g + 1) * LANES]).astype(BF16) for g in range(2)] for d in dirs]
    vp_b = [[v_refs[d][:, p * LANES:(p + 1) * LANES].astype(BF16) for p in pairs] for d in dirs]
    st = [[st_ref[d, p] for p in pairs] for d in dirs]

    cross = [[_dot(qg_b[d][p // 2], st[d][p].astype(BF16)) for p in pairs] for d in dirs]
    qk = [[None] * RET_HEADS for _ in dirs]
    for d in dirs:
        for h in heads:
            q_lo = (h % 4) * RET_DK
            hm = (lane >= q_lo) & (lane < q_lo + RET_DK)
            qk[d][h] = _dot_nt(jnp.where(hm, qg[d][h // 4], 0.0).astype(BF16), kg_b[d][h // 4])
    yield
    sv = [[_dot((qk[d][h] * dmat_ref[d, h]).astype(BF16), vp_b[d][h // 2]) for h in heads] for d in dirs]
    upd = [[_dot_tn(kw_b[d][p // 2], vp_b[d][p]) for p in pairs] for d in dirs]
    yield

    for d in dirs:
        outs = []
        for p in pairs:
            a = p % 2
            outs.append(jnp.where(lo_half, sv[d][2 * p], sv[d][2 * p + 1])
                        + xi_ref[d, :, p * LANES:(p + 1) * LANES] * cross[d][p])
            r_lo = (ri >= 2 * a * RET_DK) & (ri < (2 * a + 1) * RET_DK)
            r_hi = (ri >= (2 * a + 1) * RET_DK) & (ri < (2 * a + 2) * RET_DK)
            valid = (r_lo & (ci < RET_DV)) | (r_hi & (ci >= RET_DV))
            st_ref[d, p] = cd_ref[d, :, p * LANES:(p + 1) * LANES] * st[d][p] + jnp.where(valid, upd[d][p], 0.0)
        o_refs[d][...] = jnp.concatenate(outs, axis=1)


def _ret_tables():
    lg = jnp.log1p(-jnp.exp2(-5.0 - jnp.arange(RET_HEADS, dtype=F32)))
    idx = jnp.arange(CHUNK, dtype=F32)
    diff = idx[:, None] - idx[None, :]
    dm, xis, kds, cds = [], [], [], []
    for lgd, sign in ((lg, 1.0), (lg[::-1], -1.0)):
        dd = diff[None] * sign
        dm.append(jnp.exp(jnp.where(dd >= 0, dd * lgd[:, None, None], -jnp.inf)))
        order = idx if sign > 0 else (CHUNK - 1.0 - idx)
        xi = jnp.exp((order + 1.0)[None] * lgd[:, None])
        kdv = jnp.exp((CHUNK - 1.0 - order)[None] * lgd[:, None])
        cdv = jnp.exp(CHUNK * lgd)
        xis.append(jnp.repeat(xi.T, RET_DV, axis=1))
        kds.append(jnp.repeat(kdv.T, RET_DK, axis=1))
        cds.append(jnp.repeat(cdv, RET_DV)[None, :])
    return jnp.stack(dm), jnp.stack(xis), jnp.stack(kds), jnp.stack(cds)


def _rope_tables(n_lat, n_ctx):
    t = jnp.arange(n_lat)
    row = (t // GRID_W).astype(F32)
    col = (t % GRID_W).astype(F32)
    half = ROPE_D // 4
    inv = ROPE_BASE ** (-jnp.arange(half, dtype=F32) / half)
    ar, ac = row[:, None] * inv, col[:, None] * inv
    z = jnp.zeros_like(ar)
    c32 = jnp.concatenate([jnp.cos(ar), jnp.cos(ar), jnp.cos(ac), jnp.cos(ac)], axis=1)
    s1 = jnp.concatenate([z, jnp.sin(ar), z, jnp.sin(ac)], axis=1)
    s2 = jnp.concatenate([-jnp.sin(ar), z, -jnp.sin(ac), z], axis=1)
    tab = jnp.stack([c32, s1, s2])
    ident = jnp.stack([jnp.ones((n_ctx, ROPE_D), F32), jnp.zeros((n_ctx, ROPE_D), F32),
                       jnp.zeros((n_ctx, ROPE_D), F32)])
    return jnp.concatenate([tab, ident], axis=1)


def _mla_kernel(ql_ref, kv_ref, misc_ref, qan_ref, wqt_ref, kvn_ref, wkt_ref, wvt_ref, qnw_ref, knw_ref,
                rope_ref, qt_ref, k_ref, vt_ref):
    def norm_rope(xh, gain_ref):
        ss = jnp.sum(xh * xh, axis=0, keepdims=True)
        xh = xh * lax.rsqrt(ss * (1.0 / QK_D) + EPS) * gain_ref[...]
        return (xh * rope_ref[0] + pltpu.roll(xh, 8, axis=0) * rope_ref[1]
                + pltpu.roll(xh, LANES - 8, axis=0) * rope_ref[2])

    ql = ql_ref[...]
    qn = ql * lax.rsqrt(jnp.mean(ql * ql, axis=-1, keepdims=True) + EPS) * qan_ref[...]
    qt = _dot_nt(wqt_ref[...].astype(BF16), qn.astype(BF16))
    for h in range(MLA_HEADS):
        xh = norm_rope(qt[h * LANES:(h + 1) * LANES, :], qnw_ref)
        qt_ref[h * LANES:(h + 1) * LANES, :] = (xh * (QK_D ** -0.5 * LOG2E)).astype(BF16)

    kv = kv_ref[...]
    kvn = (kv * lax.rsqrt(jnp.mean(kv * kv, axis=-1, keepdims=True) + EPS) * kvn_ref[...]).astype(BF16)
    kt = _dot_nt(wkt_ref[...].astype(BF16), kvn)
    tm = kt.shape[1]
    shared = misc_ref[...].T[4 * ML_HEADS:4 * ML_HEADS + ROPE_D, :]
    kr = jnp.concatenate([jnp.zeros((NOPE_D, tm), F32), shared, jnp.zeros((LANES - QK_D, tm), F32)], axis=0)
    for h in range(MLA_HEADS):
        xh = norm_rope(kt[h * LANES:(h + 1) * LANES, :] + kr, knw_ref)
        k_ref[:, h * LANES:(h + 1) * LANES] = xh.T.astype(BF16)
    vt_ref[...] = _dot_nt(wvt_ref[...].astype(BF16), kvn).astype(BF16)


def _mla(slab, qan, wqt, kvn, wkt, wvt, qnw, knw, rope_q):
    bsz, ltot, _ = slab.shape
    hq = MLA_HEADS * LANES
    hv = MLA_HEADS * MLA_DV
    full = lambda shape: pl.BlockSpec(shape, lambda bb, i: (0,) * len(shape))
    return pl.pallas_call(
        _mla_kernel,
        grid=(bsz, ltot // TM),
        in_specs=[pl.BlockSpec((None, TM, Q_LORA), lambda bb, i: (bb, i, OFF_QL // Q_LORA)),
                  pl.BlockSpec((None, TM, KV_LORA), lambda bb, i: (bb, i, OFF_KV // KV_LORA)),
                  pl.BlockSpec((None, TM, LANES), lambda bb, i: (bb, i, OFF_MISC // LANES)),
                  full((1, Q_LORA)), full((hq, Q_LORA)), full((1, KV_LORA)), full((hq, KV_LORA)),
                  full((hv, KV_LORA)), full((LANES, 1)), full((LANES, 1)),
                  pl.BlockSpec((3, LANES, TM), lambda bb, i: (0, 0, i))],
        out_specs=[pl.BlockSpec((None, hq, TM), lambda bb, i: (bb, 0, i)),
                   pl.BlockSpec((None, TM, hq), lambda bb, i: (bb, i, 0)),
                   pl.BlockSpec((None, None, hv, TM), lambda bb, i: (bb, i, 0, 0))],
        out_shape=[jax.ShapeDtypeStruct((bsz, hq, ltot), BF16),
                   jax.ShapeDtypeStruct((bsz, ltot, hq), BF16),
                   jax.ShapeDtypeStruct((bsz, ltot // TM, hv, TM), BF16)],
        compiler_params=_params(("arbitrary", "arbitrary")),
        name="mla",
    )(slab, slab, slab, qan, wqt, kvn, wkt, wvt, qnw, knw, rope_q)


ATT_QC = 512
ATT_ONES = 16
ATT_AHEAD = 3


ATT_BOUND_MAX = 60.0


def _attn_kernel(flag_ref, qt_ref, k_ref, vt_ref, ub_ref, o_ref, acc_ref, m_ref, l_ref, *, r, nkb):
    tq = qt_ref.shape[1]
    tkb = r * TM
    ones = jnp.ones((ATT_ONES, TM), BF16)
    qc = min(ATT_QC, tq)
    units = [(j, c) for j in range(2) for c in range(tq // qc)]
    ahead = min(ATT_AHEAD, len(units))
    bounded = flag_ref[0] == 1
    ub = ub_ref[0:1, 0:1]

    def scores(r0, u):
        j, c = units[u]
        return _dot(k_ref[pl.ds(r0, tkb), j * LANES:(j + 1) * LANES],
                    qt_ref[j * LANES:(j + 1) * LANES, c * qc:(c + 1) * qc])

    def weighted_values(kb, j, p):
        pb = p.astype(BF16)
        pv = None
        for t in range(r):
            vt = jnp.concatenate([vt_ref[kb * r + t, j * MLA_DV:(j + 1) * MLA_DV, :], ones], axis=0)
            part = _dot(vt, pb[t * TM:(t + 1) * TM])
            pv = part if pv is None else pv + part
        return pv[0:MLA_DV], pv[MLA_DV:MLA_DV + 1]

    def bounded_block(kb, carry):
        r0 = pl.multiple_of(kb * tkb, tkb)
        pending = [scores(r0, u) for u in range(ahead)]
        for u, (j, c) in enumerate(units):
            sc = pending.pop(0)
            if u + ahead < len(units):
                pending.append(scores(r0, u + ahead))
            cs = slice(c * qc, (c + 1) * qc)
            pv, ps = weighted_values(kb, j, jnp.exp2(sc - ub))
            acc_ref[j * MLA_DV:(j + 1) * MLA_DV, cs] += pv
            l_ref[j:j + 1, cs] += ps
        return carry

    def online_block(kb, carry):
        r0 = pl.multiple_of(kb * tkb, tkb)
        pending = [scores(r0, u) for u in range(ahead)]
        for u, (j, c) in enumerate(units):
            sc = pending.pop(0)
            if u + ahead < len(units):
                pending.append(scores(r0, u + ahead))
            cs = slice(c * qc, (c + 1) * qc)
            m_old = m_ref[j:j + 1, cs]
            m_new = jnp.maximum(m_old, jnp.max(sc, axis=0, keepdims=True))
            alpha = jnp.exp2(m_old - m_new)
            pv, ps = weighted_values(kb, j, jnp.exp2(sc - m_new))
            acc_ref[j * MLA_DV:(j + 1) * MLA_DV, cs] = alpha * acc_ref[j * MLA_DV:(j + 1) * MLA_DV, cs] + pv
            l_ref[j:j + 1, cs] = alpha * l_ref[j:j + 1, cs] + ps
            m_ref[j:j + 1, cs] = m_new
        return carry

    acc_ref[...] = jnp.zeros_like(acc_ref)
    l_ref[...] = jnp.zeros_like(l_ref)

    @pl.when(bounded)
    def _():
        lax.fori_loop(0, nkb, bounded_block, 0)

    @pl.when(jnp.logical_not(bounded))
    def _():
        m_ref[...] = jnp.full_like(m_ref, -jnp.inf)
        lax.fori_loop(0, nkb, online_block, 0)

    head_rows = lax.broadcasted_iota(jnp.int32, acc_ref.shape, 0)
    inv = jnp.where(head_rows < MLA_DV, 1.0 / l_ref[0:1, :], 1.0 / l_ref[1:2, :])
    o_ref[...] = (acc_ref[...] * inv).T


def _attn(flags, qt, k, vt, ub, tq, q_off, nq, key_tile0, r, nkb):
    bsz = qt.shape[0]
    hv = vt.shape[2]
    nkt = r * nkb
    assert key_tile0 % nkt == 0
    kern = functools.partial(_attn_kernel, r=r, nkb=nkb)
    return pl.pallas_call(
        kern,
        grid_spec=pltpu.PrefetchScalarGridSpec(
            num_scalar_prefetch=1,
            grid=(bsz, MLA_HEADS // 2, nq),
            in_specs=[pl.BlockSpec((None, 2 * LANES, tq), lambda bb, hp, qi, fl: (bb, hp, qi + q_off)),
                      pl.BlockSpec((None, nkt * TM, 2 * LANES), lambda bb, hp, qi, fl: (bb, key_tile0 // nkt, hp)),
                      pl.BlockSpec((None, nkt, 2 * MLA_DV, TM),
                                   lambda bb, hp, qi, fl: (bb, key_tile0 // nkt, hp, 0)),
                      pl.BlockSpec((8, LANES), lambda bb, hp, qi, fl: (0, 0))],
            out_specs=pl.BlockSpec((None, tq, 2 * MLA_DV), lambda bb, hp, qi, fl: (bb, qi, hp)),
            scratch_shapes=[pltpu.VMEM((2 * MLA_DV, tq), F32), pltpu.VMEM((8, tq), F32),
                            pltpu.VMEM((8, tq), F32)]),
        out_shape=jax.ShapeDtypeStruct((bsz, nq * tq, hv), F32),
        compiler_params=_params(("arbitrary", "arbitrary", "arbitrary")),
        name="attn",
    )(flags, qt, k, vt, ub)


def _split_dot(x, w_b):
    hi = x.astype(BF16)
    lo = (x - hi.astype(F32)).astype(BF16)
    return _dot(hi, w_b) + _dot(lo, w_b)


def _head_ln(x, g_ref, gt_ref, width):
    mu = _split_dot(_split_dot(x, g_ref[...]) * (1.0 / width), gt_ref[...])
    xc = x - mu
    var = _split_dot(_split_dot(xc * xc, g_ref[...]) * (1.0 / width), gt_ref[...])
    return xc * lax.rsqrt(var + EPS)


def _merge_kernel(hmf_ref, hmb_ref, og_ref, mlw_ref, ybl_ref, ybc_ref, hrf_ref, hrb_ref, rg_ref, rnw_ref, bl_ref,
                  wbr_ref, wout_ref, x_ref, mod_ref, n2w_ref, rw_ref, g_ref, gt_ref,
                  xo_ref, h2_ref, aff_ref, wbr_b, wout_b, *, nlt):
    @pl.when((pl.program_id(0) == 0) & (pl.program_id(1) == 0))
    def _():
        wbr_b[...] = wbr_ref[...].astype(BF16)
        wout_b[...] = wout_ref[...].astype(BF16)

    is_ctx = pl.program_id(1) >= nlt

    subs = [slice(r0, r0 + ROW_SUB) for r0 in range(0, TM, ROW_SUB)]
    ys = []
    for rs in subs:
        ya = _sigmoid(og_ref[rs, :]) * (_head_ln(hmf_ref[rs, :] + hmb_ref[rs, :], g_ref, gt_ref, ML_DH)
                                        * mlw_ref[...])
        yc = _silu(rg_ref[rs, :]) * (_head_ln(hrf_ref[rs, :] + hrb_ref[rs, :], g_ref, gt_ref, RET_DV)
                                     * rnw_ref[...])
        yb = jnp.where(is_ctx, ybc_ref[rs, :], ybl_ref[rs, :])
        ys.append((ya.astype(BF16), yb.astype(BF16), yc.astype(BF16)))
    br = [[_dot(ys[i][n], wbr_b[n]) for n in range(N_BRANCH)] for i in range(len(subs))]
    merged = []
    for i, rs in enumerate(subs):
        m = None
        for n in range(N_BRANCH):
            term = _sigmoid(bl_ref[rs, n * D_MODEL:(n + 1) * D_MODEL]) * br[i][n]
            m = term if m is None else m + term
        merged.append(m.astype(BF16))
    y = [_dot(m, wout_b[...]) for m in merged]
    g1 = mod_ref[:, 2 * D_MODEL:3 * D_MODEL]
    h2s = []
    for i, rs in enumerate(subs):
        x = x_ref[rs, :] + g1 * y[i]
        xo_ref[rs, :] = x
        hn = x * lax.rsqrt(jnp.mean(x * x, axis=-1, keepdims=True) + EPS) * n2w_ref[...]
        h2 = hn * (1.0 + mod_ref[:, 4 * D_MODEL:5 * D_MODEL]) + mod_ref[:, 3 * D_MODEL:4 * D_MODEL]
        h2_ref[rs, :] = h2.astype(BF16)
        h2s.append(h2)
    logits = []
    for h2 in h2s:
        hi, lo = _split2(h2)
        logits.append(_dot(hi, rw_ref[0]) + _dot(hi, rw_ref[1]) + _dot(lo, rw_ref[0]))
    valid = _lane_iota((1, LANES)) < N_EXPERTS
    for i, rs in enumerate(subs):
        lg = jnp.where(valid, logits[i], -jnp.inf)
        e = jnp.exp(lg - jnp.max(lg, axis=-1, keepdims=True))
        aff_ref[:, rs] = (e / jnp.sum(e, axis=-1, keepdims=True)).T[0:N_EXPERTS, :]


def _merge(hmf, hmb, slab, mlw, yb_l, yb_c, hrf, hrb, rnw, wbr, wout, x, mods, n2w, rw, g8, g8t, nlt):
    bsz, ltot, _ = x.shape
    hw = ML_HEADS * ML_DH

    def mrow(bb, i):
        return (jnp.where(i < nlt, bb, bsz), 0, 0)

    full = lambda shape: pl.BlockSpec(shape, lambda bb, i: (0,) * len(shape))
    row = lambda width, off: pl.BlockSpec((None, TM, width), lambda bb, i: (bb, i, off // width))
    return pl.pallas_call(
        functools.partial(_merge_kernel, nlt=nlt),
        grid=(bsz, ltot // TM),
        in_specs=[row(hw, 0), row(hw, 0),
                  row(hw, OFF_OG), full((1, hw)),
                  pl.BlockSpec((None, TM, hw), lambda bb, i: (bb, jnp.minimum(i, nlt - 1), 0)),
                  pl.BlockSpec((None, TM, hw), lambda bb, i: (bb, 0, 0)),
                  row(hw, 0), row(hw, 0),
                  row(hw, OFF_RG), full((1, hw)),
                  row(N_BRANCH * D_MODEL, OFF_BL),
                  full((N_BRANCH, BRANCH_W, D_MODEL)), full((D_MODEL, D_MODEL)),
                  row(D_MODEL, 0),
                  pl.BlockSpec((None, 1, 6 * D_MODEL), mrow),
                  full((1, D_MODEL)), full((2, D_MODEL, LANES)), full((hw, LANES)), full((LANES, hw))],
        out_specs=[row(D_MODEL, 0), row(D_MODEL, 0), pl.BlockSpec((None, N_EXPERTS, TM), lambda bb, i: (bb, 0, i))],
        out_shape=[jax.ShapeDtypeStruct((bsz, ltot, D_MODEL), F32),
                   jax.ShapeDtypeStruct((bsz, ltot, D_MODEL), BF16),
                   jax.ShapeDtypeStruct((bsz, N_EXPERTS, ltot), F32)],
        scratch_shapes=[pltpu.VMEM((N_BRANCH, BRANCH_W, D_MODEL), BF16), pltpu.VMEM((D_MODEL, D_MODEL), BF16)],
        compiler_params=_params(("arbitrary", "arbitrary")),
        name="merge",
    )(hmf, hmb, slab, mlw, yb_l, yb_c, hrf, hrb, slab, rnw, slab, wbr, wout, x, mods, n2w, rw, g8, g8t)


def _select_kernel(aff_ref, pos_ref, post_ref, gwt_ref, off_ref, *, n, cap, base0, base_step):
    b = pl.program_id(0)
    base = (base0 + b * base_step).astype(F32)
    bits = lax.bitcast_convert_type(aff_ref[...], jnp.int32)
    capf = jnp.float32(cap)

    def search(it, cur):
        cand = cur | jnp.left_shift(jnp.int32(1), 30 - it)
        cnt = jnp.sum(jnp.where(bits >= cand, 1.0, 0.0), axis=1, keepdims=True)
        return jnp.where(cnt >= capf, cand, cur)

    thr = lax.fori_loop(0, 31, search, jnp.zeros((N_EXPERTS, 1), jnp.int32))
    n_gt = jnp.sum(jnp.where(bits > thr, 1.0, 0.0), axis=1, keepdims=True)
    need = capf - n_gt

    ri = lax.broadcasted_iota(jnp.int32, (TM, TM), 0)
    ci = lax.broadcasted_iota(jnp.int32, (TM, TM), 1)
    before = (ri < ci).astype(BF16)
    half_lane = _lane_iota((N_EXPERTS, LANES))
    unused = jnp.full((LANES - N_EXPERTS, TM), -1.0, F32)

    def tile(i, carry):
        c_eq, c_sel, offs = carry
        r0 = pl.multiple_of(i * TM, TM)
        a = aff_ref[:, pl.ds(r0, TM)]
        bt = lax.bitcast_convert_type(a, jnp.int32)
        eq = bt == thr
        rank = _dot(jnp.where(eq, 1.0, 0.0).astype(BF16), before) + c_eq
        sel = (bt > thr) | (eq & (rank < need))
        self_ = jnp.where(sel, 1.0, 0.0)
        pos = _dot(self_.astype(BF16), before) + c_sel
        posv = jnp.where(sel, pos + base, -1.0)
        post_ref[:, pl.ds(r0, TM)] = posv
        gwt_ref[:, pl.ds(r0, TM)] = jnp.where(sel, a, 0.0)
        pos_ref[pl.ds(r0, TM), :] = jnp.concatenate([posv, unused], axis=0).T
        c_half = c_sel + jnp.sum(self_[:, 0:TM // 2], axis=1, keepdims=True)
        offs = jnp.where(half_lane == 2 * i, c_sel, offs)
        offs = jnp.where(half_lane == 2 * i + 1, c_half, offs)
        return (c_eq + jnp.sum(jnp.where(eq, 1.0, 0.0), axis=1, keepdims=True),
                c_sel + jnp.sum(self_, axis=1, keepdims=True), offs)

    zero = jnp.zeros((N_EXPERTS, 1), F32)
    _, _, offs = lax.fori_loop(0, n // TM, tile, (zero, zero, jnp.zeros((N_EXPERTS, LANES), F32)))
    off_ref[...] = offs


def _select(afft, n, cap, tile_off, base0, base_step):
    bsz = afft.shape[0]
    assert 2 * (n // TM) <= LANES
    kern = functools.partial(_select_kernel, n=n, cap=cap, base0=base0, base_step=base_step)
    return pl.pallas_call(
        kern,
        grid=(bsz,),
        in_specs=[pl.BlockSpec((None, N_EXPERTS, n), lambda bb: (bb, 0, tile_off))],
        out_specs=[pl.BlockSpec((None, n, LANES), lambda bb: (bb, 0, 0)),
                   pl.BlockSpec((None, N_EXPERTS, n), lambda bb: (bb, 0, 0)),
                   pl.BlockSpec((None, N_EXPERTS, n), lambda bb: (bb, 0, 0)),
                   pl.BlockSpec((None, N_EXPERTS, LANES), lambda bb: (bb, 0, 0))],
        out_shape=[jax.ShapeDtypeStruct((bsz, n, LANES), F32),
                   jax.ShapeDtypeStruct((bsz, N_EXPERTS, n), F32),
                   jax.ShapeDtypeStruct((bsz, N_EXPERTS, n), F32),
                   jax.ShapeDtypeStruct((bsz, N_EXPERTS, LANES), F32)],
        compiler_params=_params(("arbitrary",)),
        name="select",
    )(afft)


SLOT_ALIGN = 16
WIN = TM + SLOT_ALIGN
TC = TM // 2
WINC = TC + SLOT_ALIGN
GATHER_MAX_TILES = 11
FFN_ROW_CHUNKS = 4
FFN_TF = 512


GATHER_EXPERTS = 2


def _moe_gather_kernel(st_ref, h_ref, pt_ref, gt_ref, xs_ref, gs_ref, *, nt256, tiles):
    e0 = pl.program_id(0) * GATHER_EXPERTS
    i = pl.program_id(1)

    @pl.when(i == 0)
    def _():
        xs_ref[...] = jnp.zeros_like(xs_ref)
        gs_ref[...] = jnp.zeros_like(gs_ref)

    slot = lax.broadcasted_iota(jnp.int32, (WIN, TM), 0)
    for u in range(tiles):
        starts = [pl.multiple_of(st_ref[(e0 + x) * nt256 + i * tiles + u], SLOT_ALIGN) for x in range(GATHER_EXPERTS)]
        match = [(slot + starts[x]).astype(F32) == pt_ref[x, :, u * TM:(u + 1) * TM] for x in range(GATHER_EXPERTS)]
        onehot = jnp.concatenate([jnp.where(m, 1.0, 0.0).astype(BF16) for m in match], axis=0)
        got_all = _dot(onehot, h_ref[u * TM:(u + 1) * TM, :])
        for x in range(GATHER_EXPERTS):
            st = starts[x]
            st2 = pl.multiple_of(st + SLOT_ALIGN, SLOT_ALIGN)
            got = got_all[x * WIN:(x + 1) * WIN]
            head = xs_ref[x, pl.ds(st, SLOT_ALIGN), :].astype(F32)
            xs_ref[x, pl.ds(st, SLOT_ALIGN), :] = (head + got[0:SLOT_ALIGN]).astype(BF16)
            xs_ref[x, pl.ds(st2, TM), :] = got[SLOT_ALIGN:WIN].astype(BF16)
            gate = jnp.sum(jnp.where(match[x], gt_ref[x, :, u * TM:(u + 1) * TM], 0.0), axis=1, keepdims=True)
            gate = jnp.broadcast_to(gate, (WIN, LANES))
            gs_ref[x, pl.ds(st, SLOT_ALIGN), :] = gs_ref[x, pl.ds(st, SLOT_ALIGN), :] + gate[0:SLOT_ALIGN]
            gs_ref[x, pl.ds(st2, TM), :] = gate[SLOT_ALIGN:WIN]


def _moe_gather(starts, h2, post, gwt, rows, tiles):
    ttot = h2.shape[0]
    tt = tiles * TM
    kern = functools.partial(_moe_gather_kernel, nt256=ttot // TM, tiles=tiles)
    return pl.pallas_call(
        kern,
        grid_spec=pltpu.PrefetchScalarGridSpec(
            num_scalar_prefetch=1,
            grid=(N_EXPERTS // GATHER_EXPERTS, ttot // tt),
            in_specs=[pl.BlockSpec((tt, D_MODEL), lambda e, i, st: (i, 0)),
                      pl.BlockSpec((GATHER_EXPERTS, 1, tt), lambda e, i, st: (e, 0, i)),
                      pl.BlockSpec((GATHER_EXPERTS, 1, tt), lambda e, i, st: (e, 0, i))],
            out_specs=[pl.BlockSpec((GATHER_EXPERTS, rows, D_MODEL), lambda e, i, st: (e, 0, 0)),
                       pl.BlockSpec((GATHER_EXPERTS, rows, LANES), lambda e, i, st: (e, 0, 0))]),
        out_shape=[jax.ShapeDtypeStruct((N_EXPERTS, rows, D_MODEL), BF16),
                   jax.ShapeDtypeStruct((N_EXPERTS, rows, LANES), F32)],
        compiler_params=_params(("arbitrary", "arbitrary")),
        name="moe_gather",
    )(starts, h2, post, gwt)


def _moe_ffn_kernel(xs_ref, gs_ref, w1_ref, w3_ref, w2_ref, ys_ref, acc_ref, *, ct, rows):
    f = pl.program_id(1)

    @pl.when(f == 0)
    def _():
        acc_ref[...] = jnp.zeros_like(acc_ref)

    w1b = w1_ref[...].astype(BF16)
    w3b = w3_ref[...].astype(BF16)
    w2b = w2_ref[...].astype(BF16)
    rc = ct // FFN_ROW_CHUNKS
    chunks = [slice(r * rc, (r + 1) * rc) for r in range(FFN_ROW_CHUNKS)]
    up = [(_dot(xs_ref[chunks[0], :], w1b), _dot(xs_ref[chunks[0], :], w3b))]
    for r, rs in enumerate(chunks):
        a, b = up[r]
        if r + 1 < len(chunks):
            up.append((_dot(xs_ref[chunks[r + 1], :], w1b), _dot(xs_ref[chunks[r + 1], :], w3b)))
        acc_ref[rs, :] += _dot((_silu(a) * b).astype(BF16), w2b)

    @pl.when(f == pl.num_programs(1) - 1)
    def _():
        gate = gs_ref[...]
        for c in range(D_MODEL // LANES):
            cs = slice(c * LANES, (c + 1) * LANES)
            ys_ref[0:ct, cs] = (acc_ref[:, cs] * gate).astype(BF16)
        ys_ref[ct:rows, :] = jnp.zeros((rows - ct, D_MODEL), BF16)


def _moe_ffn(xs, gs, w1, w3, w2, layer, ct):
    rows = xs.shape[1]
    kern = functools.partial(_moe_ffn_kernel, ct=ct, rows=rows)
    return pl.pallas_call(
        kern,
        grid=(N_EXPERTS, EXPERT_FF // FFN_TF),
        in_specs=[pl.BlockSpec((None, ct, D_MODEL), lambda e, f: (e, 0, 0)),
                  pl.BlockSpec((None, ct, LANES), lambda e, f: (e, 0, 0)),
                  pl.BlockSpec((None, None, D_MODEL, FFN_TF), lambda e, f: (layer, e, 0, f)),
                  pl.BlockSpec((None, None, D_MODEL, FFN_TF), lambda e, f: (layer, e, 0, f)),
                  pl.BlockSpec((None, None, FFN_TF, D_MODEL), lambda e, f: (layer, e, f, 0))],
        out_specs=pl.BlockSpec((None, rows, D_MODEL), lambda e, f: (e, 0, 0)),
        out_shape=jax.ShapeDtypeStruct((N_EXPERTS, rows, D_MODEL), BF16),
        scratch_shapes=[pltpu.VMEM((ct, D_MODEL), F32)],
        compiler_params=_params(("arbitrary", "arbitrary")),
        name="moe_ffn",
    )(xs, gs, w1, w3, w2)


def _combine_kernel(st_ref, x_ref, pos_ref, mod_ref, *rest, tile_of, ntc):
    nsub = TM // TC
    ys_refs = rest[:N_EXPERTS]
    o_ref = rest[N_EXPERTS]
    i = tile_of(pl.program_id(0))
    slot = lax.broadcasted_iota(jnp.int32, (TC, WINC), 1)
    g2 = mod_ref[:, 5 * D_MODEL:6 * D_MODEL]
    for u in range(nsub):
        rs = slice(u * TC, (u + 1) * TC)
        pos = pos_ref[rs, :]
        acc = None
        for e in range(N_EXPERTS):
            st = st_ref[e * ntc + i * nsub + u]
            delta = pl.multiple_of(st - st_ref[e * ntc + i * nsub], SLOT_ALIGN)
            onehot = jnp.where((slot + st).astype(F32) == pos[:, e:e + 1], 1.0, 0.0).astype(BF16)
            part = _dot(onehot, ys_refs[e][0, pl.ds(delta, WINC), :])
            acc = part if acc is None else acc + part
        o_ref[rs, :] = x_ref[rs, :] + g2 * acc


def _combine(starts, x, pos, mods_rows, ys, nlt, ltiles, latent_only):
    ttot = x.shape[0]
    nt = ttot // TM
    nsub = TM // TC
    bsz = ttot // (ltiles * TM)
    steps = bsz * nlt if latent_only else nt
    tile_of = (lambda j: (j // nlt) * ltiles + j % nlt) if latent_only else (lambda j: j)

    def mrow(j, st):
        i = tile_of(j)
        return (jnp.where(i % ltiles < nlt, i // ltiles, bsz), 0, 0)

    def ys_spec(e):
        return pl.BlockSpec(
            (pl.Element(1), pl.Element(WIN), pl.Element(D_MODEL)),
            lambda j, st: (e, pl.multiple_of(st[e * nt * nsub + tile_of(j) * nsub], SLOT_ALIGN), 0))

    return pl.pallas_call(
        functools.partial(_combine_kernel, tile_of=tile_of, ntc=nt * nsub),
        grid_spec=pltpu.PrefetchScalarGridSpec(
            num_scalar_prefetch=1,
            grid=(steps,),
            in_specs=[pl.BlockSpec((TM, D_MODEL), lambda j, st: (tile_of(j), 0)),
                      pl.BlockSpec((TM, LANES), lambda j, st: (tile_of(j), 0)),
                      pl.BlockSpec((None, 1, 6 * D_MODEL), mrow)]
                     + [ys_spec(e) for e in range(N_EXPERTS)],
            out_specs=pl.BlockSpec((TM, D_MODEL), lambda j, st: (j, 0))),
        out_shape=jax.ShapeDtypeStruct((steps * TM, D_MODEL), F32),
        compiler_params=_params(("arbitrary",)),
        name="combine",
    )(starts, x, pos, mods_rows, *([ys] * N_EXPERTS))


def _permute_w_in(w, b):
    widths = (2 * ML_HEADS * ML_DH, ML_HEADS * ML_DH, ML_HEADS * ML_DH, 4 * ML_HEADS, Q_LORA, KV_LORA, ROPE_D,
              RET_HEADS * RET_DK, RET_HEADS * RET_DK, RET_HEADS * RET_DV, RET_HEADS * RET_DV, N_BRANCH * D_MODEL)
    offs = np.concatenate([[0], np.cumsum(widths)])
    seg = lambda a, k: a[..., offs[k]:offs[k + 1]]
    zeros = lambda a, n: jnp.zeros(a.shape[:-1] + (n,), a.dtype)

    def build(a):
        parts = [seg(a, k) for k in (11, 1, 2, 9, 10, 7, 8, 5)]
        parts.append(zeros(a, OFF_QL - (OFF_KV + KV_LORA)))
        parts += [seg(a, 4), seg(a, 3), seg(a, 6)]
        parts.append(zeros(a, SLAB_W - (OFF_MISC + 4 * ML_HEADS + ROPE_D)))
        parts.append(seg(a, 0))
        return jnp.concatenate(parts, axis=-1)

    return build(w), build(b)


def kernel(x, c, ctx, c_ctx, ada_w, ada_b, norm1_w, norm2_w, w_in, b_in, conv_w, conv_b, ml_norm_w, mla_qa_norm,
           mla_wq_b, mla_kva_norm, mla_wkv_b, q_norm_w, k_norm_w, ret_norm_w, w_branch, w_out, router_w,
           exp_w1, exp_w3, exp_w2):
    bsz, n_lat, _ = x.shape
    n_ctx = ctx.shape[1]
    depth = ada_w.shape[0]
    ltot = n_lat + n_ctx
    nlt = n_lat // TM
    ltiles = ltot // TM
    nl, ncx = n_lat // CHUNK, n_ctx // CHUNK
    assert n_lat % TM == 0 and n_ctx == TM and bsz + 1 <= 8
    gather_tiles = max(t for t in range(1, GATHER_MAX_TILES + 1) if (bsz * ltiles) % t == 0)

    cond8 = jnp.zeros((8, D_MODEL), F32).at[:bsz].set(c).at[bsz].set(c_ctx)
    mods_all = _ada(cond8, ada_w, ada_b)

    rope32 = _rope_tables(n_lat, n_ctx)
    rope_r = jnp.tile(rope32, (1, 1, LANES // ROPE_D))
    ident = jnp.stack([jnp.ones((ltot, 1), F32), jnp.zeros((ltot, 1), F32), jnp.zeros((ltot, 1), F32)])
    rope_q = jnp.swapaxes(jnp.concatenate([jnp.broadcast_to(ident, (3, ltot, NOPE_D)), rope32,
                                           jnp.broadcast_to(ident, (3, ltot, LANES - QK_D))], axis=2), 1, 2)
    dmat, xi, kd, cd = _ret_tables()

    gi = jnp.arange(ML_HEADS * ML_DH) // ML_DH
    g8 = (gi[:, None] == jnp.arange(LANES)[None, :]).astype(BF16)
    g8t = g8.T

    cap_l = CAP_FACTOR * n_lat // N_EXPERTS
    cap_c = CAP_FACTOR * n_ctx // N_EXPERTS
    ct = bsz * (cap_l + cap_c)
    assert ct % (16 * FFN_ROW_CHUNKS) == 0 and cap_l % 16 == 0 and cap_c % 16 == 0
    rows = ct + WIN

    w_perm, b_perm = _permute_w_in(w_in, b_in[:, None, :])
    w_perm = w_perm.astype(BF16)
    xx = jnp.concatenate([x, ctx], axis=1)
    att_r = next(r for r in (11, 3, 1) if ltiles % r == 0)
    tq = min(2048, n_lat)

    for l in range(depth):
        mods = mods_all[l].reshape(8, 1, 6 * D_MODEL)
        slab, mq, mk = _inproj(xx, mods, norm1_w[l][None, :], w_perm, b_perm, conv_w[l], conv_b[l][None, :], l, nlt)
        hmf, hmb, hrf, hrb = _scans(mq, mk, slab, rope_r, dmat, xi, kd, cd, nl, ncx)

        wq = mla_wq_b[l].reshape(Q_LORA, MLA_HEADS, QK_D)
        wqt = jnp.pad(wq, ((0, 0), (0, 0), (0, LANES - QK_D))).reshape(Q_LORA, MLA_HEADS * LANES).T
        wkv = mla_wkv_b[l].reshape(KV_LORA, MLA_HEADS, NOPE_D + MLA_DV)
        wkt = jnp.pad(wkv[:, :, :NOPE_D], ((0, 0), (0, 0), (0, LANES - NOPE_D))).reshape(KV_LORA, MLA_HEADS * LANES).T
        wvt = wkv[:, :, NOPE_D:].reshape(KV_LORA, MLA_HEADS * MLA_DV).T
        qnw = jnp.pad(q_norm_w[l], (0, LANES - QK_D))[:, None]
        knw = jnp.pad(k_norm_w[l], (0, LANES - QK_D))[:, None]
        qt, kk, vt = _mla(slab, mla_qa_norm[l][None, :], wqt, mla_kva_norm[l][None, :], wkt, wvt, qnw, knw, rope_q)
        bound = (1.01 * QK_D ** 0.5 * LOG2E) * jnp.max(jnp.abs(q_norm_w[l])) * jnp.max(jnp.abs(k_norm_w[l]))
        flags = (bound <= ATT_BOUND_MAX).astype(jnp.int32).reshape(1)
        ub = jnp.full((8, LANES), bound, F32)
        yb_l = _attn(flags, qt, kk, vt, ub, tq, 0, n_lat // tq, 0, att_r, ltiles // att_r)
        yb_c = _attn(flags, qt, kk, vt, ub, TM, nlt, 1, nlt, 1, 1)

        rw = jnp.stack(_split2(jnp.pad(router_w[l], ((0, 0), (0, LANES - N_EXPERTS)))))
        xm, h2, aff = _merge(hmf, hmb, slab, ml_norm_w[l][None, :], yb_l, yb_c, hrf, hrb, ret_norm_w[l][None, :],
                             w_branch[l],
                             w_out[l], xx, mods, norm2_w[l][None, :], rw, g8, g8t, nlt)

        pos_l, post_l, gwt_l, off_l = _select(aff, n_lat, cap_l, 0, 0, cap_l + cap_c)
        pos_c, post_c, gwt_c, off_c = _select(aff, n_ctx, cap_c, nlt, cap_l, cap_l + cap_c)
        pos = jnp.concatenate([pos_l, pos_c], axis=1).reshape(bsz * ltot, LANES)
        expert_major = lambda a_l, a_c: jnp.swapaxes(jnp.concatenate([a_l, a_c], axis=2), 0, 1).reshape(
            N_EXPERTS, 1, bsz * ltot)
        post = expert_major(post_l, post_c)
        gwt = expert_major(gwt_l, gwt_c)
        base_l = (jnp.arange(bsz) * (cap_l + cap_c))[:, None, None]
        base_c = base_l + cap_l
        off = jnp.concatenate(
            [off_l[:, :, :2 * nlt].astype(jnp.int32) // SLOT_ALIGN * SLOT_ALIGN + base_l,
             off_c[:, :, :2].astype(jnp.int32) // SLOT_ALIGN * SLOT_ALIGN + base_c], axis=2)
        starts_c = jnp.swapaxes(off, 0, 1).reshape(-1)
        starts_g = jnp.swapaxes(off[:, :, ::2], 0, 1).reshape(-1)

        xs, gs = _moe_gather(starts_g, h2.reshape(bsz * ltot, D_MODEL), post, gwt, rows, gather_tiles)
        ys = _moe_ffn(xs, gs, exp_w1, exp_w3, exp_w2, l, ct)
        last = l == depth - 1
        xx = _combine(starts_c, xm.reshape(bsz * ltot, D_MODEL), pos, mods, ys, nlt, ltiles, last)
        xx = xx.reshape(bsz, n_lat if last else ltot, D_MODEL)

    return xx
```

```python
import functools

import jax
import jax.numpy as jnp
import numpy as np
from jax import lax
from jax.experimental import pallas as pl
from jax.experimental.pallas import tpu as pltpu

F32 = jnp.float32
BF16 = jnp.bfloat16
HIGHEST = lax.Precision.HIGHEST

D_MODEL = 1024
GRID_W = 64
N_BRANCH = 3
BRANCH_W = 512
ML_HEADS = 8
ML_DH = 64
CONV_K = 5
MLA_HEADS = 8
Q_LORA = 384
KV_LORA = 256
NOPE_D = 64
ROPE_D = 32
QK_D = NOPE_D + ROPE_D
MLA_DV = 64
RET_HEADS = 8
RET_DK = 32
RET_DV = 64
N_EXPERTS = 16
EXPERT_FF = 1024
CAP_FACTOR = 2
CHUNK = 128
ROPE_BASE = 10000.0
EPS = 1e-6
LOG2E = 1.4426950408889634

LANES = 128
TM = 256
ROW_SUB = 128
VMEM_LIMIT = 56 * 1024 * 1024

OFF_BL, OFF_V, OFF_OG, OFF_RV, OFF_RG = 0, 3072, 3584, 4096, 4608
OFF_RQ, OFF_RK, OFF_KV, OFF_QL, OFF_MISC = 5120, 5376, 5632, 6144, 6528
SLAB_W = 6656
NCOL = SLAB_W + 2 * ML_HEADS * ML_DH
TN_IN = SLAB_W // 2


def _sigmoid(x):
    return 1.0 / (1.0 + jnp.exp(-x))


def _silu(x):
    return x * _sigmoid(x)


def _log_sigmoid(x):
    return jnp.minimum(x, 0.0) - jnp.log1p(jnp.exp(-jnp.abs(x)))


def _dot(a, b, **kw):
    return jnp.dot(a, b, preferred_element_type=F32, **kw)


def _dot_nt(a, b, **kw):
    return lax.dot_general(a, b, (((1,), (1,)), ((), ())), preferred_element_type=F32, **kw)


def _dot_tn(a, b, **kw):
    return lax.dot_general(a, b, (((0,), (0,)), ((), ())), preferred_element_type=F32, **kw)


def _params(sem):
    return pltpu.CompilerParams(dimension_semantics=sem, vmem_limit_bytes=VMEM_LIMIT)


def _ada_kernel(c_ref, w_ref, b_ref, o_ref):
    s_hi, s_lo = _split2(_silu(c_ref[...]))
    w_hi, w_lo = _split2(w_ref[...])
    o_ref[...] = _dot(s_hi, w_hi) + _dot(s_hi, w_lo) + _dot(s_lo, w_hi) + b_ref[...]


def _ada(cond8, ada_w, ada_b):
    depth = ada_w.shape[0]
    tn = 1536
    return pl.pallas_call(
        _ada_kernel,
        grid=(depth, 6 * D_MODEL // tn),
        in_specs=[pl.BlockSpec((8, D_MODEL), lambda l, j: (0, 0)),
                  pl.BlockSpec((None, D_MODEL, tn), lambda l, j: (l, 0, j)),
                  pl.BlockSpec((None, 1, tn), lambda l, j: (l, 0, j))],
        out_specs=pl.BlockSpec((None, 8, tn), lambda l, j: (l, 0, j)),
        out_shape=jax.ShapeDtypeStruct((depth, 8, 6 * D_MODEL), F32),
        compiler_params=_params(("arbitrary", "arbitrary")),
        name="ada",
    )(cond8, ada_w, ada_b.reshape(depth, 1, 6 * D_MODEL))


def _inproj_kernel(x_ref, xp_ref, xn_ref, mod_ref, nw_ref, w_ref, b_ref, cw_ref, cb_ref, o_ref, q_ref, k_ref, xe_ref,
                   *, nlt, ltiles):
    i = pl.program_id(1)
    first = (i == 0) | (i == nlt)
    last = (i == nlt - 1) | (i == ltiles - 1)
    sh = mod_ref[:, 0:D_MODEL]
    sc = mod_ref[:, D_MODEL:2 * D_MODEL]
    hw = ML_HEADS * ML_DH

    def hidden(x):
        hn = x * lax.rsqrt(jnp.mean(x * x, axis=-1, keepdims=True) + EPS) * nw_ref[...]
        return (hn * (1.0 + sc) + sh).astype(BF16)

    def qk_cols(h):
        return _dot(h, w_ref[:, SLAB_W:NCOL]) + b_ref[:, SLAB_W:NCOL]

    subs = list(range(0, TM, ROW_SUB))
    halo = qk_cols(hidden(jnp.concatenate([xp_ref[...], xn_ref[...]], axis=0)))
    xe_ref[0:8, :] = jnp.where(first, 0.0, halo[0:8])
    xe_ref[8 + TM:16 + TM, :] = jnp.where(last, 0.0, halo[8:16])
    hs = [hidden(x_ref[r0:r0 + ROW_SUB, :]) for r0 in subs]
    for r0, h in zip(subs, hs):
        xe_ref[8 + r0:8 + r0 + ROW_SUB, :] = qk_cols(h)
    for r0, h in zip(subs, hs):
        for c0 in range(0, SLAB_W, TN_IN):
            o_ref[r0:r0 + ROW_SUB, c0:c0 + TN_IN] = _dot(h, w_ref[:, c0:c0 + TN_IN]) + b_ref[:, c0:c0 + TN_IN]
    acc = jnp.broadcast_to(cb_ref[...], (TM, 2 * hw))
    for j in range(CONV_K):
        acc = acc + xe_ref[8 - CONV_K // 2 + j:8 - CONV_K // 2 + j + TM, :] * cw_ref[j:j + 1, :]
    qk = _silu(acc)
    q_ref[...] = qk[:, :hw].astype(BF16)
    k_ref[...] = (qk[:, hw:] * (ML_DH ** -0.5)).astype(BF16)


def _inproj(x, mods, nw, w_all, b_all, conv_w, conv_b, layer, nlt):
    bsz, ltot, _ = x.shape
    hw = ML_HEADS * ML_DH
    ltiles = ltot // TM
    nrow8 = ltot // 8

    def mrow(bb, i):
        return (jnp.where(i < nlt, bb, bsz), 0, 0)

    qk_out = pl.BlockSpec((None, TM, hw), lambda bb, i: (bb, i, 0))
    return pl.pallas_call(
        functools.partial(_inproj_kernel, nlt=nlt, ltiles=ltiles),
        grid=(bsz, ltiles),
        in_specs=[pl.BlockSpec((None, TM, D_MODEL), lambda bb, i: (bb, i, 0)),
                  pl.BlockSpec((None, 8, D_MODEL), lambda bb, i: (bb, jnp.maximum(i * (TM // 8) - 1, 0), 0)),
                  pl.BlockSpec((None, 8, D_MODEL), lambda bb, i: (bb, jnp.minimum((i + 1) * (TM // 8), nrow8 - 1), 0)),
                  pl.BlockSpec((None, 1, 6 * D_MODEL), mrow),
                  pl.BlockSpec((1, D_MODEL), lambda bb, i: (0, 0)),
                  pl.BlockSpec((None, D_MODEL, NCOL), lambda bb, i: (layer, 0, 0)),
                  pl.BlockSpec((None, 1, NCOL), lambda bb, i: (layer, 0, 0)),
                  pl.BlockSpec((CONV_K, 2 * hw), lambda bb, i: (0, 0)),
                  pl.BlockSpec((1, 2 * hw), lambda bb, i: (0, 0))],
        out_specs=[pl.BlockSpec((None, TM, SLAB_W), lambda bb, i: (bb, i, 0)), qk_out, qk_out],
        out_shape=[jax.ShapeDtypeStruct((bsz, ltot, SLAB_W), F32),
                   jax.ShapeDtypeStruct((bsz, ltot, hw), BF16), jax.ShapeDtypeStruct((bsz, ltot, hw), BF16)],
        scratch_shapes=[pltpu.VMEM((TM + 16, 2 * hw), F32)],
        compiler_params=_params(("arbitrary", "arbitrary")),
        name="inproj",
    )(x, x, x, mods, nw, w_all, b_all, conv_w, conv_b)


def _chunk_of(d, s, nl, ncx):
    fwd = jnp.where(s < ncx, nl + s, s - ncx)
    bwd = jnp.where(s < ncx, nl + ncx - 1 - s, nl - 1 - (s - ncx))
    return jnp.where(d == 0, fwd, bwd)


def _lane_iota(shape):
    return lax.broadcasted_iota(jnp.int32, shape, len(shape) - 1)


def _split2(x):
    hi = x.astype(BF16)
    return hi, (x - hi.astype(F32)).astype(BF16)


def _split3(x):
    hi = x.astype(BF16)
    r1 = x - hi.astype(F32)
    mid = r1.astype(BF16)
    return hi, mid, (r1 - mid.astype(F32)).astype(BF16)


def _select_dot(x, sel_b):
    hi, lo = _split2(x)
    return _dot(hi, sel_b) + _dot(lo, sel_b)


GATE_LANE0 = ML_HEADS


def _mlstm2_kernel(qf_ref, kf_ref, vf_ref, gf_ref, qb_ref, kb_ref, vb_ref, gb_ref, of_ref, ob_ref, st_ref, m_ref):
    s = pl.program_id(1)

    @pl.when(s == 0)
    def _():
        st_ref[...] = jnp.zeros_like(st_ref)
        m_ref[...] = jnp.zeros_like(m_ref)

    dirs = (0, 1)
    pairs = range(ML_HEADS // 2)
    heads = range(ML_HEADS)
    q_refs, k_refs, v_refs, g_refs, o_refs = (qf_ref, qb_ref), (kf_ref, kb_ref), (vf_ref, vb_ref), (gf_ref, gb_ref), \
        (of_ref, ob_ref)
    lane = _lane_iota((1, LANES))
    head_lane = (lane >= GATE_LANE0) & (lane < GATE_LANE0 + ML_HEADS)
    lo_half = lane < ML_DH
    ti = lax.broadcasted_iota(jnp.int32, (CHUNK, CHUNK), 0)
    si = lax.broadcasted_iota(jnp.int32, (CHUNK, CHUNK), 1)
    causal = (ti >= si, ti <= si)
    tri = [jnp.where(causal[d], 1.0, 0.0).astype(BF16) for d in dirs]
    tri_t = [jnp.where(causal[1 - d], 1.0, 0.0).astype(BF16) for d in dirs]
    s8 = _lane_iota((ML_HEADS, CHUNK))
    gi = lax.broadcasted_iota(jnp.int32, (LANES, 2 * LANES), 0)
    li = lax.broadcasted_iota(jnp.int32, (LANES, 2 * LANES), 1)
    li_in = jnp.where(li >= LANES, li - LANES, li)
    same_head = (gi < ML_DH) == (li_in < ML_DH)
    lane2 = _lane_iota((1, 2 * LANES))
    lo_half2 = jnp.where(lane2 >= LANES, lane2 - LANES, lane2) < ML_DH
    hmask = [lo_half if h % 2 == 0 else jnp.logical_not(lo_half) for h in heads]
    ones_b = jnp.ones((CHUNK, LANES), BF16)
    sel_full = [jnp.where(gi == GATE_LANE0 + 2 * p + jnp.where(li >= LANES, 1, 0), 1.0, 0.0).astype(BF16)
                for p in pairs]
    quarter = jnp.where(li >= LANES, 2, 0) + jnp.where(li_in >= ML_DH, 1, 0)
    sel_half = [jnp.where(gi == GATE_LANE0 + 4 * q + quarter, 1.0, 0.0).astype(BF16)
                for q in range(ML_HEADS // 4)]
    t16 = lax.broadcasted_iota(jnp.int32, (16, LANES), 0)

    g_ig, ls = [], []
    for d in dirs:
        g_raw = g_refs[d][...]
        g_fg = g_raw if d == 0 else pltpu.roll(g_raw, LANES - 2 * ML_HEADS, axis=1)
        g_ig.append(jnp.where(head_lane, pltpu.roll(g_fg, ML_HEADS, axis=1), 0.0))
        ls.append(jnp.where(head_lane, _log_sigmoid(g_fg), 0.0))
    ls3 = [_split3(ls[d]) for d in dirs]
    lst3 = [_split3(ls[d].T) for d in dirs]
    b_cols = [sum(_dot(tri[d], t) for t in ls3[d]) for d in dirs]
    b_rows = [sum(_dot(t, tri_t[d]) for t in lst3[d]) for d in dirs]
    yield

    qp = [[q_refs[d][:, p * LANES:(p + 1) * LANES] for p in pairs] for d in dirs]
    kp = [[k_refs[d][:, p * LANES:(p + 1) * LANES] for p in pairs] for d in dirs]
    va = [[jnp.concatenate([v_refs[d][:, p * LANES:(p + 1) * LANES].astype(BF16), ones_b], axis=1) for p in pairs]
          for d in dirs]
    st = [[st_ref[d, p] for p in pairs] for d in dirs]
    qs = [[_dot(qp[d][p], st[d][p].astype(BF16)) for p in pairs] for d in dirs]
    qk = [[_dot_nt(jnp.where(hmask[h], qp[d][h // 2].astype(F32), 0.0).astype(BF16), kp[d][h // 2])
           for h in heads] for d in dirs]
    yield

    r_rows, cdiff, stacked = [], [], []
    for d in dirs:
        r_rows.append(g_ig[d].T - b_rows[d])
        bl_row = b_cols[d][CHUNK - 1:CHUNK, :] if d == 0 else b_cols[d][0:1, :]
        m_row = m_ref[d]
        cm8 = r_rows[d][GATE_LANE0:GATE_LANE0 + ML_HEADS, :]
        order = s8 if d == 0 else CHUNK - 1 - s8
        step = 1
        while step < CHUNK:
            shifted = pltpu.roll(cm8, step if d == 0 else CHUNK - step, axis=1)
            cm8 = jnp.where(order >= step, jnp.maximum(cm8, shifted), cm8)
            step *= 2
        cm = jnp.concatenate([jnp.zeros((GATE_LANE0, CHUNK), F32), cm8,
                              jnp.zeros((LANES - GATE_LANE0 - ML_HEADS, CHUNK), F32)], axis=0).T
        a_cols = b_cols[d] + m_row
        mt_cols = jnp.maximum(a_cols, b_cols[d] + cm)
        g_cols = bl_row - b_cols[d] + g_ig[d]
        m_new = jnp.maximum(bl_row + m_row, jnp.max(g_cols, axis=0, keepdims=True))
        dp_cols = jnp.where(t16 >= 0, jnp.exp(bl_row + m_row - m_new), 0.0)
        m_ref[d] = m_new
        cdiff.append(b_cols[d] - mt_cols)
        stacked.append(jnp.concatenate([jnp.exp(a_cols - mt_cols), jnp.exp(-mt_cols),
                                        jnp.exp(g_cols - m_new), dp_cols], axis=0))

    c_full = [[_select_dot(cdiff[d], sel_full[p]) for p in pairs] for d in dirs]
    halves2 = [[_select_dot(stacked[d], sel_half[q]) for q in range(ML_HEADS // 4)] for d in dirs]
    halves = [[halves2[d][p // 2][:, (p % 2) * LANES:(p % 2 + 1) * LANES] for p in pairs]
              for d in dirs]
    yield

    sc = [[None] * ML_HEADS for _ in dirs]
    for d in dirs:
        for h in heads:
            p, j = h // 2, h % 2
            dlog = c_full[d][p][:, j * LANES:(j + 1) * LANES] + r_rows[d][GATE_LANE0 + h:GATE_LANE0 + h + 1, :]
            sc[d][h] = (qk[d][h] * jnp.exp(jnp.where(causal[d], dlog, -jnp.inf))).astype(BF16)
    sv = [[_dot(sc[d][h], va[d][h // 2]) for h in heads] for d in dirs]
    upd = [[_dot_tn((kp[d][p].astype(F32) * halves[d][p][2 * CHUNK:3 * CHUNK]).astype(BF16), va[d][p])
            for p in pairs] for d in dirs]
    yield

    for d in dirs:
        outs = []
        for p in pairs:
            inter_pair = halves[d][p][0:CHUNK]
            emt_pair = halves[d][p][CHUNK:2 * CHUNK]
            dp_pair = halves[d][p][3 * CHUNK:3 * CHUNK + 1]
            sv_pair = jnp.where(lo_half2, sv[d][2 * p], sv[d][2 * p + 1])
            num = sv_pair[:, :LANES] + inter_pair * qs[d][p][:, :LANES]
            den = sv_pair[:, LANES:] + inter_pair * qs[d][p][:, LANES:]
            outs.append(num / jnp.maximum(jnp.abs(den), emt_pair))
            st_ref[d, p] = (jnp.concatenate([dp_pair, dp_pair], axis=1) * st[d][p]
                            + jnp.where(same_head, upd[d][p], 0.0))
        o_refs[d][...] = jnp.concatenate(outs, axis=1)


N_MLSTM_IN, N_RET_IN = 8, 12


def _scan_kernel(*refs):
    n_in = N_MLSTM_IN + N_RET_IN
    ml_in, rt_in = refs[:N_MLSTM_IN], refs[N_MLSTM_IN:n_in]
    mo_f, mo_b, ro_f, ro_b, ml_st, ml_m, rt_st = refs[n_in:]
    bodies = [_mlstm2_kernel(*ml_in, mo_f, mo_b, ml_st, ml_m), _ret_kernel(*rt_in, ro_f, ro_b, rt_st)]
    while bodies:
        for body in list(bodies):
            if next(body, "done") == "done":
                bodies.remove(body)


def _scans(q, k, slab, rope_r, dmat, xi, kd, cd, nl, ncx):
    bsz, ltot, _ = slab.shape
    hw = ML_HEADS * ML_DH
    qw = RET_HEADS * RET_DK
    vw = RET_HEADS * RET_DV

    def ml_specs(d):
        ch = lambda bb, s: _chunk_of(d, s, nl, ncx)
        return [pl.BlockSpec((None, CHUNK, hw), lambda bb, s: (bb, ch(bb, s), 0)),
                pl.BlockSpec((None, CHUNK, hw), lambda bb, s: (bb, ch(bb, s), 0)),
                pl.BlockSpec((None, CHUNK, hw), lambda bb, s: (bb, ch(bb, s), OFF_V // hw)),
                pl.BlockSpec((None, CHUNK, LANES), lambda bb, s: (bb, ch(bb, s), OFF_MISC // LANES))]

    def rt_specs(d):
        ch = lambda bb, s: _chunk_of(d, s, nl, ncx)
        return [pl.BlockSpec((None, CHUNK, qw), lambda bb, s: (bb, ch(bb, s), OFF_RQ // qw)),
                pl.BlockSpec((None, CHUNK, qw), lambda bb, s: (bb, ch(bb, s), OFF_RK // qw)),
                pl.BlockSpec((None, CHUNK, vw), lambda bb, s: (bb, ch(bb, s), OFF_RV // vw)),
                pl.BlockSpec((3, CHUNK, LANES), lambda bb, s: (0, ch(bb, s), 0))]

    full = lambda a: pl.BlockSpec(a.shape, lambda bb, s: (0,) * a.ndim)
    out = lambda d, w: pl.BlockSpec((None, CHUNK, w), lambda bb, s: (bb, _chunk_of(d, s, nl, ncx), 0))
    return pl.pallas_call(
        _scan_kernel,
        grid=(bsz, nl + ncx),
        in_specs=(ml_specs(0) + ml_specs(1) + rt_specs(0) + rt_specs(1)
                  + [full(dmat), full(xi), full(kd), full(cd)]),
        out_specs=[out(0, hw), out(1, hw), out(0, vw), out(1, vw)],
        out_shape=[jax.ShapeDtypeStruct((bsz, ltot, hw), F32)] * 2 + [jax.ShapeDtypeStruct((bsz, ltot, vw), F32)] * 2,
        scratch_shapes=[pltpu.VMEM((2, ML_HEADS // 2, LANES, 2 * LANES), F32),
                        pltpu.VMEM((2, 1, LANES), F32),
                        pltpu.VMEM((2, RET_HEADS // 2, LANES, LANES), F32)],
        compiler_params=_params(("arbitrary", "arbitrary")),
        name="scans",
    )(q, k, slab, slab, q, k, slab, slab, slab, slab, slab, rope_r, slab, slab, slab, rope_r, dmat, xi, kd, cd)


def _rope_lanes(x, tab_ref):
    return (x * tab_ref[0] + pltpu.roll(x, 8, axis=1) * tab_ref[1]
            + pltpu.roll(x, LANES - 8, axis=1) * tab_ref[2])


def _ret_kernel(qf_ref, kf_ref, vf_ref, rf_ref, qb_ref, kb_ref, vb_ref, rb_ref, dmat_ref, xi_ref, kd_ref, cd_ref,
                of_ref, ob_ref, st_ref):
    s = pl.program_id(1)

    @pl.when(s == 0)
    def _():
        st_ref[...] = jnp.zeros_like(st_ref)

    dirs = (0, 1)
    q_refs, k_refs, v_refs, r_refs, o_refs = (qf_ref, qb_ref), (kf_ref, kb_ref), (vf_ref, vb_ref), (rf_ref, rb_ref), \
        (of_ref, ob_ref)
    lane = _lane_iota((1, LANES))
    lo_half = lane < RET_DV
    ri = lax.broadcasted_iota(jnp.int32, (LANES, LANES), 0)
    ci = lax.broadcasted_iota(jnp.int32, (LANES, LANES), 1)
    heads = range(RET_HEADS)
    pairs = range(RET_HEADS // 2)
    qg = [[_rope_lanes(q_refs[d][:, g * LANES:(g + 1) * LANES], r_refs[d]) for g in range(2)] for d in dirs]
    kg = [[_rope_lanes(k_refs[d][:, g * LANES:(g + 1) * LANES], r_refs[d]) * (RET_DK ** -0.5) for g in range(2)]
          for d in dirs]
    qg_b = [[x.astype(BF16) for x in qg[d]] for d in dirs]
    kg_b = [[x.astype(BF16) for x in kg[d]] for d in dirs]
    kw_b = [[(kg[d][g] * kd_ref[d, :, g * LANES:(g + 1) * LANES]).astype(BF16) for g in range(2)] for d in dirs]
    vp_b = [[v_refs[d][:, p * LANES:(p + 1) * LANES].astype(BF16) for p in pairs] for d in dirs]
    st = [[st_ref[d, p] for p in pairs] for d in dirs]

    cross = [[_dot(qg_b[d][p // 2], st[d][p].astype(BF16)) for p in pairs] for d in dirs]
    qk = [[None] * RET_HEADS for _ in dirs]
    for d in dirs:
        for h in heads:
            q_lo = (h % 4) * RET_DK
            hm = (lane >= q_lo) & (lane < q_lo + RET_DK)
            qk[d][h] = _dot_nt(jnp.where(hm, qg[d][h // 4], 0.0).astype(BF16), kg_b[d][h // 4])
    yield
    sv = [[_dot((qk[d][h] * dmat_ref[d, h]).astype(BF16), vp_b[d][h // 2]) for h in heads] for d in dirs]
    upd = [[_dot_tn(kw_b[d][p // 2], vp_b[d][p]) for p in pairs] for d in dirs]
    yield

    for d in dirs:
        outs = []
        for p in pairs:
            a = p % 2
            outs.append(jnp.where(lo_half, sv[d][2 * p], sv[d][2 * p + 1])
                        + xi_ref[d, :, p * LANES:(p + 1) * LANES] * cross[d][p])
            r_lo = (ri >= 2 * a * RET_DK) & (ri < (2 * a + 1) * RET_DK)
            r_hi = (ri >= (2 * a + 1) * RET_DK) & (ri < (2 * a + 2) * RET_DK)
            valid = (r_lo & (ci < RET_DV)) | (r_hi & (ci >= RET_DV))
            st_ref[d, p] = cd_ref[d, :, p * LANES:(p + 1) * LANES] * st[d][p] + jnp.where(valid, upd[d][p], 0.0)
        o_refs[d][...] = jnp.concatenate(outs, axis=1)


def _ret_tables():
    lg = jnp.log1p(-jnp.exp2(-5.0 - jnp.arange(RET_HEADS, dtype=F32)))
    idx = jnp.arange(CHUNK, dtype=F32)
    diff = idx[:, None] - idx[None, :]
    dm, xis, kds, cds = [], [], [], []
    for lgd, sign in ((lg, 1.0), (lg[::-1], -1.0)):
        dd = diff[None] * sign
        dm.append(jnp.exp(jnp.where(dd >= 0, dd * lgd[:, None, None], -jnp.inf)))
        order = idx if sign > 0 else (CHUNK - 1.0 - idx)
        xi = jnp.exp((order + 1.0)[None] * lgd[:, None])
        kdv = jnp.exp((CHUNK - 1.0 - order)[None] * lgd[:, None])
        cdv = jnp.exp(CHUNK * lgd)
        xis.append(jnp.repeat(xi.T, RET_DV, axis=1))
        kds.append(jnp.repeat(kdv.T, RET_DK, axis=1))
        cds.append(jnp.repeat(cdv, RET_DV)[None, :])
    return jnp.stack(dm), jnp.stack(xis), jnp.stack(kds), jnp.stack(cds)


def _rope_tables(n_lat, n_ctx):
    t = jnp.arange(n_lat)
    row = (t // GRID_W).astype(F32)
    col = (t % GRID_W).astype(F32)
    half = ROPE_D // 4
    inv = ROPE_BASE ** (-jnp.arange(half, dtype=F32) / half)
    ar, ac = row[:, None] * inv, col[:, None] * inv
    z = jnp.zeros_like(ar)
    c32 = jnp.concatenate([jnp.cos(ar), jnp.cos(ar), jnp.cos(ac), jnp.cos(ac)], axis=1)
    s1 = jnp.concatenate([z, jnp.sin(ar), z, jnp.sin(ac)], axis=1)
    s2 = jnp.concatenate([-jnp.sin(ar), z, -jnp.sin(ac), z], axis=1)
    tab = jnp.stack([c32, s1, s2])
    ident = jnp.stack([jnp.ones((n_ctx, ROPE_D), F32), jnp.zeros((n_ctx, ROPE_D), F32),
                       jnp.zeros((n_ctx, ROPE_D), F32)])
    return jnp.concatenate([tab, ident], axis=1)


def _mla_kernel(ql_ref, kv_ref, misc_ref, qan_ref, wqt_ref, kvn_ref, wkt_ref, wvt_ref, qnw_ref, knw_ref,
                rope_ref, qt_ref, k_ref, vt_ref):
    def norm_rope(xh, gain_ref):
        ss = jnp.sum(xh * xh, axis=0, keepdims=True)
        xh = xh * lax.rsqrt(ss * (1.0 / QK_D) + EPS) * gain_ref[...]
        return (xh * rope_ref[0] + pltpu.roll(xh, 8, axis=0) * rope_ref[1]
                + pltpu.roll(xh, LANES - 8, axis=0) * rope_ref[2])

    ql = ql_ref[...]
    qn = ql * lax.rsqrt(jnp.mean(ql * ql, axis=-1, keepdims=True) + EPS) * qan_ref[...]
    qt = _dot_nt(wqt_ref[...].astype(BF16), qn.astype(BF16))
    for h in range(MLA_HEADS):
        xh = norm_rope(qt[h * LANES:(h + 1) * LANES, :], qnw_ref)
        qt_ref[h * LANES:(h + 1) * LANES, :] = (xh * (QK_D ** -0.5 * LOG2E)).astype(BF16)

    kv = kv_ref[...]
    kvn = (kv * lax.rsqrt(jnp.mean(kv * kv, axis=-1, keepdims=True) + EPS) * kvn_ref[...]).astype(BF16)
    kt = _dot_nt(wkt_ref[...].astype(BF16), kvn)
    tm = kt.shape[1]
    shared = misc_ref[...].T[4 * ML_HEADS:4 * ML_HEADS + ROPE_D, :]
    kr = jnp.concatenate([jnp.zeros((NOPE_D, tm), F32), shared, jnp.zeros((LANES - QK_D, tm), F32)], axis=0)
    for h in range(MLA_HEADS):
        xh = norm_rope(kt[h * LANES:(h + 1) * LANES, :] + kr, knw_ref)
        k_ref[:, h * LANES:(h + 1) * LANES] = xh.T.astype(BF16)
    vt_ref[...] = _dot_nt(wvt_ref[...].astype(BF16), kvn).astype(BF16)


def _mla(slab, qan, wqt, kvn, wkt, wvt, qnw, knw, rope_q):
    bsz, ltot, _ = slab.shape
    hq = MLA_HEADS * LANES
    hv = MLA_HEADS * MLA_DV
    full = lambda shape: pl.BlockSpec(shape, lambda bb, i: (0,) * len(shape))
    return pl.pallas_call(
        _mla_kernel,
        grid=(bsz, ltot // TM),
        in_specs=[pl.BlockSpec((None, TM, Q_LORA), lambda bb, i: (bb, i, OFF_QL // Q_LORA)),
                  pl.BlockSpec((None, TM, KV_LORA), lambda bb, i: (bb, i, OFF_KV // KV_LORA)),
                  pl.BlockSpec((None, TM, LANES), lambda bb, i: (bb, i, OFF_MISC // LANES)),
                  full((1, Q_LORA)), full((hq, Q_LORA)), full((1, KV_LORA)), full((hq, KV_LORA)),
                  full((hv, KV_LORA)), full((LANES, 1)), full((LANES, 1)),
                  pl.BlockSpec((3, LANES, TM), lambda bb, i: (0, 0, i))],
        out_specs=[pl.BlockSpec((None, hq, TM), lambda bb, i: (bb, 0, i)),
                   pl.BlockSpec((None, TM, hq), lambda bb, i: (bb, i, 0)),
                   pl.BlockSpec((None, None, hv, TM), lambda bb, i: (bb, i, 0, 0))],
        out_shape=[jax.ShapeDtypeStruct((bsz, hq, ltot), BF16),
                   jax.ShapeDtypeStruct((bsz, ltot, hq), BF16),
                   jax.ShapeDtypeStruct((bsz, ltot // TM, hv, TM), BF16)],
        compiler_params=_params(("arbitrary", "arbitrary")),
        name="mla",
    )(slab, slab, slab, qan, wqt, kvn, wkt, wvt, qnw, knw, rope_q)


ATT_QC = 512
ATT_ONES = 16
ATT_AHEAD = 3


ATT_BOUND_MAX = 60.0


def _attn_kernel(flag_ref, qt_ref, k_ref, vt_ref, ub_ref, o_ref, acc_ref, m_ref, l_ref, *, r, nkb):
    tq = qt_ref.shape[1]
    tkb = r * TM
    ones = jnp.ones((ATT_ONES, TM), BF16)
    qc = min(ATT_QC, tq)
    units = [(j, c) for j in range(2) for c in range(tq // qc)]
    ahead = min(ATT_AHEAD, len(units))
    bounded = flag_ref[0] == 1
    ub = ub_ref[0:1, 0:1]

    def scores(r0, u):
        j, c = units[u]
        return _dot(k_ref[pl.ds(r0, tkb), j * LANES:(j + 1) * LANES],
                    qt_ref[j * LANES:(j + 1) * LANES, c * qc:(c + 1) * qc])

    def weighted_values(kb, j, p):
        pb = p.astype(BF16)
        pv = None
        for t in range(r):
            vt = jnp.concatenate([vt_ref[kb * r + t, j * MLA_DV:(j + 1) * MLA_DV, :], ones], axis=0)
            part = _dot(vt, pb[t * TM:(t + 1) * TM])
            pv = part if pv is None else pv + part
        return pv[0:MLA_DV], pv[MLA_DV:MLA_DV + 1]

    def bounded_block(kb, carry):
        r0 = pl.multiple_of(kb * tkb, tkb)
        pending = [scores(r0, u) for u in range(ahead)]
        for u, (j, c) in enumerate(units):
            sc = pending.pop(0)
            if u + ahead < len(units):
                pending.append(scores(r0, u + ahead))
            cs = slice(c * qc, (c + 1) * qc)
            pv, ps = weighted_values(kb, j, jnp.exp2(sc - ub))
            acc_ref[j * MLA_DV:(j + 1) * MLA_DV, cs] += pv
            l_ref[j:j + 1, cs] += ps
        return carry

    def online_block(kb, carry):
        r0 = pl.multiple_of(kb * tkb, tkb)
        pending = [scores(r0, u) for u in range(ahead)]
        for u, (j, c) in enumerate(units):
            sc = pending.pop(0)
            if u + ahead < len(units):
                pending.append(scores(r0, u + ahead))
            cs = slice(c * qc, (c + 1) * qc)
            m_old = m_ref[j:j + 1, cs]
            m_new = jnp.maximum(m_old, jnp.max(sc, axis=0, keepdims=True))
            alpha = jnp.exp2(m_old - m_new)
            pv, ps = weighted_values(kb, j, jnp.exp2(sc - m_new))
            acc_ref[j * MLA_DV:(j + 1) * MLA_DV, cs] = alpha * acc_ref[j * MLA_DV:(j + 1) * MLA_DV, cs] + pv
            l_ref[j:j + 1, cs] = alpha * l_ref[j:j + 1, cs] + ps
            m_ref[j:j + 1, cs] = m_new
        return carry

    acc_ref[...] = jnp.zeros_like(acc_ref)
    l_ref[...] = jnp.zeros_like(l_ref)

    @pl.when(bounded)
    def _():
        lax.fori_loop(0, nkb, bounded_block, 0)

    @pl.when(jnp.logical_not(bounded))
    def _():
        m_ref[...] = jnp.full_like(m_ref, -jnp.inf)
        lax.fori_loop(0, nkb, online_block, 0)

    head_rows = lax.broadcasted_iota(jnp.int32, acc_ref.shape, 0)
    inv = jnp.where(head_rows < MLA_DV, 1.0 / l_ref[0:1, :], 1.0 / l_ref[1:2, :])
    o_ref[...] = (acc_ref[...] * inv).T


def _attn(flags, qt, k, vt, ub, tq, q_off, nq, key_tile0, r, nkb):
    bsz = qt.shape[0]
    hv = vt.shape[2]
    nkt = r * nkb
    assert key_tile0 % nkt == 0
    kern = functools.partial(_attn_kernel, r=r, nkb=nkb)
    return pl.pallas_call(
        kern,
        grid_spec=pltpu.PrefetchScalarGridSpec(
            num_scalar_prefetch=1,
            grid=(bsz, MLA_HEADS // 2, nq),
            in_specs=[pl.BlockSpec((None, 2 * LANES, tq), lambda bb, hp, qi, fl: (bb, hp, qi + q_off)),
                      pl.BlockSpec((None, nkt * TM, 2 * LANES), lambda bb, hp, qi, fl: (bb, key_tile0 // nkt, hp)),
                      pl.BlockSpec((None, nkt, 2 * MLA_DV, TM),
                                   lambda bb, hp, qi, fl: (bb, key_tile0 // nkt, hp, 0)),
                      pl.BlockSpec((8, LANES), lambda bb, hp, qi, fl: (0, 0))],
            out_specs=pl.BlockSpec((None, tq, 2 * MLA_DV), lambda bb, hp, qi, fl: (bb, qi, hp)),
            scratch_shapes=[pltpu.VMEM((2 * MLA_DV, tq), F32), pltpu.VMEM((8, tq), F32),
                            pltpu.VMEM((8, tq), F32)]),
        out_shape=jax.ShapeDtypeStruct((bsz, nq * tq, hv), F32),
        compiler_params=_params(("arbitrary", "arbitrary", "arbitrary")),
        name="attn",
    )(flags, qt, k, vt, ub)


def _split_dot(x, w_b):
    hi = x.astype(BF16)
    lo = (x - hi.astype(F32)).astype(BF16)
    return _dot(hi, w_b) + _dot(lo, w_b)


def _head_ln(x, g_ref, gt_ref, width):
    mu = _split_dot(_split_dot(x, g_ref[...]) * (1.0 / width), gt_ref[...])
    xc = x - mu
    var = _split_dot(_split_dot(xc * xc, g_ref[...]) * (1.0 / width), gt_ref[...])
    return xc * lax.rsqrt(var + EPS)


def _merge_kernel(hmf_ref, hmb_ref, og_ref, mlw_ref, ybl_ref, ybc_ref, hrf_ref, hrb_ref, rg_ref, rnw_ref, bl_ref,
                  wbr_ref, wout_ref, x_ref, mod_ref, n2w_ref, rw_ref, g_ref, gt_ref,
                  xo_ref, h2_ref, aff_ref, wbr_b, wout_b, *, nlt):
    @pl.when((pl.program_id(0) == 0) & (pl.program_id(1) == 0))
    def _():
        wbr_b[...] = wbr_ref[...].astype(BF16)
        wout_b[...] = wout_ref[...].astype(BF16)

    is_ctx = pl.program_id(1) >= nlt

    subs = [slice(r0, r0 + ROW_SUB) for r0 in range(0, TM, ROW_SUB)]
    ys = []
    for rs in subs:
        ya = _sigmoid(og_ref[rs, :]) * (_head_ln(hmf_ref[rs, :] + hmb_ref[rs, :], g_ref, gt_ref, ML_DH)
                                        * mlw_ref[...])
        yc = _silu(rg_ref[rs, :]) * (_head_ln(hrf_ref[rs, :] + hrb_ref[rs, :], g_ref, gt_ref, RET_DV)
                                     * rnw_ref[...])
        yb = jnp.where(is_ctx, ybc_ref[rs, :], ybl_ref[rs, :])
        ys.append((ya.astype(BF16), yb.astype(BF16), yc.astype(BF16)))
    br = [[_dot(ys[i][n], wbr_b[n]) for n in range(N_BRANCH)] for i in range(len(subs))]
    merged = []
    for i, rs in enumerate(subs):
        m = None
        for n in range(N_BRANCH):
            term = _sigmoid(bl_ref[rs, n * D_MODEL:(n + 1) * D_MODEL]) * br[i][n]
            m = term if m is None else m + term
        merged.append(m.astype(BF16))
    y = [_dot(m, wout_b[...]) for m in merged]
    g1 = mod_ref[:, 2 * D_MODEL:3 * D_MODEL]
    h2s = []
    for i, rs in enumerate(subs):
        x = x_ref[rs, :] + g1 * y[i]
        xo_ref[rs, :] = x
        hn = x * lax.rsqrt(jnp.mean(x * x, axis=-1, keepdims=True) + EPS) * n2w_ref[...]
        h2 = hn * (1.0 + mod_ref[:, 4 * D_MODEL:5 * D_MODEL]) + mod_ref[:, 3 * D_MODEL:4 * D_MODEL]
        h2_ref[rs, :] = h2.astype(BF16)
        h2s.append(h2)
    logits = []
    for h2 in h2s:
        hi, lo = _split2(h2)
        logits.append(_dot(hi, rw_ref[0]) + _dot(hi, rw_ref[1]) + _dot(lo, rw_ref[0]))
    valid = _lane_iota((1, LANES)) < N_EXPERTS
    for i, rs in enumerate(subs):
        lg = jnp.where(valid, logits[i], -jnp.inf)
        e = jnp.exp(lg - jnp.max(lg, axis=-1, keepdims=True))
        aff_ref[:, rs] = (e / jnp.sum(e, axis=-1, keepdims=True)).T[0:N_EXPERTS, :]


def _merge(hmf, hmb, slab, mlw, yb_l, yb_c, hrf, hrb, rnw, wbr, wout, x, mods, n2w, rw, g8, g8t, nlt):
    bsz, ltot, _ = x.shape
    hw = ML_HEADS * ML_DH

    def mrow(bb, i):
        return (jnp.where(i < nlt, bb, bsz), 0, 0)

    full = lambda shape: pl.BlockSpec(shape, lambda bb, i: (0,) * len(shape))
    row = lambda width, off: pl.BlockSpec((None, TM, width), lambda bb, i: (bb, i, off // width))
    return pl.pallas_call(
        functools.partial(_merge_kernel, nlt=nlt),
        grid=(bsz, ltot // TM),
        in_specs=[row(hw, 0), row(hw, 0),
                  row(hw, OFF_OG), full((1, hw)),
                  pl.BlockSpec((None, TM, hw), lambda bb, i: (bb, jnp.minimum(i, nlt - 1), 0)),
                  pl.BlockSpec((None, TM, hw), lambda bb, i: (bb, 0, 0)),
                  row(hw, 0), row(hw, 0),
                  row(hw, OFF_RG), full((1, hw)),
                  row(N_BRANCH * D_MODEL, OFF_BL),
                  full((N_BRANCH, BRANCH_W, D_MODEL)), full((D_MODEL, D_MODEL)),
                  row(D_MODEL, 0),
                  pl.BlockSpec((None, 1, 6 * D_MODEL), mrow),
                  full((1, D_MODEL)), full((2, D_MODEL, LANES)), full((hw, LANES)), full((LANES, hw))],
        out_specs=[row(D_MODEL, 0), row(D_MODEL, 0), pl.BlockSpec((None, N_EXPERTS, TM), lambda bb, i: (bb, 0, i))],
        out_shape=[jax.ShapeDtypeStruct((bsz, ltot, D_MODEL), F32),
                   jax.ShapeDtypeStruct((bsz, ltot, D_MODEL), BF16),
                   jax.ShapeDtypeStruct((bsz, N_EXPERTS, ltot), F32)],
        scratch_shapes=[pltpu.VMEM((N_BRANCH, BRANCH_W, D_MODEL), BF16), pltpu.VMEM((D_MODEL, D_MODEL), BF16)],
        compiler_params=_params(("arbitrary", "arbitrary")),
        name="merge",
    )(hmf, hmb, slab, mlw, yb_l, yb_c, hrf, hrb, slab, rnw, slab, wbr, wout, x, mods, n2w, rw, g8, g8t)


def _select_kernel(aff_ref, pos_ref, post_ref, gwt_ref, off_ref, *, n, cap, base0, base_step):
    b = pl.program_id(0)
    base = (base0 + b * base_step).astype(F32)
    bits = lax.bitcast_convert_type(aff_ref[...], jnp.int32)
    capf = jnp.float32(cap)

    def search(it, cur):
        cand = cur | jnp.left_shift(jnp.int32(1), 30 - it)
        cnt = jnp.sum(jnp.where(bits >= cand, 1.0, 0.0), axis=1, keepdims=True)
        return jnp.where(cnt >= capf, cand, cur)

    thr = lax.fori_loop(0, 31, search, jnp.zeros((N_EXPERTS, 1), jnp.int32))
    n_gt = jnp.sum(jnp.where(bits > thr, 1.0, 0.0), axis=1, keepdims=True)
    need = capf - n_gt

    ri = lax.broadcasted_iota(jnp.int32, (TM, TM), 0)
    ci = lax.broadcasted_iota(jnp.int32, (TM, TM), 1)
    before = (ri < ci).astype(BF16)
    half_lane = _lane_iota((N_EXPERTS, LANES))
    unused = jnp.full((LANES - N_EXPERTS, TM), -1.0, F32)

    def tile(i, carry):
        c_eq, c_sel, offs = carry
        r0 = pl.multiple_of(i * TM, TM)
        a = aff_ref[:, pl.ds(r0, TM)]
        bt = lax.bitcast_convert_type(a, jnp.int32)
        eq = bt == thr
        rank = _dot(jnp.where(eq, 1.0, 0.0).astype(BF16), before) + c_eq
        sel = (bt > thr) | (eq & (rank < need))
        self_ = jnp.where(sel, 1.0, 0.0)
        pos = _dot(self_.astype(BF16), before) + c_sel
        posv = jnp.where(sel, pos + base, -1.0)
        post_ref[:, pl.ds(r0, TM)] = posv
        gwt_ref[:, pl.ds(r0, TM)] = jnp.where(sel, a, 0.0)
        pos_ref[pl.ds(r0, TM), :] = jnp.concatenate([posv, unused], axis=0).T
        c_half = c_sel + jnp.sum(self_[:, 0:TM // 2], axis=1, keepdims=True)
        offs = jnp.where(half_lane == 2 * i, c_sel, offs)
        offs = jnp.where(half_lane == 2 * i + 1, c_half, offs)
        return (c_eq + jnp.sum(jnp.where(eq, 1.0, 0.0), axis=1, keepdims=True),
                c_sel + jnp.sum(self_, axis=1, keepdims=True), offs)

    zero = jnp.zeros((N_EXPERTS, 1), F32)
    _, _, offs = lax.fori_loop(0, n // TM, tile, (zero, zero, jnp.zeros((N_EXPERTS, LANES), F32)))
    off_ref[...] = offs


def _select(afft, n, cap, tile_off, base0, base_step):
    bsz = afft.shape[0]
    assert 2 * (n // TM) <= LANES
    kern = functools.partial(_select_kernel, n=n, cap=cap, base0=base0, base_step=base_step)
    return pl.pallas_call(
        kern,
        grid=(bsz,),
        in_specs=[pl.BlockSpec((None, N_EXPERTS, n), lambda bb: (bb, 0, tile_off))],
        out_specs=[pl.BlockSpec((None, n, LANES), lambda bb: (bb, 0, 0)),
                   pl.BlockSpec((None, N_EXPERTS, n), lambda bb: (bb, 0, 0)),
                   pl.BlockSpec((None, N_EXPERTS, n), lambda bb: (bb, 0, 0)),
                   pl.BlockSpec((None, N_EXPERTS, LANES), lambda bb: (bb, 0, 0))],
        out_shape=[jax.ShapeDtypeStruct((bsz, n, LANES), F32),
                   jax.ShapeDtypeStruct((bsz, N_EXPERTS, n), F32),
                   jax.ShapeDtypeStruct((bsz, N_EXPERTS, n), F32),
                   jax.ShapeDtypeStruct((bsz, N_EXPERTS, LANES), F32)],
        compiler_params=_params(("arbitrary",)),
        name="select",
    )(afft)


SLOT_ALIGN = 16
WIN = TM + SLOT_ALIGN
TC = TM // 2
WINC = TC + SLOT_ALIGN
GATHER_MAX_TILES = 11
FFN_ROW_CHUNKS = 4
FFN_TF = 512


GATHER_EXPERTS = 2


def _moe_gather_kernel(st_ref, h_ref, pt_ref, gt_ref, xs_ref, gs_ref, *, nt256, tiles):
    e0 = pl.program_id(0) * GATHER_EXPERTS
    i = pl.program_id(1)

    @pl.when(i == 0)
    def _():
        xs_ref[...] = jnp.zeros_like(xs_ref)
        gs_ref[...] = jnp.zeros_like(gs_ref)

    slot = lax.broadcasted_iota(jnp.int32, (WIN, TM), 0)
    for u in range(tiles):
        starts = [pl.multiple_of(st_ref[(e0 + x) * nt256 + i * tiles + u], SLOT_ALIGN) for x in range(GATHER_EXPERTS)]
        match = [(slot + starts[x]).astype(F32) == pt_ref[x, :, u * TM:(u + 1) * TM] for x in range(GATHER_EXPERTS)]
        onehot = jnp.concatenate([jnp.where(m, 1.0, 0.0).astype(BF16) for m in match], axis=0)
        got_all = _dot(onehot, h_ref[u * TM:(u + 1) * TM, :])
        for x in range(GATHER_EXPERTS):
            st = starts[x]
            st2 = pl.multiple_of(st + SLOT_ALIGN, SLOT_ALIGN)
            got = got_all[x * WIN:(x + 1) * WIN]
            head = xs_ref[x, pl.ds(st, SLOT_ALIGN), :].astype(F32)
            xs_ref[x, pl.ds(st, SLOT_ALIGN), :] = (head + got[0:SLOT_ALIGN]).astype(BF16)
            xs_ref[x, pl.ds(st2, TM), :] = got[SLOT_ALIGN:WIN].astype(BF16)
            gate = jnp.sum(jnp.where(match[x], gt_ref[x, :, u * TM:(u + 1) * TM], 0.0), axis=1, keepdims=True)
            gate = jnp.broadcast_to(gate, (WIN, LANES))
            gs_ref[x, pl.ds(st, SLOT_ALIGN), :] = gs_ref[x, pl.ds(st, SLOT_ALIGN), :] + gate[0:SLOT_ALIGN]
            gs_ref[x, pl.ds(st2, TM), :] = gate[SLOT_ALIGN:WIN]


def _moe_gather(starts, h2, post, gwt, rows, tiles):
    ttot = h2.shape[0]
    tt = tiles * TM
    kern = functools.partial(_moe_gather_kernel, nt256=ttot // TM, tiles=tiles)
    return pl.pallas_call(
        kern,
        grid_spec=pltpu.PrefetchScalarGridSpec(
            num_scalar_prefetch=1,
            grid=(N_EXPERTS // GATHER_EXPERTS, ttot // tt),
            in_specs=[pl.BlockSpec((tt, D_MODEL), lambda e, i, st: (i, 0)),
                      pl.BlockSpec((GATHER_EXPERTS, 1, tt), lambda e, i, st: (e, 0, i)),
                      pl.BlockSpec((GATHER_EXPERTS, 1, tt), lambda e, i, st: (e, 0, i))],
            out_specs=[pl.BlockSpec((GATHER_EXPERTS, rows, D_MODEL), lambda e, i, st: (e, 0, 0)),
                       pl.BlockSpec((GATHER_EXPERTS, rows, LANES), lambda e, i, st: (e, 0, 0))]),
        out_shape=[jax.ShapeDtypeStruct((N_EXPERTS, rows, D_MODEL), BF16),
                   jax.ShapeDtypeStruct((N_EXPERTS, rows, LANES), F32)],
        compiler_params=_params(("arbitrary", "arbitrary")),
        name="moe_gather",
    )(starts, h2, post, gwt)


def _moe_ffn_kernel(xs_ref, gs_ref, w1_ref, w3_ref, w2_ref, ys_ref, acc_ref, *, ct, rows):
    f = pl.program_id(1)

    @pl.when(f == 0)
    def _():
        acc_ref[...] = jnp.zeros_like(acc_ref)

    w1b = w1_ref[...].astype(BF16)
    w3b = w3_ref[...].astype(BF16)
    w2b = w2_ref[...].astype(BF16)
    rc = ct // FFN_ROW_CHUNKS
    chunks = [slice(r * rc, (r + 1) * rc) for r in range(FFN_ROW_CHUNKS)]
    up = [(_dot(xs_ref[chunks[0], :], w1b), _dot(xs_ref[chunks[0], :], w3b))]
    for r, rs in enumerate(chunks):
        a, b = up[r]
        if r + 1 < len(chunks):
            up.append((_dot(xs_ref[chunks[r + 1], :], w1b), _dot(xs_ref[chunks[r + 1], :], w3b)))
        acc_ref[rs, :] += _dot((_silu(a) * b).astype(BF16), w2b)

    @pl.when(f == pl.num_programs(1) - 1)
    def _():
        gate = gs_ref[...]
        for c in range(D_MODEL // LANES):
            cs = slice(c * LANES, (c + 1) * LANES)
            ys_ref[0:ct, cs] = (acc_ref[:, cs] * gate).astype(BF16)
        ys_ref[ct:rows, :] = jnp.zeros((rows - ct, D_MODEL), BF16)


def _moe_ffn(xs, gs, w1, w3, w2, layer, ct):
    rows = xs.shape[1]
    kern = functools.partial(_moe_ffn_kernel, ct=ct, rows=rows)
    return pl.pallas_call(
        kern,
        grid=(N_EXPERTS, EXPERT_FF // FFN_TF),
        in_specs=[pl.BlockSpec((None, ct, D_MODEL), lambda e, f: (e, 0, 0)),
                  pl.BlockSpec((None, ct, LANES), lambda e, f: (e, 0, 0)),
                  pl.BlockSpec((None, None, D_MODEL, FFN_TF), lambda e, f: (layer, e, 0, f)),
                  pl.BlockSpec((None, None, D_MODEL, FFN_TF), lambda e, f: (layer, e, 0, f)),
                  pl.BlockSpec((None, None, FFN_TF, D_MODEL), lambda e, f: (layer, e, f, 0))],
        out_specs=pl.BlockSpec((None, rows, D_MODEL), lambda e, f: (e, 0, 0)),
        out_shape=jax.ShapeDtypeStruct((N_EXPERTS, rows, D_MODEL), BF16),
        scratch_shapes=[pltpu.VMEM((ct, D_MODEL), F32)],
        compiler_params=_params(("arbitrary", "arbitrary")),
        name="moe_ffn",
    )(xs, gs, w1, w3, w2)


def _combine_kernel(st_ref, x_ref, pos_ref, mod_ref, *rest, tile_of, ntc):
    nsub = TM // TC
    ys_refs = rest[:N_EXPERTS]
    o_ref = rest[N_EXPERTS]
    i = tile_of(pl.program_id(0))
    slot = lax.broadcasted_iota(jnp.int32, (TC, WINC), 1)
    g2 = mod_ref[:, 5 * D_MODEL:6 * D_MODEL]
    for u in range(nsub):
        rs = slice(u * TC, (u + 1) * TC)
        pos = pos_ref[rs, :]
        acc = None
        for e in range(N_EXPERTS):
            st = st_ref[e * ntc + i * nsub + u]
            delta = pl.multiple_of(st - st_ref[e * ntc + i * nsub], SLOT_ALIGN)
            onehot = jnp.where((slot + st).astype(F32) == pos[:, e:e + 1], 1.0, 0.0).astype(BF16)
            part = _dot(onehot, ys_refs[e][0, pl.ds(delta, WINC), :])
            acc = part if acc is None else acc + part
        o_ref[rs, :] = x_ref[rs, :] + g2 * acc


def _combine(starts, x, pos, mods_rows, ys, nlt, ltiles, latent_only):
    ttot = x.shape[0]
    nt = ttot // TM
    nsub = TM // TC
    bsz = ttot // (ltiles * TM)
    steps = bsz * nlt if latent_only else nt
    tile_of = (lambda j: (j // nlt) * ltiles + j % nlt) if latent_only else (lambda j: j)

    def mrow(j, st):
        i = tile_of(j)
        return (jnp.where(i % ltiles < nlt, i // ltiles, bsz), 0, 0)

    def ys_spec(e):
        return pl.BlockSpec(
            (pl.Element(1), pl.Element(WIN), pl.Element(D_MODEL)),
            lambda j, st: (e, pl.multiple_of(st[e * nt * nsub + tile_of(j) * nsub], SLOT_ALIGN), 0))

    return pl.pallas_call(
        functools.partial(_combine_kernel, tile_of=tile_of, ntc=nt * nsub),
        grid_spec=pltpu.PrefetchScalarGridSpec(
            num_scalar_prefetch=1,
            grid=(steps,),
            in_specs=[pl.BlockSpec((TM, D_MODEL), lambda j, st: (tile_of(j), 0)),
                      pl.BlockSpec((TM, LANES), lambda j, st: (tile_of(j), 0)),
                      pl.BlockSpec((None, 1, 6 * D_MODEL), mrow)]
                     + [ys_spec(e) for e in range(N_EXPERTS)],
            out_specs=pl.BlockSpec((TM, D_MODEL), lambda j, st: (j, 0))),
        out_shape=jax.ShapeDtypeStruct((steps * TM, D_MODEL), F32),
        compiler_params=_params(("arbitrary",)),
        name="combine",
    )(starts, x, pos, mods_rows, *([ys] * N_EXPERTS))


def _permute_w_in(w, b):
    widths = (2 * ML_HEADS * ML_DH, ML_HEADS * ML_DH, ML_HEADS * ML_DH, 4 * ML_HEADS, Q_LORA, KV_LORA, ROPE_D,
              RET_HEADS * RET_DK, RET_HEADS * RET_DK, RET_HEADS * RET_DV, RET_HEADS * RET_DV, N_BRANCH * D_MODEL)
    offs = np.concatenate([[0], np.cumsum(widths)])
    seg = lambda a, k: a[..., offs[k]:offs[k + 1]]
    zeros = lambda a, n: jnp.zeros(a.shape[:-1] + (n,), a.dtype)

    def build(a):
        parts = [seg(a, k) for k in (11, 1, 2, 9, 10, 7, 8, 5)]
        parts.append(zeros(a, OFF_QL - (OFF_KV + KV_LORA)))
        parts += [seg(a, 4), seg(a, 3), seg(a, 6)]
        parts.append(zeros(a, SLAB_W - (OFF_MISC + 4 * ML_HEADS + ROPE_D)))
        parts.append(seg(a, 0))
        return jnp.concatenate(parts, axis=-1)

    return build(w), build(b)


def kernel(x, c, ctx, c_ctx, ada_w, ada_b, norm1_w, norm2_w, w_in, b_in, conv_w, conv_b, ml_norm_w, mla_qa_norm,
           mla_wq_b, mla_kva_norm, mla_wkv_b, q_norm_w, k_norm_w, ret_norm_w, w_branch, w_out, router_w,
           exp_w1, exp_w3, exp_w2):
    bsz, n_lat, _ = x.shape
    n_ctx = ctx.shape[1]
    depth = ada_w.shape[0]
    ltot = n_lat + n_ctx
    nlt = n_lat // TM
    ltiles = ltot // TM
    nl, ncx = n_lat // CHUNK, n_ctx // CHUNK
    assert n_lat % TM == 0 and n_ctx == TM and bsz + 1 <= 8
    gather_tiles = max(t for t in range(1, GATHER_MAX_TILES + 1) if (bsz * ltiles) % t == 0)

    cond8 = jnp.zeros((8, D_MODEL), F32).at[:bsz].set(c).at[bsz].set(c_ctx)
    mods_all = _ada(cond8, ada_w, ada_b)

    rope32 = _rope_tables(n_lat, n_ctx)
    rope_r = jnp.tile(rope32, (1, 1, LANES // ROPE_D))
    ident = jnp.stack([jnp.ones((ltot, 1), F32), jnp.zeros((ltot, 1), F32), jnp.zeros((ltot, 1), F32)])
    rope_q = jnp.swapaxes(jnp.concatenate([jnp.broadcast_to(ident, (3, ltot, NOPE_D)), rope32,
                                           jnp.broadcast_to(ident, (3, ltot, LANES - QK_D))], axis=2), 1, 2)
    dmat, xi, kd, cd = _ret_tables()

    gi = jnp.arange(ML_HEADS * ML_DH) // ML_DH
    g8 = (gi[:, None] == jnp.arange(LANES)[None, :]).astype(BF16)
    g8t = g8.T

    cap_l = CAP_FACTOR * n_lat // N_EXPERTS
    cap_c = CAP_FACTOR * n_ctx // N_EXPERTS
    ct = bsz * (cap_l + cap_c)
    assert ct % (16 * FFN_ROW_CHUNKS) == 0 and cap_l % 16 == 0 and cap_c % 16 == 0
    rows = ct + WIN

    w_perm, b_perm = _permute_w_in(w_in, b_in[:, None, :])
    w_perm = w_perm.astype(BF16)
    xx = jnp.concatenate([x, ctx], axis=1)
    att_r = next(r for r in (11, 3, 1) if ltiles % r == 0)
    tq = min(2048, n_lat)

    for l in range(depth):
        mods = mods_all[l].reshape(8, 1, 6 * D_MODEL)
        slab, mq, mk = _inproj(xx, mods, norm1_w[l][None, :], w_perm, b_perm, conv_w[l], conv_b[l][None, :], l, nlt)
        hmf, hmb, hrf, hrb = _scans(mq, mk, slab, rope_r, dmat, xi, kd, cd, nl, ncx)

        wq = mla_wq_b[l].reshape(Q_LORA, MLA_HEADS, QK_D)
        wqt = jnp.pad(wq, ((0, 0), (0, 0), (0, LANES - QK_D))).reshape(Q_LORA, MLA_HEADS * LANES).T
        wkv = mla_wkv_b[l].reshape(KV_LORA, MLA_HEADS, NOPE_D + MLA_DV)
        wkt = jnp.pad(wkv[:, :, :NOPE_D], ((0, 0), (0, 0), (0, LANES - NOPE_D))).reshape(KV_LORA, MLA_HEADS * LANES).T
        wvt = wkv[:, :, NOPE_D:].reshape(KV_LORA, MLA_HEADS * MLA_DV).T
        qnw = jnp.pad(q_norm_w[l], (0, LANES - QK_D))[:, None]
        knw = jnp.pad(k_norm_w[l], (0, LANES - QK_D))[:, None]
        qt, kk, vt = _mla(slab, mla_qa_norm[l][None, :], wqt, mla_kva_norm[l][None, :], wkt, wvt, qnw, knw, rope_q)
        bound = (1.01 * QK_D ** 0.5 * LOG2E) * jnp.max(jnp.abs(q_norm_w[l])) * jnp.max(jnp.abs(k_norm_w[l]))
        flags = (bound <= ATT_BOUND_MAX).astype(jnp.int32).reshape(1)
        ub = jnp.full((8, LANES), bound, F32)
        yb_l = _attn(flags, qt, kk, vt, ub, tq, 0, n_lat // tq, 0, att_r, ltiles // att_r)
        yb_c = _attn(flags, qt, kk, vt, ub, TM, nlt, 1, nlt, 1, 1)

        rw = jnp.stack(_split2(jnp.pad(router_w[l], ((0, 0), (0, LANES - N_EXPERTS)))))
        xm, h2, aff = _merge(hmf, hmb, slab, ml_norm_w[l][None, :], yb_l, yb_c, hrf, hrb, ret_norm_w[l][None, :],
                             w_branch[l],
                             w_out[l], xx, mods, norm2_w[l][None, :], rw, g8, g8t, nlt)

        pos_l, post_l, gwt_l, off_l = _select(aff, n_lat, cap_l, 0, 0, cap_l + cap_c)
        pos_c, post_c, gwt_c, off_c = _select(aff, n_ctx, cap_c, nlt, cap_l, cap_l + cap_c)
        pos = jnp.concatenate([pos_l, pos_c], axis=1).reshape(bsz * ltot, LANES)
        expert_major = lambda a_l, a_c: jnp.swapaxes(jnp.concatenate([a_l, a_c], axis=2), 0, 1).reshape(
            N_EXPERTS, 1, bsz * ltot)
        post = expert_major(post_l, post_c)
        gwt = expert_major(gwt_l, gwt_c)
        base_l = (jnp.arange(bsz) * (cap_l + cap_c))[:, None, None]
        base_c = base_l + cap_l
        off = jnp.concatenate(
            [off_l[:, :, :2 * nlt].astype(jnp.int32) // SLOT_ALIGN * SLOT_ALIGN + base_l,
             off_c[:, :, :2].astype(jnp.int32) // SLOT_ALIGN * SLOT_ALIGN + base_c], axis=2)
        starts_c = jnp.swapaxes(off, 0, 1).reshape(-1)
        starts_g = jnp.swapaxes(off[:, :, ::2], 0, 1).reshape(-1)

        xs, gs = _moe_gather(starts_g, h2.reshape(bsz * ltot, D_MODEL), post, gwt, rows, gather_tiles)
        ys = _moe_ffn(xs, gs, exp_w1, exp_w3, exp_w2, l, ct)
        last = l == depth - 1
        xx = _combine(starts_c, xm.reshape(bsz * ltot, D_MODEL), pos, mods, ys, nlt, ltiles, last)
        xx = xx.reshape(bsz, n_lat if last else ltot, D_MODEL)

    return xx
```
